```python
import math
import jax, jax.numpy as jnp
from jax import lax
import numpy as np

D_MODEL = 1024
BATCH = 2
SEQ = 8192
DEPTH = 1
DEC_BATCH = 128
DEC_SEQ = 4
PAST_LEN = 8192
PAGE_SIZE = 128

N_HEADS = 8
QK_NOPE = 64
QK_ROPE = 32
QK_HEAD = QK_NOPE + QK_ROPE
V_HEAD = 64
Q_LORA = 384
KV_LORA = 256
ATTN_WIDTH = N_HEADS * V_HEAD
ROPE_THETA = 10000.0
SCALE = QK_HEAD ** -0.5
Q_BLOCK = 128
NEG_INF = -1e30
SSM_WIDTH = 512
GROUP = 16
N_GROUPS = SSM_WIDTH // GROUP
STATE = 64
DT_MIN = 1e-3
DT_MAX = 1e-1
D_FF = 2816
CONV_W = 3
PLE_DIM = 256
EPS = 1e-6
OFF_CKV = Q_LORA
OFF_KR = OFF_CKV + KV_LORA
OFF_U = OFF_KR + QK_ROPE
OFF_GA = OFF_U + SSM_WIDTH
OFF_GS = OFF_GA + D_MODEL
IN_COLS = OFF_GS + D_MODEL
SPLITS = (OFF_CKV, OFF_KR, OFF_U, OFF_GA, OFF_GS)

kernel_name = 'hybrid_mla_s5_convffn_ple_step'


def rmsnorm(x, g):
    xf = x.astype(jnp.float32)
    xf = xf * lax.rsqrt(jnp.mean(xf * xf, axis=-1, keepdims=True) + EPS)
    return (xf * g.astype(jnp.float32)).astype(x.dtype)


def rope_angles(pos):
    inv_freq = jnp.power(ROPE_THETA, -jnp.arange(0, QK_ROPE, 2, dtype=jnp.float32) / QK_ROPE)
    ang = pos.astype(jnp.float32)[:, None] * inv_freq[None, :]
    return jnp.cos(ang)[:, None, :], jnp.sin(ang)[:, None, :]


def apply_rope(x, cos, sin):
    half = QK_ROPE // 2
    x1, x2 = x[..., :half], x[..., half:]
    cos = cos.astype(x.dtype)
    sin = sin.astype(x.dtype)
    return jnp.concatenate([x1 * cos - x2 * sin, x1 * sin + x2 * cos], axis=-1)


def head_queries(cq, w_uq, g_q, cos, sin):
    q = rmsnorm(jnp.einsum('...sc,chd->...shd', cq, w_uq), g_q)
    return jnp.concatenate([q[..., :QK_NOPE], apply_rope(q[..., QK_NOPE:], cos, sin)], axis=-1)


def head_keys(ckv, kr, w_uk, g_k, cos, sin):
    k_nope = jnp.einsum('...sc,chd->...shd', ckv, w_uk)
    k_rope = jnp.broadcast_to(kr[..., None, :], k_nope.shape[:-1] + (QK_ROPE,))
    k = rmsnorm(jnp.concatenate([k_nope, k_rope], axis=-1), g_k)
    return jnp.concatenate([k[..., :QK_NOPE], apply_rope(k[..., QK_NOPE:], cos, sin)], axis=-1)


def mla_prompt(cq, ckv, kr, lw):
    b, s, _ = cq.shape
    cos, sin = rope_angles(jnp.arange(s))
    q = head_queries(cq, lw['w_uq'], lw['g_q'], cos, sin)
    k = head_keys(ckv, kr, lw['w_uk'], lw['g_k'], cos, sin)
    nb = s // Q_BLOCK
    qb = q.reshape(b, nb, Q_BLOCK, N_HEADS, QK_HEAD).transpose(1, 0, 2, 3, 4)
    kpos = jnp.arange(s)

    def block(args):
        q_blk, blk = args
        qpos = blk * Q_BLOCK + jnp.arange(Q_BLOCK)
        sc = jnp.einsum('bqhd,bkhd->bhqk', q_blk, k).astype(jnp.float32) * SCALE
        sc = jnp.where(kpos[None, :] <= qpos[:, None], sc, NEG_INF)
        pr = jax.nn.softmax(sc, axis=-1).astype(ckv.dtype)
        o_lat = jnp.einsum('bhqk,bkc->bhqc', pr, ckv)
        return jnp.einsum('bhqc,chd->bqhd', o_lat, lw['w_uv'])

    o = lax.map(block, (qb, jnp.arange(nb)))
    return o.transpose(1, 0, 2, 3, 4).reshape(b, s, ATTN_WIDTH)


def mla_sample(cq, ckv, kr, cache_ckv, cache_kr, page_table, layer, lw):
    b, t, _ = cq.shape
    cos, sin = rope_angles(PAST_LEN + jnp.arange(t))
    q = head_queries(cq, lw['w_uq'], lw['g_q'], cos, sin)
    k_new = head_keys(ckv, kr, lw['w_uk'], lw['g_k'], cos, sin)
    causal = jnp.arange(t)[None, :] <= jnp.arange(t)[:, None]
    sc = jnp.einsum('bthd,bshd->bhts', q, k_new).astype(jnp.float32) * SCALE
    sc = jnp.where(causal, sc, NEG_INF)
    m0 = sc.max(axis=-1)
    pr = jnp.exp(sc - m0[..., None])
    l0 = pr.sum(axis=-1)
    acc0 = jnp.einsum('bhts,bsc->bhtc', pr, ckv.astype(jnp.float32))

    def page_step(carry, j):
        m, l, acc = carry
        phys = page_table[:, j]
        c = cache_ckv[layer, phys]
        r = cache_kr[layer, phys]
        pc, ps = rope_angles(j * PAGE_SIZE + jnp.arange(PAGE_SIZE))
        kp = head_keys(c, r, lw['w_uk'], lw['g_k'], pc, ps)
        s_p = jnp.einsum('bthd,bphd->bhtp', q, kp).astype(jnp.float32) * SCALE
        m_new = jnp.maximum(m, s_p.max(axis=-1))
        corr = jnp.exp(m - m_new)
        p_p = jnp.exp(s_p - m_new[..., None])
        l = l * corr + p_p.sum(axis=-1)
        acc = acc * corr[..., None] + jnp.einsum('bhtp,bpc->bhtc', p_p, c.astype(jnp.float32))
        return (m_new, l, acc), None

    (m, l, acc), _ = lax.scan(page_step, (m0, l0, acc0), jnp.arange(page_table.shape[1]))
    o_lat = (acc / l[..., None]).astype(cq.dtype)
    return jnp.einsum('bhtc,chd->bthd', o_lat, lw['w_uv']).reshape(b, t, ATTN_WIDTH)


def ssm_discretize(lw):
    a_re = lw['a_re'].astype(jnp.float32)
    a_im = lw['a_im'].astype(jnp.float32)
    dt = jnp.exp(lw['log_dt'].astype(jnp.float32))[:, None]
    mag = jnp.exp(dt * a_re)
    ab_re = mag * jnp.cos(dt * a_im)
    ab_im = mag * jnp.sin(dt * a_im)
    den = a_re * a_re + a_im * a_im
    nr = ab_re - 1.0
    f_re = (nr * a_re + ab_im * a_im) / den
    f_im = (ab_im * a_re - nr * a_im) / den
    b_re = lw['b_re'].astype(jnp.float32)
    b_im = lw['b_im'].astype(jnp.float32)
    bb_re = f_re[..., None] * b_re - f_im[..., None] * b_im
    bb_im = f_re[..., None] * b_im + f_im[..., None] * b_re
    return ab_re, ab_im, bb_re, bb_im


def complex_affine_combine(e1, e2):
    a1r, a1i, b1r, b1i = e1
    a2r, a2i, b2r, b2i = e2
    return (a2r * a1r - a2i * a1i, a2r * a1i + a2i * a1r,
            a2r * b1r - a2i * b1i + b2r, a2r * b1i + a2i * b1r + b2i)


def ssm_scan(u, h0_re, h0_im, lw):
    ab_re, ab_im, bb_re, bb_im = ssm_discretize(lw)
    b, s, _ = u.shape
    ug = u.astype(jnp.float32).reshape(b, s, N_GROUPS, GROUP)
    bu_re = jnp.einsum('bsgi,gpi->bsgp', ug, bb_re)
    bu_im = jnp.einsum('bsgi,gpi->bsgp', ug, bb_im)
    h0_re = h0_re.astype(jnp.float32)
    h0_im = h0_im.astype(jnp.float32)
    bu_re = bu_re.at[:, 0].add(ab_re * h0_re - ab_im * h0_im)
    bu_im = bu_im.at[:, 0].add(ab_re * h0_im + ab_im * h0_re)
    a_re = jnp.broadcast_to(ab_re, bu_re.shape)
    a_im = jnp.broadcast_to(ab_im, bu_im.shape)
    _, _, h_re, h_im = lax.associative_scan(complex_affine_combine, (a_re, a_im, bu_re, bu_im), axis=1)
    c_re = lw['c_re'].astype(jnp.float32)
    c_im = lw['c_im'].astype(jnp.float32)
    y = jnp.einsum('bsgp,gip->bsgi', h_re, c_re) - jnp.einsum('bsgp,gip->bsgi', h_im, c_im)
    y = y + lw['d_skip'].astype(jnp.float32).reshape(N_GROUPS, GROUP) * ug
    return y.reshape(b, s, SSM_WIDTH).astype(u.dtype), h_re[:, -1], h_im[:, -1]


def conv_ffn(x, buf, lw):
    up = rmsnorm(x, lw['g_ffn']) @ lw['w_up']
    s = up.shape[1]
    full = jnp.concatenate([buf.astype(up.dtype), up], axis=1)
    conv = lw['conv_b'] + full[:, 0:s] * lw['conv_w'][0]
    for tap in range(1, CONV_W):
        conv = conv + full[:, tap:tap + s] * lw['conv_w'][tap]
    a, v = jnp.split(conv, 2, axis=-1)
    return (jax.nn.gelu(a) * v) @ lw['w_down'], full[:, s:]


def trunk_layer(x, p, attend, h0_re, h0_im, conv_buf, lw):
    xn = rmsnorm(x, lw['g_mix'])
    cq, ckv, kr, u, ga, gs = jnp.split(xn @ lw['w_in'], SPLITS, axis=-1)
    cq = rmsnorm(cq, lw['g_cq'])
    ckv = rmsnorm(ckv, lw['g_ckv'])
    att = attend(cq, ckv, kr)
    ys, h_re, h_im = ssm_scan(u, h0_re, h0_im, lw)
    gv, gg = jnp.split(jax.nn.gelu(ys) @ lw['w_glu'], 2, axis=-1)
    ys = gv * jax.nn.sigmoid(gg)
    mixed = jax.nn.sigmoid(ga) * (att @ lw['w_oa']) + jax.nn.sigmoid(gs) * (ys @ lw['w_os'])
    x = x + mixed @ lw['w_out']
    f, new_buf = conv_ffn(x, conv_buf, lw)
    x = x + f
    x = x + jax.nn.sigmoid(rmsnorm(x, lw['g_ple']) @ lw['w_ple_gate']) * (p @ lw['w_ple_proj'])
    return x, ckv, kr, h_re, h_im, new_buf


def setup_inputs(seed: int = 0) -> dict:
    key = jax.random.key(seed)
    keys = iter(jax.random.split(key, 64))
    f32 = jnp.float32

    def nrm(shape, scale):
        return jax.random.normal(next(keys), shape, f32) * scale

    def gain(n):
        return 1.0 + nrm((DEPTH, n), 0.02)

    n_pages = PAST_LEN // PAGE_SIZE
    n_used = DEC_BATCH * n_pages
    n_pool = n_used + n_used // 4
    page_table = jax.random.permutation(next(keys), n_pool)[:n_used].reshape(DEC_BATCH, n_pages).astype(jnp.int32)
    a_im = jnp.pi * jnp.arange(STATE, dtype=f32)[None, None, :] + nrm((DEPTH, N_GROUPS, STATE), 0.01)
    log_dt = jax.random.uniform(next(keys), (DEPTH, N_GROUPS), f32, math.log(DT_MIN), math.log(DT_MAX))
    return {
        'x_prompt': nrm((BATCH, SEQ, D_MODEL), 1.0),
        'x_sample': nrm((DEC_BATCH, DEC_SEQ, D_MODEL), 1.0),
        'p_prompt': nrm((DEPTH, BATCH, SEQ, PLE_DIM), 1.0),
        'p_sample': nrm((DEPTH, DEC_BATCH, DEC_SEQ, PLE_DIM), 1.0),
        'cache_ckv': nrm((DEPTH, n_pool, PAGE_SIZE, KV_LORA), 1.0),
        'cache_kr': nrm((DEPTH, n_pool, PAGE_SIZE, QK_ROPE), 1.0),
        'page_table': page_table,
        'state_ssm_re': nrm((DEPTH, DEC_BATCH, N_GROUPS, STATE), 0.5),
        'state_ssm_im': nrm((DEPTH, DEC_BATCH, N_GROUPS, STATE), 0.5),
        'state_conv': nrm((DEPTH, DEC_BATCH, CONV_W - 1, 2 * D_FF), 1.0),
        'g_mix': gain(D_MODEL),
        'w_in': nrm((DEPTH, D_MODEL, IN_COLS), D_MODEL ** -0.5),
        'g_cq': gain(Q_LORA),
        'g_ckv': gain(KV_LORA),
        'w_uq': nrm((DEPTH, Q_LORA, N_HEADS, QK_HEAD), Q_LORA ** -0.5),
        'w_uk': nrm((DEPTH, KV_LORA, N_HEADS, QK_NOPE), KV_LORA ** -0.5),
        'w_uv': nrm((DEPTH, KV_LORA, N_HEADS, V_HEAD), KV_LORA ** -0.5),
        'g_q': gain(QK_HEAD),
        'g_k': gain(QK_HEAD),
        'a_re': -0.5 + nrm((DEPTH, N_GROUPS, STATE), 0.01),
        'a_im': a_im,
        'log_dt': log_dt,
        'b_re': nrm((DEPTH, N_GROUPS, STATE, GROUP), (2 * GROUP) ** -0.5),
        'b_im': nrm((DEPTH, N_GROUPS, STATE, GROUP), (2 * GROUP) ** -0.5),
        'c_re': nrm((DEPTH, N_GROUPS, GROUP, STATE), STATE ** -0.5),
        'c_im': nrm((DEPTH, N_GROUPS, GROUP, STATE), STATE ** -0.5),
        'd_skip': nrm((DEPTH, SSM_WIDTH), 1.0),
        'w_glu': nrm((DEPTH, SSM_WIDTH, 2 * SSM_WIDTH), SSM_WIDTH ** -0.5),
        'w_oa': nrm((DEPTH, ATTN_WIDTH, D_MODEL), ATTN_WIDTH ** -0.5),
        'w_os': nrm((DEPTH, SSM_WIDTH, D_MODEL), SSM_WIDTH ** -0.5),
        'w_out': nrm((DEPTH, D_MODEL, D_MODEL), D_MODEL ** -0.5),
        'g_ffn': gain(D_MODEL),
        'w_up': nrm((DEPTH, D_MODEL, 2 * D_FF), D_MODEL ** -0.5),
        'conv_w': nrm((DEPTH, CONV_W, 2 * D_FF), CONV_W ** -0.5),
        'conv_b': nrm((DEPTH, 2 * D_FF), 0.01),
        'w_down': nrm((DEPTH, D_FF, D_MODEL), D_FF ** -0.5),
        'g_ple': gain(D_MODEL),
        'w_ple_gate': nrm((DEPTH, D_MODEL, D_MODEL), D_MODEL ** -0.5),
        'w_ple_proj': nrm((DEPTH, PLE_DIM, D_MODEL), PLE_DIM ** -0.5),
    }


def reference(x_prompt, x_sample, p_prompt, p_sample, cache_ckv, cache_kr, page_table,
              state_ssm_re, state_ssm_im, state_conv, g_mix, w_in, g_cq, g_ckv, w_uq, w_uk, w_uv,
              g_q, g_k, a_re, a_im, log_dt, b_re, b_im, c_re, c_im, d_skip, w_glu, w_oa, w_os,
              w_out, g_ffn, w_up, conv_w, conv_b, w_down, g_ple, w_ple_gate, w_ple_proj):
    yp, ys = x_prompt, x_sample
    bp = x_prompt.shape[0]
    ckv_p, kr_p, ckv_s, kr_s = [], [], [], []
    sre_p, sim_p, sre_s, sim_s, cv_p, cv_s = [], [], [], [], [], []
    for i in range(DEPTH):
        lw = dict(g_mix=g_mix[i], w_in=w_in[i], g_cq=g_cq[i], g_ckv=g_ckv[i], w_uq=w_uq[i],
                  w_uk=w_uk[i], w_uv=w_uv[i], g_q=g_q[i], g_k=g_k[i], a_re=a_re[i], a_im=a_im[i],
                  log_dt=log_dt[i], b_re=b_re[i], b_im=b_im[i], c_re=c_re[i], c_im=c_im[i],
                  d_skip=d_skip[i], w_glu=w_glu[i], w_oa=w_oa[i], w_os=w_os[i], w_out=w_out[i],
                  g_ffn=g_ffn[i], w_up=w_up[i], conv_w=conv_w[i], conv_b=conv_b[i], w_down=w_down[i],
                  g_ple=g_ple[i], w_ple_gate=w_ple_gate[i], w_ple_proj=w_ple_proj[i])
        zero_h = jnp.zeros((bp, N_GROUPS, STATE), jnp.float32)
        zero_buf = jnp.zeros((bp, CONV_W - 1, 2 * D_FF), yp.dtype)
        yp, ckv, kr, hr, hi, buf = trunk_layer(
            yp, p_prompt[i], lambda cq, c, r: mla_prompt(cq, c, r, lw), zero_h, zero_h, zero_buf, lw)
        ckv_p.append(ckv); kr_p.append(kr); sre_p.append(hr); sim_p.append(hi); cv_p.append(buf)
        ys, ckv, kr, hr, hi, buf = trunk_layer(
            ys, p_sample[i],
            lambda cq, c, r: mla_sample(cq, c, r, cache_ckv, cache_kr, page_table, i, lw),
            state_ssm_re[i], state_ssm_im[i], state_conv[i], lw)
        ckv_s.append(ckv); kr_s.append(kr); sre_s.append(hr); sim_s.append(hi); cv_s.append(buf)
    ckv_prompt = jnp.stack(ckv_p)
    kr_prompt = jnp.stack(kr_p)
    ckv_sample = jnp.stack(ckv_s)
    kr_sample = jnp.stack(kr_s)
    ssm_re_prompt = jnp.stack(sre_p)
    ssm_im_prompt = jnp.stack(sim_p)
    ssm_re_sample = jnp.stack(sre_s)
    ssm_im_sample = jnp.stack(sim_s)
    conv_prompt = jnp.stack(cv_p)
    conv_sample = jnp.stack(cv_s)
    return (yp, ys, ckv_prompt, kr_prompt, ckv_sample, kr_sample, ssm_re_prompt, ssm_im_prompt,
            ssm_re_sample, ssm_im_sample, conv_prompt, conv_sample)
```

```python
import functools
import math

import jax
import jax.numpy as jnp
from jax import lax
from jax.experimental import pallas as pl
from jax.experimental.pallas import tpu as pltpu

F32 = jnp.float32
BF16 = jnp.bfloat16

N_HEADS = 8
QK_NOPE = 64
QK_ROPE = 32
QK_HEAD = QK_NOPE + QK_ROPE
V_HEAD = 64
Q_LORA = 384
KV_LORA = 256
ROPE_THETA = 10000.0
SCALE = QK_HEAD ** -0.5
NEG_INF = -1e30
SSM_WIDTH = 512
GROUP = 16
N_GROUPS = SSM_WIDTH // GROUP
STATE = 64
N_STATE = N_GROUPS * STATE
CONV_W = 3
EPS = 1e-6
PAGE_SIZE = 128

LANES = 128
SUBLANES = 8
HEAD_PAD = LANES
NEW_PAD = 2 * SUBLANES
VMEM_LIMIT_BYTES = 56 * 1024 * 1024

INPROJ_ROWS = 512
FLASH_TQ = 512
FLASH_TK = 512
SSM_ROWS = 256
SSM_COLS = 512
POST_ROWS = 256
POST_ROWS_SAMPLE = 64
FFN_CHUNK = 256
PAGES_PER_STEP = 8
SSM_GROUP_BLOCK = 8


def _cparams(sem):
    return pltpu.CompilerParams(dimension_semantics=sem, vmem_limit_bytes=VMEM_LIMIT_BYTES)


def _const_spec(shape):
    nd = len(shape)
    return pl.BlockSpec(shape, lambda *_: (0,) * nd, pipeline_mode=pl.Buffered(1))


def _rms(x, g):
    return x * lax.rsqrt(jnp.mean(x * x, axis=-1, keepdims=True) + EPS) * g


def _bdot(a, b):
    return jnp.dot(a.astype(BF16), b, preferred_element_type=F32)


def _dot_nt(a, b):
    return lax.dot_general(a, b, (((1,), (1,)), ((), ())), preferred_element_type=F32)


def _ssm_disc_kernel(are_ref, aim_ref, ldt_ref, arer_ref, aimr_ref, ldtr_ref, bre_ref, bim_ref,
                     bbre_ref, bbim_ref, pre_ref, pim_ref, *, n_pow):
    def zoh(a_re, a_im, ldt):
        dt = jnp.exp(ldt)
        mag = jnp.exp(dt * a_re)
        ab_re = mag * jnp.cos(dt * a_im)
        ab_im = mag * jnp.sin(dt * a_im)
        return ab_re, ab_im

    a_re = arer_ref[...]
    a_im = aimr_ref[...]
    ab_re, ab_im = zoh(a_re, a_im, ldtr_ref[...])
    den = a_re * a_re + a_im * a_im
    nr = ab_re - 1.0
    f_re = (nr * a_re + ab_im * a_im) / den
    f_im = (ab_im * a_re - nr * a_im) / den
    b_re = bre_ref[...]
    b_im = bim_ref[...]
    bbre_ref[...] = f_re * b_re - f_im * b_im
    bbim_ref[...] = f_re * b_im + f_im * b_re

    p_re, p_im = zoh(are_ref[...], aim_ref[...], ldt_ref[...])
    c_re, c_im = p_re, p_im
    for j in range(n_pow):
        pre_ref[j] = c_re
        pim_ref[j] = c_im
        c_re, c_im = c_re * p_re - c_im * p_im, c_re * p_im + c_im * p_re


def _ssm_discretize(a_re, a_im, log_dt, b_re, b_im, n_pow):
    g, p = a_re.shape
    rows = g * GROUP
    rep = lambda z: jnp.repeat(z, GROUP, axis=0)
    bt = lambda z: z.transpose(0, 2, 1).reshape(rows, p)
    ldt = log_dt.reshape(g, 1)
    out_shape = (jax.ShapeDtypeStruct((rows, p), F32), jax.ShapeDtypeStruct((rows, p), F32),
                 jax.ShapeDtypeStruct((n_pow, g, p), F32), jax.ShapeDtypeStruct((n_pow, g, p), F32))
    bb_re, bb_im, pow_re, pow_im = pl.pallas_call(
        functools.partial(_ssm_disc_kernel, n_pow=n_pow),
        out_shape=out_shape, name="ssm_disc",
    )(a_re, a_im, ldt, rep(a_re), rep(a_im), rep(ldt), bt(b_re), bt(b_im))
    return bb_re, bb_im, pow_re.reshape(n_pow, g * p), pow_im.reshape(n_pow, g * p)


def _block_diag_in(bb):
    nb = N_GROUPS // SSM_GROUP_BLOCK
    z = bb.reshape(nb, SSM_GROUP_BLOCK, GROUP, STATE)
    eye = jnp.eye(SSM_GROUP_BLOCK, dtype=bb.dtype)
    out = jnp.einsum('kgip,gh->kgihp', z, eye)
    return out.reshape(nb, SSM_GROUP_BLOCK * GROUP, SSM_GROUP_BLOCK * STATE).astype(BF16)


def _block_diag_out(c):
    nb = N_GROUPS // SSM_GROUP_BLOCK
    z = c.reshape(nb, SSM_GROUP_BLOCK, GROUP, STATE)
    eye = jnp.eye(SSM_GROUP_BLOCK, dtype=c.dtype)
    out = jnp.einsum('kgip,gh->kgphi', z, eye)
    return out.reshape(nb, SSM_GROUP_BLOCK * STATE, SSM_GROUP_BLOCK * GROUP).astype(BF16)


OFF_CQ = 0
OFF_CKV = OFF_CQ + Q_LORA
OFF_U = OFF_CKV + KV_LORA


def _head_norm_rope(z, g_ref, rc, rsa, rsb, out_ref, scale):
    inv = 1.0 / QK_HEAD
    for h in range(N_HEADS):
        blk = z[:, h * HEAD_PAD:(h + 1) * HEAD_PAD]
        ss = jnp.sum(blk * blk, axis=-1, keepdims=True)
        blk = blk * lax.rsqrt(ss * inv + EPS) * g_ref[...]
        blk = blk * rc + pltpu.roll(blk, HEAD_PAD - QK_ROPE // 2, 1) * rsa + pltpu.roll(blk, QK_ROPE // 2, 1) * rsb
        if scale != 1.0:
            blk = blk * scale
        out_ref[:, h * HEAD_PAD:(h + 1) * HEAD_PAD] = blk.astype(out_ref.dtype)


def _inproj_kernel(x_ref, gmix_ref, win_ref, gcq_ref, gckv_ref, wuq_ref, wuk_ref, ekr_ref, wuv_ref,
                   gq_ref, gk_ref, rc_ref, rsa_ref, rsb_ref,
                   q_ref, k_ref, v_ref, ckv_ref, kr_ref, u_ref, sga_ref, sgs_ref, *, d_model):
    off_ga = OFF_U + SSM_WIDTH
    off_gs = off_ga + d_model
    off_kr = off_gs + d_model
    xn = _rms(x_ref[...], gmix_ref[...])
    proj = _bdot(xn, win_ref[...])
    cq = _rms(proj[:, OFF_CQ:OFF_CKV], gcq_ref[...])
    ckv = _rms(proj[:, OFF_CKV:OFF_U], gckv_ref[...])
    kr = proj[:, off_kr:off_kr + QK_ROPE]
    u_ref[...] = proj[:, OFF_U:off_ga]
    sga_ref[...] = jax.nn.sigmoid(proj[:, off_ga:off_gs]).astype(sga_ref.dtype)
    sgs_ref[...] = jax.nn.sigmoid(proj[:, off_gs:off_kr]).astype(sgs_ref.dtype)
    ckv_ref[...] = ckv
    kr_ref[...] = kr
    ckv_b = ckv.astype(BF16)
    v_ref[...] = jnp.dot(ckv_b, wuv_ref[...], preferred_element_type=F32).astype(v_ref.dtype)
    rc = rc_ref[...]
    rsa = rsa_ref[...]
    rsb = rsb_ref[...]
    qf = _bdot(cq, wuq_ref[...])
    _head_norm_rope(qf, gq_ref, rc, rsa, rsb, q_ref, SCALE)
    kf = jnp.dot(ckv_b, wuk_ref[...], preferred_element_type=F32) + _bdot(kr, ekr_ref[...])
    _head_norm_rope(kf, gk_ref, rc, rsa, rsb, k_ref, 1.0)


def _inproj(x, w, rope, act_dtype):
    nb, rows, d = x.shape
    tr = min(INPROJ_ROWS, rows)
    assert rows % tr == 0
    hp = N_HEADS * HEAD_PAD
    row_spec = lambda n: pl.BlockSpec((None, tr, n), lambda b, t: (b, t, 0))
    tab_spec = pl.BlockSpec((tr, HEAD_PAD), lambda b, t: (t, 0))
    weights = (w['g_mix'], w['w_in'], w['g_cq'], w['g_ckv'], w['w_uq'], w['w_uk'], w['e_kr'],
               w['w_uv'], w['g_q'], w['g_k'])
    out_shape = (
        jax.ShapeDtypeStruct((nb, rows, hp), act_dtype),
        jax.ShapeDtypeStruct((nb, rows, hp), act_dtype),
        jax.ShapeDtypeStruct((nb, rows, hp), act_dtype),
        jax.ShapeDtypeStruct((nb, rows, KV_LORA), F32),
        jax.ShapeDtypeStruct((nb, rows, QK_ROPE), F32),
        jax.ShapeDtypeStruct((nb, rows, SSM_WIDTH), F32),
        jax.ShapeDtypeStruct((nb, rows, d), BF16),
        jax.ShapeDtypeStruct((nb, rows, d), BF16),
    )
    return pl.pallas_call(
        functools.partial(_inproj_kernel, d_model=d),
        grid=(nb, rows // tr),
        in_specs=[row_spec(d)] + [_const_spec(z.shape) for z in weights] + [tab_spec] * 3,
        out_specs=tuple(row_spec(s.shape[-1]) for s in out_shape),
        out_shape=out_shape,
        compiler_params=_cparams(("parallel", "parallel")),
        name="inproj",
    )(x, *weights, *rope)


def _flash_kernel(q_ref, k_ref, v_ref, o_ref, *, tq, tk):
    qi = pl.program_id(2)
    q = q_ref[...]
    n_diag = tq // tk

    def step(kj, carry, masked):
        m, l, acc = carry
        start = pl.multiple_of(kj * tk, tk)
        ks = k_ref[pl.ds(start, tk), :]
        vs = v_ref[pl.ds(start, tk), :]
        s = _dot_nt(q, ks)
        if masked:
            row = lax.broadcasted_iota(jnp.int32, (tq, tk), 0) + qi * tq
            col = lax.broadcasted_iota(jnp.int32, (tq, tk), 1) + kj * tk
            s = jnp.where(col <= row, s, NEG_INF)
        m_new = jnp.maximum(m, jnp.max(s, axis=-1, keepdims=True))
        p = jnp.exp(s - m_new)
        corr = jnp.exp(m - m_new)
        l = l * corr + jnp.sum(p, axis=-1, keepdims=True)
        acc = acc * corr + jnp.dot(p.astype(BF16), vs, preferred_element_type=F32)
        return m_new, l, acc

    init = (jnp.full((tq, 1), NEG_INF, F32), jnp.zeros((tq, 1), F32), jnp.zeros((tq, HEAD_PAD), F32))
    carry = lax.fori_loop(0, qi * n_diag, lambda kj, c: step(kj, c, False), init)
    for d in range(n_diag):
        carry = step(qi * n_diag + d, carry, True)
    _, l, acc = carry
    o_ref[...] = (acc / l).astype(o_ref.dtype)


def _flash_prompt(q, k, v):
    b, s, hp = q.shape
    tq = min(FLASH_TQ, s)
    tk = min(FLASH_TK, tq)
    assert s % tq == 0 and tq % tk == 0
    return pl.pallas_call(
        functools.partial(_flash_kernel, tq=tq, tk=tk),
        grid=(b, N_HEADS, s // tq),
        in_specs=[pl.BlockSpec((None, tq, HEAD_PAD), lambda bi, h, qi: (bi, qi, h)),
                  pl.BlockSpec((None, s, HEAD_PAD), lambda bi, h, qi: (bi, 0, h)),
                  pl.BlockSpec((None, s, HEAD_PAD), lambda bi, h, qi: (bi, 0, h))],
        out_specs=pl.BlockSpec((None, tq, HEAD_PAD), lambda bi, h, qi: (bi, qi, h)),
        out_shape=jax.ShapeDtypeStruct((b, s, hp), BF16),
        compiler_params=_cparams(("parallel", "parallel", "arbitrary")),
        name="flash_prompt",
    )(q, k, v)


def _sample_attn_kernel(pt_ref, qn_ref, qr_ref, qrs_ref, qf_ref, knew_ref, cnew_ref, ca_ref, cb_ref,
                        wukp_ref, gkn_ref, gkr_ref, wuv_ref, hmask_ref, *rest, n_pages, n_new):
    c_refs = rest[:n_pages]
    r_refs = rest[n_pages:2 * n_pages]
    o_ref = rest[2 * n_pages]
    cbuf, m_sc, l_sc, acc_sc = rest[2 * n_pages + 1:]
    ci = pl.program_id(1)
    nq = n_new * N_HEADS
    keys = n_pages * PAGE_SIZE

    def online(s, vals):
        m = m_sc[...]
        m_new = jnp.maximum(m, jnp.max(s, axis=-1, keepdims=True))
        p = jnp.exp(s - m_new)
        corr = jnp.exp(m - m_new)
        l_sc[...] = l_sc[...] * corr + jnp.sum(p, axis=-1, keepdims=True)
        acc_sc[...] = acc_sc[...] * corr + jnp.dot(p.astype(BF16), vals, preferred_element_type=F32)
        m_sc[...] = m_new

    @pl.when(ci == 0)
    def _():
        m_sc[...] = jnp.full(m_sc.shape, NEG_INF, F32)
        l_sc[...] = jnp.zeros(l_sc.shape, F32)
        acc_sc[...] = jnp.zeros(acc_sc.shape, F32)
        s_t = _dot_nt(knew_ref[...].astype(BF16), qf_ref[...].astype(BF16))
        s = jnp.transpose(jnp.concatenate([s_t] * (LANES // NEW_PAD), axis=0))[:nq, :NEW_PAD]
        tok = lax.broadcasted_iota(jnp.int32, (nq, NEW_PAD), 0) // N_HEADS
        key = lax.broadcasted_iota(jnp.int32, (nq, NEW_PAD), 1)
        s = jnp.where(key <= tok, s, NEG_INF)
        online(s, cnew_ref[...].astype(BF16))

    for i in range(n_pages):
        cbuf[i * PAGE_SIZE:(i + 1) * PAGE_SIZE, :] = c_refs[i][...].astype(BF16)
    r = jnp.concatenate([r_refs[i][...] for i in range(n_pages)], axis=0)
    c = cbuf[...]
    kn = jnp.dot(c, wukp_ref[...], preferred_element_type=F32)
    sq = kn * kn
    ss = sq[:, 0:LANES]
    for j in range(1, N_HEADS * QK_NOPE // LANES):
        ss = ss + sq[:, j * LANES:(j + 1) * LANES]
    shift = N_HEADS
    while shift < LANES:
        ss = ss + pltpu.roll(ss, shift, 1)
        shift *= 2
    r2 = jnp.sum(r * r, axis=-1, keepdims=True)
    rn = lax.rsqrt((ss + r2) * (1.0 / QK_HEAD) + EPS)
    rg = r * gkr_ref[...]
    qn = (qn_ref[...] * gkn_ref[...]).astype(BF16)
    s_t = (_dot_nt(kn.astype(BF16), qn)
           + _dot_nt((rg * ca_ref[...]).astype(BF16), qr_ref[...].astype(BF16))
           + _dot_nt((rg * cb_ref[...]).astype(BF16), qrs_ref[...].astype(BF16)))
    s = jnp.transpose(s_t * rn)[:nq, :]
    online(s, c)

    @pl.when(ci == pl.num_programs(1) - 1)
    def _():
        o_lat = acc_sc[...] / l_sc[...]
        full = _bdot(o_lat, wuv_ref[...]) * hmask_ref[...]
        o_ref[...] = jnp.sum(full.reshape(n_new, N_HEADS, N_HEADS * V_HEAD), axis=1)


def _sample_attention(q, k, ckv, cache_ckv, cache_kr, page_table, layer, w, rope_k):
    bd, t, hp = q.shape
    n_logical = page_table.shape[1]
    npg = min(PAGES_PER_STEP, n_logical)
    assert n_logical % npg == 0 and t <= NEW_PAD
    nq = t * N_HEADS
    head_mask = ((jnp.arange(nq) % N_HEADS)[:, None] == (jnp.arange(N_HEADS * V_HEAD) // V_HEAD)[None, :]).astype(F32)
    qh = q.reshape(bd, t, N_HEADS, HEAD_PAD)
    eye = jnp.eye(N_HEADS, dtype=F32)

    def pad_rows(z):
        return jnp.pad(z, ((0, 0), (0, LANES - nq), (0, 0)))

    qn = pad_rows(jnp.einsum('bthd,hk->bthdk', qh[..., :QK_NOPE], eye).reshape(bd, nq, QK_NOPE * N_HEADS))
    qr = qh[..., QK_NOPE:QK_HEAD].reshape(bd, nq, QK_ROPE)
    qrs = pad_rows(jnp.concatenate([qr[..., QK_ROPE // 2:], qr[..., :QK_ROPE // 2]], axis=-1))
    qr = pad_rows(qr)
    qf = pad_rows(jnp.einsum('bthl,hk->bthkl', qh, eye).reshape(bd, nq, hp))
    pad_t = lambda z: jnp.pad(z, ((0, 0), (0, NEW_PAD - t), (0, 0)))
    knew = pad_t(k)
    cnew = pad_t(ckv)
    ca, cb = rope_k
    keys = npg * PAGE_SIZE
    n_chunks = n_logical // npg

    def bspec(rows, n):
        return pl.BlockSpec((None, rows, n), lambda b, c, pt: (b, 0, 0))

    def page_spec(i, n):
        return pl.BlockSpec((None, None, PAGE_SIZE, n),
                            lambda b, c, pt: (layer, pt[b * n_logical + c * npg + i], 0, 0))

    def wspec(shape):
        nd = len(shape)
        return pl.BlockSpec(shape, lambda b, c, pt: (0,) * nd)

    weights = (w['w_uk_perm'], w['g_k_nope_perm'], w['g_k_rope'], w['w_uv_flat'], head_mask)
    in_specs = ([bspec(LANES, QK_NOPE * N_HEADS), bspec(LANES, QK_ROPE), bspec(LANES, QK_ROPE), bspec(LANES, hp),
                 bspec(NEW_PAD, hp), bspec(NEW_PAD, KV_LORA),
                 pl.BlockSpec((keys, QK_ROPE), lambda b, c, pt: (c, 0)),
                 pl.BlockSpec((keys, QK_ROPE), lambda b, c, pt: (c, 0))]
                + [wspec(z.shape) for z in weights]
                + [page_spec(i, KV_LORA) for i in range(npg)]
                + [page_spec(i, QK_ROPE) for i in range(npg)])
    grid_spec = pltpu.PrefetchScalarGridSpec(
        num_scalar_prefetch=1, grid=(bd, n_chunks), in_specs=in_specs,
        out_specs=pl.BlockSpec((None, t, N_HEADS * V_HEAD), lambda b, c, pt: (b, 0, 0)),
        scratch_shapes=[pltpu.VMEM((keys, KV_LORA), BF16), pltpu.VMEM((nq, 1), F32),
                        pltpu.VMEM((nq, 1), F32), pltpu.VMEM((nq, KV_LORA), F32)])
    return pl.pallas_call(
        functools.partial(_sample_attn_kernel, n_pages=npg, n_new=t),
        grid_spec=grid_spec,
        out_shape=jax.ShapeDtypeStruct((bd, t, N_HEADS * V_HEAD), F32),
        compiler_params=_cparams(("parallel", "arbitrary")),
        name="sample_attn",
    )(page_table.reshape(-1), qn, qr, qrs, qf, knew, cnew, ca, cb, *weights,
      *([cache_ckv] * npg), *([cache_kr] * npg))


def _ssm_in(u_b, bre_ref, bim_ref, store_re, store_im):
    nb = bre_ref.shape[0]
    kw = bre_ref.shape[1]
    nw = bre_ref.shape[2]
    for kb in range(nb):
        blk = u_b[:, kb * kw:(kb + 1) * kw]
        store_re(kb * nw, nw, jnp.dot(blk, bre_ref[kb], preferred_element_type=F32))
        store_im(kb * nw, nw, jnp.dot(blk, bim_ref[kb], preferred_element_type=F32))


def _ssm_out(h_re, h_im, cre_ref, cim_ref):
    nb = cre_ref.shape[0]
    kw = cre_ref.shape[1]
    parts = []
    for kb in range(nb):
        parts.append(jnp.dot(h_re[:, kb * kw:(kb + 1) * kw].astype(BF16), cre_ref[kb], preferred_element_type=F32)
                     - jnp.dot(h_im[:, kb * kw:(kb + 1) * kw].astype(BF16), cim_ref[kb], preferred_element_type=F32))
    return jnp.concatenate(parts, axis=-1)


def _glu(y, wglu_ref):
    g = _bdot(jax.nn.gelu(y), wglu_ref[...])
    half = g.shape[-1] // 2
    return g[:, :half] * jax.nn.sigmoid(g[:, half:])


def _ssm_prompt_kernel(u_ref, perm_ref, permt_ref, bre_ref, bim_ref, pre_ref, pim_ref, cre_ref, cim_ref,
                       dskip_ref, wglu_ref, ys_ref, hre_out, him_out,
                       hre, him, car_re, car_im, hin_re, hin_im, *, rows, sub):
    ti = pl.program_id(1)

    @pl.when(ti == 0)
    def _():
        car_re[...] = jnp.zeros(car_re.shape, F32)
        car_im[...] = jnp.zeros(car_im.shape, F32)

    u = u_ref[...]
    u_hi = u.astype(BF16)
    u_lo = (u - u_hi.astype(F32)).astype(BF16)
    perm = perm_ref[...]
    up = jnp.dot(perm, u_hi, preferred_element_type=F32) + jnp.dot(perm, u_lo, preferred_element_type=F32)

    def st_re(c0, n, val):
        hre[:, c0:c0 + n] = val

    def st_im(c0, n, val):
        him[:, c0:c0 + n] = val

    _ssm_in(up.astype(BF16), bre_ref, bim_ref, st_re, st_im)

    n_cb = N_STATE // SSM_COLS
    for cb in range(n_cb):
        cols = slice(cb * SSM_COLS, (cb + 1) * SSM_COLS)
        a_re = jnp.broadcast_to(pre_ref[0:1, cols], (SUBLANES, SSM_COLS))
        a_im = jnp.broadcast_to(pim_ref[0:1, cols], (SUBLANES, SSM_COLS))

        def local(j, st):
            s_re, s_im = st
            r0 = pl.multiple_of(j * SUBLANES, SUBLANES)
            n_re = a_re * s_re - a_im * s_im + hre[pl.ds(r0, SUBLANES), cols]
            n_im = a_re * s_im + a_im * s_re + him[pl.ds(r0, SUBLANES), cols]
            hre[pl.ds(r0, SUBLANES), cols] = n_re
            him[pl.ds(r0, SUBLANES), cols] = n_im
            return n_re, n_im

        zero = jnp.zeros((SUBLANES, SSM_COLS), F32)
        e_re, e_im = lax.fori_loop(0, sub, local, (zero, zero))

        as_re = pre_ref[sub - 1:sub, cols]
        as_im = pim_ref[sub - 1:sub, cols]
        c_re = car_re[:, cols]
        c_im = car_im[:, cols]
        for s in range(SUBLANES):
            hin_re[s:s + 1, cols] = c_re
            hin_im[s:s + 1, cols] = c_im
            c_re, c_im = (as_re * c_re - as_im * c_im + e_re[s:s + 1],
                          as_re * c_im + as_im * c_re + e_im[s:s + 1])
        car_re[:, cols] = c_re
        car_im[:, cols] = c_im
        g_re = hin_re[:, cols]
        g_im = hin_im[:, cols]

        def fix(j, _):
            r0 = pl.multiple_of(j * SUBLANES, SUBLANES)
            p_re = jnp.broadcast_to(pre_ref[pl.ds(j, 1), cols], (SUBLANES, SSM_COLS))
            p_im = jnp.broadcast_to(pim_ref[pl.ds(j, 1), cols], (SUBLANES, SSM_COLS))
            hre[pl.ds(r0, SUBLANES), cols] = hre[pl.ds(r0, SUBLANES), cols] + (p_re * g_re - p_im * g_im)
            him[pl.ds(r0, SUBLANES), cols] = him[pl.ds(r0, SUBLANES), cols] + (p_re * g_im + p_im * g_re)
            return 0

        lax.fori_loop(0, sub, fix, 0)

    y = _ssm_out(hre[...], him[...], cre_ref, cim_ref) + dskip_ref[...] * up
    ys = _glu(y, wglu_ref).astype(BF16)
    ys_ref[...] = jnp.dot(permt_ref[...], ys, preferred_element_type=F32).astype(ys_ref.dtype)

    @pl.when(ti == pl.num_programs(1) - 1)
    def _():
        hre_out[...] = car_re[...]
        him_out[...] = car_im[...]


def _ssm_prompt(u, w):
    b, s, width = u.shape
    rows = min(SSM_ROWS, s)
    assert s % rows == 0 and rows % SUBLANES == 0
    sub = rows // SUBLANES
    r = jnp.arange(rows)
    t_of_r = (r % SUBLANES) * sub + r // SUBLANES
    perm = (t_of_r[:, None] == jnp.arange(rows)[None, :]).astype(BF16)
    pre = w['pow_re'][:sub]
    pim = w['pow_im'][:sub]
    weights = (perm, perm.T, w['bb_re'], w['bb_im'], pre, pim, w['cc_re'], w['cc_im'], w['d_skip'], w['w_glu'])
    ys, h_re, h_im = pl.pallas_call(
        functools.partial(_ssm_prompt_kernel, rows=rows, sub=sub),
        grid=(b, s // rows),
        in_specs=[pl.BlockSpec((None, rows, width), lambda bi, t: (bi, t, 0))]
                 + [_const_spec(z.shape) for z in weights],
        out_specs=(pl.BlockSpec((None, rows, width), lambda bi, t: (bi, t, 0)),
                   pl.BlockSpec((None, 1, N_STATE), lambda bi, t: (bi, 0, 0)),
                   pl.BlockSpec((None, 1, N_STATE), lambda bi, t: (bi, 0, 0))),
        out_shape=(jax.ShapeDtypeStruct((b, s, width), BF16),
                   jax.ShapeDtypeStruct((b, 1, N_STATE), F32),
                   jax.ShapeDtypeStruct((b, 1, N_STATE), F32)),
        scratch_shapes=[pltpu.VMEM((rows, N_STATE), F32), pltpu.VMEM((rows, N_STATE), F32),
                        pltpu.VMEM((1, N_STATE), F32), pltpu.VMEM((1, N_STATE), F32),
                        pltpu.VMEM((SUBLANES, N_STATE), F32), pltpu.VMEM((SUBLANES, N_STATE), F32)],
        compiler_params=_cparams(("parallel", "arbitrary")),
        name="ssm_prompt",
    )(u, *weights)
    return ys, h_re.reshape(b, N_GROUPS, STATE), h_im.reshape(b, N_GROUPS, STATE)


def _ssm_sample_kernel(u_ref, h0re_ref, h0im_ref, bre_ref, bim_ref, pre_ref, pim_ref, cre_ref, cim_ref,
                       dskip_ref, wglu_ref, ys_ref, hre_out, him_out, bu_re, bu_im, *, steps):
    a_re = pre_ref[0:1, :]
    a_im = pim_ref[0:1, :]
    h_re = h0re_ref[...]
    h_im = h0im_ref[...]

    def st_re(c0, n, val):
        bu_re[:, c0:c0 + n] = val

    def st_im(c0, n, val):
        bu_im[:, c0:c0 + n] = val

    for t in range(steps):
        u = u_ref[t]
        _ssm_in(u.astype(BF16), bre_ref, bim_ref, st_re, st_im)
        h_re, h_im = (a_re * h_re - a_im * h_im + bu_re[...], a_re * h_im + a_im * h_re + bu_im[...])
        y = _ssm_out(h_re, h_im, cre_ref, cim_ref) + dskip_ref[...] * u
        ys_ref[t] = _glu(y, wglu_ref).astype(ys_ref.dtype)
    hre_out[...] = h_re
    him_out[...] = h_im


def _ssm_sample(u, h0_re, h0_im, w):
    bd, t, width = u.shape
    ut = u.transpose(1, 0, 2)
    ys, h_re, h_im = pl.pallas_call(
        functools.partial(_ssm_sample_kernel, steps=t),
        out_shape=(jax.ShapeDtypeStruct((t, bd, width), BF16),
                   jax.ShapeDtypeStruct((bd, N_STATE), F32),
                   jax.ShapeDtypeStruct((bd, N_STATE), F32)),
        scratch_shapes=[pltpu.VMEM((bd, N_STATE), F32), pltpu.VMEM((bd, N_STATE), F32)],
        compiler_params=pltpu.CompilerParams(vmem_limit_bytes=VMEM_LIMIT_BYTES),
        name="ssm_sample",
    )(ut, h0_re.reshape(bd, N_STATE).astype(F32), h0_im.reshape(bd, N_STATE).astype(F32),
      w['bb_re'], w['bb_im'], w['pow_re'][:1], w['pow_im'][:1], w['cc_re'], w['cc_im'], w['d_skip'], w['w_glu'])
    return ys.transpose(1, 0, 2), h_re.reshape(bd, N_GROUPS, STATE), h_im.reshape(bd, N_GROUPS, STATE)


def _post_kernel(*refs, rows, d_ff, seq_len, carry):
    if carry:
        (x_ref, att_ref, ys_ref, sga_ref, sgs_ref, p_ref, woa_ref, wos_ref, wout_ref, gffn_ref, wup_ref,
         cw_ref, cbias_ref, wdown_ref, gple_ref, wpg_ref, wpp_ref, y_ref, tail_ref, ext) = refs
        prev1_ref = prev2_ref = None
    else:
        (x_ref, att_ref, ys_ref, sga_ref, sgs_ref, p_ref, prev1_ref, prev2_ref, woa_ref, wos_ref, wout_ref,
         gffn_ref, wup_ref, cw_ref, cbias_ref, wdown_ref, gple_ref, wpg_ref, wpp_ref, y_ref, up_ref, ext) = refs
    ti = pl.program_id(1)
    pad = SUBLANES

    mixed = (sga_ref[...].astype(F32) * jnp.dot(att_ref[...], woa_ref[...], preferred_element_type=F32)
             + sgs_ref[...].astype(F32) * jnp.dot(ys_ref[...], wos_ref[...], preferred_element_type=F32))
    x1 = x_ref[...] + _bdot(mixed, wout_ref[...])
    xn = _rms(x1, gffn_ref[...]).astype(BF16)

    if carry:
        @pl.when(ti == 0)
        def _():
            ext[0:pad, :] = jnp.zeros((pad, 2 * d_ff), F32)
    else:
        ext[0:pad, :] = jnp.zeros((pad, 2 * d_ff), F32)
        pos = lax.broadcasted_iota(jnp.int32, (rows, 1), 0) % seq_len

    def conv_cols(c0, n):
        cols = slice(c0, c0 + n)
        up = jnp.dot(xn, wup_ref[:, cols], preferred_element_type=F32)
        ext[pad:pad + rows, cols] = up
        s1 = ext[pad - 1:pad - 1 + rows, cols]
        s2 = ext[pad - 2:pad - 2 + rows, cols]
        if not carry:
            up_ref[:, cols] = up
            s1 = jnp.where(pos >= 1, s1, prev1_ref[:, cols])
            s2 = jnp.where(pos >= 2, s2, prev2_ref[:, cols])
        return cbias_ref[:, cols] + s2 * cw_ref[0:1, cols] + s1 * cw_ref[1:2, cols] + up * cw_ref[2:3, cols]

    f = jnp.zeros((rows, x1.shape[-1]), F32)
    for c in range(d_ff // FFN_CHUNK):
        a = conv_cols(c * FFN_CHUNK, FFN_CHUNK)
        v = conv_cols(d_ff + c * FFN_CHUNK, FFN_CHUNK)
        hdn = (jax.nn.gelu(a) * v).astype(BF16)
        f = f + jnp.dot(hdn, wdown_ref[c * FFN_CHUNK:(c + 1) * FFN_CHUNK, :], preferred_element_type=F32)
    x2 = x1 + f

    if carry:
        tail = ext[rows:rows + pad, :]
        ext[0:pad, :] = tail
        tail_ref[...] = tail

    gate = jax.nn.sigmoid(_bdot(_rms(x2, gple_ref[...]), wpg_ref[...]))
    y_ref[...] = x2 + gate * _bdot(p_ref[...], wpp_ref[...])


def _post(x, att, ys, sga, sgs, p, w, prev=None):
    nb, total, d = x.shape
    d_ff = w['w_down'].shape[0]
    assert d_ff % FFN_CHUNK == 0
    carry = prev is None
    rows = min(POST_ROWS if carry else POST_ROWS_SAMPLE, total)
    assert total % rows == 0
    row_spec = lambda n: pl.BlockSpec((None, rows, n), lambda b, t: (b, t, 0))
    acts = [x, att, ys, sga, sgs, p] + ([] if carry else list(prev[:2]))
    weights = (w['w_oa'], w['w_os'], w['w_out'], w['g_ffn'], w['w_up'], w['conv_w'], w['conv_b'], w['w_down'],
               w['g_ple'], w['w_ple_gate'], w['w_ple_proj'])
    if carry:
        seq_len = total
        extra_spec = pl.BlockSpec((None, SUBLANES, 2 * d_ff), lambda b, t: (b, 0, 0))
        extra_shape = jax.ShapeDtypeStruct((nb, SUBLANES, 2 * d_ff), F32)
    else:
        seq_len = prev[2]
        assert rows % seq_len == 0
        extra_spec = row_spec(2 * d_ff)
        extra_shape = jax.ShapeDtypeStruct((nb, total, 2 * d_ff), F32)
    return pl.pallas_call(
        functools.partial(_post_kernel, rows=rows, d_ff=d_ff, seq_len=seq_len, carry=carry),
        grid=(nb, total // rows),
        in_specs=[row_spec(z.shape[-1]) for z in acts] + [_const_spec(z.shape) for z in weights],
        out_specs=(row_spec(d), extra_spec),
        out_shape=(jax.ShapeDtypeStruct((nb, total, d), F32), extra_shape),
        scratch_shapes=[pltpu.VMEM((rows + SUBLANES, 2 * d_ff), F32)],
        compiler_params=_cparams(("parallel", "arbitrary")),
        name="post_prompt" if carry else "post_sample",
    )(*acts, *weights)


def _rope_tables_q(pos):
    inv_freq = jnp.power(ROPE_THETA, -jnp.arange(0, QK_ROPE, 2, dtype=F32) / QK_ROPE)
    ang = pos.astype(F32)[:, None] * inv_freq[None, :]
    cos, sin = jnp.cos(ang), jnp.sin(ang)
    n = pos.shape[0]
    half = QK_ROPE // 2
    z = lambda k: jnp.zeros((n, k), F32)
    rc = jnp.concatenate([jnp.ones((n, QK_NOPE), F32), cos, cos, z(HEAD_PAD - QK_HEAD)], axis=1)
    rsa = jnp.concatenate([z(QK_NOPE), -sin, z(half), z(HEAD_PAD - QK_HEAD)], axis=1)
    rsb = jnp.concatenate([z(QK_NOPE), z(half), sin, z(HEAD_PAD - QK_HEAD)], axis=1)
    return rc, rsa, rsb


def _rope_tables_k(pos):
    inv_freq = jnp.power(ROPE_THETA, -jnp.arange(0, QK_ROPE, 2, dtype=F32) / QK_ROPE)
    ang = pos.astype(F32)[:, None] * inv_freq[None, :]
    cos, sin = jnp.cos(ang), jnp.sin(ang)
    return jnp.concatenate([cos, cos], axis=1), jnp.concatenate([sin, -sin], axis=1)


def _pad_heads(wm, width):
    rows = wm.shape[0]
    return jnp.pad(wm, ((0, 0), (0, 0), (0, HEAD_PAD - width))).reshape(rows, N_HEADS * HEAD_PAD)


def _layer_weights(lw, d_model, n_pow):
    w = {}
    row = lambda z: z.reshape(1, -1).astype(F32)
    off_ckv = Q_LORA
    off_kr = off_ckv + KV_LORA
    off_u = off_kr + QK_ROPE
    off_ga = off_u + SSM_WIDTH
    off_gs = off_ga + d_model
    win = lw['w_in']
    w['w_in'] = jnp.concatenate(
        [win[:, :off_ckv], win[:, off_ckv:off_kr], win[:, off_u:off_ga], win[:, off_ga:off_gs], win[:, off_gs:],
         win[:, off_kr:off_u], jnp.zeros((d_model, LANES - QK_ROPE), win.dtype)], axis=1).astype(BF16)
    w['g_mix'] = row(lw['g_mix'])
    w['g_cq'] = row(lw['g_cq'])
    w['g_ckv'] = row(lw['g_ckv'])
    w['w_uq'] = _pad_heads(lw['w_uq'], QK_HEAD).astype(BF16)
    w['w_uk'] = _pad_heads(lw['w_uk'], QK_NOPE).astype(BF16)
    w['w_uv'] = _pad_heads(lw['w_uv'], V_HEAD).astype(BF16)
    lane = jnp.arange(N_HEADS * HEAD_PAD)
    w['e_kr'] = ((lane[None, :] % HEAD_PAD) == (QK_NOPE + jnp.arange(QK_ROPE))[:, None]).astype(BF16)
    pad_g = lambda g: row(jnp.pad(g, (0, HEAD_PAD - QK_HEAD)))
    w['g_q'] = pad_g(lw['g_q'])
    w['g_k'] = pad_g(lw['g_k'])
    w['w_uk_perm'] = lw['w_uk'].transpose(0, 2, 1).reshape(KV_LORA, QK_NOPE * N_HEADS).astype(BF16)
    w['g_k_nope_perm'] = row(jnp.repeat(lw['g_k'][:QK_NOPE], N_HEADS))
    w['g_k_rope'] = row(lw['g_k'][QK_NOPE:])
    w['w_uv_flat'] = lw['w_uv'].reshape(KV_LORA, N_HEADS * V_HEAD).astype(BF16)
    bb_re, bb_im, pow_re, pow_im = _ssm_discretize(lw['a_re'].astype(F32), lw['a_im'].astype(F32),
                                                   lw['log_dt'].astype(F32), lw['b_re'].astype(F32),
                                                   lw['b_im'].astype(F32), n_pow)
    w['bb_re'] = _block_diag_in(bb_re)
    w['bb_im'] = _block_diag_in(bb_im)
    w['pow_re'] = pow_re
    w['pow_im'] = pow_im
    w['cc_re'] = _block_diag_out(lw['c_re'].astype(F32))
    w['cc_im'] = _block_diag_out(lw['c_im'].astype(F32))
    w['d_skip'] = row(lw['d_skip'])
    w['w_glu'] = lw['w_glu'].astype(BF16)
    woa = lw['w_oa'].reshape(N_HEADS, V_HEAD, d_model)
    w['w_oa'] = jnp.pad(woa, ((0, 0), (0, HEAD_PAD - V_HEAD), (0, 0))).reshape(N_HEADS * HEAD_PAD, d_model).astype(BF16)
    w['w_oa_flat'] = lw['w_oa'].astype(BF16)
    w['w_os'] = lw['w_os'].astype(BF16)
    w['w_out'] = lw['w_out'].astype(BF16)
    w['g_ffn'] = row(lw['g_ffn'])
    w['w_up'] = lw['w_up'].astype(BF16)
    w['conv_w'] = lw['conv_w'].astype(F32)
    w['conv_b'] = row(lw['conv_b'])
    w['w_down'] = lw['w_down'].astype(BF16)
    w['g_ple'] = row(lw['g_ple'])
    w['w_ple_gate'] = lw['w_ple_gate'].astype(BF16)
    w['w_ple_proj'] = lw['w_ple_proj'].astype(BF16)
    return w


def kernel(x_prompt, x_sample, p_prompt, p_sample, cache_ckv, cache_kr, page_table, state_ssm_re, state_ssm_im,
           state_conv, g_mix, w_in, g_cq, g_ckv, w_uq, w_uk, w_uv, g_q, g_k, a_re, a_im, log_dt, b_re, b_im,
           c_re, c_im, d_skip, w_glu, w_oa, w_os, w_out, g_ffn, w_up, conv_w, conv_b, w_down, g_ple,
           w_ple_gate, w_ple_proj):
    params = dict(g_mix=g_mix, w_in=w_in, g_cq=g_cq, g_ckv=g_ckv, w_uq=w_uq, w_uk=w_uk, w_uv=w_uv, g_q=g_q,
                  g_k=g_k, a_re=a_re, a_im=a_im, log_dt=log_dt, b_re=b_re, b_im=b_im, c_re=c_re, c_im=c_im,
                  d_skip=d_skip, w_glu=w_glu, w_oa=w_oa, w_os=w_os, w_out=w_out, g_ffn=g_ffn, w_up=w_up,
                  conv_w=conv_w, conv_b=conv_b, w_down=w_down, g_ple=g_ple, w_ple_gate=w_ple_gate,
                  w_ple_proj=w_ple_proj)
    depth = w_in.shape[0]
    bp, seq, d_model = x_prompt.shape
    bd, t_new, _ = x_sample.shape
    past_len = page_table.shape[1] * PAGE_SIZE
    d_ff = w_down.shape[1]
    n_pow = max(min(SSM_ROWS, seq) // SUBLANES, 1)

    rope_prompt = _rope_tables_q(jnp.arange(seq))
    rope_sample = _rope_tables_q(jnp.tile(past_len + jnp.arange(t_new), bd))
    rope_cache = _rope_tables_k(jnp.arange(past_len))

    yp, ys = x_prompt, x_sample
    outs = [[] for _ in range(10)]
    for i in range(depth):
        w = _layer_weights({k: v[i] for k, v in params.items()}, d_model, n_pow)

        q, k, v, ckv, kr, u, sga, sgs = _inproj(yp, w, rope_prompt, BF16)
        att = _flash_prompt(q, k, v)
        yssm, hr, hi = _ssm_prompt(u, w)
        yp, tail = _post(yp, att, yssm, sga, sgs, p_prompt[i], w)
        outs[0].append(ckv); outs[1].append(kr); outs[4].append(hr); outs[5].append(hi)
        outs[8].append(tail[:, SUBLANES - (CONV_W - 1):, :])

        n_tok = bd * t_new
        flat = lambda z: z.reshape(1, n_tok, z.shape[-1])
        q, k, v, ckv, kr, u, sga, sgs = _inproj(flat(ys), w, rope_sample, F32)
        att = _sample_attention(q.reshape(bd, t_new, -1), k.reshape(bd, t_new, -1), ckv.reshape(bd, t_new, -1),
                                cache_ckv, cache_kr, page_table, i, w, rope_cache)
        yssm, hr, hi = _ssm_sample(u.reshape(bd, t_new, -1), state_ssm_re[i], state_ssm_im[i], w)
        buf = state_conv[i].astype(F32)
        zrow = jnp.zeros((bd, 1, 2 * d_ff), F32)
        prev1 = jnp.concatenate([buf[:, 1:2], jnp.tile(zrow, (1, t_new - 1, 1))], axis=1)
        prev2 = jnp.concatenate([buf, jnp.tile(zrow, (1, t_new - 2, 1))], axis=1)
        w_s = dict(w, w_oa=w['w_oa_flat'])
        y2, up = _post(flat(ys), flat(att).astype(BF16), flat(yssm), sga, sgs, flat(p_sample[i]), w_s,
                       prev=(flat(prev1), flat(prev2), t_new))
        ys = y2.reshape(bd, t_new, d_model)
        outs[2].append(ckv.reshape(bd, t_new, -1)); outs[3].append(kr.reshape(bd, t_new, -1))
        outs[6].append(hr); outs[7].append(hi)
        outs[9].append(up.reshape(bd, t_new, 2 * d_ff)[:, t_new - (CONV_W - 1):, :])

    st = [jnp.stack(o) for o in outs]
    return (yp, ys, st[0], st[1], st[2], st[3], st[4], st[5], st[6], st[7], st[8], st[9])
```

```python
import functools
import math

import jax
import jax.numpy as jnp
from jax import lax
from jax.experimental import pallas as pl
from jax.experimental.pallas import tpu as pltpu

F32 = jnp.float32
BF16 = jnp.bfloat16

N_HEADS = 8
QK_NOPE = 64
QK_ROPE = 32
QK_HEAD = QK_NOPE + QK_ROPE
V_HEAD = 64
Q_LORA = 384
KV_LORA = 256
ROPE_THETA = 10000.0
SCALE = QK_HEAD ** -0.5
NEG_INF = -1e30
SSM_WIDTH = 512
GROUP = 16
N_GROUPS = SSM_WIDTH // GROUP
STATE = 64
N_STATE = N_GROUPS * STATE
CONV_W = 3
EPS = 1e-6
PAGE_SIZE = 128

LANES = 128
SUBLANES = 8
HEAD_PAD = LANES
NEW_PAD = 2 * SUBLANES
VMEM_LIMIT_BYTES = 56 * 1024 * 1024

INPROJ_ROWS = 512
FLASH_TQ = 512
FLASH_TK = 512
FLASH_HEADS = 2
SSM_ROWS = 256
SSM_COLS = 512
POST_ROWS = 256
POST_ROWS_SAMPLE = 64
FFN_CHUNK = 256
SAMPLE_CHUNK_PAGES = 8
SSM_GROUP_BLOCK = 8


def _cparams(sem):
    return pltpu.CompilerParams(dimension_semantics=sem, vmem_limit_bytes=VMEM_LIMIT_BYTES)


def _const_spec(shape):
    nd = len(shape)
    return pl.BlockSpec(shape, lambda *_: (0,) * nd, pipeline_mode=pl.Buffered(1))


def _rms(x, g):
    return x * lax.rsqrt(jnp.mean(x * x, axis=-1, keepdims=True) + EPS) * g


def _bdot(a, b):
    return jnp.dot(a.astype(BF16), b, preferred_element_type=F32)


def _dot_nt(a, b):
    return lax.dot_general(a, b, (((1,), (1,)), ((), ())), preferred_element_type=F32)


def _ssm_disc_kernel(are_ref, aim_ref, ldt_ref, arer_ref, aimr_ref, ldtr_ref, bre_ref, bim_ref,
                     bbre_ref, bbim_ref, pre_ref, pim_ref, *, n_pow):
    def zoh(a_re, a_im, ldt):
        dt = jnp.exp(ldt)
        mag = jnp.exp(dt * a_re)
        ab_re = mag * jnp.cos(dt * a_im)
        ab_im = mag * jnp.sin(dt * a_im)
        return ab_re, ab_im

    a_re = arer_ref[...]
    a_im = aimr_ref[...]
    ab_re, ab_im = zoh(a_re, a_im, ldtr_ref[...])
    den = a_re * a_re + a_im * a_im
    nr = ab_re - 1.0
    f_re = (nr * a_re + ab_im * a_im) / den
    f_im = (ab_im * a_re - nr * a_im) / den
    b_re = bre_ref[...]
    b_im = bim_ref[...]
    bbre_ref[...] = f_re * b_re - f_im * b_im
    bbim_ref[...] = f_re * b_im + f_im * b_re

    p_re, p_im = zoh(are_ref[...], aim_ref[...], ldt_ref[...])
    c_re, c_im = p_re, p_im
    for j in range(n_pow):
        pre_ref[j] = c_re
        pim_ref[j] = c_im
        c_re, c_im = c_re * p_re - c_im * p_im, c_re * p_im + c_im * p_re


def _ssm_discretize(a_re, a_im, log_dt, b_re, b_im, n_pow):
    g, p = a_re.shape
    rows = g * GROUP
    rep = lambda z: jnp.repeat(z, GROUP, axis=0)
    bt = lambda z: z.transpose(0, 2, 1).reshape(rows, p)
    ldt = log_dt.reshape(g, 1)
    out_shape = (jax.ShapeDtypeStruct((rows, p), F32), jax.ShapeDtypeStruct((rows, p), F32),
                 jax.ShapeDtypeStruct((n_pow, g, p), F32), jax.ShapeDtypeStruct((n_pow, g, p), F32))
    bb_re, bb_im, pow_re, pow_im = pl.pallas_call(
        functools.partial(_ssm_disc_kernel, n_pow=n_pow),
        out_shape=out_shape, name="ssm_disc",
    )(a_re, a_im, ldt, rep(a_re), rep(a_im), rep(ldt), bt(b_re), bt(b_im))
    return bb_re, bb_im, pow_re.reshape(n_pow, g * p), pow_im.reshape(n_pow, g * p)


def _block_diag_in(bb):
    nb = N_GROUPS // SSM_GROUP_BLOCK
    z = bb.reshape(nb, SSM_GROUP_BLOCK, GROUP, STATE)
    eye = jnp.eye(SSM_GROUP_BLOCK, dtype=bb.dtype)
    out = jnp.einsum('kgip,gh->kgihp', z, eye)
    return out.reshape(nb, SSM_GROUP_BLOCK * GROUP, SSM_GROUP_BLOCK * STATE).astype(BF16)


def _block_diag_out(c):
    nb = N_GROUPS // SSM_GROUP_BLOCK
    z = c.reshape(nb, SSM_GROUP_BLOCK, GROUP, STATE)
    eye = jnp.eye(SSM_GROUP_BLOCK, dtype=c.dtype)
    out = jnp.einsum('kgip,gh->kgphi', z, eye)
    return out.reshape(nb, SSM_GROUP_BLOCK * STATE, SSM_GROUP_BLOCK * GROUP).astype(BF16)


OFF_CQ = 0
OFF_CKV = OFF_CQ + Q_LORA
OFF_U = OFF_CKV + KV_LORA


def _head_norm_rope(z, g_ref, rc, rsa, rsb, out_ref, scale):
    inv = 1.0 / QK_HEAD
    for h in range(N_HEADS):
        blk = z[:, h * HEAD_PAD:(h + 1) * HEAD_PAD]
        ss = jnp.sum(blk * blk, axis=-1, keepdims=True)
        blk = blk * lax.rsqrt(ss * inv + EPS) * g_ref[...]
        blk = blk * rc + pltpu.roll(blk, HEAD_PAD - QK_ROPE // 2, 1) * rsa + pltpu.roll(blk, QK_ROPE // 2, 1) * rsb
        if scale != 1.0:
            blk = blk * scale
        out_ref[:, h * HEAD_PAD:(h + 1) * HEAD_PAD] = blk.astype(out_ref.dtype)


def _inproj_kernel(x_ref, gmix_ref, win_ref, gcq_ref, gckv_ref, wuq_ref, wuk_ref, ekr_ref, wuv_ref,
                   gq_ref, gk_ref, rc_ref, rsa_ref, rsb_ref,
                   q_ref, k_ref, v_ref, ckv_ref, kr_ref, u_ref, sga_ref, sgs_ref, *, d_model):
    off_ga = OFF_U + SSM_WIDTH
    off_gs = off_ga + d_model
    off_kr = off_gs + d_model
    xn = _rms(x_ref[...], gmix_ref[...])
    proj = _bdot(xn, win_ref[...])
    cq = _rms(proj[:, OFF_CQ:OFF_CKV], gcq_ref[...])
    ckv = _rms(proj[:, OFF_CKV:OFF_U], gckv_ref[...])
    kr = proj[:, off_kr:off_kr + QK_ROPE]
    u_ref[...] = proj[:, OFF_U:off_ga]
    sga_ref[...] = jax.nn.sigmoid(proj[:, off_ga:off_gs]).astype(sga_ref.dtype)
    sgs_ref[...] = jax.nn.sigmoid(proj[:, off_gs:off_kr]).astype(sgs_ref.dtype)
    ckv_ref[...] = ckv
    kr_ref[...] = kr
    ckv_b = ckv.astype(BF16)
    v_ref[...] = jnp.dot(ckv_b, wuv_ref[...], preferred_element_type=F32).astype(v_ref.dtype)
    rc = rc_ref[...]
    rsa = rsa_ref[...]
    rsb = rsb_ref[...]
    qf = _bdot(cq, wuq_ref[...])
    _head_norm_rope(qf, gq_ref, rc, rsa, rsb, q_ref, SCALE)
    kf = jnp.dot(ckv_b, wuk_ref[...], preferred_element_type=F32) + _bdot(kr, ekr_ref[...])
    _head_norm_rope(kf, gk_ref, rc, rsa, rsb, k_ref, 1.0)


def _inproj(x, w, rope, act_dtype):
    nb, rows, d = x.shape
    tr = min(INPROJ_ROWS, rows)
    assert rows % tr == 0
    hp = N_HEADS * HEAD_PAD
    row_spec = lambda n: pl.BlockSpec((None, tr, n), lambda b, t: (b, t, 0))
    tab_spec = pl.BlockSpec((tr, HEAD_PAD), lambda b, t: (t, 0))
    weights = (w['g_mix'], w['w_in'], w['g_cq'], w['g_ckv'], w['w_uq'], w['w_uk'], w['e_kr'],
               w['w_uv'], w['g_q'], w['g_k'])
    out_shape = (
        jax.ShapeDtypeStruct((nb, rows, hp), act_dtype),
        jax.ShapeDtypeStruct((nb, rows, hp), act_dtype),
        jax.ShapeDtypeStruct((nb, rows, hp), act_dtype),
        jax.ShapeDtypeStruct((nb, rows, KV_LORA), F32),
        jax.ShapeDtypeStruct((nb, rows, QK_ROPE), F32),
        jax.ShapeDtypeStruct((nb, rows, SSM_WIDTH), F32),
        jax.ShapeDtypeStruct((nb, rows, d), BF16),
        jax.ShapeDtypeStruct((nb, rows, d), BF16),
    )
    return pl.pallas_call(
        functools.partial(_inproj_kernel, d_model=d),
        grid=(nb, rows // tr),
        in_specs=[row_spec(d)] + [_const_spec(z.shape) for z in weights] + [tab_spec] * 3,
        out_specs=tuple(row_spec(s.shape[-1]) for s in out_shape),
        out_shape=out_shape,
        compiler_params=_cparams(("parallel", "parallel")),
        name="inproj",
    )(x, *weights, *rope)


def _flash_kernel(q_ref, k_ref, v_ref, o_ref, *, tq, tk, nh):
    qi = pl.program_id(2)
    n_diag = tq // tk
    heads = [slice(h * HEAD_PAD, (h + 1) * HEAD_PAD) for h in range(nh)]
    qs = [q_ref[:, hs] for hs in heads]

    def step(kj, carry, masked):
        start = pl.multiple_of(kj * tk, tk)
        if masked:
            row = lax.broadcasted_iota(jnp.int32, (tq, tk), 0) + qi * tq
            col = lax.broadcasted_iota(jnp.int32, (tq, tk), 1) + kj * tk
            keep = col <= row
        out = []
        for h, hs in enumerate(heads):
            m, l, acc = carry[h]
            ks = k_ref[pl.ds(start, tk), hs]
            vs = v_ref[pl.ds(start, tk), hs]
            s = _dot_nt(qs[h], ks)
            if masked:
                s = jnp.where(keep, s, NEG_INF)
            m_new = jnp.maximum(m, jnp.max(s, axis=-1, keepdims=True))
            p = jnp.exp(s - m_new)
            corr = jnp.exp(m - m_new)
            l = l * corr + jnp.sum(p, axis=-1, keepdims=True)
            acc = acc * corr + jnp.dot(p.astype(BF16), vs, preferred_element_type=F32)
            out.append((m_new, l, acc))
        return tuple(out)

    init = tuple((jnp.full((tq, 1), NEG_INF, F32), jnp.zeros((tq, 1), F32), jnp.zeros((tq, HEAD_PAD), F32))
                 for _ in heads)
    carry = lax.fori_loop(0, qi * n_diag, lambda kj, c: step(kj, c, False), init)
    for d in range(n_diag):
        carry = step(qi * n_diag + d, carry, True)
    for h, hs in enumerate(heads):
        _, l, acc = carry[h]
        o_ref[:, hs] = (acc / l).astype(o_ref.dtype)


def _flash_prompt(q, k, v):
    b, s, hp = q.shape
    tq = min(FLASH_TQ, s)
    tk = min(FLASH_TK, tq)
    nh = FLASH_HEADS
    assert s % tq == 0 and tq % tk == 0 and N_HEADS % nh == 0
    return pl.pallas_call(
        functools.partial(_flash_kernel, tq=tq, tk=tk, nh=nh),
        grid=(b, N_HEADS // nh, s // tq),
        in_specs=[pl.BlockSpec((None, tq, nh * HEAD_PAD), lambda bi, h, qi: (bi, qi, h)),
                  pl.BlockSpec((None, s, nh * HEAD_PAD), lambda bi, h, qi: (bi, 0, h)),
                  pl.BlockSpec((None, s, nh * HEAD_PAD), lambda bi, h, qi: (bi, 0, h))],
        out_specs=pl.BlockSpec((None, tq, nh * HEAD_PAD), lambda bi, h, qi: (bi, qi, h)),
        out_shape=jax.ShapeDtypeStruct((b, s, hp), BF16),
        compiler_params=_cparams(("parallel", "parallel", "arbitrary")),
        name="flash_prompt",
    )(q, k, v)


def _sample_attn_kernel(pt_ref, qn_ref, qr_ref, qrs_ref, qf_ref, knew_ref, cnew_ref, ca_ref, cb_ref,
                        wukp_ref, gkn_ref, gkr_ref, wuv_ref, hmask_ref, ckv_hbm, kr_hbm, o_ref,
                        cf, rf, cbuf, s_all, sem_c, sem_r, *, layer, n_pages, n_new, chunk_pages):
    b = pl.program_id(0)
    slot = b % 2
    nq = n_new * N_HEADS
    keys = chunk_pages * PAGE_SIZE
    n_keys = n_pages * PAGE_SIZE

    def page_copies(entry, sl, j):
        pg = pt_ref[entry * n_pages + j]
        return (pltpu.make_async_copy(ckv_hbm.at[layer, pg], cf.at[sl, j], sem_c.at[sl]),
                pltpu.make_async_copy(kr_hbm.at[layer, pg], rf.at[sl, j], sem_r.at[sl]))

    def start_entry(entry, sl):
        def body(j, carry):
            for cp in page_copies(entry, sl, j):
                cp.start()
            return carry
        lax.fori_loop(0, n_pages, body, 0)

    @pl.when(b == 0)
    def _():
        start_entry(0, 0)

    @pl.when(b + 1 < pl.num_programs(0))
    def _():
        start_entry(b + 1, 1 - slot)

    def wait_body(j, carry):
        for cp in page_copies(b, slot, j):
            cp.wait()
        return carry
    lax.fori_loop(0, n_pages, wait_body, 0)

    qn = (qn_ref[...] * gkn_ref[...]).astype(BF16)
    qabs = _dot_nt(qn, wukp_ref[...]).astype(BF16)

    s_t = _dot_nt(knew_ref[...].astype(BF16), qf_ref[...].astype(BF16))
    s_new = jnp.transpose(jnp.concatenate([s_t] * (LANES // NEW_PAD), axis=0))[:nq, :]
    tok = lax.broadcasted_iota(jnp.int32, (nq, LANES), 0) // N_HEADS
    key = lax.broadcasted_iota(jnp.int32, (nq, LANES), 1)
    s_all[:, n_keys:n_keys + LANES] = jnp.where(key <= tok, s_new, NEG_INF)
    cbuf[n_keys:n_keys + NEW_PAD, :] = cnew_ref[...].astype(BF16)
    cbuf[n_keys + NEW_PAD:n_keys + LANES, :] = jnp.zeros((LANES - NEW_PAD, KV_LORA), BF16)

    qr = qr_ref[...].astype(BF16)
    qrs = qrs_ref[...].astype(BF16)
    lane = lax.broadcasted_iota(jnp.int32, (keys, LANES), 1)
    for ch in range(n_pages // chunk_pages):
        pages = range(ch * chunk_pages, (ch + 1) * chunk_pages)
        span = slice(ch * keys, (ch + 1) * keys)
        for i in pages:
            cbuf[i * PAGE_SIZE:(i + 1) * PAGE_SIZE, :] = cf[slot, i].astype(BF16)
        c = cbuf[span, :]
        kn = jnp.dot(c, wukp_ref[...], preferred_element_type=F32)
        sq = kn * kn
        ss = sq[:, 0:LANES]
        for j in range(1, N_HEADS * QK_NOPE // LANES):
            ss = ss + sq[:, j * LANES:(j + 1) * LANES]
        shift = N_HEADS
        while shift < LANES:
            ss = ss + pltpu.roll(ss, shift, 1)
            shift *= 2
        s_t = _dot_nt(c, qabs)
        both = jnp.transpose(jnp.where(lane < nq, s_t, ss))
        s_nope = both[0:nq]
        ss_t = both[nq:2 * nq]
        r_t = jnp.concatenate([rf[slot, i] for i in pages], axis=1)
        r2 = jnp.sum(r_t * r_t, axis=0, keepdims=True)
        rn = lax.rsqrt((ss_t + r2) * (1.0 / QK_HEAD) + EPS)
        rg = r_t * gkr_ref[...]
        s_rope = (jnp.dot(qr, (rg * ca_ref[:, span]).astype(BF16), preferred_element_type=F32)
                  + jnp.dot(qrs, (rg * cb_ref[:, span]).astype(BF16), preferred_element_type=F32))
        s_all[:, span] = rn * (s_nope + s_rope)

    s = s_all[...]
    p = jnp.exp(s - jnp.max(s, axis=-1, keepdims=True))
    l = jnp.sum(p, axis=-1, keepdims=True)
    acc = jnp.dot(p.astype(BF16), cbuf[...], preferred_element_type=F32)
    o_lat = acc / l
    full = _bdot(o_lat, wuv_ref[...]) * hmask_ref[...]
    o_ref[...] = jnp.sum(full.reshape(n_new, N_HEADS, N_HEADS * V_HEAD), axis=1)


def _sample_attention(q, k, ckv, cache_ckv, cache_kr, page_table, layer, w, rope_k):
    bd, t, hp = q.shape
    n_logical = page_table.shape[1]
    chunk_pages = min(SAMPLE_CHUNK_PAGES, n_logical)
    assert n_logical % chunk_pages == 0 and t <= NEW_PAD
    nq = t * N_HEADS
    n_keys = n_logical * PAGE_SIZE
    head_mask = ((jnp.arange(nq) % N_HEADS)[:, None] == (jnp.arange(N_HEADS * V_HEAD) // V_HEAD)[None, :]).astype(F32)
    qh = q.reshape(bd, t, N_HEADS, HEAD_PAD)
    eye = jnp.eye(N_HEADS, dtype=F32)

    def pad_rows(z):
        return jnp.pad(z, ((0, 0), (0, LANES - nq), (0, 0)))

    qn = pad_rows(jnp.einsum('bthd,hk->bthdk', qh[..., :QK_NOPE], eye).reshape(bd, nq, QK_NOPE * N_HEADS))
    qr = qh[..., QK_NOPE:QK_HEAD].reshape(bd, nq, QK_ROPE)
    qrs = jnp.concatenate([qr[..., QK_ROPE // 2:], qr[..., :QK_ROPE // 2]], axis=-1)
    qf = pad_rows(jnp.einsum('bthl,hk->bthkl', qh, eye).reshape(bd, nq, hp))
    pad_t = lambda z: jnp.pad(z, ((0, 0), (0, NEW_PAD - t), (0, 0)))
    knew = pad_t(k)
    cnew = pad_t(ckv)
    ca, cb = rope_k
    cache_kr_t = jnp.swapaxes(cache_kr, 2, 3)

    def bspec(rows, n):
        return pl.BlockSpec((None, rows, n), lambda b, pt: (b, 0, 0))

    def wspec(shape):
        nd = len(shape)
        return pl.BlockSpec(shape, lambda b, pt: (0,) * nd)

    weights = (ca, cb, w['w_uk_perm'], w['g_k_nope_perm'], w['g_k_rope_col'], w['w_uv_flat'], head_mask)
    in_specs = ([bspec(LANES, QK_NOPE * N_HEADS), bspec(nq, QK_ROPE), bspec(nq, QK_ROPE), bspec(LANES, hp),
                 bspec(NEW_PAD, hp), bspec(NEW_PAD, KV_LORA)]
                + [wspec(z.shape) for z in weights]
                + [pl.BlockSpec(memory_space=pl.ANY), pl.BlockSpec(memory_space=pl.ANY)])
    grid_spec = pltpu.PrefetchScalarGridSpec(
        num_scalar_prefetch=1, grid=(bd,), in_specs=in_specs,
        out_specs=pl.BlockSpec((None, t, N_HEADS * V_HEAD), lambda b, pt: (b, 0, 0)),
        scratch_shapes=[pltpu.VMEM((2, n_logical, PAGE_SIZE, KV_LORA), F32),
                        pltpu.VMEM((2, n_logical, QK_ROPE, PAGE_SIZE), F32),
                        pltpu.VMEM((n_keys + LANES, KV_LORA), BF16),
                        pltpu.VMEM((nq, n_keys + LANES), F32),
                        pltpu.SemaphoreType.DMA((2,)), pltpu.SemaphoreType.DMA((2,))])
    return pl.pallas_call(
        functools.partial(_sample_attn_kernel, layer=layer, n_pages=n_logical, n_new=t, chunk_pages=chunk_pages),
        grid_spec=grid_spec,
        out_shape=jax.ShapeDtypeStruct((bd, t, N_HEADS * V_HEAD), F32),
        compiler_params=_cparams(("arbitrary",)),
        name="sample_attn",
    )(page_table.reshape(-1), qn, qr, qrs, qf, knew, cnew, *weights, cache_ckv, cache_kr_t)


def _ssm_in(u_b, bre_ref, bim_ref, store_re, store_im):
    nb = bre_ref.shape[0]
    kw = bre_ref.shape[1]
    nw = bre_ref.shape[2]
    for kb in range(nb):
        blk = u_b[:, kb * kw:(kb + 1) * kw]
        store_re(kb * nw, nw, jnp.dot(blk, bre_ref[kb], preferred_element_type=F32))
        store_im(kb * nw, nw, jnp.dot(blk, bim_ref[kb], preferred_element_type=F32))


def _ssm_out(h_re, h_im, cre_ref, cim_ref):
    nb = cre_ref.shape[0]
    kw = cre_ref.shape[1]
    parts = []
    for kb in range(nb):
        parts.append(jnp.dot(h_re[:, kb * kw:(kb + 1) * kw].astype(BF16), cre_ref[kb], preferred_element_type=F32)
                     - jnp.dot(h_im[:, kb * kw:(kb + 1) * kw].astype(BF16), cim_ref[kb], preferred_element_type=F32))
    return jnp.concatenate(parts, axis=-1)


def _glu(y, wglu_ref):
    g = _bdot(jax.nn.gelu(y), wglu_ref[...])
    half = g.shape[-1] // 2
    return g[:, :half] * jax.nn.sigmoid(g[:, half:])


def _ssm_prompt_kernel(u_ref, perm_ref, permt_ref, bre_ref, bim_ref, pre_ref, pim_ref, cre_ref, cim_ref,
                       dskip_ref, wglu_ref, ys_ref, hre_out, him_out,
                       hre, him, car_re, car_im, hin_re, hin_im, *, rows, sub):
    ti = pl.program_id(1)

    @pl.when(ti == 0)
    def _():
        car_re[...] = jnp.zeros(car_re.shape, F32)
        car_im[...] = jnp.zeros(car_im.shape, F32)

    u = u_ref[...]
    u_hi = u.astype(BF16)
    u_lo = (u - u_hi.astype(F32)).astype(BF16)
    perm = perm_ref[...]
    up = jnp.dot(perm, u_hi, preferred_element_type=F32) + jnp.dot(perm, u_lo, preferred_element_type=F32)

    def st_re(c0, n, val):
        hre[:, c0:c0 + n] = val

    def st_im(c0, n, val):
        him[:, c0:c0 + n] = val

    _ssm_in(up.astype(BF16), bre_ref, bim_ref, st_re, st_im)

    n_cb = N_STATE // SSM_COLS
    for cb in range(n_cb):
        cols = slice(cb * SSM_COLS, (cb + 1) * SSM_COLS)
        a_re = jnp.broadcast_to(pre_ref[0:1, cols], (SUBLANES, SSM_COLS))
        a_im = jnp.broadcast_to(pim_ref[0:1, cols], (SUBLANES, SSM_COLS))

        def local(j, st):
            s_re, s_im = st
            r0 = pl.multiple_of(j * SUBLANES, SUBLANES)
            n_re = a_re * s_re - a_im * s_im + hre[pl.ds(r0, SUBLANES), cols]
            n_im = a_re * s_im + a_im * s_re + him[pl.ds(r0, SUBLANES), cols]
            hre[pl.ds(r0, SUBLANES), cols] = n_re
            him[pl.ds(r0, SUBLANES), cols] = n_im
            return n_re, n_im

        zero = jnp.zeros((SUBLANES, SSM_COLS), F32)
        e_re, e_im = lax.fori_loop(0, sub, local, (zero, zero))

        as_re = pre_ref[sub - 1:sub, cols]
        as_im = pim_ref[sub - 1:sub, cols]
        c_re = car_re[:, cols]
        c_im = car_im[:, cols]
        for s in range(SUBLANES):
            hin_re[s:s + 1, cols] = c_re
            hin_im[s:s + 1, cols] = c_im
            c_re, c_im = (as_re * c_re - as_im * c_im + e_re[s:s + 1],
                          as_re * c_im + as_im * c_re + e_im[s:s + 1])
        car_re[:, cols] = c_re
        car_im[:, cols] = c_im
        g_re = hin_re[:, cols]
        g_im = hin_im[:, cols]

        def fix(j, _):
            r0 = pl.multiple_of(j * SUBLANES, SUBLANES)
            p_re = jnp.broadcast_to(pre_ref[pl.ds(j, 1), cols], (SUBLANES, SSM_COLS))
            p_im = jnp.broadcast_to(pim_ref[pl.ds(j, 1), cols], (SUBLANES, SSM_COLS))
            hre[pl.ds(r0, SUBLANES), cols] = hre[pl.ds(r0, SUBLANES), cols] + (p_re * g_re - p_im * g_im)
            him[pl.ds(r0, SUBLANES), cols] = him[pl.ds(r0, SUBLANES), cols] + (p_re * g_im + p_im * g_re)
            return 0

        lax.fori_loop(0, sub, fix, 0)

    y = _ssm_out(hre[...], him[...], cre_ref, cim_ref) + dskip_ref[...] * up
    ys = _glu(y, wglu_ref).astype(BF16)
    ys_ref[...] = jnp.dot(permt_ref[...], ys, preferred_element_type=F32).astype(ys_ref.dtype)

    @pl.when(ti == pl.num_programs(1) - 1)
    def _():
        hre_out[...] = car_re[...]
        him_out[...] = car_im[...]


def _ssm_prompt(u, w):
    b, s, width = u.shape
    rows = min(SSM_ROWS, s)
    assert s % rows == 0 and rows % SUBLANES == 0
    sub = rows // SUBLANES
    r = jnp.arange(rows)
    t_of_r = (r % SUBLANES) * sub + r // SUBLANES
    perm = (t_of_r[:, None] == jnp.arange(rows)[None, :]).astype(BF16)
    pre = w['pow_re'][:sub]
    pim = w['pow_im'][:sub]
    weights = (perm, perm.T, w['bb_re'], w['bb_im'], pre, pim, w['cc_re'], w['cc_im'], w['d_skip'], w['w_glu'])
    ys, h_re, h_im = pl.pallas_call(
        functools.partial(_ssm_prompt_kernel, rows=rows, sub=sub),
        grid=(b, s // rows),
        in_specs=[pl.BlockSpec((None, rows, width), lambda bi, t: (bi, t, 0))]
                 + [_const_spec(z.shape) for z in weights],
        out_specs=(pl.BlockSpec((None, rows, width), lambda bi, t: (bi, t, 0)),
                   pl.BlockSpec((None, 1, N_STATE), lambda bi, t: (bi, 0, 0)),
                   pl.BlockSpec((None, 1, N_STATE), lambda bi, t: (bi, 0, 0))),
        out_shape=(jax.ShapeDtypeStruct((b, s, width), BF16),
                   jax.ShapeDtypeStruct((b, 1, N_STATE), F32),
                   jax.ShapeDtypeStruct((b, 1, N_STATE), F32)),
        scratch_shapes=[pltpu.VMEM((rows, N_STATE), F32), pltpu.VMEM((rows, N_STATE), F32),
                        pltpu.VMEM((1, N_STATE), F32), pltpu.VMEM((1, N_STATE), F32),
                        pltpu.VMEM((SUBLANES, N_STATE), F32), pltpu.VMEM((SUBLANES, N_STATE), F32)],
        compiler_params=_cparams(("parallel", "arbitrary")),
        name="ssm_prompt",
    )(u, *weights)
    return ys, h_re.reshape(b, N_GROUPS, STATE), h_im.reshape(b, N_GROUPS, STATE)


def _ssm_sample_kernel(u_ref, h0re_ref, h0im_ref, bre_ref, bim_ref, pre_ref, pim_ref, cre_ref, cim_ref,
                       dskip_ref, wglu_ref, ys_ref, hre_out, him_out, bu_re, bu_im, *, steps):
    a_re = pre_ref[0:1, :]
    a_im = pim_ref[0:1, :]
    h_re = h0re_ref[...]
    h_im = h0im_ref[...]

    def st_re(c0, n, val):
        bu_re[:, c0:c0 + n] = val

    def st_im(c0, n, val):
        bu_im[:, c0:c0 + n] = val

    for t in range(steps):
        u = u_ref[t]
        _ssm_in(u.astype(BF16), bre_ref, bim_ref, st_re, st_im)
        h_re, h_im = (a_re * h_re - a_im * h_im + bu_re[...], a_re * h_im + a_im * h_re + bu_im[...])
        y = _ssm_out(h_re, h_im, cre_ref, cim_ref) + dskip_ref[...] * u
        ys_ref[t] = _glu(y, wglu_ref).astype(ys_ref.dtype)
    hre_out[...] = h_re
    him_out[...] = h_im


def _ssm_sample(u, h0_re, h0_im, w):
    bd, t, width = u.shape
    ut = u.transpose(1, 0, 2)
    ys, h_re, h_im = pl.pallas_call(
        functools.partial(_ssm_sample_kernel, steps=t),
        out_shape=(jax.ShapeDtypeStruct((t, bd, width), BF16),
                   jax.ShapeDtypeStruct((bd, N_STATE), F32),
                   jax.ShapeDtypeStruct((bd, N_STATE), F32)),
        scratch_shapes=[pltpu.VMEM((bd, N_STATE), F32), pltpu.VMEM((bd, N_STATE), F32)],
        compiler_params=pltpu.CompilerParams(vmem_limit_bytes=VMEM_LIMIT_BYTES),
        name="ssm_sample",
    )(ut, h0_re.reshape(bd, N_STATE).astype(F32), h0_im.reshape(bd, N_STATE).astype(F32),
      w['bb_re'], w['bb_im'], w['pow_re'][:1], w['pow_im'][:1], w['cc_re'], w['cc_im'], w['d_skip'], w['w_glu'])
    return ys.transpose(1, 0, 2), h_re.reshape(bd, N_GROUPS, STATE), h_im.reshape(bd, N_GROUPS, STATE)


def _post_kernel(*refs, rows, d_ff, seq_len, carry):
    if carry:
        (x_ref, att_ref, ys_ref, sga_ref, sgs_ref, p_ref, woa_ref, wos_ref, wout_ref, gffn_ref, wup_ref,
         cw_ref, cbias_ref, wdown_ref, gple_ref, wpg_ref, wpp_ref, y_ref, tail_ref, ext) = refs
        prev1_ref = prev2_ref = None
    else:
        (x_ref, att_ref, ys_ref, sga_ref, sgs_ref, p_ref, prev1_ref, prev2_ref, woa_ref, wos_ref, wout_ref,
         gffn_ref, wup_ref, cw_ref, cbias_ref, wdown_ref, gple_ref, wpg_ref, wpp_ref, y_ref, up_ref, ext) = refs
    ti = pl.program_id(1)
    pad = SUBLANES

    mixed = (sga_ref[...].astype(F32) * jnp.dot(att_ref[...], woa_ref[...], preferred_element_type=F32)
             + sgs_ref[...].astype(F32) * jnp.dot(ys_ref[...], wos_ref[...], preferred_element_type=F32))
    x1 = x_ref[...] + _bdot(mixed, wout_ref[...])
    xn = _rms(x1, gffn_ref[...]).astype(BF16)

    if carry:
        @pl.when(ti == 0)
        def _():
            ext[0:pad, :] = jnp.zeros((pad, 2 * d_ff), F32)
    else:
        ext[0:pad, :] = jnp.zeros((pad, 2 * d_ff), F32)
        pos = lax.broadcasted_iota(jnp.int32, (rows, 1), 0) % seq_len

    def conv_cols(c0, n):
        cols = slice(c0, c0 + n)
        up = jnp.dot(xn, wup_ref[:, cols], preferred_element_type=F32)
        ext[pad:pad + rows, cols] = up
        s1 = ext[pad - 1:pad - 1 + rows, cols]
        s2 = ext[pad - 2:pad - 2 + rows, cols]
        if not carry:
            up_ref[:, cols] = up
            s1 = jnp.where(pos >= 1, s1, prev1_ref[:, cols])
            s2 = jnp.where(pos >= 2, s2, prev2_ref[:, cols])
        return cbias_ref[:, cols] + s2 * cw_ref[0:1, cols] + s1 * cw_ref[1:2, cols] + up * cw_ref[2:3, cols]

    f = jnp.zeros((rows, x1.shape[-1]), F32)
    for c in range(d_ff // FFN_CHUNK):
        a = conv_cols(c * FFN_CHUNK, FFN_CHUNK)
        v = conv_cols(d_ff + c * FFN_CHUNK, FFN_CHUNK)
        hdn = (jax.nn.gelu(a) * v).astype(BF16)
        f = f + jnp.dot(hdn, wdown_ref[c * FFN_CHUNK:(c + 1) * FFN_CHUNK, :], preferred_element_type=F32)
    x2 = x1 + f

    if carry:
        tail = ext[rows:rows + pad, :]
        ext[0:pad, :] = tail
        tail_ref[...] = tail

    gate = jax.nn.sigmoid(_bdot(_rms(x2, gple_ref[...]), wpg_ref[...]))
    y_ref[...] = x2 + gate * _bdot(p_ref[...], wpp_ref[...])


def _post(x, att, ys, sga, sgs, p, w, prev=None):
    nb, total, d = x.shape
    d_ff = w['w_down'].shape[0]
    assert d_ff % FFN_CHUNK == 0
    carry = prev is None
    rows = min(POST_ROWS if carry else POST_ROWS_SAMPLE, total)
    assert total % rows == 0
    row_spec = lambda n: pl.BlockSpec((None, rows, n), lambda b, t: (b, t, 0))
    acts = [x, att, ys, sga, sgs, p] + ([] if carry else list(prev[:2]))
    weights = (w['w_oa'], w['w_os'], w['w_out'], w['g_ffn'], w['w_up'], w['conv_w'], w['conv_b'], w['w_down'],
               w['g_ple'], w['w_ple_gate'], w['w_ple_proj'])
    if carry:
        seq_len = total
        extra_spec = pl.BlockSpec((None, SUBLANES, 2 * d_ff), lambda b, t: (b, 0, 0))
        extra_shape = jax.ShapeDtypeStruct((nb, SUBLANES, 2 * d_ff), F32)
    else:
        seq_len = prev[2]
        assert rows % seq_len == 0
        extra_spec = row_spec(2 * d_ff)
        extra_shape = jax.ShapeDtypeStruct((nb, total, 2 * d_ff), F32)
    return pl.pallas_call(
        functools.partial(_post_kernel, rows=rows, d_ff=d_ff, seq_len=seq_len, carry=carry),
        grid=(nb, total // rows),
        in_specs=[row_spec(z.shape[-1]) for z in acts] + [_const_spec(z.shape) for z in weights],
        out_specs=(row_spec(d), extra_spec),
        out_shape=(jax.ShapeDtypeStruct((nb, total, d), F32), extra_shape),
        scratch_shapes=[pltpu.VMEM((rows + SUBLANES, 2 * d_ff), F32)],
        compiler_params=_cparams(("parallel", "arbitrary")),
        name="post_prompt" if carry else "post_sample",
    )(*acts, *weights)


def _rope_tables_q(pos):
    inv_freq = jnp.power(ROPE_THETA, -jnp.arange(0, QK_ROPE, 2, dtype=F32) / QK_ROPE)
    ang = pos.astype(F32)[:, None] * inv_freq[None, :]
    cos, sin = jnp.cos(ang), jnp.sin(ang)
    n = pos.shape[0]
    half = QK_ROPE // 2
    z = lambda k: jnp.zeros((n, k), F32)
    rc = jnp.concatenate([jnp.ones((n, QK_NOPE), F32), cos, cos, z(HEAD_PAD - QK_HEAD)], axis=1)
    rsa = jnp.concatenate([z(QK_NOPE), -sin, z(half), z(HEAD_PAD - QK_HEAD)], axis=1)
    rsb = jnp.concatenate([z(QK_NOPE), z(half), sin, z(HEAD_PAD - QK_HEAD)], axis=1)
    return rc, rsa, rsb


def _rope_tables_k(pos):
    inv_freq = jnp.power(ROPE_THETA, -jnp.arange(0, QK_ROPE, 2, dtype=F32) / QK_ROPE)
    ang = pos.astype(F32)[:, None] * inv_freq[None, :]
    cos, sin = jnp.cos(ang), jnp.sin(ang)
    return jnp.concatenate([cos, cos], axis=1).T, jnp.concatenate([sin, -sin], axis=1).T


def _pad_heads(wm, width):
    rows = wm.shape[0]
    return jnp.pad(wm, ((0, 0), (0, 0), (0, HEAD_PAD - width))).reshape(rows, N_HEADS * HEAD_PAD)


def _layer_weights(lw, d_model, n_pow):
    w = {}
    row = lambda z: z.reshape(1, -1).astype(F32)
    off_ckv = Q_LORA
    off_kr = off_ckv + KV_LORA
    off_u = off_kr + QK_ROPE
    off_ga = off_u + SSM_WIDTH
    off_gs = off_ga + d_model
    win = lw['w_in']
    w['w_in'] = jnp.concatenate(
        [win[:, :off_ckv], win[:, off_ckv:off_kr], win[:, off_u:off_ga], win[:, off_ga:off_gs], win[:, off_gs:],
         win[:, off_kr:off_u], jnp.zeros((d_model, LANES - QK_ROPE), win.dtype)], axis=1).astype(BF16)
    w['g_mix'] = row(lw['g_mix'])
    w['g_cq'] = row(lw['g_cq'])
    w['g_ckv'] = row(lw['g_ckv'])
    w['w_uq'] = _pad_heads(lw['w_uq'], QK_HEAD).astype(BF16)
    w['w_uk'] = _pad_heads(lw['w_uk'], QK_NOPE).astype(BF16)
    w['w_uv'] = _pad_heads(lw['w_uv'], V_HEAD).astype(BF16)
    lane = jnp.arange(N_HEADS * HEAD_PAD)
    w['e_kr'] = ((lane[None, :] % HEAD_PAD) == (QK_NOPE + jnp.arange(QK_ROPE))[:, None]).astype(BF16)
    pad_g = lambda g: row(jnp.pad(g, (0, HEAD_PAD - QK_HEAD)))
    w['g_q'] = pad_g(lw['g_q'])
    w['g_k'] = pad_g(lw['g_k'])
    w['w_uk_perm'] = lw['w_uk'].transpose(0, 2, 1).reshape(KV_LORA, QK_NOPE * N_HEADS).astype(BF16)
    w['g_k_nope_perm'] = row(jnp.repeat(lw['g_k'][:QK_NOPE], N_HEADS))
    w['g_k_rope_col'] = lw['g_k'][QK_NOPE:].reshape(QK_ROPE, 1).astype(F32)
    w['w_uv_flat'] = lw['w_uv'].reshape(KV_LORA, N_HEADS * V_HEAD).astype(BF16)
    bb_re, bb_im, pow_re, pow_im = _ssm_discretize(lw['a_re'].astype(F32), lw['a_im'].astype(F32),
                                                   lw['log_dt'].astype(F32), lw['b_re'].astype(F32),
                                                   lw['b_im'].astype(F32), n_pow)
    w['bb_re'] = _block_diag_in(bb_re)
    w['bb_im'] = _block_diag_in(bb_im)
    w['pow_re'] = pow_re
    w['pow_im'] = pow_im
    w['cc_re'] = _block_diag_out(lw['c_re'].astype(F32))
    w['cc_im'] = _block_diag_out(lw['c_im'].astype(F32))
    w['d_skip'] = row(lw['d_skip'])
    w['w_glu'] = lw['w_glu'].astype(BF16)
    woa = lw['w_oa'].reshape(N_HEADS, V_HEAD, d_model)
    w['w_oa'] = jnp.pad(woa, ((0, 0), (0, HEAD_PAD - V_HEAD), (0, 0))).reshape(N_HEADS * HEAD_PAD, d_model).astype(BF16)
    w['w_oa_flat'] = lw['w_oa'].astype(BF16)
    w['w_os'] = lw['w_os'].astype(BF16)
    w['w_out'] = lw['w_out'].astype(BF16)
    w['g_ffn'] = row(lw['g_ffn'])
    w['w_up'] = lw['w_up'].astype(BF16)
    w['conv_w'] = lw['conv_w'].astype(F32)
    w['conv_b'] = row(lw['conv_b'])
    w['w_down'] = lw['w_down'].astype(BF16)
    w['g_ple'] = row(lw['g_ple'])
    w['w_ple_gate'] = lw['w_ple_gate'].astype(BF16)
    w['w_ple_proj'] = lw['w_ple_proj'].astype(BF16)
    return w


def kernel(x_prompt, x_sample, p_prompt, p_sample, cache_ckv, cache_kr, page_table, state_ssm_re, state_ssm_im,
           state_conv, g_mix, w_in, g_cq, g_ckv, w_uq, w_uk, w_uv, g_q, g_k, a_re, a_im, log_dt, b_re, b_im,
           c_re, c_im, d_skip, w_glu, w_oa, w_os, w_out, g_ffn, w_up, conv_w, conv_b, w_down, g_ple,
           w_ple_gate, w_ple_proj):
    params = dict(g_mix=g_mix, w_in=w_in, g_cq=g_cq, g_ckv=g_ckv, w_uq=w_uq, w_uk=w_uk, w_uv=w_uv, g_q=g_q,
                  g_k=g_k, a_re=a_re, a_im=a_im, log_dt=log_dt, b_re=b_re, b_im=b_im, c_re=c_re, c_im=c_im,
                  d_skip=d_skip, w_glu=w_glu, w_oa=w_oa, w_os=w_os, w_out=w_out, g_ffn=g_ffn, w_up=w_up,
                  conv_w=conv_w, conv_b=conv_b, w_down=w_down, g_ple=g_ple, w_ple_gate=w_ple_gate,
                  w_ple_proj=w_ple_proj)
    depth = w_in.shape[0]
    bp, seq, d_model = x_prompt.shape
    bd, t_new, _ = x_sample.shape
    past_len = page_table.shape[1] * PAGE_SIZE
    d_ff = w_down.shape[1]
    n_pow = max(min(SSM_ROWS, seq) // SUBLANES, 1)

    rope_prompt = _rope_tables_q(jnp.arange(seq))
    rope_sample = _rope_tables_q(jnp.tile(past_len + jnp.arange(t_new), bd))
    rope_cache = _rope_tables_k(jnp.arange(past_len))

    yp, ys = x_prompt, x_sample
    outs = [[] for _ in range(10)]
    for i in range(depth):
        w = _layer_weights({k: v[i] for k, v in params.items()}, d_model, n_pow)

        q, k, v, ckv, kr, u, sga, sgs = _inproj(yp, w, rope_prompt, BF16)
        att = _flash_prompt(q, k, v)
        yssm, hr, hi = _ssm_prompt(u, w)
        yp, tail = _post(yp, att, yssm, sga, sgs, p_prompt[i], w)
        outs[0].append(ckv); outs[1].append(kr); outs[4].append(hr); outs[5].append(hi)
        outs[8].append(tail[:, SUBLANES - (CONV_W - 1):, :])

        n_tok = bd * t_new
        flat = lambda z: z.reshape(1, n_tok, z.shape[-1])
        q, k, v, ckv, kr, u, sga, sgs = _inproj(flat(ys), w, rope_sample, F32)
        att = _sample_attention(q.reshape(bd, t_new, -1), k.reshape(bd, t_new, -1), ckv.reshape(bd, t_new, -1),
                                cache_ckv, cache_kr, page_table, i, w, rope_cache)
        yssm, hr, hi = _ssm_sample(u.reshape(bd, t_new, -1), state_ssm_re[i], state_ssm_im[i], w)
        buf = state_conv[i].astype(F32)
        zrow = jnp.zeros((bd, 1, 2 * d_ff), F32)
        prev1 = jnp.concatenate([buf[:, 1:2], jnp.tile(zrow, (1, t_new - 1, 1))], axis=1)
        prev2 = jnp.concatenate([buf, jnp.tile(zrow, (1, t_new - 2, 1))], axis=1)
        w_s = dict(w, w_oa=w['w_oa_flat'])
        y2, up = _post(flat(ys), flat(att).astype(BF16), flat(yssm), sga, sgs, flat(p_sample[i]), w_s,
                       prev=(flat(prev1), flat(prev2), t_new))
        ys = y2.reshape(bd, t_new, d_model)
        outs[2].append(ckv.reshape(bd, t_new, -1)); outs[3].append(kr.reshape(bd, t_new, -1))
        outs[6].append(hr); outs[7].append(hi)
        outs[9].append(up.reshape(bd, t_new, 2 * d_ff)[:, t_new - (CONV_W - 1):, :])

    st = [jnp.stack(o) for o in outs]
    return (yp, ys, st[0], st[1], st[2], st[3], st[4], st[5], st[6], st[7], st[8], st[9])
```

```python
import functools
import math

import jax
import jax.numpy as jnp
from jax import lax
from jax.experimental import pallas as pl
from jax.experimental.pallas import tpu as pltpu

F32 = jnp.float32
BF16 = jnp.bfloat16

N_HEADS = 8
QK_NOPE = 64
QK_ROPE = 32
QK_HEAD = QK_NOPE + QK_ROPE
V_HEAD = 64
Q_LORA = 384
KV_LORA = 256
ROPE_THETA = 10000.0
SCALE = QK_HEAD ** -0.5
NEG_INF = -1e30
SSM_WIDTH = 512
GROUP = 16
N_GROUPS = SSM_WIDTH // GROUP
STATE = 64
N_STATE = N_GROUPS * STATE
CONV_W = 3
EPS = 1e-6
PAGE_SIZE = 128

LANES = 128
SUBLANES = 8
HEAD_PAD = LANES
NEW_PAD = 2 * SUBLANES
VMEM_LIMIT_BYTES = 56 * 1024 * 1024

INPROJ_ROWS = 512
FLASH_T = 512
FLASH_HEADS = 2
SSM_ROWS = 256
SSM_COLS = 512
POST_ROWS = 256
FFN_CHUNK = 256
SAMPLE_CHUNK_PAGES = 8
SSM_GROUP_BLOCK = 8


def _cparams(sem):
    return pltpu.CompilerParams(dimension_semantics=sem, vmem_limit_bytes=VMEM_LIMIT_BYTES)


def _const_spec(shape):
    nd = len(shape)
    return pl.BlockSpec(shape, lambda *_: (0,) * nd, pipeline_mode=pl.Buffered(1))


def _rms(x, g):
    return x * lax.rsqrt(jnp.mean(x * x, axis=-1, keepdims=True) + EPS) * g


def _bdot(a, b):
    return jnp.dot(a.astype(BF16), b, preferred_element_type=F32)


def _dot_nt(a, b):
    return lax.dot_general(a, b, (((1,), (1,)), ((), ())), preferred_element_type=F32)


def _ssm_disc_kernel(are_ref, aim_ref, ldt_ref, arer_ref, aimr_ref, ldtr_ref, bre_ref, bim_ref,
                     bbre_ref, bbim_ref, pre_ref, pim_ref, *, n_pow):
    def zoh(a_re, a_im, ldt):
        dt = jnp.exp(ldt)
        mag = jnp.exp(dt * a_re)
        ab_re = mag * jnp.cos(dt * a_im)
        ab_im = mag * jnp.sin(dt * a_im)
        return ab_re, ab_im

    a_re = arer_ref[...]
    a_im = aimr_ref[...]
    ab_re, ab_im = zoh(a_re, a_im, ldtr_ref[...])
    den = a_re * a_re + a_im * a_im
    nr = ab_re - 1.0
    f_re = (nr * a_re + ab_im * a_im) / den
    f_im = (ab_im * a_re - nr * a_im) / den
    b_re = bre_ref[...]
    b_im = bim_ref[...]
    bbre_ref[...] = f_re * b_re - f_im * b_im
    bbim_ref[...] = f_re * b_im + f_im * b_re

    p_re, p_im = zoh(are_ref[...], aim_ref[...], ldt_ref[...])
    c_re, c_im = p_re, p_im
    for j in range(n_pow):
        pre_ref[j] = c_re
        pim_ref[j] = c_im
        c_re, c_im = c_re * p_re - c_im * p_im, c_re * p_im + c_im * p_re


def _ssm_discretize(a_re, a_im, log_dt, b_re, b_im, n_pow):
    g, p = a_re.shape
    rows = g * GROUP
    rep = lambda z: jnp.repeat(z, GROUP, axis=0)
    bt = lambda z: z.transpose(0, 2, 1).reshape(rows, p)
    ldt = log_dt.reshape(g, 1)
    out_shape = (jax.ShapeDtypeStruct((rows, p), F32), jax.ShapeDtypeStruct((rows, p), F32),
                 jax.ShapeDtypeStruct((n_pow, g, p), F32), jax.ShapeDtypeStruct((n_pow, g, p), F32))
    bb_re, bb_im, pow_re, pow_im = pl.pallas_call(
        functools.partial(_ssm_disc_kernel, n_pow=n_pow),
        out_shape=out_shape, name="ssm_disc",
    )(a_re, a_im, ldt, rep(a_re), rep(a_im), rep(ldt), bt(b_re), bt(b_im))
    return bb_re, bb_im, pow_re.reshape(n_pow, g * p), pow_im.reshape(n_pow, g * p)


def _block_diag_in(bb):
    nb = N_GROUPS // SSM_GROUP_BLOCK
    z = bb.reshape(nb, SSM_GROUP_BLOCK, GROUP, STATE)
    eye = jnp.eye(SSM_GROUP_BLOCK, dtype=bb.dtype)
    out = jnp.einsum('kgip,gh->kgihp', z, eye)
    return out.reshape(nb, SSM_GROUP_BLOCK * GROUP, SSM_GROUP_BLOCK * STATE).astype(BF16)


def _block_diag_out(c):
    nb = N_GROUPS // SSM_GROUP_BLOCK
    z = c.reshape(nb, SSM_GROUP_BLOCK, GROUP, STATE)
    eye = jnp.eye(SSM_GROUP_BLOCK, dtype=c.dtype)
    out = jnp.einsum('kgip,gh->kgphi', z, eye)
    return out.reshape(nb, SSM_GROUP_BLOCK * STATE, SSM_GROUP_BLOCK * GROUP).astype(BF16)


OFF_CQ = 0
OFF_CKV = OFF_CQ + Q_LORA
OFF_U = OFF_CKV + KV_LORA


def _head_norm_rope(z, g_ref, rc, rsa, rsb, out_ref, scale):
    inv = 1.0 / QK_HEAD
    for h in range(N_HEADS):
        blk = z[:, h * HEAD_PAD:(h + 1) * HEAD_PAD]
        ss = jnp.sum(blk * blk, axis=-1, keepdims=True)
        blk = blk * lax.rsqrt(ss * inv + EPS) * g_ref[...]
        blk = blk * rc + pltpu.roll(blk, HEAD_PAD - QK_ROPE // 2, 1) * rsa + pltpu.roll(blk, QK_ROPE // 2, 1) * rsb
        if scale != 1.0:
            blk = blk * scale
        out_ref[:, h * HEAD_PAD:(h + 1) * HEAD_PAD] = blk.astype(out_ref.dtype)


def _inproj_kernel(x_ref, gmix_ref, win_ref, gcq_ref, gckv_ref, wuq_ref, wuk_ref, ekr_ref, wuv_ref,
                   gq_ref, gk_ref, rc_ref, rsa_ref, rsb_ref,
                   q_ref, k_ref, v_ref, ckv_ref, kr_ref, u_ref, sga_ref, sgs_ref, *, d_model):
    off_ga = OFF_U + SSM_WIDTH
    off_gs = off_ga + d_model
    off_kr = off_gs + d_model
    xn = _rms(x_ref[...], gmix_ref[...])
    proj = _bdot(xn, win_ref[...])
    cq = _rms(proj[:, OFF_CQ:OFF_CKV], gcq_ref[...])
    ckv = _rms(proj[:, OFF_CKV:OFF_U], gckv_ref[...])
    kr = proj[:, off_kr:off_kr + QK_ROPE]
    u_ref[...] = proj[:, OFF_U:off_ga]
    sga_ref[...] = jax.nn.sigmoid(proj[:, off_ga:off_gs]).astype(sga_ref.dtype)
    sgs_ref[...] = jax.nn.sigmoid(proj[:, off_gs:off_kr]).astype(sgs_ref.dtype)
    ckv_ref[...] = ckv
    kr_ref[...] = kr
    ckv_b = ckv.astype(BF16)
    v_ref[...] = _dot_nt(wuv_ref[...], ckv_b).astype(v_ref.dtype)
    rc = rc_ref[...]
    rsa = rsa_ref[...]
    rsb = rsb_ref[...]
    qf = _bdot(cq, wuq_ref[...])
    _head_norm_rope(qf, gq_ref, rc, rsa, rsb, q_ref, SCALE)
    kf = jnp.dot(ckv_b, wuk_ref[...], preferred_element_type=F32) + _bdot(kr, ekr_ref[...])
    _head_norm_rope(kf, gk_ref, rc, rsa, rsb, k_ref, 1.0)


def _inproj(x, w, rope, act_dtype):
    nb, rows, d = x.shape
    tr = min(INPROJ_ROWS, rows)
    assert rows % tr == 0
    hp = N_HEADS * HEAD_PAD
    row_spec = lambda n: pl.BlockSpec((None, tr, n), lambda b, t: (b, t, 0))
    tab_spec = pl.BlockSpec((tr, HEAD_PAD), lambda b, t: (t, 0))
    weights = (w['g_mix'], w['w_in'], w['g_cq'], w['g_ckv'], w['w_uq'], w['w_uk'], w['e_kr'],
               w['w_uv_t'], w['g_q'], w['g_k'])
    out_shape = (
        jax.ShapeDtypeStruct((nb, rows, hp), act_dtype),
        jax.ShapeDtypeStruct((nb, rows, hp), act_dtype),
        jax.ShapeDtypeStruct((nb, hp, rows), act_dtype),
        jax.ShapeDtypeStruct((nb, rows, KV_LORA), F32),
        jax.ShapeDtypeStruct((nb, rows, QK_ROPE), F32),
        jax.ShapeDtypeStruct((nb, rows, SSM_WIDTH), F32),
        jax.ShapeDtypeStruct((nb, rows, d), BF16),
        jax.ShapeDtypeStruct((nb, rows, d), BF16),
    )
    out_specs = [row_spec(s.shape[-1]) for s in out_shape]
    out_specs[2] = pl.BlockSpec((None, hp, tr), lambda b, t: (b, 0, t))
    return pl.pallas_call(
        functools.partial(_inproj_kernel, d_model=d),
        grid=(nb, rows // tr),
        in_specs=[row_spec(d)] + [_const_spec(z.shape) for z in weights] + [tab_spec] * 3,
        out_specs=tuple(out_specs),
        out_shape=out_shape,
        compiler_params=_cparams(("parallel", "parallel")),
        name="inproj",
    )(x, *weights, *rope)


def _flash_kernel(q_ref, k_ref, vt_ref, o_ref, s0, s1, m_sc, l_sc, acc_sc, *, t, nh):
    qi = pl.program_id(2)
    heads = [slice(h * HEAD_PAD, (h + 1) * HEAD_PAD) for h in range(nh)]
    m_sc[...] = jnp.full(m_sc.shape, NEG_INF, F32)
    l_sc[...] = jnp.zeros(l_sc.shape, F32)
    acc_sc[...] = jnp.zeros(acc_sc.shape, F32)

    def scores(kj, dst):
        start = pl.multiple_of(kj * t, t)
        for h, hs in enumerate(heads):
            dst[h] = _dot_nt(k_ref[pl.ds(start, t), hs], q_ref[:, hs])

    def fold(kj, src, diagonal):
        start = pl.multiple_of(kj * t, t)
        if diagonal:
            keep = lax.broadcasted_iota(jnp.int32, (t, t), 0) <= lax.broadcasted_iota(jnp.int32, (t, t), 1)
        for h, hs in enumerate(heads):
            st = src[h]
            if diagonal:
                st = jnp.where(keep, st, NEG_INF)
            m = m_sc[h]
            m_new = jnp.maximum(m, jnp.max(st, axis=0, keepdims=True))
            p = jnp.exp(st - m_new)
            corr = jnp.exp(m - m_new)
            l_sc[h] = l_sc[h] * corr + jnp.sum(p, axis=0, keepdims=True)
            acc_sc[h] = acc_sc[h] * corr + jnp.dot(vt_ref[hs, pl.ds(start, t)], p.astype(BF16),
                                                   preferred_element_type=F32)
            m_sc[h] = m_new

    scores(0, s0)

    def pair(kk, carry):
        j = 2 * kk
        scores(j + 1, s1)
        fold(j, s0, False)
        scores(j + 2, s0)
        fold(j + 1, s1, False)
        return carry

    lax.fori_loop(0, qi // 2, pair, 0)

    @pl.when(qi % 2 == 0)
    def _():
        fold(qi, s0, True)

    @pl.when(qi % 2 == 1)
    def _():
        scores(qi, s1)
        fold(qi - 1, s0, False)
        fold(qi, s1, True)

    for h, hs in enumerate(heads):
        o_ref[:, hs] = jnp.transpose(acc_sc[h] / l_sc[h]).astype(o_ref.dtype)


def _flash_prompt(q, k, vt):
    b, s, hp = q.shape
    tq = min(FLASH_T, s)
    nh = FLASH_HEADS
    assert s % tq == 0 and N_HEADS % nh == 0
    return pl.pallas_call(
        functools.partial(_flash_kernel, t=tq, nh=nh),
        grid=(b, N_HEADS // nh, s // tq),
        in_specs=[pl.BlockSpec((None, tq, nh * HEAD_PAD), lambda bi, h, qi: (bi, qi, h)),
                  pl.BlockSpec((None, s, nh * HEAD_PAD), lambda bi, h, qi: (bi, 0, h)),
                  pl.BlockSpec((None, nh * HEAD_PAD, s), lambda bi, h, qi: (bi, h, 0))],
        out_specs=pl.BlockSpec((None, tq, nh * HEAD_PAD), lambda bi, h, qi: (bi, qi, h)),
        out_shape=jax.ShapeDtypeStruct((b, s, hp), BF16),
        scratch_shapes=[pltpu.VMEM((nh, tq, tq), F32), pltpu.VMEM((nh, tq, tq), F32),
                        pltpu.VMEM((nh, 1, tq), F32), pltpu.VMEM((nh, 1, tq), F32),
                        pltpu.VMEM((nh, HEAD_PAD, tq), F32)],
        compiler_params=_cparams(("parallel", "parallel", "arbitrary")),
        name="flash_prompt",
    )(q, k, vt)


def _sample_attn_kernel(pt_ref, qn_ref, qr_ref, qrs_ref, qf_ref, knew_ref, cnew_ref, ca_ref, cb_ref,
                        wukp_ref, gkn_ref, gkr_ref, wuv_ref, hmask_ref, ckv_hbm, kr_hbm, o_ref,
                        cf, rf, cbuf, s_all, sem_c, sem_r, *, layer, n_pages, n_new, chunk_pages):
    b = pl.program_id(0)
    slot = b % 2
    nq = n_new * N_HEADS
    keys = chunk_pages * PAGE_SIZE
    n_keys = n_pages * PAGE_SIZE

    def page_copies(entry, sl, j):
        pg = pt_ref[entry * n_pages + j]
        return (pltpu.make_async_copy(ckv_hbm.at[layer, pg], cf.at[sl, j], sem_c.at[sl]),
                pltpu.make_async_copy(kr_hbm.at[layer, pg], rf.at[sl, j], sem_r.at[sl]))

    def start_entry(entry, sl):
        def body(j, carry):
            for cp in page_copies(entry, sl, j):
                cp.start()
            return carry
        lax.fori_loop(0, n_pages, body, 0, unroll=min(8, n_pages))

    @pl.when(b == 0)
    def _():
        start_entry(0, 0)

    @pl.when(b + 1 < pl.num_programs(0))
    def _():
        start_entry(b + 1, 1 - slot)

    pltpu.make_async_copy(ckv_hbm.at[layer, pl.ds(0, n_pages)], cf.at[slot], sem_c.at[slot]).wait()
    pltpu.make_async_copy(kr_hbm.at[layer, pl.ds(0, n_pages)], rf.at[slot], sem_r.at[slot]).wait()

    qn = (qn_ref[...] * gkn_ref[...]).astype(BF16)
    qabs = _dot_nt(qn, wukp_ref[...]).astype(BF16)

    knew = jnp.concatenate([knew_ref[...].astype(BF16), jnp.zeros((LANES - NEW_PAD, knew_ref.shape[-1]), BF16)],
                           axis=0)
    s_new = _dot_nt(qf_ref[...].astype(BF16), knew)
    tok = lax.broadcasted_iota(jnp.int32, (nq, LANES), 0) // N_HEADS
    key = lax.broadcasted_iota(jnp.int32, (nq, LANES), 1)
    s_all[:, n_keys:n_keys + LANES] = jnp.where(key <= tok, s_new, NEG_INF)
    cbuf[n_keys:n_keys + NEW_PAD, :] = cnew_ref[...].astype(BF16)
    cbuf[n_keys + NEW_PAD:n_keys + LANES, :] = jnp.zeros((LANES - NEW_PAD, KV_LORA), BF16)

    qr = qr_ref[...].astype(BF16)
    qrs = qrs_ref[...].astype(BF16)
    for ch in range(n_pages // chunk_pages):
        pages = range(ch * chunk_pages, (ch + 1) * chunk_pages)
        span = slice(ch * keys, (ch + 1) * keys)
        for i in pages:
            cbuf[i * PAGE_SIZE:(i + 1) * PAGE_SIZE, :] = cf[slot, i].astype(BF16)
        c = cbuf[span, :]
        kn = jnp.dot(c, wukp_ref[...], preferred_element_type=F32)
        sq = kn * kn
        ss = sq[:, 0:LANES]
        for j in range(1, N_HEADS * QK_NOPE // LANES):
            ss = ss + sq[:, j * LANES:(j + 1) * LANES]
        ss_t = jnp.transpose(ss)
        ss_h = ss_t[0:N_HEADS]
        for j in range(1, LANES // N_HEADS):
            ss_h = ss_h + ss_t[j * N_HEADS:(j + 1) * N_HEADS]
        r_t = jnp.concatenate([rf[slot, i] for i in pages], axis=1)
        r2 = jnp.sum(r_t * r_t, axis=0, keepdims=True)
        rn_h = lax.rsqrt((ss_h + r2) * (1.0 / QK_HEAD) + EPS)
        rn = jnp.concatenate([rn_h] * n_new, axis=0)
        rg = r_t * gkr_ref[...]
        s_nope = _dot_nt(qabs, c)
        s_rope = (jnp.dot(qr, (rg * ca_ref[:, span]).astype(BF16), preferred_element_type=F32)
                  + jnp.dot(qrs, (rg * cb_ref[:, span]).astype(BF16), preferred_element_type=F32))
        s_all[:, span] = rn * (s_nope + s_rope)

    s = s_all[...]
    p = jnp.exp(s - jnp.max(s, axis=-1, keepdims=True))
    l = jnp.sum(p, axis=-1, keepdims=True)
    acc = jnp.dot(p.astype(BF16), cbuf[...], preferred_element_type=F32)
    o_lat = acc / l
    full = _bdot(o_lat, wuv_ref[...]) * hmask_ref[...]
    o_ref[...] = jnp.sum(full.reshape(n_new, N_HEADS, N_HEADS * V_HEAD), axis=1)


def _sample_attention(q, k, ckv, cache_ckv, cache_kr, page_table, layer, w, rope_k):
    bd, t, hp = q.shape
    n_logical = page_table.shape[1]
    chunk_pages = min(SAMPLE_CHUNK_PAGES, n_logical)
    assert n_logical % chunk_pages == 0 and t <= NEW_PAD
    nq = t * N_HEADS
    n_keys = n_logical * PAGE_SIZE
    head_mask = ((jnp.arange(nq) % N_HEADS)[:, None] == (jnp.arange(N_HEADS * V_HEAD) // V_HEAD)[None, :]).astype(F32)
    qh = q.reshape(bd, t, N_HEADS, HEAD_PAD)
    eye = jnp.eye(N_HEADS, dtype=F32)

    qn = jnp.einsum('bthd,hk->bthdk', qh[..., :QK_NOPE], eye).reshape(bd, nq, QK_NOPE * N_HEADS)
    qr = qh[..., QK_NOPE:QK_HEAD].reshape(bd, nq, QK_ROPE)
    qrs = jnp.concatenate([qr[..., QK_ROPE // 2:], qr[..., :QK_ROPE // 2]], axis=-1)
    qf = jnp.einsum('bthl,hk->bthkl', qh, eye).reshape(bd, nq, hp)
    pad_t = lambda z: jnp.pad(z, ((0, 0), (0, NEW_PAD - t), (0, 0)))
    knew = pad_t(k)
    cnew = pad_t(ckv)
    ca, cb = rope_k
    cache_kr_t = jnp.swapaxes(cache_kr, 2, 3)

    def bspec(rows, n):
        return pl.BlockSpec((None, rows, n), lambda b, pt: (b, 0, 0))

    def wspec(shape):
        nd = len(shape)
        return pl.BlockSpec(shape, lambda b, pt: (0,) * nd)

    weights = (ca, cb, w['w_uk_perm'], w['g_k_nope_perm'], w['g_k_rope_col'], w['w_uv_flat'], head_mask)
    in_specs = ([bspec(nq, QK_NOPE * N_HEADS), bspec(nq, QK_ROPE), bspec(nq, QK_ROPE), bspec(nq, hp),
                 bspec(NEW_PAD, hp), bspec(NEW_PAD, KV_LORA)]
                + [wspec(z.shape) for z in weights]
                + [pl.BlockSpec(memory_space=pl.ANY), pl.BlockSpec(memory_space=pl.ANY)])
    grid_spec = pltpu.PrefetchScalarGridSpec(
        num_scalar_prefetch=1, grid=(bd,), in_specs=in_specs,
        out_specs=pl.BlockSpec((None, t, N_HEADS * V_HEAD), lambda b, pt: (b, 0, 0)),
        scratch_shapes=[pltpu.VMEM((2, n_logical, PAGE_SIZE, KV_LORA), F32),
                        pltpu.VMEM((2, n_logical, QK_ROPE, PAGE_SIZE), F32),
                        pltpu.VMEM((n_keys + LANES, KV_LORA), BF16),
                        pltpu.VMEM((nq, n_keys + LANES), F32),
                        pltpu.SemaphoreType.DMA((2,)), pltpu.SemaphoreType.DMA((2,))])
    return pl.pallas_call(
        functools.partial(_sample_attn_kernel, layer=layer, n_pages=n_logical, n_new=t, chunk_pages=chunk_pages),
        grid_spec=grid_spec,
        out_shape=jax.ShapeDtypeStruct((bd, t, N_HEADS * V_HEAD), F32),
        compiler_params=_cparams(("arbitrary",)),
        name="sample_attn",
    )(page_table.reshape(-1), qn, qr, qrs, qf, knew, cnew, *weights, cache_ckv, cache_kr_t)


def _ssm_in(u_b, bre_ref, bim_ref, store_re, store_im):
    nb = bre_ref.shape[0]
    kw = bre_ref.shape[1]
    nw = bre_ref.shape[2]
    for kb in range(nb):
        blk = u_b[:, kb * kw:(kb + 1) * kw]
        store_re(kb * nw, nw, jnp.dot(blk, bre_ref[kb], preferred_element_type=F32))
        store_im(kb * nw, nw, jnp.dot(blk, bim_ref[kb], preferred_element_type=F32))


def _ssm_out(h_re, h_im, cre_ref, cim_ref):
    nb = cre_ref.shape[0]
    kw = cre_ref.shape[1]
    parts = []
    for kb in range(nb):
        parts.append(jnp.dot(h_re[:, kb * kw:(kb + 1) * kw].astype(BF16), cre_ref[kb], preferred_element_type=F32)
                     - jnp.dot(h_im[:, kb * kw:(kb + 1) * kw].astype(BF16), cim_ref[kb], preferred_element_type=F32))
    return jnp.concatenate(parts, axis=-1)


def _glu(y, wglu_ref):
    g = _bdot(jax.nn.gelu(y), wglu_ref[...])
    half = g.shape[-1] // 2
    return g[:, :half] * jax.nn.sigmoid(g[:, half:])


def _ssm_prompt_kernel(u_ref, perm_ref, permt_ref, bre_ref, bim_ref, pre_ref, pim_ref, cre_ref, cim_ref,
                       dskip_ref, wglu_ref, ys_ref, hre_out, him_out,
                       hre, him, car_re, car_im, hin_re, hin_im, *, rows, sub):
    ti = pl.program_id(1)

    @pl.when(ti == 0)
    def _():
        car_re[...] = jnp.zeros(car_re.shape, F32)
        car_im[...] = jnp.zeros(car_im.shape, F32)

    u = u_ref[...]
    u_hi = u.astype(BF16)
    u_lo = (u - u_hi.astype(F32)).astype(BF16)
    perm = perm_ref[...]
    up = jnp.dot(perm, u_hi, preferred_element_type=F32) + jnp.dot(perm, u_lo, preferred_element_type=F32)

    def st_re(c0, n, val):
        hre[:, c0:c0 + n] = val

    def st_im(c0, n, val):
        him[:, c0:c0 + n] = val

    _ssm_in(up.astype(BF16), bre_ref, bim_ref, st_re, st_im)

    n_cb = N_STATE // SSM_COLS
    for cb in range(n_cb):
        cols = slice(cb * SSM_COLS, (cb + 1) * SSM_COLS)
        a_re = jnp.broadcast_to(pre_ref[0:1, cols], (SUBLANES, SSM_COLS))
        a_im = jnp.broadcast_to(pim_ref[0:1, cols], (SUBLANES, SSM_COLS))

        def local(j, st):
            s_re, s_im = st
            r0 = pl.multiple_of(j * SUBLANES, SUBLANES)
            n_re = a_re * s_re - a_im * s_im + hre[pl.ds(r0, SUBLANES), cols]
            n_im = a_re * s_im + a_im * s_re + him[pl.ds(r0, SUBLANES), cols]
            hre[pl.ds(r0, SUBLANES), cols] = n_re
            him[pl.ds(r0, SUBLANES), cols] = n_im
            return n_re, n_im

        zero = jnp.zeros((SUBLANES, SSM_COLS), F32)
        e_re, e_im = lax.fori_loop(0, sub, local, (zero, zero))

        as_re = pre_ref[sub - 1:sub, cols]
        as_im = pim_ref[sub - 1:sub, cols]
        c_re = car_re[:, cols]
        c_im = car_im[:, cols]
        for s in range(SUBLANES):
            hin_re[s:s + 1, cols] = c_re
            hin_im[s:s + 1, cols] = c_im
            c_re, c_im = (as_re * c_re - as_im * c_im + e_re[s:s + 1],
                          as_re * c_im + as_im * c_re + e_im[s:s + 1])
        car_re[:, cols] = c_re
        car_im[:, cols] = c_im
        g_re = hin_re[:, cols]
        g_im = hin_im[:, cols]

        def fix(j, _):
            r0 = pl.multiple_of(j * SUBLANES, SUBLANES)
            p_re = jnp.broadcast_to(pre_ref[pl.ds(j, 1), cols], (SUBLANES, SSM_COLS))
            p_im = jnp.broadcast_to(pim_ref[pl.ds(j, 1), cols], (SUBLANES, SSM_COLS))
            hre[pl.ds(r0, SUBLANES), cols] = hre[pl.ds(r0, SUBLANES), cols] + (p_re * g_re - p_im * g_im)
            him[pl.ds(r0, SUBLANES), cols] = him[pl.ds(r0, SUBLANES), cols] + (p_re * g_im + p_im * g_re)
            return 0

        lax.fori_loop(0, sub, fix, 0)

    y = _ssm_out(hre[...], him[...], cre_ref, cim_ref) + dskip_ref[...] * up
    ys = _glu(y, wglu_ref).astype(BF16)
    ys_ref[...] = jnp.dot(permt_ref[...], ys, preferred_element_type=F32).astype(ys_ref.dtype)

    @pl.when(ti == pl.num_programs(1) - 1)
    def _():
        hre_out[...] = car_re[...]
        him_out[...] = car_im[...]


def _ssm_prompt(u, w):
    b, s, width = u.shape
    rows = min(SSM_ROWS, s)
    assert s % rows == 0 and rows % SUBLANES == 0
    sub = rows // SUBLANES
    r = jnp.arange(rows)
    t_of_r = (r % SUBLANES) * sub + r // SUBLANES
    perm = (t_of_r[:, None] == jnp.arange(rows)[None, :]).astype(BF16)
    pre = w['pow_re'][:sub]
    pim = w['pow_im'][:sub]
    weights = (perm, perm.T, w['bb_re'], w['bb_im'], pre, pim, w['cc_re'], w['cc_im'], w['d_skip'], w['w_glu'])
    ys, h_re, h_im = pl.pallas_call(
        functools.partial(_ssm_prompt_kernel, rows=rows, sub=sub),
        grid=(b, s // rows),
        in_specs=[pl.BlockSpec((None, rows, width), lambda bi, t: (bi, t, 0))]
                 + [_const_spec(z.shape) for z in weights],
        out_specs=(pl.BlockSpec((None, rows, width), lambda bi, t: (bi, t, 0)),
                   pl.BlockSpec((None, 1, N_STATE), lambda bi, t: (bi, 0, 0)),
                   pl.BlockSpec((None, 1, N_STATE), lambda bi, t: (bi, 0, 0))),
        out_shape=(jax.ShapeDtypeStruct((b, s, width), BF16),
                   jax.ShapeDtypeStruct((b, 1, N_STATE), F32),
                   jax.ShapeDtypeStruct((b, 1, N_STATE), F32)),
        scratch_shapes=[pltpu.VMEM((rows, N_STATE), F32), pltpu.VMEM((rows, N_STATE), F32),
                        pltpu.VMEM((1, N_STATE), F32), pltpu.VMEM((1, N_STATE), F32),
                        pltpu.VMEM((SUBLANES, N_STATE), F32), pltpu.VMEM((SUBLANES, N_STATE), F32)],
        compiler_params=_cparams(("parallel", "arbitrary")),
        name="ssm_prompt",
    )(u, *weights)
    return ys, h_re.reshape(b, N_GROUPS, STATE), h_im.reshape(b, N_GROUPS, STATE)


def _ssm_sample_kernel(u_ref, h0re_ref, h0im_ref, bre_ref, bim_ref, pre_ref, pim_ref, cre_ref, cim_ref,
                       dskip_ref, wglu_ref, ys_ref, hre_out, him_out, bu_re, bu_im, *, steps):
    a_re = pre_ref[0:1, :]
    a_im = pim_ref[0:1, :]
    h_re = h0re_ref[...]
    h_im = h0im_ref[...]

    def st_re(c0, n, val):
        bu_re[:, c0:c0 + n] = val

    def st_im(c0, n, val):
        bu_im[:, c0:c0 + n] = val

    for t in range(steps):
        u = u_ref[t]
        _ssm_in(u.astype(BF16), bre_ref, bim_ref, st_re, st_im)
        h_re, h_im = (a_re * h_re - a_im * h_im + bu_re[...], a_re * h_im + a_im * h_re + bu_im[...])
        y = _ssm_out(h_re, h_im, cre_ref, cim_ref) + dskip_ref[...] * u
        ys_ref[t] = _glu(y, wglu_ref).astype(ys_ref.dtype)
    hre_out[...] = h_re
    him_out[...] = h_im


def _ssm_sample(ut, h0_re, h0_im, w):
    t, bd, width = ut.shape
    ys, h_re, h_im = pl.pallas_call(
        functools.partial(_ssm_sample_kernel, steps=t),
        out_shape=(jax.ShapeDtypeStruct((t, bd, width), BF16),
                   jax.ShapeDtypeStruct((bd, N_STATE), F32),
                   jax.ShapeDtypeStruct((bd, N_STATE), F32)),
        scratch_shapes=[pltpu.VMEM((bd, N_STATE), F32), pltpu.VMEM((bd, N_STATE), F32)],
        compiler_params=pltpu.CompilerParams(vmem_limit_bytes=VMEM_LIMIT_BYTES),
        name="ssm_sample",
    )(ut, h0_re.reshape(bd, N_STATE).astype(F32), h0_im.reshape(bd, N_STATE).astype(F32),
      w['bb_re'], w['bb_im'], w['pow_re'][:1], w['pow_im'][:1], w['cc_re'], w['cc_im'], w['d_skip'], w['w_glu'])
    return ys, h_re.reshape(bd, N_GROUPS, STATE), h_im.reshape(bd, N_GROUPS, STATE)


def _post_kernel(*refs, rows, d_ff, row_major):
    if row_major:
        (x_ref, att_ref, ys_ref, sga_ref, sgs_ref, p_ref, woa_ref, wos_ref, wout_ref, gffn_ref, wup_ref,
         cw_ref, cbias_ref, wdown_ref, gple_ref, wpg_ref, wpp_ref, y_ref, tail_ref, hist) = refs
        hist0_ref = None
    else:
        (x_ref, att_ref, ys_ref, sga_ref, sgs_ref, p_ref, hist0_ref, woa_ref, wos_ref, wout_ref, gffn_ref, wup_ref,
         cw_ref, cbias_ref, wdown_ref, gple_ref, wpg_ref, wpp_ref, y_ref, tail_ref, hist) = refs
    ti = pl.program_id(1)
    pad = SUBLANES

    mixed = (sga_ref[...].astype(F32) * jnp.dot(att_ref[...], woa_ref[...], preferred_element_type=F32)
             + sgs_ref[...].astype(F32) * jnp.dot(ys_ref[...], wos_ref[...], preferred_element_type=F32))
    x1 = x_ref[...] + _bdot(mixed, wout_ref[...])
    xn = _rms(x1, gffn_ref[...]).astype(BF16)

    @pl.when(ti == 0)
    def _():
        if row_major:
            hist[0:pad, :] = jnp.zeros((pad, 2 * d_ff), F32)
        else:
            hist[...] = hist0_ref[...]

    def conv_cols(c0, n):
        cols = slice(c0, c0 + n)
        up = jnp.dot(xn, wup_ref[:, cols], preferred_element_type=F32)
        if row_major:
            hist[pad:pad + rows, cols] = up
            s1 = hist[pad - 1:pad - 1 + rows, cols]
            s2 = hist[pad - 2:pad - 2 + rows, cols]
        else:
            s2 = hist[0:rows, cols]
            s1 = hist[rows:2 * rows, cols]
            hist[0:rows, cols] = s1
            hist[rows:2 * rows, cols] = up
        return cbias_ref[:, cols] + s2 * cw_ref[0:1, cols] + s1 * cw_ref[1:2, cols] + up * cw_ref[2:3, cols]

    f = jnp.zeros((rows, x1.shape[-1]), F32)
    for c in range(d_ff // FFN_CHUNK):
        a = conv_cols(c * FFN_CHUNK, FFN_CHUNK)
        v = conv_cols(d_ff + c * FFN_CHUNK, FFN_CHUNK)
        hdn = (jax.nn.gelu(a) * v).astype(BF16)
        f = f + jnp.dot(hdn, wdown_ref[c * FFN_CHUNK:(c + 1) * FFN_CHUNK, :], preferred_element_type=F32)
    x2 = x1 + f

    if row_major:
        tail = hist[rows:rows + pad, :]
        hist[0:pad, :] = tail
        tail_ref[...] = tail
    else:
        @pl.when(ti == pl.num_programs(1) - 1)
        def _():
            tail_ref[...] = hist[...]

    gate = jax.nn.sigmoid(_bdot(_rms(x2, gple_ref[...]), wpg_ref[...]))
    y_ref[...] = x2 + gate * _bdot(p_ref[...], wpp_ref[...])


def _post(x, att, ys, sga, sgs, p, w, hist0=None, step_rows=None):
    nb, total, d = x.shape
    d_ff = w['w_down'].shape[0]
    assert d_ff % FFN_CHUNK == 0
    row_major = hist0 is None
    rows = min(POST_ROWS, total) if row_major else step_rows
    assert total % rows == 0
    row_spec = lambda n: pl.BlockSpec((None, rows, n), lambda b, t: (b, t, 0))
    acts = [x, att, ys, sga, sgs, p]
    act_specs = [row_spec(z.shape[-1]) for z in acts]
    weights = (w['w_oa'], w['w_os'], w['w_out'], w['g_ffn'], w['w_up'], w['conv_w'], w['conv_b'], w['w_down'],
               w['g_ple'], w['w_ple_gate'], w['w_ple_proj'])
    if row_major:
        hist_rows = SUBLANES
        tail_spec = pl.BlockSpec((None, hist_rows, 2 * d_ff), lambda b, t: (b, 0, 0))
        tail_shape = jax.ShapeDtypeStruct((nb, hist_rows, 2 * d_ff), F32)
        scratch = pltpu.VMEM((rows + hist_rows, 2 * d_ff), F32)
    else:
        assert nb == 1
        hist_rows = 2 * rows
        acts.append(hist0)
        act_specs.append(_const_spec(hist0.shape))
        tail_spec = pl.BlockSpec((hist_rows, 2 * d_ff), lambda b, t: (0, 0), pipeline_mode=pl.Buffered(1))
        tail_shape = jax.ShapeDtypeStruct((hist_rows, 2 * d_ff), F32)
        scratch = pltpu.VMEM((hist_rows, 2 * d_ff), F32)
    return pl.pallas_call(
        functools.partial(_post_kernel, rows=rows, d_ff=d_ff, row_major=row_major),
        grid=(nb, total // rows),
        in_specs=act_specs + [_const_spec(z.shape) for z in weights],
        out_specs=(row_spec(d), tail_spec),
        out_shape=(jax.ShapeDtypeStruct((nb, total, d), F32), tail_shape),
        scratch_shapes=[scratch],
        compiler_params=_cparams(("parallel", "arbitrary")),
        name="post_prompt" if row_major else "post_sample",
    )(*acts, *weights)


def _rope_tables_q(pos):
    inv_freq = jnp.power(ROPE_THETA, -jnp.arange(0, QK_ROPE, 2, dtype=F32) / QK_ROPE)
    ang = pos.astype(F32)[:, None] * inv_freq[None, :]
    cos, sin = jnp.cos(ang), jnp.sin(ang)
    n = pos.shape[0]
    half = QK_ROPE // 2
    z = lambda k: jnp.zeros((n, k), F32)
    rc = jnp.concatenate([jnp.ones((n, QK_NOPE), F32), cos, cos, z(HEAD_PAD - QK_HEAD)], axis=1)
    rsa = jnp.concatenate([z(QK_NOPE), -sin, z(half), z(HEAD_PAD - QK_HEAD)], axis=1)
    rsb = jnp.concatenate([z(QK_NOPE), z(half), sin, z(HEAD_PAD - QK_HEAD)], axis=1)
    return rc, rsa, rsb


def _rope_tables_k(pos):
    inv_freq = jnp.power(ROPE_THETA, -jnp.arange(0, QK_ROPE, 2, dtype=F32) / QK_ROPE)
    ang = pos.astype(F32)[:, None] * inv_freq[None, :]
    cos, sin = jnp.cos(ang), jnp.sin(ang)
    return jnp.concatenate([cos, cos], axis=1).T, jnp.concatenate([sin, -sin], axis=1).T


def _pad_heads(wm, width):
    rows = wm.shape[0]
    return jnp.pad(wm, ((0, 0), (0, 0), (0, HEAD_PAD - width))).reshape(rows, N_HEADS * HEAD_PAD)


def _layer_weights(lw, d_model, n_pow):
    w = {}
    row = lambda z: z.reshape(1, -1).astype(F32)
    off_ckv = Q_LORA
    off_kr = off_ckv + KV_LORA
    off_u = off_kr + QK_ROPE
    off_ga = off_u + SSM_WIDTH
    off_gs = off_ga + d_model
    win = lw['w_in']
    w['w_in'] = jnp.concatenate(
        [win[:, :off_ckv], win[:, off_ckv:off_kr], win[:, off_u:off_ga], win[:, off_ga:off_gs], win[:, off_gs:],
         win[:, off_kr:off_u], jnp.zeros((d_model, LANES - QK_ROPE), win.dtype)], axis=1).astype(BF16)
    w['g_mix'] = row(lw['g_mix'])
    w['g_cq'] = row(lw['g_cq'])
    w['g_ckv'] = row(lw['g_ckv'])
    w['w_uq'] = _pad_heads(lw['w_uq'], QK_HEAD).astype(BF16)
    w['w_uk'] = _pad_heads(lw['w_uk'], QK_NOPE).astype(BF16)
    w['w_uv_t'] = _pad_heads(lw['w_uv'], V_HEAD).T.astype(BF16)
    lane = jnp.arange(N_HEADS * HEAD_PAD)
    w['e_kr'] = ((lane[None, :] % HEAD_PAD) == (QK_NOPE + jnp.arange(QK_ROPE))[:, None]).astype(BF16)
    pad_g = lambda g: row(jnp.pad(g, (0, HEAD_PAD - QK_HEAD)))
    w['g_q'] = pad_g(lw['g_q'])
    w['g_k'] = pad_g(lw['g_k'])
    w['w_uk_perm'] = lw['w_uk'].transpose(0, 2, 1).reshape(KV_LORA, QK_NOPE * N_HEADS).astype(BF16)
    w['g_k_nope_perm'] = row(jnp.repeat(lw['g_k'][:QK_NOPE], N_HEADS))
    w['g_k_rope_col'] = lw['g_k'][QK_NOPE:].reshape(QK_ROPE, 1).astype(F32)
    w['w_uv_flat'] = lw['w_uv'].reshape(KV_LORA, N_HEADS * V_HEAD).astype(BF16)
    bb_re, bb_im, pow_re, pow_im = _ssm_discretize(lw['a_re'].astype(F32), lw['a_im'].astype(F32),
                                                   lw['log_dt'].astype(F32), lw['b_re'].astype(F32),
                                                   lw['b_im'].astype(F32), n_pow)
    w['bb_re'] = _block_diag_in(bb_re)
    w['bb_im'] = _block_diag_in(bb_im)
    w['pow_re'] = pow_re
    w['pow_im'] = pow_im
    w['cc_re'] = _block_diag_out(lw['c_re'].astype(F32))
    w['cc_im'] = _block_diag_out(lw['c_im'].astype(F32))
    w['d_skip'] = row(lw['d_skip'])
    w['w_glu'] = lw['w_glu'].astype(BF16)
    woa = lw['w_oa'].reshape(N_HEADS, V_HEAD, d_model)
    w['w_oa'] = jnp.pad(woa, ((0, 0), (0, HEAD_PAD - V_HEAD), (0, 0))).reshape(N_HEADS * HEAD_PAD, d_model).astype(BF16)
    w['w_oa_flat'] = lw['w_oa'].astype(BF16)
    w['w_os'] = lw['w_os'].astype(BF16)
    w['w_out'] = lw['w_out'].astype(BF16)
    w['g_ffn'] = row(lw['g_ffn'])
    w['w_up'] = lw['w_up'].astype(BF16)
    w['conv_w'] = lw['conv_w'].astype(F32)
    w['conv_b'] = row(lw['conv_b'])
    w['w_down'] = lw['w_down'].astype(BF16)
    w['g_ple'] = row(lw['g_ple'])
    w['w_ple_gate'] = lw['w_ple_gate'].astype(BF16)
    w['w_ple_proj'] = lw['w_ple_proj'].astype(BF16)
    return w


def kernel(x_prompt, x_sample, p_prompt, p_sample, cache_ckv, cache_kr, page_table, state_ssm_re, state_ssm_im,
           state_conv, g_mix, w_in, g_cq, g_ckv, w_uq, w_uk, w_uv, g_q, g_k, a_re, a_im, log_dt, b_re, b_im,
           c_re, c_im, d_skip, w_glu, w_oa, w_os, w_out, g_ffn, w_up, conv_w, conv_b, w_down, g_ple,
           w_ple_gate, w_ple_proj):
    params = dict(g_mix=g_mix, w_in=w_in, g_cq=g_cq, g_ckv=g_ckv, w_uq=w_uq, w_uk=w_uk, w_uv=w_uv, g_q=g_q,
                  g_k=g_k, a_re=a_re, a_im=a_im, log_dt=log_dt, b_re=b_re, b_im=b_im, c_re=c_re, c_im=c_im,
                  d_skip=d_skip, w_glu=w_glu, w_oa=w_oa, w_os=w_os, w_out=w_out, g_ffn=g_ffn, w_up=w_up,
                  conv_w=conv_w, conv_b=conv_b, w_down=w_down, g_ple=g_ple, w_ple_gate=w_ple_gate,
                  w_ple_proj=w_ple_proj)
    depth = w_in.shape[0]
    bp, seq, d_model = x_prompt.shape
    bd, t_new, _ = x_sample.shape
    past_len = page_table.shape[1] * PAGE_SIZE
    d_ff = w_down.shape[1]
    n_pow = max(min(SSM_ROWS, seq) // SUBLANES, 1)

    rope_prompt = _rope_tables_q(jnp.arange(seq))
    rope_sample = _rope_tables_q(jnp.repeat(past_len + jnp.arange(t_new), bd))
    rope_cache = _rope_tables_k(jnp.arange(past_len))

    yp, ys = x_prompt, x_sample
    outs = [[] for _ in range(10)]
    for i in range(depth):
        w = _layer_weights({k: v[i] for k, v in params.items()}, d_model, n_pow)

        q, k, v, ckv, kr, u, sga, sgs = _inproj(yp, w, rope_prompt, BF16)
        att = _flash_prompt(q, k, v)
        yssm, hr, hi = _ssm_prompt(u, w)
        yp, tail = _post(yp, att, yssm, sga, sgs, p_prompt[i], w)
        outs[0].append(ckv); outs[1].append(kr); outs[4].append(hr); outs[5].append(hi)
        outs[8].append(tail[:, SUBLANES - (CONV_W - 1):, :])

        n_tok = bd * t_new
        steps = lambda z: z.transpose(1, 0, 2).reshape(1, n_tok, z.shape[-1])
        entries = lambda z: z.reshape(t_new, bd, z.shape[-1]).transpose(1, 0, 2)
        q, k, v, ckv, kr, u, sga, sgs = _inproj(steps(ys), w, rope_sample, F32)
        ckv = entries(ckv)
        att = _sample_attention(entries(q), entries(k), ckv, cache_ckv, cache_kr, page_table, i, w, rope_cache)
        yssm, hr, hi = _ssm_sample(u.reshape(t_new, bd, -1), state_ssm_re[i], state_ssm_im[i], w)
        hist0 = state_conv[i].astype(F32).transpose(1, 0, 2).reshape((CONV_W - 1) * bd, 2 * d_ff)
        w_s = dict(w, w_oa=w['w_oa_flat'])
        y2, tail = _post(steps(ys), steps(att).astype(BF16), yssm.reshape(1, n_tok, -1), sga, sgs,
                         steps(p_sample[i]), w_s, hist0=hist0, step_rows=bd)
        ys = entries(y2)
        outs[2].append(ckv); outs[3].append(entries(kr))
        outs[6].append(hr); outs[7].append(hi)
        outs[9].append(tail.reshape(CONV_W - 1, bd, 2 * d_ff).transpose(1, 0, 2))

    st = [jnp.stack(o) for o in outs]
    return (yp, ys, st[0], st[1], st[2], st[3], st[4], st[5], st[6], st[7], st[8], st[9])
```

```python
import functools
import math

import jax
import jax.numpy as jnp
from jax import lax
from jax.experimental import pallas as pl
from jax.experimental.pallas import tpu as pltpu

F32 = jnp.float32
BF16 = jnp.bfloat16

N_HEADS = 8
QK_NOPE = 64
QK_ROPE = 32
QK_HEAD = QK_NOPE + QK_ROPE
V_HEAD = 64
Q_LORA = 384
KV_LORA = 256
ROPE_THETA = 10000.0
SCALE = QK_HEAD ** -0.5
LOG2_E = math.log2(math.e)
NEG_INF = -1e30
SSM_WIDTH = 512
GROUP = 16
N_GROUPS = SSM_WIDTH // GROUP
STATE = 64
N_STATE = N_GROUPS * STATE
CONV_W = 3
EPS = 1e-6
PAGE_SIZE = 128

LANES = 128
SUBLANES = 8
HEAD_PAD = LANES
NEW_PAD = 2 * SUBLANES
V_ROWS = V_HEAD + 2 * SUBLANES
VMEM_LIMIT_BYTES = 56 * 1024 * 1024

INPROJ_ROWS = 512
FLASH_T = 512
FLASH_HEADS = 2
SSM_ROWS = 256
SSM_COLS = 512
POST_ROWS = 256
FFN_CHUNK = 256
SAMPLE_CHUNK_PAGES = 8
SSM_GROUP_BLOCK = 8


def _cparams(sem):
    return pltpu.CompilerParams(dimension_semantics=sem, vmem_limit_bytes=VMEM_LIMIT_BYTES)


def _const_spec(shape):
    nd = len(shape)
    return pl.BlockSpec(shape, lambda *_: (0,) * nd, pipeline_mode=pl.Buffered(1))


def _rms(x, g):
    return x * lax.rsqrt(jnp.mean(x * x, axis=-1, keepdims=True) + EPS) * g


def _bdot(a, b):
    return jnp.dot(a.astype(BF16), b, preferred_element_type=F32)


def _dot_nt(a, b):
    return lax.dot_general(a, b, (((1,), (1,)), ((), ())), preferred_element_type=F32)


def _ssm_disc_kernel(are_ref, aim_ref, ldt_ref, arer_ref, aimr_ref, ldtr_ref, bre_ref, bim_ref,
                     bbre_ref, bbim_ref, pre_ref, pim_ref, *, n_pow):
    def zoh(a_re, a_im, ldt):
        dt = jnp.exp(ldt)
        mag = jnp.exp(dt * a_re)
        ab_re = mag * jnp.cos(dt * a_im)
        ab_im = mag * jnp.sin(dt * a_im)
        return ab_re, ab_im

    a_re = arer_ref[...]
    a_im = aimr_ref[...]
    ab_re, ab_im = zoh(a_re, a_im, ldtr_ref[...])
    den = a_re * a_re + a_im * a_im
    nr = ab_re - 1.0
    f_re = (nr * a_re + ab_im * a_im) / den
    f_im = (ab_im * a_re - nr * a_im) / den
    b_re = bre_ref[...]
    b_im = bim_ref[...]
    bbre_ref[...] = f_re * b_re - f_im * b_im
    bbim_ref[...] = f_re * b_im + f_im * b_re

    p_re, p_im = zoh(are_ref[...], aim_ref[...], ldt_ref[...])
    c_re, c_im = p_re, p_im
    for j in range(n_pow):
        pre_ref[j] = c_re
        pim_ref[j] = c_im
        c_re, c_im = c_re * p_re - c_im * p_im, c_re * p_im + c_im * p_re


def _ssm_discretize(a_re, a_im, log_dt, b_re, b_im, n_pow):
    g, p = a_re.shape
    rows = g * GROUP
    rep = lambda z: jnp.repeat(z, GROUP, axis=0)
    bt = lambda z: z.transpose(0, 2, 1).reshape(rows, p)
    ldt = log_dt.reshape(g, 1)
    out_shape = (jax.ShapeDtypeStruct((rows, p), F32), jax.ShapeDtypeStruct((rows, p), F32),
                 jax.ShapeDtypeStruct((n_pow, g, p), F32), jax.ShapeDtypeStruct((n_pow, g, p), F32))
    bb_re, bb_im, pow_re, pow_im = pl.pallas_call(
        functools.partial(_ssm_disc_kernel, n_pow=n_pow),
        out_shape=out_shape, name="ssm_disc",
    )(a_re, a_im, ldt, rep(a_re), rep(a_im), rep(ldt), bt(b_re), bt(b_im))
    return bb_re, bb_im, pow_re.reshape(n_pow, g * p), pow_im.reshape(n_pow, g * p)


def _block_diag_in(bb):
    nb = N_GROUPS // SSM_GROUP_BLOCK
    z = bb.reshape(nb, SSM_GROUP_BLOCK, GROUP, STATE)
    eye = jnp.eye(SSM_GROUP_BLOCK, dtype=bb.dtype)
    out = jnp.einsum('kgip,gh->kgihp', z, eye)
    return out.reshape(nb, SSM_GROUP_BLOCK * GROUP, SSM_GROUP_BLOCK * STATE).astype(BF16)


def _block_diag_out(c):
    nb = N_GROUPS // SSM_GROUP_BLOCK
    z = c.reshape(nb, SSM_GROUP_BLOCK, GROUP, STATE)
    eye = jnp.eye(SSM_GROUP_BLOCK, dtype=c.dtype)
    out = jnp.einsum('kgip,gh->kgphi', z, eye)
    return out.reshape(nb, SSM_GROUP_BLOCK * STATE, SSM_GROUP_BLOCK * GROUP).astype(BF16)


OFF_CQ = 0
OFF_CKV = OFF_CQ + Q_LORA
OFF_U = OFF_CKV + KV_LORA


def _head_sumsq(z, ones2_ref):
    sq = (z * z).astype(BF16)
    span = ones2_ref.shape[0]
    parts = [jnp.dot(sq[:, j:j + span], ones2_ref[...], preferred_element_type=F32)
             for j in range(0, z.shape[-1], span)]
    return jnp.concatenate(parts, axis=-1)


def _inproj_kernel(x_ref, gmix_ref, win_ref, gcq_ref, gckv_ref, wuq_ref, wuqr_ref, wuk_ref, ekr_ref, swap_ref,
                   ones2_ref, wuvt_ref, vone_ref, gq_ref, gkn_ref, gkr_ref, rc_ref, rs_ref, cc_ref, cs_ref,
                   q_ref, k_ref, v_ref, ckv_ref, kr_ref, u_ref, sga_ref, sgs_ref, *, d_model, q_scale):
    off_ga = OFF_U + SSM_WIDTH
    off_gs = off_ga + d_model
    off_kr = off_gs + d_model
    inv = 1.0 / QK_HEAD
    blocks = [slice(h * HEAD_PAD, (h + 1) * HEAD_PAD) for h in range(N_HEADS)]
    xn = _rms(x_ref[...], gmix_ref[...])
    proj = _bdot(xn, win_ref[...])
    cq = _rms(proj[:, OFF_CQ:OFF_CKV], gcq_ref[...])
    ckv = _rms(proj[:, OFF_CKV:OFF_U], gckv_ref[...])
    kr = proj[:, off_kr:off_kr + QK_ROPE]
    u_ref[...] = proj[:, OFF_U:off_ga]
    sga_ref[...] = jax.nn.sigmoid(proj[:, off_ga:off_gs]).astype(sga_ref.dtype)
    sgs_ref[...] = jax.nn.sigmoid(proj[:, off_gs:off_kr]).astype(sgs_ref.dtype)
    ckv_ref[...] = ckv
    kr_ref[...] = kr
    ckv_b = ckv.astype(BF16)
    v_ref[...] = (_dot_nt(wuvt_ref[...], ckv_b) + vone_ref[...]).astype(v_ref.dtype)

    cq_b = cq.astype(BF16)
    qf = jnp.dot(cq_b, wuq_ref[...], preferred_element_type=F32)
    qrot = jnp.dot(cq_b, wuqr_ref[...], preferred_element_type=F32)
    rn_q = lax.rsqrt(_head_sumsq(qf, ones2_ref) * inv + EPS)
    tc = rc_ref[...] * (gq_ref[...] * q_scale)
    ts = rs_ref[...] * q_scale
    for blk in blocks:
        q_ref[:, blk] = (rn_q[:, blk] * (qf[:, blk] * tc + qrot[:, blk] * ts)).astype(q_ref.dtype)

    kn = jnp.dot(ckv_b, wuk_ref[...], preferred_element_type=F32)
    krg = kr * gkr_ref[...]
    krr = krg * cc_ref[...] + _bdot(krg, swap_ref[...]) * cs_ref[...]
    placed = _bdot(krr, ekr_ref[...])
    r2 = jnp.sum(kr * kr, axis=-1, keepdims=True)
    rn_k = lax.rsqrt((_head_sumsq(kn, ones2_ref) + r2) * inv + EPS)
    for blk in blocks:
        k_ref[:, blk] = (rn_k[:, blk] * (kn[:, blk] * gkn_ref[...] + placed[:, blk])).astype(k_ref.dtype)


def _inproj(x, w, rope, act_dtype, q_scale):
    nb, rows, d = x.shape
    tr = min(INPROJ_ROWS, rows)
    assert rows % tr == 0
    hp = N_HEADS * HEAD_PAD
    row_spec = lambda n: pl.BlockSpec((None, tr, n), lambda b, t: (b, t, 0))
    tab_specs = [pl.BlockSpec((tr, z.shape[-1]), lambda b, t: (t, 0)) for z in rope]
    weights = (w['g_mix'], w['w_in'], w['g_cq'], w['g_ckv'], w['w_uq'], w['w_uq_rot'], w['w_uk'], w['e_kr'],
               w['swap_rope'], w['ones2'], w['w_uv_t'], w['v_one'], w['g_q'], w['g_k_nope'], w['g_k_rope'])
    out_shape = (
        jax.ShapeDtypeStruct((nb, rows, hp), act_dtype),
        jax.ShapeDtypeStruct((nb, rows, hp), act_dtype),
        jax.ShapeDtypeStruct((nb, hp, rows), act_dtype),
        jax.ShapeDtypeStruct((nb, rows, KV_LORA), F32),
        jax.ShapeDtypeStruct((nb, rows, QK_ROPE), F32),
        jax.ShapeDtypeStruct((nb, rows, SSM_WIDTH), F32),
        jax.ShapeDtypeStruct((nb, rows, d), BF16),
        jax.ShapeDtypeStruct((nb, rows, d), BF16),
    )
    out_specs = [row_spec(s.shape[-1]) for s in out_shape]
    out_specs[2] = pl.BlockSpec((None, hp, tr), lambda b, t: (b, 0, t))
    return pl.pallas_call(
        functools.partial(_inproj_kernel, d_model=d, q_scale=q_scale),
        grid=(nb, rows // tr),
        in_specs=[row_spec(d)] + [_const_spec(z.shape) for z in weights] + tab_specs,
        out_specs=tuple(out_specs),
        out_shape=out_shape,
        compiler_params=_cparams(("parallel", "parallel")),
        name="inproj",
    )(x, *weights, *rope)


def _flash_kernel(q_ref, k_ref, vt_ref, o_ref, s0, s1, m_sc, acc_sc, *, t, nh):
    qi = pl.program_id(2)
    heads = [slice(h * HEAD_PAD, (h + 1) * HEAD_PAD) for h in range(nh)]
    m_sc[...] = jnp.full(m_sc.shape, NEG_INF, F32)
    acc_sc[...] = jnp.zeros(acc_sc.shape, F32)

    def scores(kj, dst):
        start = pl.multiple_of(kj * t, t)
        for h, hs in enumerate(heads):
            dst[h] = _dot_nt(k_ref[pl.ds(start, t), hs], q_ref[:, hs])

    def fold(kj, src, diagonal):
        start = pl.multiple_of(kj * t, t)
        if diagonal:
            keep = lax.broadcasted_iota(jnp.int32, (t, t), 0) <= lax.broadcasted_iota(jnp.int32, (t, t), 1)
        for h, hs in enumerate(heads):
            st = src[h]
            if diagonal:
                st = jnp.where(keep, st, NEG_INF)
            m = m_sc[h]
            m_new = jnp.maximum(m, jnp.max(st, axis=0, keepdims=True))
            p = jnp.exp2(st - m_new)
            corr = jnp.exp2(m - m_new)
            vrows = slice(h * HEAD_PAD, h * HEAD_PAD + V_ROWS)
            acc_sc[h] = acc_sc[h] * corr + jnp.dot(vt_ref[vrows, pl.ds(start, t)], p.astype(BF16),
                                                   preferred_element_type=F32)
            m_sc[h] = m_new

    scores(0, s0)

    def pair(kk, carry):
        j = 2 * kk
        scores(j + 1, s1)
        fold(j, s0, False)
        scores(j + 2, s0)
        fold(j + 1, s1, False)
        return carry

    lax.fori_loop(0, qi // 2, pair, 0)

    @pl.when(qi % 2 == 0)
    def _():
        fold(qi, s0, True)

    @pl.when(qi % 2 == 1)
    def _():
        scores(qi, s1)
        fold(qi - 1, s0, False)
        fold(qi, s1, True)

    for h, hs in enumerate(heads):
        acc = acc_sc[h]
        out = jnp.concatenate([acc / acc[V_HEAD:V_HEAD + 1, :], jnp.zeros((HEAD_PAD - V_ROWS, t), F32)], axis=0)
        o_ref[:, hs] = jnp.transpose(out).astype(o_ref.dtype)


def _flash_prompt(q, k, vt):
    b, s, hp = q.shape
    tq = min(FLASH_T, s)
    nh = FLASH_HEADS
    assert s % tq == 0 and N_HEADS % nh == 0
    return pl.pallas_call(
        functools.partial(_flash_kernel, t=tq, nh=nh),
        grid=(b, N_HEADS // nh, s // tq),
        in_specs=[pl.BlockSpec((None, tq, nh * HEAD_PAD), lambda bi, h, qi: (bi, qi, h)),
                  pl.BlockSpec((None, s, nh * HEAD_PAD), lambda bi, h, qi: (bi, 0, h)),
                  pl.BlockSpec((None, nh * HEAD_PAD, s), lambda bi, h, qi: (bi, h, 0))],
        out_specs=pl.BlockSpec((None, tq, nh * HEAD_PAD), lambda bi, h, qi: (bi, qi, h)),
        out_shape=jax.ShapeDtypeStruct((b, s, hp), BF16),
        scratch_shapes=[pltpu.VMEM((nh, tq, tq), F32), pltpu.VMEM((nh, tq, tq), F32),
                        pltpu.VMEM((nh, 1, tq), F32), pltpu.VMEM((nh, V_ROWS, tq), F32)],
        compiler_params=_cparams(("parallel", "parallel", "arbitrary")),
        name="flash_prompt",
    )(q, k, vt)


def _sample_attn_kernel(pt_ref, qn_ref, qr_ref, qrs_ref, qf_ref, knew_ref, cnew_ref, ca_ref, cb_ref,
                        wukp_ref, gkn_ref, gkr_ref, wuv_ref, hmask_ref, ckv_hbm, kr_hbm, o_ref,
                        cf, rf, cbuf, s_all, sem_c, sem_r, *, layer, n_pages, n_new, chunk_pages):
    b = pl.program_id(0)
    slot = b % 2
    nq = n_new * N_HEADS
    keys = chunk_pages * PAGE_SIZE
    n_keys = n_pages * PAGE_SIZE

    def page_copies(entry, sl, j):
        pg = pt_ref[entry * n_pages + j]
        return (pltpu.make_async_copy(ckv_hbm.at[layer, pg], cf.at[sl, j], sem_c.at[sl]),
                pltpu.make_async_copy(kr_hbm.at[layer, pg], rf.at[sl, j], sem_r.at[sl]))

    def start_entry(entry, sl):
        def body(j, carry):
            for cp in page_copies(entry, sl, j):
                cp.start()
            return carry
        lax.fori_loop(0, n_pages, body, 0, unroll=min(8, n_pages))

    @pl.when(b == 0)
    def _():
        start_entry(0, 0)

    @pl.when(b + 1 < pl.num_programs(0))
    def _():
        start_entry(b + 1, 1 - slot)

    pltpu.make_async_copy(ckv_hbm.at[layer, pl.ds(0, n_pages)], cf.at[slot], sem_c.at[slot]).wait()
    pltpu.make_async_copy(kr_hbm.at[layer, pl.ds(0, n_pages)], rf.at[slot], sem_r.at[slot]).wait()

    qn = (qn_ref[...] * gkn_ref[...]).astype(BF16)
    qabs = _dot_nt(qn, wukp_ref[...]).astype(BF16)

    knew = jnp.concatenate([knew_ref[...].astype(BF16), jnp.zeros((LANES - NEW_PAD, knew_ref.shape[-1]), BF16)],
                           axis=0)
    s_new = _dot_nt(qf_ref[...].astype(BF16), knew)
    tok = lax.broadcasted_iota(jnp.int32, (nq, LANES), 0) // N_HEADS
    key = lax.broadcasted_iota(jnp.int32, (nq, LANES), 1)
    s_all[:, n_keys:n_keys + LANES] = jnp.where(key <= tok, s_new, NEG_INF)
    cbuf[n_keys:n_keys + NEW_PAD, :] = cnew_ref[...].astype(BF16)
    cbuf[n_keys + NEW_PAD:n_keys + LANES, :] = jnp.zeros((LANES - NEW_PAD, KV_LORA), BF16)

    qr = qr_ref[...].astype(BF16)
    qrs = qrs_ref[...].astype(BF16)
    for ch in range(n_pages // chunk_pages):
        pages = range(ch * chunk_pages, (ch + 1) * chunk_pages)
        span = slice(ch * keys, (ch + 1) * keys)
        for i in pages:
            cbuf[i * PAGE_SIZE:(i + 1) * PAGE_SIZE, :] = cf[slot, i].astype(BF16)
        c = cbuf[span, :]
        kn = jnp.dot(c, wukp_ref[...], preferred_element_type=F32)
        sq = kn * kn
        ss = sq[:, 0:LANES]
        for j in range(1, N_HEADS * QK_NOPE // LANES):
            ss = ss + sq[:, j * LANES:(j + 1) * LANES]
        ss_t = jnp.transpose(ss)
        ss_h = ss_t[0:N_HEADS]
        for j in range(1, LANES // N_HEADS):
            ss_h = ss_h + ss_t[j * N_HEADS:(j + 1) * N_HEADS]
        r_t = jnp.concatenate([rf[slot, i] for i in pages], axis=1)
        r2 = jnp.sum(r_t * r_t, axis=0, keepdims=True)
        rn_h = lax.rsqrt((ss_h + r2) * (1.0 / QK_HEAD) + EPS)
        rn = jnp.concatenate([rn_h] * n_new, axis=0)
        rg = r_t * gkr_ref[...]
        s_nope = _dot_nt(qabs, c)
        s_rope = (jnp.dot(qr, (rg * ca_ref[:, span]).astype(BF16), preferred_element_type=F32)
                  + jnp.dot(qrs, (rg * cb_ref[:, span]).astype(BF16), preferred_element_type=F32))
        s_all[:, span] = rn * (s_nope + s_rope)

    s = s_all[...]
    p = jnp.exp(s - jnp.max(s, axis=-1, keepdims=True))
    l = jnp.sum(p, axis=-1, keepdims=True)
    acc = jnp.dot(p.astype(BF16), cbuf[...], preferred_element_type=F32)
    o_lat = acc / l
    full = _bdot(o_lat, wuv_ref[...]) * hmask_ref[...]
    o_ref[...] = jnp.sum(full.reshape(n_new, N_HEADS, N_HEADS * V_HEAD), axis=1)


def _sample_attention(q, k, ckv, cache_ckv, cache_kr, page_table, layer, w, rope_k):
    bd, t, hp = q.shape
    n_logical = page_table.shape[1]
    chunk_pages = min(SAMPLE_CHUNK_PAGES, n_logical)
    assert n_logical % chunk_pages == 0 and t <= NEW_PAD
    nq = t * N_HEADS
    n_keys = n_logical * PAGE_SIZE
    head_mask = ((jnp.arange(nq) % N_HEADS)[:, None] == (jnp.arange(N_HEADS * V_HEAD) // V_HEAD)[None, :]).astype(F32)
    qh = q.reshape(bd, t, N_HEADS, HEAD_PAD)
    eye = jnp.eye(N_HEADS, dtype=F32)

    qn = jnp.einsum('bthd,hk->bthdk', qh[..., :QK_NOPE], eye).reshape(bd, nq, QK_NOPE * N_HEADS)
    qr = qh[..., QK_NOPE:QK_HEAD].reshape(bd, nq, QK_ROPE)
    qrs = jnp.concatenate([qr[..., QK_ROPE // 2:], qr[..., :QK_ROPE // 2]], axis=-1)
    qf = jnp.einsum('bthl,hk->bthkl', qh, eye).reshape(bd, nq, hp)
    pad_t = lambda z: jnp.pad(z, ((0, 0), (0, NEW_PAD - t), (0, 0)))
    knew = pad_t(k)
    cnew = pad_t(ckv)
    ca, cb = rope_k
    cache_kr_t = jnp.swapaxes(cache_kr, 2, 3)

    def bspec(rows, n):
        return pl.BlockSpec((None, rows, n), lambda b, pt: (b, 0, 0))

    def wspec(shape):
        nd = len(shape)
        return pl.BlockSpec(shape, lambda b, pt: (0,) * nd)

    weights = (ca, cb, w['w_uk_perm'], w['g_k_nope_perm'], w['g_k_rope_col'], w['w_uv_flat'], head_mask)
    in_specs = ([bspec(nq, QK_NOPE * N_HEADS), bspec(nq, QK_ROPE), bspec(nq, QK_ROPE), bspec(nq, hp),
                 bspec(NEW_PAD, hp), bspec(NEW_PAD, KV_LORA)]
                + [wspec(z.shape) for z in weights]
                + [pl.BlockSpec(memory_space=pl.ANY), pl.BlockSpec(memory_space=pl.ANY)])
    grid_spec = pltpu.PrefetchScalarGridSpec(
        num_scalar_prefetch=1, grid=(bd,), in_specs=in_specs,
        out_specs=pl.BlockSpec((None, t, N_HEADS * V_HEAD), lambda b, pt: (b, 0, 0)),
        scratch_shapes=[pltpu.VMEM((2, n_logical, PAGE_SIZE, KV_LORA), F32),
                        pltpu.VMEM((2, n_logical, QK_ROPE, PAGE_SIZE), F32),
                        pltpu.VMEM((n_keys + LANES, KV_LORA), BF16),
                        pltpu.VMEM((nq, n_keys + LANES), F32),
                        pltpu.SemaphoreType.DMA((2,)), pltpu.SemaphoreType.DMA((2,))])
    return pl.pallas_call(
        functools.partial(_sample_attn_kernel, layer=layer, n_pages=n_logical, n_new=t, chunk_pages=chunk_pages),
        grid_spec=grid_spec,
        out_shape=jax.ShapeDtypeStruct((bd, t, N_HEADS * V_HEAD), F32),
        compiler_params=_cparams(("arbitrary",)),
        name="sample_attn",
    )(page_table.reshape(-1), qn, qr, qrs, qf, knew, cnew, *weights, cache_ckv, cache_kr_t)


def _ssm_in(u_b, bre_ref, bim_ref, store_re, store_im):
    nb = bre_ref.shape[0]
    kw = bre_ref.shape[1]
    nw = bre_ref.shape[2]
    for kb in range(nb):
        blk = u_b[:, kb * kw:(kb + 1) * kw]
        store_re(kb * nw, nw, jnp.dot(blk, bre_ref[kb], preferred_element_type=F32))
        store_im(kb * nw, nw, jnp.dot(blk, bim_ref[kb], preferred_element_type=F32))


def _ssm_out(h_re, h_im, cre_ref, cim_ref):
    nb = cre_ref.shape[0]
    kw = cre_ref.shape[1]
    parts = []
    for kb in range(nb):
        parts.append(jnp.dot(h_re[:, kb * kw:(kb + 1) * kw].astype(BF16), cre_ref[kb], preferred_element_type=F32)
                     - jnp.dot(h_im[:, kb * kw:(kb + 1) * kw].astype(BF16), cim_ref[kb], preferred_element_type=F32))
    return jnp.concatenate(parts, axis=-1)


def _glu(y, wglu_ref):
    g = _bdot(jax.nn.gelu(y), wglu_ref[...])
    half = g.shape[-1] // 2
    return g[:, :half] * jax.nn.sigmoid(g[:, half:])


def _ssm_prompt_kernel(u_ref, perm_ref, permt_ref, bre_ref, bim_ref, pre_ref, pim_ref, cre_ref, cim_ref,
                       dskip_ref, wglu_ref, ys_ref, hre_out, him_out,
                       hre, him, car_re, car_im, hin_re, hin_im, *, rows, sub):
    ti = pl.program_id(1)

    @pl.when(ti == 0)
    def _():
        car_re[...] = jnp.zeros(car_re.shape, F32)
        car_im[...] = jnp.zeros(car_im.shape, F32)

    u = u_ref[...]
    u_hi = u.astype(BF16)
    u_lo = (u - u_hi.astype(F32)).astype(BF16)
    perm = perm_ref[...]
    up = jnp.dot(perm, u_hi, preferred_element_type=F32) + jnp.dot(perm, u_lo, preferred_element_type=F32)

    def st_re(c0, n, val):
        hre[:, c0:c0 + n] = val

    def st_im(c0, n, val):
        him[:, c0:c0 + n] = val

    _ssm_in(up.astype(BF16), bre_ref, bim_ref, st_re, st_im)

    n_cb = N_STATE // SSM_COLS
    for cb in range(n_cb):
        cols = slice(cb * SSM_COLS, (cb + 1) * SSM_COLS)
        a_re = jnp.broadcast_to(pre_ref[0:1, cols], (SUBLANES, SSM_COLS))
        a_im = jnp.broadcast_to(pim_ref[0:1, cols], (SUBLANES, SSM_COLS))

        def local(j, st):
            s_re, s_im = st
            r0 = pl.multiple_of(j * SUBLANES, SUBLANES)
            n_re = a_re * s_re - a_im * s_im + hre[pl.ds(r0, SUBLANES), cols]
            n_im = a_re * s_im + a_im * s_re + him[pl.ds(r0, SUBLANES), cols]
            hre[pl.ds(r0, SUBLANES), cols] = n_re
            him[pl.ds(r0, SUBLANES), cols] = n_im
            return n_re, n_im

        zero = jnp.zeros((SUBLANES, SSM_COLS), F32)
        e_re, e_im = lax.fori_loop(0, sub, local, (zero, zero))

        as_re = pre_ref[sub - 1:sub, cols]
        as_im = pim_ref[sub - 1:sub, cols]
        c_re = car_re[:, cols]
        c_im = car_im[:, cols]
        for s in range(SUBLANES):
            hin_re[s:s + 1, cols] = c_re
            hin_im[s:s + 1, cols] = c_im
            c_re, c_im = (as_re * c_re - as_im * c_im + e_re[s:s + 1],
                          as_re * c_im + as_im * c_re + e_im[s:s + 1])
        car_re[:, cols] = c_re
        car_im[:, cols] = c_im
        g_re = hin_re[:, cols]
        g_im = hin_im[:, cols]

        def fix(j, _):
            r0 = pl.multiple_of(j * SUBLANES, SUBLANES)
            p_re = jnp.broadcast_to(pre_ref[pl.ds(j, 1), cols], (SUBLANES, SSM_COLS))
            p_im = jnp.broadcast_to(pim_ref[pl.ds(j, 1), cols], (SUBLANES, SSM_COLS))
            hre[pl.ds(r0, SUBLANES), cols] = hre[pl.ds(r0, SUBLANES), cols] + (p_re * g_re - p_im * g_im)
            him[pl.ds(r0, SUBLANES), cols] = him[pl.ds(r0, SUBLANES), cols] + (p_re * g_im + p_im * g_re)
            return 0

        lax.fori_loop(0, sub, fix, 0)

    y = _ssm_out(hre[...], him[...], cre_ref, cim_ref) + dskip_ref[...] * up
    ys = _glu(y, wglu_ref).astype(BF16)
    ys_ref[...] = jnp.dot(permt_ref[...], ys, preferred_element_type=F32).astype(ys_ref.dtype)

    @pl.when(ti == pl.num_programs(1) - 1)
    def _():
        hre_out[...] = car_re[...]
        him_out[...] = car_im[...]


def _ssm_prompt(u, w):
    b, s, width = u.shape
    rows = min(SSM_ROWS, s)
    assert s % rows == 0 and rows % SUBLANES == 0
    sub = rows // SUBLANES
    r = jnp.arange(rows)
    t_of_r = (r % SUBLANES) * sub + r // SUBLANES
    perm = (t_of_r[:, None] == jnp.arange(rows)[None, :]).astype(BF16)
    pre = w['pow_re'][:sub]
    pim = w['pow_im'][:sub]
    weights = (perm, perm.T, w['bb_re'], w['bb_im'], pre, pim, w['cc_re'], w['cc_im'], w['d_skip'], w['w_glu'])
    ys, h_re, h_im = pl.pallas_call(
        functools.partial(_ssm_prompt_kernel, rows=rows, sub=sub),
        grid=(b, s // rows),
        in_specs=[pl.BlockSpec((None, rows, width), lambda bi, t: (bi, t, 0))]
                 + [_const_spec(z.shape) for z in weights],
        out_specs=(pl.BlockSpec((None, rows, width), lambda bi, t: (bi, t, 0)),
                   pl.BlockSpec((None, 1, N_STATE), lambda bi, t: (bi, 0, 0)),
                   pl.BlockSpec((None, 1, N_STATE), lambda bi, t: (bi, 0, 0))),
        out_shape=(jax.ShapeDtypeStruct((b, s, width), BF16),
                   jax.ShapeDtypeStruct((b, 1, N_STATE), F32),
                   jax.ShapeDtypeStruct((b, 1, N_STATE), F32)),
        scratch_shapes=[pltpu.VMEM((rows, N_STATE), F32), pltpu.VMEM((rows, N_STATE), F32),
                        pltpu.VMEM((1, N_STATE), F32), pltpu.VMEM((1, N_STATE), F32),
                        pltpu.VMEM((SUBLANES, N_STATE), F32), pltpu.VMEM((SUBLANES, N_STATE), F32)],
        compiler_params=_cparams(("parallel", "arbitrary")),
        name="ssm_prompt",
    )(u, *weights)
    return ys, h_re.reshape(b, N_GROUPS, STATE), h_im.reshape(b, N_GROUPS, STATE)


def _ssm_sample_kernel(u_ref, h0re_ref, h0im_ref, bre_ref, bim_ref, pre_ref, pim_ref, cre_ref, cim_ref,
                       dskip_ref, wglu_ref, ys_ref, hre_out, him_out, bu_re, bu_im, *, steps):
    a_re = pre_ref[0:1, :]
    a_im = pim_ref[0:1, :]
    h_re = h0re_ref[...]
    h_im = h0im_ref[...]

    def st_re(c0, n, val):
        bu_re[:, c0:c0 + n] = val

    def st_im(c0, n, val):
        bu_im[:, c0:c0 + n] = val

    for t in range(steps):
        u = u_ref[t]
        _ssm_in(u.astype(BF16), bre_ref, bim_ref, st_re, st_im)
        h_re, h_im = (a_re * h_re - a_im * h_im + bu_re[...], a_re * h_im + a_im * h_re + bu_im[...])
        y = _ssm_out(h_re, h_im, cre_ref, cim_ref) + dskip_ref[...] * u
        ys_ref[t] = _glu(y, wglu_ref).astype(ys_ref.dtype)
    hre_out[...] = h_re
    him_out[...] = h_im


def _ssm_sample(ut, h0_re, h0_im, w):
    t, bd, width = ut.shape
    ys, h_re, h_im = pl.pallas_call(
        functools.partial(_ssm_sample_kernel, steps=t),
        out_shape=(jax.ShapeDtypeStruct((t, bd, width), BF16),
                   jax.ShapeDtypeStruct((bd, N_STATE), F32),
                   jax.ShapeDtypeStruct((bd, N_STATE), F32)),
        scratch_shapes=[pltpu.VMEM((bd, N_STATE), F32), pltpu.VMEM((bd, N_STATE), F32)],
        compiler_params=pltpu.CompilerParams(vmem_limit_bytes=VMEM_LIMIT_BYTES),
        name="ssm_sample",
    )(ut, h0_re.reshape(bd, N_STATE).astype(F32), h0_im.reshape(bd, N_STATE).astype(F32),
      w['bb_re'], w['bb_im'], w['pow_re'][:1], w['pow_im'][:1], w['cc_re'], w['cc_im'], w['d_skip'], w['w_glu'])
    return ys, h_re.reshape(bd, N_GROUPS, STATE), h_im.reshape(bd, N_GROUPS, STATE)


def _post_kernel(*refs, rows, d_ff, row_major):
    if row_major:
        (x_ref, att_ref, ys_ref, sga_ref, sgs_ref, p_ref, woa_ref, wos_ref, wout_ref, gffn_ref, wup_ref,
         cw_ref, cbias_ref, wdown_ref, gple_ref, wpg_ref, wpp_ref, y_ref, tail_ref, hist) = refs
        hist0_ref = None
    else:
        (x_ref, att_ref, ys_ref, sga_ref, sgs_ref, p_ref, hist0_ref, woa_ref, wos_ref, wout_ref, gffn_ref, wup_ref,
         cw_ref, cbias_ref, wdown_ref, gple_ref, wpg_ref, wpp_ref, y_ref, tail_ref, hist) = refs
    ti = pl.program_id(1)
    pad = SUBLANES

    mixed = (sga_ref[...].astype(F32) * jnp.dot(att_ref[...], woa_ref[...], preferred_element_type=F32)
             + sgs_ref[...].astype(F32) * jnp.dot(ys_ref[...], wos_ref[...], preferred_element_type=F32))
    x1 = x_ref[...] + _bdot(mixed, wout_ref[...])
    xn = _rms(x1, gffn_ref[...]).astype(BF16)

    @pl.when(ti == 0)
    def _():
        if row_major:
            hist[0:pad, :] = jnp.zeros((pad, 2 * d_ff), F32)
        else:
            hist[...] = hist0_ref[...]

    def conv_cols(c0, n):
        cols = slice(c0, c0 + n)
        up = jnp.dot(xn, wup_ref[:, cols], preferred_element_type=F32)
        if row_major:
            hist[pad:pad + rows, cols] = up
            s1 = hist[pad - 1:pad - 1 + rows, cols]
            s2 = hist[pad - 2:pad - 2 + rows, cols]
        else:
            s2 = hist[0:rows, cols]
            s1 = hist[rows:2 * rows, cols]
            hist[0:rows, cols] = s1
            hist[rows:2 * rows, cols] = up
        return cbias_ref[:, cols] + s2 * cw_ref[0:1, cols] + s1 * cw_ref[1:2, cols] + up * cw_ref[2:3, cols]

    f = jnp.zeros((rows, x1.shape[-1]), F32)
    for c in range(d_ff // FFN_CHUNK):
        a = conv_cols(c * FFN_CHUNK, FFN_CHUNK)
        v = conv_cols(d_ff + c * FFN_CHUNK, FFN_CHUNK)
        hdn = (jax.nn.gelu(a) * v).astype(BF16)
        f = f + jnp.dot(hdn, wdown_ref[c * FFN_CHUNK:(c + 1) * FFN_CHUNK, :], preferred_element_type=F32)
    x2 = x1 + f

    if row_major:
        tail = hist[rows:rows + pad, :]
        hist[0:pad, :] = tail
        tail_ref[...] = tail
    else:
        @pl.when(ti == pl.num_programs(1) - 1)
        def _():
            tail_ref[...] = hist[...]

    gate = jax.nn.sigmoid(_bdot(_rms(x2, gple_ref[...]), wpg_ref[...]))
    y_ref[...] = x2 + gate * _bdot(p_ref[...], wpp_ref[...])


def _post(x, att, ys, sga, sgs, p, w, hist0=None, step_rows=None):
    nb, total, d = x.shape
    d_ff = w['w_down'].shape[0]
    assert d_ff % FFN_CHUNK == 0
    row_major = hist0 is None
    rows = min(POST_ROWS, total) if row_major else step_rows
    assert total % rows == 0
    row_spec = lambda n: pl.BlockSpec((None, rows, n), lambda b, t: (b, t, 0))
    acts = [x, att, ys, sga, sgs, p]
    act_specs = [row_spec(z.shape[-1]) for z in acts]
    weights = (w['w_oa'], w['w_os'], w['w_out'], w['g_ffn'], w['w_up'], w['conv_w'], w['conv_b'], w['w_down'],
               w['g_ple'], w['w_ple_gate'], w['w_ple_proj'])
    if row_major:
        hist_rows = SUBLANES
        tail_spec = pl.BlockSpec((None, hist_rows, 2 * d_ff), lambda b, t: (b, 0, 0))
        tail_shape = jax.ShapeDtypeStruct((nb, hist_rows, 2 * d_ff), F32)
        scratch = pltpu.VMEM((rows + hist_rows, 2 * d_ff), F32)
    else:
        assert nb == 1
        hist_rows = 2 * rows
        acts.append(hist0)
        act_specs.append(_const_spec(hist0.shape))
        tail_spec = pl.BlockSpec((hist_rows, 2 * d_ff), lambda b, t: (0, 0), pipeline_mode=pl.Buffered(1))
        tail_shape = jax.ShapeDtypeStruct((hist_rows, 2 * d_ff), F32)
        scratch = pltpu.VMEM((hist_rows, 2 * d_ff), F32)
    return pl.pallas_call(
        functools.partial(_post_kernel, rows=rows, d_ff=d_ff, row_major=row_major),
        grid=(nb, total // rows),
        in_specs=act_specs + [_const_spec(z.shape) for z in weights],
        out_specs=(row_spec(d), tail_spec),
        out_shape=(jax.ShapeDtypeStruct((nb, total, d), F32), tail_shape),
        scratch_shapes=[scratch],
        compiler_params=_cparams(("parallel", "arbitrary")),
        name="post_prompt" if row_major else "post_sample",
    )(*acts, *weights)


def _rope_tables_q(pos):
    inv_freq = jnp.power(ROPE_THETA, -jnp.arange(0, QK_ROPE, 2, dtype=F32) / QK_ROPE)
    ang = pos.astype(F32)[:, None] * inv_freq[None, :]
    cos, sin = jnp.cos(ang), jnp.sin(ang)
    n = pos.shape[0]
    half = QK_ROPE // 2
    z = lambda k: jnp.zeros((n, k), F32)
    rc = jnp.concatenate([jnp.ones((n, QK_NOPE), F32), cos, cos, z(HEAD_PAD - QK_HEAD)], axis=1)
    rs = jnp.concatenate([z(QK_NOPE), -sin, sin, z(HEAD_PAD - QK_HEAD)], axis=1)
    return rc, rs, jnp.concatenate([cos, cos], axis=1), jnp.concatenate([-sin, sin], axis=1)


def _rope_tables_k(pos):
    inv_freq = jnp.power(ROPE_THETA, -jnp.arange(0, QK_ROPE, 2, dtype=F32) / QK_ROPE)
    ang = pos.astype(F32)[:, None] * inv_freq[None, :]
    cos, sin = jnp.cos(ang), jnp.sin(ang)
    return jnp.concatenate([cos, cos], axis=1).T, jnp.concatenate([sin, -sin], axis=1).T


def _pad_heads(wm, width):
    rows = wm.shape[0]
    return jnp.pad(wm, ((0, 0), (0, 0), (0, HEAD_PAD - width))).reshape(rows, N_HEADS * HEAD_PAD)


def _layer_weights(lw, d_model, n_pow):
    w = {}
    row = lambda z: z.reshape(1, -1).astype(F32)
    off_ckv = Q_LORA
    off_kr = off_ckv + KV_LORA
    off_u = off_kr + QK_ROPE
    off_ga = off_u + SSM_WIDTH
    off_gs = off_ga + d_model
    win = lw['w_in']
    w['w_in'] = jnp.concatenate(
        [win[:, :off_ckv], win[:, off_ckv:off_kr], win[:, off_u:off_ga], win[:, off_ga:off_gs], win[:, off_gs:],
         win[:, off_kr:off_u], jnp.zeros((d_model, LANES - QK_ROPE), win.dtype)], axis=1).astype(BF16)
    w['g_mix'] = row(lw['g_mix'])
    w['g_cq'] = row(lw['g_cq'])
    w['g_ckv'] = row(lw['g_ckv'])
    w['w_uq'] = _pad_heads(lw['w_uq'], QK_HEAD).astype(BF16)
    w['w_uk'] = _pad_heads(lw['w_uk'], QK_NOPE).astype(BF16)
    w['w_uv_t'] = _pad_heads(lw['w_uv'], V_HEAD).T.astype(BF16)
    lane = jnp.arange(N_HEADS * HEAD_PAD)
    w['e_kr'] = ((lane[None, :] % HEAD_PAD) == (QK_NOPE + jnp.arange(QK_ROPE))[:, None]).astype(BF16)
    w['g_q'] = row(jnp.pad(lw['g_q'], (0, HEAD_PAD - QK_HEAD)))
    w['g_k_nope'] = row(jnp.pad(lw['g_k'][:QK_NOPE], (0, HEAD_PAD - QK_NOPE)))
    w['g_k_rope'] = row(lw['g_k'][QK_NOPE:])
    half = QK_ROPE // 2
    wq = lw['w_uq'] * lw['g_q'][None, None, :]
    partner = jnp.concatenate([jnp.zeros_like(wq[..., :QK_NOPE]), wq[..., QK_NOPE + half:QK_HEAD],
                               wq[..., QK_NOPE:QK_NOPE + half]], axis=-1)
    w['w_uq_rot'] = _pad_heads(partner, QK_HEAD).astype(BF16)
    idx = jnp.arange(QK_ROPE)
    w['swap_rope'] = (idx[:, None] == ((idx + half) % QK_ROPE)[None, :]).astype(BF16)
    blk = jnp.arange(2 * HEAD_PAD) // HEAD_PAD
    w['ones2'] = (blk[:, None] == blk[None, :]).astype(BF16)
    w['v_one'] = ((lane % HEAD_PAD) == V_HEAD).astype(F32).reshape(-1, 1)
    w['w_uk_perm'] = lw['w_uk'].transpose(0, 2, 1).reshape(KV_LORA, QK_NOPE * N_HEADS).astype(BF16)
    w['g_k_nope_perm'] = row(jnp.repeat(lw['g_k'][:QK_NOPE], N_HEADS))
    w['g_k_rope_col'] = lw['g_k'][QK_NOPE:].reshape(QK_ROPE, 1).astype(F32)
    w['w_uv_flat'] = lw['w_uv'].reshape(KV_LORA, N_HEADS * V_HEAD).astype(BF16)
    bb_re, bb_im, pow_re, pow_im = _ssm_discretize(lw['a_re'].astype(F32), lw['a_im'].astype(F32),
                                                   lw['log_dt'].astype(F32), lw['b_re'].astype(F32),
                                                   lw['b_im'].astype(F32), n_pow)
    w['bb_re'] = _block_diag_in(bb_re)
    w['bb_im'] = _block_diag_in(bb_im)
    w['pow_re'] = pow_re
    w['pow_im'] = pow_im
    w['cc_re'] = _block_diag_out(lw['c_re'].astype(F32))
    w['cc_im'] = _block_diag_out(lw['c_im'].astype(F32))
    w['d_skip'] = row(lw['d_skip'])
    w['w_glu'] = lw['w_glu'].astype(BF16)
    woa = lw['w_oa'].reshape(N_HEADS, V_HEAD, d_model)
    w['w_oa'] = jnp.pad(woa, ((0, 0), (0, HEAD_PAD - V_HEAD), (0, 0))).reshape(N_HEADS * HEAD_PAD, d_model).astype(BF16)
    w['w_oa_flat'] = lw['w_oa'].astype(BF16)
    w['w_os'] = lw['w_os'].astype(BF16)
    w['w_out'] = lw['w_out'].astype(BF16)
    w['g_ffn'] = row(lw['g_ffn'])
    w['w_up'] = lw['w_up'].astype(BF16)
    w['conv_w'] = lw['conv_w'].astype(F32)
    w['conv_b'] = row(lw['conv_b'])
    w['w_down'] = lw['w_down'].astype(BF16)
    w['g_ple'] = row(lw['g_ple'])
    w['w_ple_gate'] = lw['w_ple_gate'].astype(BF16)
    w['w_ple_proj'] = lw['w_ple_proj'].astype(BF16)
    return w


def kernel(x_prompt, x_sample, p_prompt, p_sample, cache_ckv, cache_kr, page_table, state_ssm_re, state_ssm_im,
           state_conv, g_mix, w_in, g_cq, g_ckv, w_uq, w_uk, w_uv, g_q, g_k, a_re, a_im, log_dt, b_re, b_im,
           c_re, c_im, d_skip, w_glu, w_oa, w_os, w_out, g_ffn, w_up, conv_w, conv_b, w_down, g_ple,
           w_ple_gate, w_ple_proj):
    params = dict(g_mix=g_mix, w_in=w_in, g_cq=g_cq, g_ckv=g_ckv, w_uq=w_uq, w_uk=w_uk, w_uv=w_uv, g_q=g_q,
                  g_k=g_k, a_re=a_re, a_im=a_im, log_dt=log_dt, b_re=b_re, b_im=b_im, c_re=c_re, c_im=c_im,
                  d_skip=d_skip, w_glu=w_glu, w_oa=w_oa, w_os=w_os, w_out=w_out, g_ffn=g_ffn, w_up=w_up,
                  conv_w=conv_w, conv_b=conv_b, w_down=w_down, g_ple=g_ple, w_ple_gate=w_ple_gate,
                  w_ple_proj=w_ple_proj)
    depth = w_in.shape[0]
    bp, seq, d_model = x_prompt.shape
    bd, t_new, _ = x_sample.shape
    past_len = page_table.shape[1] * PAGE_SIZE
    d_ff = w_down.shape[1]
    n_pow = max(min(SSM_ROWS, seq) // SUBLANES, 1)

    rope_prompt = _rope_tables_q(jnp.arange(seq))
    rope_sample = _rope_tables_q(jnp.repeat(past_len + jnp.arange(t_new), bd))
    rope_cache = _rope_tables_k(jnp.arange(past_len))

    yp, ys = x_prompt, x_sample
    outs = [[] for _ in range(10)]
    for i in range(depth):
        w = _layer_weights({k: v[i] for k, v in params.items()}, d_model, n_pow)

        q, k, v, ckv, kr, u, sga, sgs = _inproj(yp, w, rope_prompt, BF16, SCALE * LOG2_E)
        att = _flash_prompt(q, k, v)
        yssm, hr, hi = _ssm_prompt(u, w)
        yp, tail = _post(yp, att, yssm, sga, sgs, p_prompt[i], w)
        outs[0].append(ckv); outs[1].append(kr); outs[4].append(hr); outs[5].append(hi)
        outs[8].append(tail[:, SUBLANES - (CONV_W - 1):, :])

        n_tok = bd * t_new
        steps = lambda z: z.transpose(1, 0, 2).reshape(1, n_tok, z.shape[-1])
        entries = lambda z: z.reshape(t_new, bd, z.shape[-1]).transpose(1, 0, 2)
        q, k, v, ckv, kr, u, sga, sgs = _inproj(steps(ys), w, rope_sample, F32, SCALE)
        ckv = entries(ckv)
        att = _sample_attention(entries(q), entries(k), ckv, cache_ckv, cache_kr, page_table, i, w, rope_cache)
        yssm, hr, hi = _ssm_sample(u.reshape(t_new, bd, -1), state_ssm_re[i], state_ssm_im[i], w)
        hist0 = state_conv[i].astype(F32).transpose(1, 0, 2).reshape((CONV_W - 1) * bd, 2 * d_ff)
        w_s = dict(w, w_oa=w['w_oa_flat'])
        y2, tail = _post(steps(ys), steps(att).astype(BF16), yssm.reshape(1, n_tok, -1), sga, sgs,
                         steps(p_sample[i]), w_s, hist0=hist0, step_rows=bd)
        ys = entries(y2)
        outs[2].append(ckv); outs[3].append(entries(kr))
        outs[6].append(hr); outs[7].append(hi)
        outs[9].append(tail.reshape(CONV_W - 1, bd, 2 * d_ff).transpose(1, 0, 2))

    st = [jnp.stack(o) for o in outs]
    return (yp, ys, st[0], st[1], st[2], st[3], st[4], st[5], st[6], st[7], st[8], st[9])
```

```python
import functools
import math

import jax
import jax.numpy as jnp
from jax import lax
from jax.experimental import pallas as pl
from jax.experimental.pallas import tpu as pltpu

F32 = jnp.float32
BF16 = jnp.bfloat16

N_HEADS = 8
QK_NOPE = 64
QK_ROPE = 32
QK_HEAD = QK_NOPE + QK_ROPE
V_HEAD = 64
Q_LORA = 384
KV_LORA = 256
ROPE_THETA = 10000.0
SCALE = QK_HEAD ** -0.5
LOG2_E = math.log2(math.e)
NEG_INF = -1e30
SSM_WIDTH = 512
GROUP = 16
N_GROUPS = SSM_WIDTH // GROUP
STATE = 64
N_STATE = N_GROUPS * STATE
CONV_W = 3
EPS = 1e-6
PAGE_SIZE = 128

LANES = 128
SUBLANES = 8
HEAD_PAD = LANES
NEW_PAD = 2 * SUBLANES
V_ROWS = V_HEAD + 2 * SUBLANES
VMEM_LIMIT_BYTES = 56 * 1024 * 1024

INPROJ_ROWS = 512
FLASH_T = 512
FLASH_HEADS = 2
SSM_ROWS = 256
SSM_COLS = 1024
POST_ROWS = 256
FFN_CHUNK = 256
FFN_AHEAD = 4
SAMPLE_CHUNK_PAGES = 8
SAMPLE_AHEAD = 2
SSM_GROUP_BLOCK = 8


def _cparams(sem):
    return pltpu.CompilerParams(dimension_semantics=sem, vmem_limit_bytes=VMEM_LIMIT_BYTES)


def _const_spec(shape):
    nd = len(shape)
    return pl.BlockSpec(shape, lambda *_: (0,) * nd, pipeline_mode=pl.Buffered(1))


def _rms(x, g):
    return x * lax.rsqrt(jnp.mean(x * x, axis=-1, keepdims=True) + EPS) * g


def _bdot(a, b):
    return jnp.dot(a.astype(BF16), b, preferred_element_type=F32)


def _dot_nt(a, b):
    return lax.dot_general(a, b, (((1,), (1,)), ((), ())), preferred_element_type=F32)


def _ssm_disc_kernel(are_ref, aim_ref, ldt_ref, arer_ref, aimr_ref, ldtr_ref, bre_ref, bim_ref,
                     bbre_ref, bbim_ref, pre_ref, pim_ref, *, n_pow):
    def zoh(a_re, a_im, ldt):
        dt = jnp.exp(ldt)
        mag = jnp.exp(dt * a_re)
        ab_re = mag * jnp.cos(dt * a_im)
        ab_im = mag * jnp.sin(dt * a_im)
        return ab_re, ab_im

    a_re = arer_ref[...]
    a_im = aimr_ref[...]
    ab_re, ab_im = zoh(a_re, a_im, ldtr_ref[...])
    den = a_re * a_re + a_im * a_im
    nr = ab_re - 1.0
    f_re = (nr * a_re + ab_im * a_im) / den
    f_im = (ab_im * a_re - nr * a_im) / den
    b_re = bre_ref[...]
    b_im = bim_ref[...]
    bbre_ref[...] = f_re * b_re - f_im * b_im
    bbim_ref[...] = f_re * b_im + f_im * b_re

    p_re, p_im = zoh(are_ref[...], aim_ref[...], ldt_ref[...])
    c_re, c_im = p_re, p_im
    for j in range(n_pow):
        pre_ref[j] = c_re
        pim_ref[j] = c_im
        c_re, c_im = c_re * p_re - c_im * p_im, c_re * p_im + c_im * p_re


def _ssm_discretize(a_re, a_im, log_dt, b_re, b_im, n_pow):
    g, p = a_re.shape
    rows = g * GROUP
    rep = lambda z: jnp.repeat(z, GROUP, axis=0)
    bt = lambda z: z.transpose(0, 2, 1).reshape(rows, p)
    ldt = log_dt.reshape(g, 1)
    out_shape = (jax.ShapeDtypeStruct((rows, p), F32), jax.ShapeDtypeStruct((rows, p), F32),
                 jax.ShapeDtypeStruct((n_pow, g, p), F32), jax.ShapeDtypeStruct((n_pow, g, p), F32))
    bb_re, bb_im, pow_re, pow_im = pl.pallas_call(
        functools.partial(_ssm_disc_kernel, n_pow=n_pow),
        out_shape=out_shape, name="ssm_disc",
    )(a_re, a_im, ldt, rep(a_re), rep(a_im), rep(ldt), bt(b_re), bt(b_im))
    return bb_re, bb_im, pow_re.reshape(n_pow, g * p), pow_im.reshape(n_pow, g * p)


def _block_diag_in(bb):
    nb = N_GROUPS // SSM_GROUP_BLOCK
    z = bb.reshape(nb, SSM_GROUP_BLOCK, GROUP, STATE)
    eye = jnp.eye(SSM_GROUP_BLOCK, dtype=bb.dtype)
    out = jnp.einsum('kgip,gh->kgihp', z, eye)
    return out.reshape(nb, SSM_GROUP_BLOCK * GROUP, SSM_GROUP_BLOCK * STATE).astype(BF16)


def _block_diag_out(c):
    nb = N_GROUPS // SSM_GROUP_BLOCK
    z = c.reshape(nb, SSM_GROUP_BLOCK, GROUP, STATE)
    eye = jnp.eye(SSM_GROUP_BLOCK, dtype=c.dtype)
    out = jnp.einsum('kgip,gh->kgphi', z, eye)
    return out.reshape(nb, SSM_GROUP_BLOCK * STATE, SSM_GROUP_BLOCK * GROUP).astype(BF16)


OFF_CQ = 0
OFF_CKV = OFF_CQ + Q_LORA
OFF_U = OFF_CKV + KV_LORA


def _head_sumsq(z, ones2_ref):
    sq = (z * z).astype(BF16)
    span = ones2_ref.shape[0]
    parts = [jnp.dot(sq[:, j:j + span], ones2_ref[...], preferred_element_type=F32)
             for j in range(0, z.shape[-1], span)]
    return jnp.concatenate(parts, axis=-1)


def _inproj_kernel(x_ref, gmix_ref, win_ref, gcq_ref, gckv_ref, wuq_ref, wuqr_ref, wuk_ref, ekr_ref, swap_ref,
                   ones2_ref, wuvt_ref, vone_ref, gq_ref, gkn_ref, gkr_ref, rc_ref, rs_ref, cc_ref, cs_ref,
                   q_ref, k_ref, v_ref, ckv_ref, kr_ref, u_ref, sga_ref, sgs_ref, *, d_model, q_scale):
    off_ga = OFF_U + SSM_WIDTH
    off_gs = off_ga + d_model
    off_kr = off_gs + d_model
    inv = 1.0 / QK_HEAD
    blocks = [slice(h * HEAD_PAD, (h + 1) * HEAD_PAD) for h in range(N_HEADS)]
    xn = _rms(x_ref[...], gmix_ref[...])
    proj = _bdot(xn, win_ref[...])
    cq = _rms(proj[:, OFF_CQ:OFF_CKV], gcq_ref[...])
    ckv = _rms(proj[:, OFF_CKV:OFF_U], gckv_ref[...])
    kr = proj[:, off_kr:off_kr + QK_ROPE]
    u_ref[...] = proj[:, OFF_U:off_ga]
    sga_ref[...] = jax.nn.sigmoid(proj[:, off_ga:off_gs]).astype(sga_ref.dtype)
    sgs_ref[...] = jax.nn.sigmoid(proj[:, off_gs:off_kr]).astype(sgs_ref.dtype)
    ckv_ref[...] = ckv
    kr_ref[...] = kr
    ckv_b = ckv.astype(BF16)
    v_ref[...] = (_dot_nt(wuvt_ref[...], ckv_b) + vone_ref[...]).astype(v_ref.dtype)

    cq_b = cq.astype(BF16)
    qf = jnp.dot(cq_b, wuq_ref[...], preferred_element_type=F32)
    qrot = jnp.dot(cq_b, wuqr_ref[...], preferred_element_type=F32)
    rn_q = lax.rsqrt(_head_sumsq(qf, ones2_ref) * inv + EPS)
    tc = rc_ref[...] * (gq_ref[...] * q_scale)
    ts = rs_ref[...] * q_scale
    for blk in blocks:
        q_ref[:, blk] = (rn_q[:, blk] * (qf[:, blk] * tc + qrot[:, blk] * ts)).astype(q_ref.dtype)

    kn = jnp.dot(ckv_b, wuk_ref[...], preferred_element_type=F32)
    krg = kr * gkr_ref[...]
    krr = krg * cc_ref[...] + _bdot(krg, swap_ref[...]) * cs_ref[...]
    placed = _bdot(krr, ekr_ref[...])
    r2 = jnp.sum(kr * kr, axis=-1, keepdims=True)
    rn_k = lax.rsqrt((_head_sumsq(kn, ones2_ref) + r2) * inv + EPS)
    for blk in blocks:
        k_ref[:, blk] = (rn_k[:, blk] * (kn[:, blk] * gkn_ref[...] + placed[:, blk])).astype(k_ref.dtype)


def _inproj(x, w, rope, act_dtype, q_scale):
    nb, rows, d = x.shape
    tr = min(INPROJ_ROWS, rows)
    assert rows % tr == 0
    hp = N_HEADS * HEAD_PAD
    row_spec = lambda n: pl.BlockSpec((None, tr, n), lambda b, t: (b, t, 0))
    tab_specs = [pl.BlockSpec((tr, z.shape[-1]), lambda b, t: (t, 0)) for z in rope]
    weights = (w['g_mix'], w['w_in'], w['g_cq'], w['g_ckv'], w['w_uq'], w['w_uq_rot'], w['w_uk'], w['e_kr'],
               w['swap_rope'], w['ones2'], w['w_uv_t'], w['v_one'], w['g_q'], w['g_k_nope'], w['g_k_rope'])
    out_shape = (
        jax.ShapeDtypeStruct((nb, rows, hp), act_dtype),
        jax.ShapeDtypeStruct((nb, rows, hp), act_dtype),
        jax.ShapeDtypeStruct((nb, hp, rows), act_dtype),
        jax.ShapeDtypeStruct((nb, rows, KV_LORA), F32),
        jax.ShapeDtypeStruct((nb, rows, QK_ROPE), F32),
        jax.ShapeDtypeStruct((nb, rows, SSM_WIDTH), F32),
        jax.ShapeDtypeStruct((nb, rows, d), BF16),
        jax.ShapeDtypeStruct((nb, rows, d), BF16),
    )
    out_specs = [row_spec(s.shape[-1]) for s in out_shape]
    out_specs[2] = pl.BlockSpec((None, hp, tr), lambda b, t: (b, 0, t))
    return pl.pallas_call(
        functools.partial(_inproj_kernel, d_model=d, q_scale=q_scale),
        grid=(nb, rows // tr),
        in_specs=[row_spec(d)] + [_const_spec(z.shape) for z in weights] + tab_specs,
        out_specs=tuple(out_specs),
        out_shape=out_shape,
        compiler_params=_cparams(("parallel", "parallel")),
        name="inproj",
    )(x, *weights, *rope)


def _flash_kernel(q_ref, k_ref, vt_ref, o_ref, s0, s1, m_sc, acc_sc, *, t, nh):
    qi = pl.program_id(2)
    heads = [slice(h * HEAD_PAD, (h + 1) * HEAD_PAD) for h in range(nh)]
    m_sc[...] = jnp.full(m_sc.shape, NEG_INF, F32)
    acc_sc[...] = jnp.zeros(acc_sc.shape, F32)

    def scores(kj, dst):
        start = pl.multiple_of(kj * t, t)
        for h, hs in enumerate(heads):
            dst[h] = _dot_nt(k_ref[pl.ds(start, t), hs], q_ref[:, hs])

    def fold(kj, src, diagonal):
        start = pl.multiple_of(kj * t, t)
        if diagonal:
            keep = lax.broadcasted_iota(jnp.int32, (t, t), 0) <= lax.broadcasted_iota(jnp.int32, (t, t), 1)
        for h, hs in enumerate(heads):
            st = src[h]
            if diagonal:
                st = jnp.where(keep, st, NEG_INF)
            m = m_sc[h]
            m_new = jnp.maximum(m, jnp.max(st, axis=0, keepdims=True))
            p = jnp.exp2(st - m_new)
            corr = jnp.exp2(m - m_new)
            vrows = slice(h * HEAD_PAD, h * HEAD_PAD + V_ROWS)
            acc_sc[h] = acc_sc[h] * corr + jnp.dot(vt_ref[vrows, pl.ds(start, t)], p.astype(BF16),
                                                   preferred_element_type=F32)
            m_sc[h] = m_new

    scores(0, s0)

    def pair(kk, carry):
        j = 2 * kk
        scores(j + 1, s1)
        fold(j, s0, False)
        scores(j + 2, s0)
        fold(j + 1, s1, False)
        return carry

    lax.fori_loop(0, qi // 2, pair, 0)

    @pl.when(qi % 2 == 0)
    def _():
        fold(qi, s0, True)

    @pl.when(qi % 2 == 1)
    def _():
        scores(qi, s1)
        fold(qi - 1, s0, False)
        fold(qi, s1, True)

    for h, hs in enumerate(heads):
        acc = acc_sc[h]
        out = jnp.concatenate([acc / acc[V_HEAD:V_HEAD + 1, :], jnp.zeros((HEAD_PAD - V_ROWS, t), F32)], axis=0)
        o_ref[:, hs] = jnp.transpose(out).astype(o_ref.dtype)


def _flash_prompt(q, k, vt):
    b, s, hp = q.shape
    tq = min(FLASH_T, s)
    nh = FLASH_HEADS
    assert s % tq == 0 and N_HEADS % nh == 0
    return pl.pallas_call(
        functools.partial(_flash_kernel, t=tq, nh=nh),
        grid=(b, N_HEADS // nh, s // tq),
        in_specs=[pl.BlockSpec((None, tq, nh * HEAD_PAD), lambda bi, h, qi: (bi, qi, h)),
                  pl.BlockSpec((None, s, nh * HEAD_PAD), lambda bi, h, qi: (bi, 0, h)),
                  pl.BlockSpec((None, nh * HEAD_PAD, s), lambda bi, h, qi: (bi, h, 0))],
        out_specs=pl.BlockSpec((None, tq, nh * HEAD_PAD), lambda bi, h, qi: (bi, qi, h)),
        out_shape=jax.ShapeDtypeStruct((b, s, hp), BF16),
        scratch_shapes=[pltpu.VMEM((nh, tq, tq), F32), pltpu.VMEM((nh, tq, tq), F32),
                        pltpu.VMEM((nh, 1, tq), F32), pltpu.VMEM((nh, V_ROWS, tq), F32)],
        compiler_params=_cparams(("parallel", "parallel", "arbitrary")),
        name="flash_prompt",
    )(q, k, vt)


def _sample_attn_kernel(pt_ref, qn_ref, qr_ref, qrs_ref, qf_ref, knew_ref, cnew_ref, ca_ref, cb_ref,
                        wukp_ref, gkn_ref, gkr_ref, wuv_ref, hmask_ref, ckv_hbm, kr_hbm, o_ref,
                        cf, rf, cbuf, sem_c, sem_r, *, layer, n_pages, n_new, chunk_pages):
    b = pl.program_id(0)
    slot = b % 2
    nq = n_new * N_HEADS
    keys = chunk_pages * PAGE_SIZE

    def page_copies(entry, sl, j):
        pg = pt_ref[entry * n_pages + j]
        return (pltpu.make_async_copy(ckv_hbm.at[layer, pg], cf.at[sl, j], sem_c.at[sl]),
                pltpu.make_async_copy(kr_hbm.at[layer, pg], rf.at[sl, j], sem_r.at[sl]))

    def start_entry(entry, sl):
        def body(j, carry):
            for cp in page_copies(entry, sl, j):
                cp.start()
            return carry
        lax.fori_loop(0, n_pages, body, 0, unroll=min(8, n_pages))

    @pl.when(b == 0)
    def _():
        start_entry(0, 0)

    @pl.when(b + 1 < pl.num_programs(0))
    def _():
        start_entry(b + 1, 1 - slot)

    pltpu.make_async_copy(ckv_hbm.at[layer, pl.ds(0, n_pages)], cf.at[slot], sem_c.at[slot]).wait()
    pltpu.make_async_copy(kr_hbm.at[layer, pl.ds(0, n_pages)], rf.at[slot], sem_r.at[slot]).wait()

    qn = (qn_ref[...] * gkn_ref[...]).astype(BF16)
    qabs = _dot_nt(qn, wukp_ref[...]).astype(BF16)

    zrows = LANES - NEW_PAD
    knew = jnp.concatenate([knew_ref[...].astype(BF16), jnp.zeros((zrows, knew_ref.shape[-1]), BF16)], axis=0)
    cnew = jnp.concatenate([cnew_ref[...].astype(BF16), jnp.zeros((zrows, KV_LORA), BF16)], axis=0)
    s_new = _dot_nt(qf_ref[...].astype(BF16), knew)
    tok = lax.broadcasted_iota(jnp.int32, (nq, LANES), 0) // N_HEADS
    key = lax.broadcasted_iota(jnp.int32, (nq, LANES), 1)
    scored = [(jnp.where(key <= tok, s_new, NEG_INF), cnew)]

    qr = qr_ref[...].astype(BF16)
    qrs = qrs_ref[...].astype(BF16)
    def front(ch):
        span = slice(ch * keys, (ch + 1) * keys)
        for i in range(ch * chunk_pages, (ch + 1) * chunk_pages):
            cbuf[i * PAGE_SIZE:(i + 1) * PAGE_SIZE, :] = cf[slot, i].astype(BF16)
        c = cbuf[span, :]
        kn = jnp.dot(c, wukp_ref[...], preferred_element_type=F32)
        sq = kn * kn
        ss = sq[:, 0:LANES]
        for j in range(1, N_HEADS * QK_NOPE // LANES):
            ss = ss + sq[:, j * LANES:(j + 1) * LANES]
        return ch, c, ss, _dot_nt(qabs, c)

    def back(args):
        ch, c, ss, s_nope = args
        span = slice(ch * keys, (ch + 1) * keys)
        ss_t = jnp.transpose(ss)
        ss_h = ss_t[0:N_HEADS]
        for j in range(1, LANES // N_HEADS):
            ss_h = ss_h + ss_t[j * N_HEADS:(j + 1) * N_HEADS]
        r_t = jnp.concatenate([rf[slot, i] for i in range(ch * chunk_pages, (ch + 1) * chunk_pages)], axis=1)
        r2 = jnp.sum(r_t * r_t, axis=0, keepdims=True)
        rn_h = lax.rsqrt((ss_h + r2) * (1.0 / QK_HEAD) + EPS)
        rn = jnp.concatenate([rn_h] * n_new, axis=0)
        rg = r_t * gkr_ref[...]
        s_rope = (jnp.dot(qr, (rg * ca_ref[:, span]).astype(BF16), preferred_element_type=F32)
                  + jnp.dot(qrs, (rg * cb_ref[:, span]).astype(BF16), preferred_element_type=F32))
        return rn * (s_nope + s_rope), c

    def weights(s):
        m = jnp.max(s, axis=-1, keepdims=True)
        p = jnp.exp(s - m)
        return m, p.astype(BF16), jnp.sum(p, axis=-1, keepdims=True)

    n_chunks = n_pages // chunk_pages
    ahead = [front(ch) for ch in range(min(SAMPLE_AHEAD, n_chunks))]
    pending = [(weights(scored[0][0]), scored[0][1])]
    parts = []

    def value_product(item):
        (m, p, l), vals = item
        parts.append((m, l, jnp.dot(p, vals, preferred_element_type=F32)))

    for ch in range(n_chunks):
        cur = ahead.pop(0)
        if ch + SAMPLE_AHEAD < n_chunks:
            ahead.append(front(ch + SAMPLE_AHEAD))
        s, vals = back(cur)
        pending.append((weights(s), vals))
        value_product(pending.pop(0))
    while pending:
        value_product(pending.pop(0))

    m = parts[0][0]
    for pm, _, _ in parts[1:]:
        m = jnp.maximum(m, pm)
    l = jnp.zeros_like(m)
    acc = jnp.zeros((nq, KV_LORA), F32)
    for pm, pl_, pacc in parts:
        wgt = jnp.exp(pm - m)
        l = l + pl_ * wgt
        acc = acc + pacc * wgt
    o_lat = acc / l
    full = _bdot(o_lat, wuv_ref[...]) * hmask_ref[...]
    o_ref[...] = jnp.sum(full.reshape(n_new, N_HEADS, N_HEADS * V_HEAD), axis=1)


def _sample_attention(q, k, ckv, cache_ckv, cache_kr, page_table, layer, w, rope_k):
    bd, t, hp = q.shape
    n_logical = page_table.shape[1]
    chunk_pages = min(SAMPLE_CHUNK_PAGES, n_logical)
    assert n_logical % chunk_pages == 0 and t <= NEW_PAD
    nq = t * N_HEADS
    n_keys = n_logical * PAGE_SIZE
    head_mask = ((jnp.arange(nq) % N_HEADS)[:, None] == (jnp.arange(N_HEADS * V_HEAD) // V_HEAD)[None, :]).astype(F32)
    qh = q.reshape(bd, t, N_HEADS, HEAD_PAD)
    eye = jnp.eye(N_HEADS, dtype=F32)

    qn = jnp.einsum('bthd,hk->bthdk', qh[..., :QK_NOPE], eye).reshape(bd, nq, QK_NOPE * N_HEADS)
    qr = qh[..., QK_NOPE:QK_HEAD].reshape(bd, nq, QK_ROPE)
    qrs = jnp.concatenate([qr[..., QK_ROPE // 2:], qr[..., :QK_ROPE // 2]], axis=-1)
    qf = jnp.einsum('bthl,hk->bthkl', qh, eye).reshape(bd, nq, hp)
    pad_t = lambda z: jnp.pad(z, ((0, 0), (0, NEW_PAD - t), (0, 0)))
    knew = pad_t(k)
    cnew = pad_t(ckv)
    ca, cb = rope_k
    cache_kr_t = jnp.swapaxes(cache_kr, 2, 3)

    def bspec(rows, n):
        return pl.BlockSpec((None, rows, n), lambda b, pt: (b, 0, 0))

    def wspec(shape):
        nd = len(shape)
        return pl.BlockSpec(shape, lambda b, pt: (0,) * nd)

    weights = (ca, cb, w['w_uk_perm'], w['g_k_nope_perm'], w['g_k_rope_col'], w['w_uv_flat'], head_mask)
    in_specs = ([bspec(nq, QK_NOPE * N_HEADS), bspec(nq, QK_ROPE), bspec(nq, QK_ROPE), bspec(nq, hp),
                 bspec(NEW_PAD, hp), bspec(NEW_PAD, KV_LORA)]
                + [wspec(z.shape) for z in weights]
                + [pl.BlockSpec(memory_space=pl.ANY), pl.BlockSpec(memory_space=pl.ANY)])
    grid_spec = pltpu.PrefetchScalarGridSpec(
        num_scalar_prefetch=1, grid=(bd,), in_specs=in_specs,
        out_specs=pl.BlockSpec((None, t, N_HEADS * V_HEAD), lambda b, pt: (b, 0, 0)),
        scratch_shapes=[pltpu.VMEM((2, n_logical, PAGE_SIZE, KV_LORA), F32),
                        pltpu.VMEM((2, n_logical, QK_ROPE, PAGE_SIZE), F32),
                        pltpu.VMEM((n_keys, KV_LORA), BF16),
                        pltpu.SemaphoreType.DMA((2,)), pltpu.SemaphoreType.DMA((2,))])
    return pl.pallas_call(
        functools.partial(_sample_attn_kernel, layer=layer, n_pages=n_logical, n_new=t, chunk_pages=chunk_pages),
        grid_spec=grid_spec,
        out_shape=jax.ShapeDtypeStruct((bd, t, N_HEADS * V_HEAD), F32),
        compiler_params=_cparams(("arbitrary",)),
        name="sample_attn",
    )(page_table.reshape(-1), qn, qr, qrs, qf, knew, cnew, *weights, cache_ckv, cache_kr_t)


def _ssm_in(u_b, bre_ref, bim_ref, store_re, store_im):
    nb = bre_ref.shape[0]
    kw = bre_ref.shape[1]
    nw = bre_ref.shape[2]
    for kb in range(nb):
        blk = u_b[:, kb * kw:(kb + 1) * kw]
        store_re(kb * nw, nw, jnp.dot(blk, bre_ref[kb], preferred_element_type=F32))
        store_im(kb * nw, nw, jnp.dot(blk, bim_ref[kb], preferred_element_type=F32))


def _ssm_out(h_re, h_im, cre_ref, cim_ref):
    nb = cre_ref.shape[0]
    kw = cre_ref.shape[1]
    parts = []
    for kb in range(nb):
        parts.append(jnp.dot(h_re[:, kb * kw:(kb + 1) * kw].astype(BF16), cre_ref[kb], preferred_element_type=F32)
                     - jnp.dot(h_im[:, kb * kw:(kb + 1) * kw].astype(BF16), cim_ref[kb], preferred_element_type=F32))
    return jnp.concatenate(parts, axis=-1)


def _glu(y, wglu_ref):
    g = _bdot(jax.nn.gelu(y), wglu_ref[...])
    half = g.shape[-1] // 2
    return g[:, :half] * jax.nn.sigmoid(g[:, half:])


def _ssm_prompt_kernel(u_ref, perm_ref, permt_ref, bre_ref, bim_ref, pre_ref, pim_ref, cre_ref, cim_ref,
                       dskip_ref, wglu_ref, ys_ref, hre_out, him_out,
                       hre, him, car_re, car_im, hin_re, hin_im, *, rows, sub):
    ti = pl.program_id(1)

    @pl.when(ti == 0)
    def _():
        car_re[...] = jnp.zeros(car_re.shape, F32)
        car_im[...] = jnp.zeros(car_im.shape, F32)

    u = u_ref[...]
    u_hi = u.astype(BF16)
    u_lo = (u - u_hi.astype(F32)).astype(BF16)
    perm = perm_ref[...]
    up = jnp.dot(perm, u_hi, preferred_element_type=F32) + jnp.dot(perm, u_lo, preferred_element_type=F32)

    def st_re(c0, n, val):
        hre[:, c0:c0 + n] = val

    def st_im(c0, n, val):
        him[:, c0:c0 + n] = val

    _ssm_in(up.astype(BF16), bre_ref, bim_ref, st_re, st_im)

    n_cb = N_STATE // SSM_COLS
    for cb in range(n_cb):
        cols = slice(cb * SSM_COLS, (cb + 1) * SSM_COLS)
        a_re = jnp.broadcast_to(pre_ref[0:1, cols], (SUBLANES, SSM_COLS))
        a_im = jnp.broadcast_to(pim_ref[0:1, cols], (SUBLANES, SSM_COLS))

        def local(j, st):
            s_re, s_im = st
            r0 = pl.multiple_of(j * SUBLANES, SUBLANES)
            n_re = a_re * s_re - a_im * s_im + hre[pl.ds(r0, SUBLANES), cols]
            n_im = a_re * s_im + a_im * s_re + him[pl.ds(r0, SUBLANES), cols]
            hre[pl.ds(r0, SUBLANES), cols] = n_re
            him[pl.ds(r0, SUBLANES), cols] = n_im
            return n_re, n_im

        zero = jnp.zeros((SUBLANES, SSM_COLS), F32)
        e_re, e_im = lax.fori_loop(0, sub, local, (zero, zero))

        as_re = pre_ref[sub - 1:sub, cols]
        as_im = pim_ref[sub - 1:sub, cols]
        c_re = car_re[:, cols]
        c_im = car_im[:, cols]
        for s in range(SUBLANES):
            hin_re[s:s + 1, cols] = c_re
            hin_im[s:s + 1, cols] = c_im
            c_re, c_im = (as_re * c_re - as_im * c_im + e_re[s:s + 1],
                          as_re * c_im + as_im * c_re + e_im[s:s + 1])
        car_re[:, cols] = c_re
        car_im[:, cols] = c_im
        g_re = hin_re[:, cols]
        g_im = hin_im[:, cols]

        def fix(j, d):
            d_re, d_im = d
            d_re, d_im = a_re * d_re - a_im * d_im, a_re * d_im + a_im * d_re
            r0 = pl.multiple_of(j * SUBLANES, SUBLANES)
            hre[pl.ds(r0, SUBLANES), cols] = hre[pl.ds(r0, SUBLANES), cols] + d_re
            him[pl.ds(r0, SUBLANES), cols] = him[pl.ds(r0, SUBLANES), cols] + d_im
            return d_re, d_im

        lax.fori_loop(0, sub, fix, (g_re, g_im))

    y = _ssm_out(hre[...], him[...], cre_ref, cim_ref) + dskip_ref[...] * up
    ys = _glu(y, wglu_ref).astype(BF16)
    ys_ref[...] = jnp.dot(permt_ref[...], ys, preferred_element_type=F32).astype(ys_ref.dtype)

    @pl.when(ti == pl.num_programs(1) - 1)
    def _():
        hre_out[...] = car_re[...]
        him_out[...] = car_im[...]


def _ssm_prompt(u, w):
    b, s, width = u.shape
    rows = min(SSM_ROWS, s)
    assert s % rows == 0 and rows % SUBLANES == 0
    sub = rows // SUBLANES
    r = jnp.arange(rows)
    t_of_r = (r % SUBLANES) * sub + r // SUBLANES
    perm = (t_of_r[:, None] == jnp.arange(rows)[None, :]).astype(BF16)
    pre = w['pow_re'][:sub]
    pim = w['pow_im'][:sub]
    weights = (perm, perm.T, w['bb_re'], w['bb_im'], pre, pim, w['cc_re'], w['cc_im'], w['d_skip'], w['w_glu'])
    ys, h_re, h_im = pl.pallas_call(
        functools.partial(_ssm_prompt_kernel, rows=rows, sub=sub),
        grid=(b, s // rows),
        in_specs=[pl.BlockSpec((None, rows, width), lambda bi, t: (bi, t, 0))]
                 + [_const_spec(z.shape) for z in weights],
        out_specs=(pl.BlockSpec((None, rows, width), lambda bi, t: (bi, t, 0)),
                   pl.BlockSpec((None, 1, N_STATE), lambda bi, t: (bi, 0, 0)),
                   pl.BlockSpec((None, 1, N_STATE), lambda bi, t: (bi, 0, 0))),
        out_shape=(jax.ShapeDtypeStruct((b, s, width), BF16),
                   jax.ShapeDtypeStruct((b, 1, N_STATE), F32),
                   jax.ShapeDtypeStruct((b, 1, N_STATE), F32)),
        scratch_shapes=[pltpu.VMEM((rows, N_STATE), F32), pltpu.VMEM((rows, N_STATE), F32),
                        pltpu.VMEM((1, N_STATE), F32), pltpu.VMEM((1, N_STATE), F32),
                        pltpu.VMEM((SUBLANES, N_STATE), F32), pltpu.VMEM((SUBLANES, N_STATE), F32)],
        compiler_params=_cparams(("parallel", "arbitrary")),
        name="ssm_prompt",
    )(u, *weights)
    return ys, h_re.reshape(b, N_GROUPS, STATE), h_im.reshape(b, N_GROUPS, STATE)


def _ssm_sample_kernel(u_ref, h0re_ref, h0im_ref, bre_ref, bim_ref, pre_ref, pim_ref, cre_ref, cim_ref,
                       dskip_ref, wglu_ref, ys_ref, hre_out, him_out, bu_re, bu_im, *, steps):
    a_re = pre_ref[0:1, :]
    a_im = pim_ref[0:1, :]
    h_re = h0re_ref[...]
    h_im = h0im_ref[...]

    def st_re(c0, n, val):
        bu_re[:, c0:c0 + n] = val

    def st_im(c0, n, val):
        bu_im[:, c0:c0 + n] = val

    for t in range(steps):
        u = u_ref[t]
        _ssm_in(u.astype(BF16), bre_ref, bim_ref, st_re, st_im)
        h_re, h_im = (a_re * h_re - a_im * h_im + bu_re[...], a_re * h_im + a_im * h_re + bu_im[...])
        y = _ssm_out(h_re, h_im, cre_ref, cim_ref) + dskip_ref[...] * u
        ys_ref[t] = _glu(y, wglu_ref).astype(ys_ref.dtype)
    hre_out[...] = h_re
    him_out[...] = h_im


def _ssm_sample(ut, h0_re, h0_im, w):
    t, bd, width = ut.shape
    ys, h_re, h_im = pl.pallas_call(
        functools.partial(_ssm_sample_kernel, steps=t),
        out_shape=(jax.ShapeDtypeStruct((t, bd, width), BF16),
                   jax.ShapeDtypeStruct((bd, N_STATE), F32),
                   jax.ShapeDtypeStruct((bd, N_STATE), F32)),
        scratch_shapes=[pltpu.VMEM((bd, N_STATE), F32), pltpu.VMEM((bd, N_STATE), F32)],
        compiler_params=pltpu.CompilerParams(vmem_limit_bytes=VMEM_LIMIT_BYTES),
        name="ssm_sample",
    )(ut, h0_re.reshape(bd, N_STATE).astype(F32), h0_im.reshape(bd, N_STATE).astype(F32),
      w['bb_re'], w['bb_im'], w['pow_re'][:1], w['pow_im'][:1], w['cc_re'], w['cc_im'], w['d_skip'], w['w_glu'])
    return ys, h_re.reshape(bd, N_GROUPS, STATE), h_im.reshape(bd, N_GROUPS, STATE)


def _post_kernel(*refs, rows, d_ff, row_major):
    if row_major:
        (x_ref, att_ref, ys_ref, sga_ref, sgs_ref, p_ref, woa_ref, wos_ref, wout_ref, gffn_ref, wup_ref,
         cw_ref, cbias_ref, wdown_ref, gple_ref, wpg_ref, wpp_ref, y_ref, tail_ref, hist) = refs
        hist0_ref = None
    else:
        (x_ref, att_ref, ys_ref, sga_ref, sgs_ref, p_ref, hist0_ref, woa_ref, wos_ref, wout_ref, gffn_ref, wup_ref,
         cw_ref, cbias_ref, wdown_ref, gple_ref, wpg_ref, wpp_ref, y_ref, tail_ref, hist) = refs
    ti = pl.program_id(1)
    pad = SUBLANES

    mixed = (sga_ref[...].astype(F32) * jnp.dot(att_ref[...], woa_ref[...], preferred_element_type=F32)
             + sgs_ref[...].astype(F32) * jnp.dot(ys_ref[...], wos_ref[...], preferred_element_type=F32))
    x1 = x_ref[...] + _bdot(mixed, wout_ref[...])
    xn = _rms(x1, gffn_ref[...]).astype(BF16)

    @pl.when(ti == 0)
    def _():
        if row_major:
            hist[0:pad, :] = jnp.zeros((pad, 2 * d_ff), F32)
        else:
            hist[...] = hist0_ref[...]

    def up_cols(c0):
        cols = slice(c0, c0 + FFN_CHUNK)
        up = jnp.dot(xn, wup_ref[:, cols], preferred_element_type=F32)
        if row_major:
            hist[pad:pad + rows, cols] = up
            s1 = hist[pad - 1:pad - 1 + rows, cols]
            s2 = hist[pad - 2:pad - 2 + rows, cols]
        else:
            s2 = hist[0:rows, cols]
            s1 = hist[rows:2 * rows, cols]
            hist[0:rows, cols] = s1
            hist[rows:2 * rows, cols] = up
        return cols, up, s1, s2

    def conv(args):
        cols, up, s1, s2 = args
        return cbias_ref[:, cols] + s2 * cw_ref[0:1, cols] + s1 * cw_ref[1:2, cols] + up * cw_ref[2:3, cols]

    n_chunks = d_ff // FFN_CHUNK
    ahead = [(up_cols(c * FFN_CHUNK), up_cols(d_ff + c * FFN_CHUNK)) for c in range(min(FFN_AHEAD, n_chunks))]
    f = jnp.zeros((rows, x1.shape[-1]), F32)
    for c in range(n_chunks):
        cur = ahead.pop(0)
        if c + FFN_AHEAD < n_chunks:
            ahead.append((up_cols((c + FFN_AHEAD) * FFN_CHUNK), up_cols(d_ff + (c + FFN_AHEAD) * FFN_CHUNK)))
        hdn = (jax.nn.gelu(conv(cur[0])) * conv(cur[1])).astype(BF16)
        f = f + jnp.dot(hdn, wdown_ref[c * FFN_CHUNK:(c + 1) * FFN_CHUNK, :], preferred_element_type=F32)
    x2 = x1 + f

    if row_major:
        tail = hist[rows:rows + pad, :]
        hist[0:pad, :] = tail
        tail_ref[...] = tail
    else:
        @pl.when(ti == pl.num_programs(1) - 1)
        def _():
            tail_ref[...] = hist[...]

    gate = jax.nn.sigmoid(_bdot(_rms(x2, gple_ref[...]), wpg_ref[...]))
    y_ref[...] = x2 + gate * _bdot(p_ref[...], wpp_ref[...])


def _post(x, att, ys, sga, sgs, p, w, hist0=None, step_rows=None):
    nb, total, d = x.shape
    d_ff = w['w_down'].shape[0]
    assert d_ff % FFN_CHUNK == 0
    row_major = hist0 is None
    rows = min(POST_ROWS, total) if row_major else step_rows
    assert total % rows == 0
    row_spec = lambda n: pl.BlockSpec((None, rows, n), lambda b, t: (b, t, 0))
    acts = [x, att, ys, sga, sgs, p]
    act_specs = [row_spec(z.shape[-1]) for z in acts]
    weights = (w['w_oa'], w['w_os'], w['w_out'], w['g_ffn'], w['w_up'], w['conv_w'], w['conv_b'], w['w_down'],
               w['g_ple'], w['w_ple_gate'], w['w_ple_proj'])
    if row_major:
        hist_rows = SUBLANES
        tail_spec = pl.BlockSpec((None, hist_rows, 2 * d_ff), lambda b, t: (b, 0, 0))
        tail_shape = jax.ShapeDtypeStruct((nb, hist_rows, 2 * d_ff), F32)
        scratch = pltpu.VMEM((rows + hist_rows, 2 * d_ff), F32)
    else:
        assert nb == 1
        hist_rows = 2 * rows
        acts.append(hist0)
        act_specs.append(_const_spec(hist0.shape))
        tail_spec = pl.BlockSpec((hist_rows, 2 * d_ff), lambda b, t: (0, 0), pipeline_mode=pl.Buffered(1))
        tail_shape = jax.ShapeDtypeStruct((hist_rows, 2 * d_ff), F32)
        scratch = pltpu.VMEM((hist_rows, 2 * d_ff), F32)
    return pl.pallas_call(
        functools.partial(_post_kernel, rows=rows, d_ff=d_ff, row_major=row_major),
        grid=(nb, total // rows),
        in_specs=act_specs + [_const_spec(z.shape) for z in weights],
        out_specs=(row_spec(d), tail_spec),
        out_shape=(jax.ShapeDtypeStruct((nb, total, d), F32), tail_shape),
        scratch_shapes=[scratch],
        compiler_params=_cparams(("parallel", "arbitrary")),
        name="post_prompt" if row_major else "post_sample",
    )(*acts, *weights)


def _rope_tables_q(pos):
    inv_freq = jnp.power(ROPE_THETA, -jnp.arange(0, QK_ROPE, 2, dtype=F32) / QK_ROPE)
    ang = pos.astype(F32)[:, None] * inv_freq[None, :]
    cos, sin = jnp.cos(ang), jnp.sin(ang)
    n = pos.shape[0]
    half = QK_ROPE // 2
    z = lambda k: jnp.zeros((n, k), F32)
    rc = jnp.concatenate([jnp.ones((n, QK_NOPE), F32), cos, cos, z(HEAD_PAD - QK_HEAD)], axis=1)
    rs = jnp.concatenate([z(QK_NOPE), -sin, sin, z(HEAD_PAD - QK_HEAD)], axis=1)
    return rc, rs, jnp.concatenate([cos, cos], axis=1), jnp.concatenate([-sin, sin], axis=1)


def _rope_tables_k(pos):
    inv_freq = jnp.power(ROPE_THETA, -jnp.arange(0, QK_ROPE, 2, dtype=F32) / QK_ROPE)
    ang = pos.astype(F32)[:, None] * inv_freq[None, :]
    cos, sin = jnp.cos(ang), jnp.sin(ang)
    return jnp.concatenate([cos, cos], axis=1).T, jnp.concatenate([sin, -sin], axis=1).T


def _pad_heads(wm, width):
    rows = wm.shape[0]
    return jnp.pad(wm, ((0, 0), (0, 0), (0, HEAD_PAD - width))).reshape(rows, N_HEADS * HEAD_PAD)


def _layer_weights(lw, d_model, n_pow):
    w = {}
    row = lambda z: z.reshape(1, -1).astype(F32)
    off_ckv = Q_LORA
    off_kr = off_ckv + KV_LORA
    off_u = off_kr + QK_ROPE
    off_ga = off_u + SSM_WIDTH
    off_gs = off_ga + d_model
    win = lw['w_in']
    w['w_in'] = jnp.concatenate(
        [win[:, :off_ckv], win[:, off_ckv:off_kr], win[:, off_u:off_ga], win[:, off_ga:off_gs], win[:, off_gs:],
         win[:, off_kr:off_u], jnp.zeros((d_model, LANES - QK_ROPE), win.dtype)], axis=1).astype(BF16)
    w['g_mix'] = row(lw['g_mix'])
    w['g_cq'] = row(lw['g_cq'])
    w['g_ckv'] = row(lw['g_ckv'])
    w['w_uq'] = _pad_heads(lw['w_uq'], QK_HEAD).astype(BF16)
    w['w_uk'] = _pad_heads(lw['w_uk'], QK_NOPE).astype(BF16)
    w['w_uv_t'] = _pad_heads(lw['w_uv'], V_HEAD).T.astype(BF16)
    lane = jnp.arange(N_HEADS * HEAD_PAD)
    w['e_kr'] = ((lane[None, :] % HEAD_PAD) == (QK_NOPE + jnp.arange(QK_ROPE))[:, None]).astype(BF16)
    w['g_q'] = row(jnp.pad(lw['g_q'], (0, HEAD_PAD - QK_HEAD)))
    w['g_k_nope'] = row(jnp.pad(lw['g_k'][:QK_NOPE], (0, HEAD_PAD - QK_NOPE)))
    w['g_k_rope'] = row(lw['g_k'][QK_NOPE:])
    half = QK_ROPE // 2
    wq = lw['w_uq'] * lw['g_q'][None, None, :]
    partner = jnp.concatenate([jnp.zeros_like(wq[..., :QK_NOPE]), wq[..., QK_NOPE + half:QK_HEAD],
                               wq[..., QK_NOPE:QK_NOPE + half]], axis=-1)
    w['w_uq_rot'] = _pad_heads(partner, QK_HEAD).astype(BF16)
    idx = jnp.arange(QK_ROPE)
    w['swap_rope'] = (idx[:, None] == ((idx + half) % QK_ROPE)[None, :]).astype(BF16)
    blk = jnp.arange(2 * HEAD_PAD) // HEAD_PAD
    w['ones2'] = (blk[:, None] == blk[None, :]).astype(BF16)
    w['v_one'] = ((lane % HEAD_PAD) == V_HEAD).astype(F32).reshape(-1, 1)
    w['w_uk_perm'] = lw['w_uk'].transpose(0, 2, 1).reshape(KV_LORA, QK_NOPE * N_HEADS).astype(BF16)
    w['g_k_nope_perm'] = row(jnp.repeat(lw['g_k'][:QK_NOPE], N_HEADS))
    w['g_k_rope_col'] = lw['g_k'][QK_NOPE:].reshape(QK_ROPE, 1).astype(F32)
    w['w_uv_flat'] = lw['w_uv'].reshape(KV_LORA, N_HEADS * V_HEAD).astype(BF16)
    bb_re, bb_im, pow_re, pow_im = _ssm_discretize(lw['a_re'].astype(F32), lw['a_im'].astype(F32),
                                                   lw['log_dt'].astype(F32), lw['b_re'].astype(F32),
                                                   lw['b_im'].astype(F32), n_pow)
    w['bb_re'] = _block_diag_in(bb_re)
    w['bb_im'] = _block_diag_in(bb_im)
    w['pow_re'] = pow_re
    w['pow_im'] = pow_im
    w['cc_re'] = _block_diag_out(lw['c_re'].astype(F32))
    w['cc_im'] = _block_diag_out(lw['c_im'].astype(F32))
    w['d_skip'] = row(lw['d_skip'])
    w['w_glu'] = lw['w_glu'].astype(BF16)
    woa = lw['w_oa'].reshape(N_HEADS, V_HEAD, d_model)
    w['w_oa'] = jnp.pad(woa, ((0, 0), (0, HEAD_PAD - V_HEAD), (0, 0))).reshape(N_HEADS * HEAD_PAD, d_model).astype(BF16)
    w['w_oa_flat'] = lw['w_oa'].astype(BF16)
    w['w_os'] = lw['w_os'].astype(BF16)
    w['w_out'] = lw['w_out'].astype(BF16)
    w['g_ffn'] = row(lw['g_ffn'])
    w['w_up'] = lw['w_up'].astype(BF16)
    w['conv_w'] = lw['conv_w'].astype(F32)
    w['conv_b'] = row(lw['conv_b'])
    w['w_down'] = lw['w_down'].astype(BF16)
    w['g_ple'] = row(lw['g_ple'])
    w['w_ple_gate'] = lw['w_ple_gate'].astype(BF16)
    w['w_ple_proj'] = lw['w_ple_proj'].astype(BF16)
    return w


def kernel(x_prompt, x_sample, p_prompt, p_sample, cache_ckv, cache_kr, page_table, state_ssm_re, state_ssm_im,
           state_conv, g_mix, w_in, g_cq, g_ckv, w_uq, w_uk, w_uv, g_q, g_k, a_re, a_im, log_dt, b_re, b_im,
           c_re, c_im, d_skip, w_glu, w_oa, w_os, w_out, g_ffn, w_up, conv_w, conv_b, w_down, g_ple,
           w_ple_gate, w_ple_proj):
    params = dict(g_mix=g_mix, w_in=w_in, g_cq=g_cq, g_ckv=g_ckv, w_uq=w_uq, w_uk=w_uk, w_uv=w_uv, g_q=g_q,
                  g_k=g_k, a_re=a_re, a_im=a_im, log_dt=log_dt, b_re=b_re, b_im=b_im, c_re=c_re, c_im=c_im,
                  d_skip=d_skip, w_glu=w_glu, w_oa=w_oa, w_os=w_os, w_out=w_out, g_ffn=g_ffn, w_up=w_up,
                  conv_w=conv_w, conv_b=conv_b, w_down=w_down, g_ple=g_ple, w_ple_gate=w_ple_gate,
                  w_ple_proj=w_ple_proj)
    depth = w_in.shape[0]
    bp, seq, d_model = x_prompt.shape
    bd, t_new, _ = x_sample.shape
    past_len = page_table.shape[1] * PAGE_SIZE
    d_ff = w_down.shape[1]
    n_pow = max(min(SSM_ROWS, seq) // SUBLANES, 1)

    rope_prompt = _rope_tables_q(jnp.arange(seq))
    rope_sample = _rope_tables_q(jnp.repeat(past_len + jnp.arange(t_new), bd))
    rope_cache = _rope_tables_k(jnp.arange(past_len))

    yp, ys = x_prompt, x_sample
    outs = [[] for _ in range(10)]
    for i in range(depth):
        w = _layer_weights({k: v[i] for k, v in params.items()}, d_model, n_pow)

        q, k, v, ckv, kr, u, sga, sgs = _inproj(yp, w, rope_prompt, BF16, SCALE * LOG2_E)
        att = _flash_prompt(q, k, v)
        yssm, hr, hi = _ssm_prompt(u, w)
        yp, tail = _post(yp, att, yssm, sga, sgs, p_prompt[i], w)
        outs[0].append(ckv); outs[1].append(kr); outs[4].append(hr); outs[5].append(hi)
        outs[8].append(tail[:, SUBLANES - (CONV_W - 1):, :])

        n_tok = bd * t_new
        steps = lambda z: z.transpose(1, 0, 2).reshape(1, n_tok, z.shape[-1])
        entries = lambda z: z.reshape(t_new, bd, z.shape[-1]).transpose(1, 0, 2)
        q, k, v, ckv, kr, u, sga, sgs = _inproj(steps(ys), w, rope_sample, F32, SCALE)
        ckv = entries(ckv)
        att = _sample_attention(entries(q), entries(k), ckv, cache_ckv, cache_kr, page_table, i, w, rope_cache)
        yssm, hr, hi = _ssm_sample(u.reshape(t_new, bd, -1), state_ssm_re[i], state_ssm_im[i], w)
        hist0 = state_conv[i].astype(F32).transpose(1, 0, 2).reshape((CONV_W - 1) * bd, 2 * d_ff)
        w_s = dict(w, w_oa=w['w_oa_flat'])
        y2, tail = _post(steps(ys), steps(att).astype(BF16), yssm.reshape(1, n_tok, -1), sga, sgs,
                         steps(p_sample[i]), w_s, hist0=hist0, step_rows=bd)
        ys = entries(y2)
        outs[2].append(ckv); outs[3].append(entries(kr))
        outs[6].append(hr); outs[7].append(hi)
        outs[9].append(tail.reshape(CONV_W - 1, bd, 2 * d_ff).transpose(1, 0, 2))

    st = [jnp.stack(o) for o in outs]
    return (yp, ys, st[0], st[1], st[2], st[3], st[4], st[5], st[6], st[7], st[8], st[9])
```

```python
import functools
import math

import jax
import jax.numpy as jnp
import numpy as np
from jax import lax
from jax.experimental import pallas as pl
from jax.experimental.pallas import tpu as pltpu

F32 = jnp.float32
BF16 = jnp.bfloat16

N_HEADS = 8
QK_NOPE = 64
QK_ROPE = 32
QK_HEAD = QK_NOPE + QK_ROPE
V_HEAD = 64
Q_LORA = 384
KV_LORA = 256
ROPE_THETA = 10000.0
SCALE = QK_HEAD ** -0.5
LOG2_E = math.log2(math.e)
NEG_INF = -1e30
SSM_WIDTH = 512
GROUP = 16
N_GROUPS = SSM_WIDTH // GROUP
STATE = 64
N_STATE = N_GROUPS * STATE
CONV_W = 3
EPS = 1e-6
PAGE_SIZE = 128

LANES = 128
SUBLANES = 8
HEAD_PAD = LANES
NEW_PAD = 2 * SUBLANES
V_ROWS = V_HEAD + 2 * SUBLANES
VMEM_LIMIT_BYTES = 56 * 1024 * 1024

INPROJ_ROWS = 512
FLASH_T = 512
FLASH_HEADS = 4
SSM_ROWS = 256
SSM_COLS = 1024
POST_ROWS = 256
FFN_CHUNK = 256
FFN_AHEAD = 4
SAMPLE_CHUNK_PAGES = 8
SAMPLE_AHEAD = 2
SSM_GROUP_BLOCK = 8


def _cparams(sem):
    return pltpu.CompilerParams(dimension_semantics=sem, vmem_limit_bytes=VMEM_LIMIT_BYTES)


def _const_spec(shape):
    nd = len(shape)
    return pl.BlockSpec(shape, lambda *_: (0,) * nd, pipeline_mode=pl.Buffered(1))


def _rms(x, g):
    return x * lax.rsqrt(jnp.mean(x * x, axis=-1, keepdims=True) + EPS) * g


def _bdot(a, b):
    return jnp.dot(a.astype(BF16), b, preferred_element_type=F32)


def _dot_nt(a, b):
    return lax.dot_general(a, b, (((1,), (1,)), ((), ())), preferred_element_type=F32)


def _ssm_disc_kernel(are_ref, aim_ref, ldt_ref, arer_ref, aimr_ref, ldtr_ref, bre_ref, bim_ref,
                     bbre_ref, bbim_ref, pre_ref, pim_ref, *, n_pow):
    def zoh(a_re, a_im, ldt):
        dt = jnp.exp(ldt)
        mag = jnp.exp(dt * a_re)
        ab_re = mag * jnp.cos(dt * a_im)
        ab_im = mag * jnp.sin(dt * a_im)
        return ab_re, ab_im

    a_re = arer_ref[...]
    a_im = aimr_ref[...]
    ab_re, ab_im = zoh(a_re, a_im, ldtr_ref[...])
    den = a_re * a_re + a_im * a_im
    nr = ab_re - 1.0
    f_re = (nr * a_re + ab_im * a_im) / den
    f_im = (ab_im * a_re - nr * a_im) / den
    b_re = bre_ref[...]
    b_im = bim_ref[...]
    bbre_ref[...] = f_re * b_re - f_im * b_im
    bbim_ref[...] = f_re * b_im + f_im * b_re

    p_re, p_im = zoh(are_ref[...], aim_ref[...], ldt_ref[...])
    c_re, c_im = p_re, p_im
    for j in range(n_pow):
        pre_ref[j] = c_re
        pim_ref[j] = c_im
        c_re, c_im = c_re * p_re - c_im * p_im, c_re * p_im + c_im * p_re


def _ssm_discretize(a_re, a_im, log_dt, b_re, b_im, n_pow):
    g, p = a_re.shape
    rows = g * GROUP
    rep = lambda z: jnp.repeat(z, GROUP, axis=0)
    bt = lambda z: z.transpose(0, 2, 1).reshape(rows, p)
    ldt = log_dt.reshape(g, 1)
    out_shape = (jax.ShapeDtypeStruct((rows, p), F32), jax.ShapeDtypeStruct((rows, p), F32),
                 jax.ShapeDtypeStruct((n_pow, g, p), F32), jax.ShapeDtypeStruct((n_pow, g, p), F32))
    bb_re, bb_im, pow_re, pow_im = pl.pallas_call(
        functools.partial(_ssm_disc_kernel, n_pow=n_pow),
        out_shape=out_shape, name="ssm_disc",
    )(a_re, a_im, ldt, rep(a_re), rep(a_im), rep(ldt), bt(b_re), bt(b_im))
    return bb_re, bb_im, pow_re.reshape(n_pow, g * p), pow_im.reshape(n_pow, g * p)


def _block_diag_in(bb):
    nb = N_GROUPS // SSM_GROUP_BLOCK
    z = jnp.tile(bb.reshape(nb, SSM_GROUP_BLOCK * GROUP, STATE), (1, 1, SSM_GROUP_BLOCK))
    mask = (np.arange(SSM_GROUP_BLOCK * GROUP) // GROUP)[:, None] == (np.arange(SSM_GROUP_BLOCK * STATE) // STATE)[None, :]
    return (z * mask.astype(np.float32)).astype(BF16)


def _block_diag_out(c):
    nb = N_GROUPS // SSM_GROUP_BLOCK
    z = jnp.tile(c.transpose(0, 2, 1).reshape(nb, SSM_GROUP_BLOCK * STATE, GROUP), (1, 1, SSM_GROUP_BLOCK))
    mask = (np.arange(SSM_GROUP_BLOCK * STATE) // STATE)[:, None] == (np.arange(SSM_GROUP_BLOCK * GROUP) // GROUP)[None, :]
    return (z * mask.astype(np.float32)).astype(BF16)


OFF_CQ = 0
OFF_CKV = OFF_CQ + Q_LORA
OFF_U = OFF_CKV + KV_LORA


def _head_sumsq(z, ones2_ref):
    sq = (z * z).astype(BF16)
    span = ones2_ref.shape[0]
    parts = [jnp.dot(sq[:, j:j + span], ones2_ref[...], preferred_element_type=F32)
             for j in range(0, z.shape[-1], span)]
    return jnp.concatenate(parts, axis=-1)


def _inproj_kernel(x_ref, gmix_ref, win_ref, gcq_ref, gckv_ref, wuq_ref, wuqr_ref, wuk_ref, ekr_ref, swap_ref,
                   ones2_ref, wuvt_ref, vone_ref, gq_ref, gkn_ref, gkr_ref, rc_ref, rs_ref, cc_ref, cs_ref,
                   q_ref, k_ref, v_ref, ckv_ref, kr_ref, u_ref, sga_ref, sgs_ref, *, d_model, q_scale):
    off_ga = OFF_U + SSM_WIDTH
    off_gs = off_ga + d_model
    off_kr = off_gs + d_model
    inv = 1.0 / QK_HEAD
    blocks = [slice(h * HEAD_PAD, (h + 1) * HEAD_PAD) for h in range(N_HEADS)]
    xn = _rms(x_ref[...], gmix_ref[...])
    proj = _bdot(xn, win_ref[...])
    cq = _rms(proj[:, OFF_CQ:OFF_CKV], gcq_ref[...])
    ckv = _rms(proj[:, OFF_CKV:OFF_U], gckv_ref[...])
    kr = proj[:, off_kr:off_kr + QK_ROPE]
    u_ref[...] = proj[:, OFF_U:off_ga]
    sga_ref[...] = jax.nn.sigmoid(proj[:, off_ga:off_gs]).astype(sga_ref.dtype)
    sgs_ref[...] = jax.nn.sigmoid(proj[:, off_gs:off_kr]).astype(sgs_ref.dtype)
    ckv_ref[...] = ckv
    kr_ref[...] = kr
    ckv_b = ckv.astype(BF16)
    v_ref[...] = (_dot_nt(wuvt_ref[...], ckv_b) + vone_ref[...]).astype(v_ref.dtype)

    cq_b = cq.astype(BF16)
    qf = jnp.dot(cq_b, wuq_ref[...], preferred_element_type=F32)
    qrot = jnp.dot(cq_b, wuqr_ref[...], preferred_element_type=F32)
    rn_q = lax.rsqrt(_head_sumsq(qf, ones2_ref) * inv + EPS)
    tc = rc_ref[...] * (gq_ref[...] * q_scale)
    ts = rs_ref[...] * q_scale
    for blk in blocks:
        q_ref[:, blk] = (rn_q[:, blk] * (qf[:, blk] * tc + qrot[:, blk] * ts)).astype(q_ref.dtype)

    kn = jnp.dot(ckv_b, wuk_ref[...], preferred_element_type=F32)
    krg = kr * gkr_ref[...]
    krr = krg * cc_ref[...] + _bdot(krg, swap_ref[...]) * cs_ref[...]
    placed = _bdot(krr, ekr_ref[...])
    r2 = jnp.sum(kr * kr, axis=-1, keepdims=True)
    rn_k = lax.rsqrt((_head_sumsq(kn, ones2_ref) + r2) * inv + EPS)
    for blk in blocks:
        k_ref[:, blk] = (rn_k[:, blk] * (kn[:, blk] * gkn_ref[...] + placed[:, blk])).astype(k_ref.dtype)


def _inproj(x, w, rope, act_dtype, q_scale):
    nb, rows, d = x.shape
    tr = min(INPROJ_ROWS, rows)
    assert rows % tr == 0
    hp = N_HEADS * HEAD_PAD
    row_spec = lambda n: pl.BlockSpec((None, tr, n), lambda b, t: (b, t, 0))
    tab_specs = [pl.BlockSpec((tr, z.shape[-1]), lambda b, t: (t, 0)) for z in rope]
    weights = (w['g_mix'], w['w_in'], w['g_cq'], w['g_ckv'], w['w_uq'], w['w_uq_rot'], w['w_uk'], w['e_kr'],
               w['swap_rope'], w['ones2'], w['w_uv_t'], w['v_one'], w['g_q'], w['g_k_nope'], w['g_k_rope'])
    out_shape = (
        jax.ShapeDtypeStruct((nb, rows, hp), act_dtype),
        jax.ShapeDtypeStruct((nb, rows, hp), act_dtype),
        jax.ShapeDtypeStruct((nb, hp, rows), act_dtype),
        jax.ShapeDtypeStruct((nb, rows, KV_LORA), F32),
        jax.ShapeDtypeStruct((nb, rows, QK_ROPE), F32),
        jax.ShapeDtypeStruct((nb, rows, SSM_WIDTH), F32),
        jax.ShapeDtypeStruct((nb, rows, d), BF16),
        jax.ShapeDtypeStruct((nb, rows, d), BF16),
    )
    out_specs = [row_spec(s.shape[-1]) for s in out_shape]
    out_specs[2] = pl.BlockSpec((None, hp, tr), lambda b, t: (b, 0, t))
    return pl.pallas_call(
        functools.partial(_inproj_kernel, d_model=d, q_scale=q_scale),
        grid=(nb, rows // tr),
        in_specs=[row_spec(d)] + [_const_spec(z.shape) for z in weights] + tab_specs,
        out_specs=tuple(out_specs),
        out_shape=out_shape,
        compiler_params=_cparams(("parallel", "parallel")),
        name="inproj",
    )(x, *weights, *rope)


def _flash_kernel(q_ref, k_ref, vt_ref, o_ref, s0, s1, m_sc, acc_sc, *, t, nh):
    qi = pl.program_id(2)
    heads = [slice(h * HEAD_PAD, (h + 1) * HEAD_PAD) for h in range(nh)]
    m_sc[...] = jnp.full(m_sc.shape, NEG_INF, F32)
    acc_sc[...] = jnp.zeros(acc_sc.shape, F32)

    def scores(kj, dst):
        start = pl.multiple_of(kj * t, t)
        for h, hs in enumerate(heads):
            dst[h] = _dot_nt(k_ref[pl.ds(start, t), hs], q_ref[:, hs])

    def fold(kj, src, diagonal):
        start = pl.multiple_of(kj * t, t)
        if diagonal:
            keep = lax.broadcasted_iota(jnp.int32, (t, t), 0) <= lax.broadcasted_iota(jnp.int32, (t, t), 1)
        for h, hs in enumerate(heads):
            st = src[h]
            if diagonal:
                st = jnp.where(keep, st, NEG_INF)
            m = m_sc[h]
            m_new = jnp.maximum(m, jnp.max(st, axis=0, keepdims=True))
            p = jnp.exp2(st - m_new)
            corr = jnp.exp2(m - m_new)
            vrows = slice(h * HEAD_PAD, h * HEAD_PAD + V_ROWS)
            acc_sc[h] = acc_sc[h] * corr + jnp.dot(vt_ref[vrows, pl.ds(start, t)], p.astype(BF16),
                                                   preferred_element_type=F32)
            m_sc[h] = m_new

    scores(0, s0)

    def pair(kk, carry):
        j = 2 * kk
        scores(j + 1, s1)
        fold(j, s0, False)
        scores(j + 2, s0)
        fold(j + 1, s1, False)
        return carry

    lax.fori_loop(0, qi // 2, pair, 0)

    @pl.when(qi % 2 == 0)
    def _():
        fold(qi, s0, True)

    @pl.when(qi % 2 == 1)
    def _():
        scores(qi, s1)
        fold(qi - 1, s0, False)
        fold(qi, s1, True)

    for h, hs in enumerate(heads):
        acc = acc_sc[h]
        out = jnp.concatenate([acc / acc[V_HEAD:V_HEAD + 1, :], jnp.zeros((HEAD_PAD - V_ROWS, t), F32)], axis=0)
        o_ref[:, hs] = jnp.transpose(out).astype(o_ref.dtype)


def _flash_prompt(q, k, vt):
    b, s, hp = q.shape
    tq = min(FLASH_T, s)
    nh = FLASH_HEADS
    assert s % tq == 0 and N_HEADS % nh == 0
    return pl.pallas_call(
        functools.partial(_flash_kernel, t=tq, nh=nh),
        grid=(b, N_HEADS // nh, s // tq),
        in_specs=[pl.BlockSpec((None, tq, nh * HEAD_PAD), lambda bi, h, qi: (bi, qi, h)),
                  pl.BlockSpec((None, s, nh * HEAD_PAD), lambda bi, h, qi: (bi, 0, h)),
                  pl.BlockSpec((None, nh * HEAD_PAD, s), lambda bi, h, qi: (bi, h, 0))],
        out_specs=pl.BlockSpec((None, tq, nh * HEAD_PAD), lambda bi, h, qi: (bi, qi, h)),
        out_shape=jax.ShapeDtypeStruct((b, s, hp), BF16),
        scratch_shapes=[pltpu.VMEM((nh, tq, tq), F32), pltpu.VMEM((nh, tq, tq), F32),
                        pltpu.VMEM((nh, 1, tq), F32), pltpu.VMEM((nh, V_ROWS, tq), F32)],
        compiler_params=_cparams(("parallel", "parallel", "arbitrary")),
        name="flash_prompt",
    )(q, k, vt)


def _sample_attn_kernel(pt_ref, qn_ref, qr_ref, qrs_ref, qf_ref, knew_ref, cnew_ref, ca_ref, cb_ref,
                        wukp_ref, gkn_ref, gkr_ref, wuv_ref, hmask_ref, ckv_hbm, kr_hbm, o_ref,
                        cf, rf, cbuf, sem_c, sem_r, *, layer, n_pages, n_new, chunk_pages):
    b = pl.program_id(0)
    slot = b % 2
    nq = n_new * N_HEADS
    keys = chunk_pages * PAGE_SIZE

    def page_copies(entry, sl, j):
        pg = pt_ref[entry * n_pages + j]
        return (pltpu.make_async_copy(ckv_hbm.at[layer, pg], cf.at[sl, j], sem_c.at[sl]),
                pltpu.make_async_copy(kr_hbm.at[layer, pg], rf.at[sl, j], sem_r.at[sl]))

    def start_entry(entry, sl):
        def body(j, carry):
            for cp in page_copies(entry, sl, j):
                cp.start()
            return carry
        lax.fori_loop(0, n_pages, body, 0, unroll=min(8, n_pages))

    @pl.when(b == 0)
    def _():
        start_entry(0, 0)

    @pl.when(b + 1 < pl.num_programs(0))
    def _():
        start_entry(b + 1, 1 - slot)

    pltpu.make_async_copy(ckv_hbm.at[layer, pl.ds(0, n_pages)], cf.at[slot], sem_c.at[slot]).wait()
    pltpu.make_async_copy(kr_hbm.at[layer, pl.ds(0, n_pages)], rf.at[slot], sem_r.at[slot]).wait()

    qn = (qn_ref[...] * gkn_ref[...]).astype(BF16)
    qabs = _dot_nt(qn, wukp_ref[...]).astype(BF16)

    zrows = LANES - NEW_PAD
    knew = jnp.concatenate([knew_ref[...].astype(BF16), jnp.zeros((zrows, knew_ref.shape[-1]), BF16)], axis=0)
    cnew = jnp.concatenate([cnew_ref[...].astype(BF16), jnp.zeros((zrows, KV_LORA), BF16)], axis=0)
    s_new = _dot_nt(qf_ref[...].astype(BF16), knew)
    tok = lax.broadcasted_iota(jnp.int32, (nq, LANES), 0) // N_HEADS
    key = lax.broadcasted_iota(jnp.int32, (nq, LANES), 1)
    scored = [(jnp.where(key <= tok, s_new, NEG_INF), cnew)]

    qr = qr_ref[...].astype(BF16)
    qrs = qrs_ref[...].astype(BF16)
    def front(ch):
        span = slice(ch * keys, (ch + 1) * keys)
        for i in range(ch * chunk_pages, (ch + 1) * chunk_pages):
            cbuf[i * PAGE_SIZE:(i + 1) * PAGE_SIZE, :] = cf[slot, i].astype(BF16)
        c = cbuf[span, :]
        kn = jnp.dot(c, wukp_ref[...], preferred_element_type=F32)
        sq = kn * kn
        ss = sq[:, 0:LANES]
        for j in range(1, N_HEADS * QK_NOPE // LANES):
            ss = ss + sq[:, j * LANES:(j + 1) * LANES]
        return ch, c, ss, _dot_nt(qabs, c)

    def back(args):
        ch, c, ss, s_nope = args
        span = slice(ch * keys, (ch + 1) * keys)
        ss_t = jnp.transpose(ss)
        ss_h = ss_t[0:N_HEADS]
        for j in range(1, LANES // N_HEADS):
            ss_h = ss_h + ss_t[j * N_HEADS:(j + 1) * N_HEADS]
        r_t = jnp.concatenate([rf[slot, i] for i in range(ch * chunk_pages, (ch + 1) * chunk_pages)], axis=1)
        r2 = jnp.sum(r_t * r_t, axis=0, keepdims=True)
        rn_h = lax.rsqrt((ss_h + r2) * (1.0 / QK_HEAD) + EPS)
        rn = jnp.concatenate([rn_h] * n_new, axis=0)
        rg = r_t * gkr_ref[...]
        s_rope = (jnp.dot(qr, (rg * ca_ref[:, span]).astype(BF16), preferred_element_type=F32)
                  + jnp.dot(qrs, (rg * cb_ref[:, span]).astype(BF16), preferred_element_type=F32))
        return rn * (s_nope + s_rope), c

    def weights(s):
        m = jnp.max(s, axis=-1, keepdims=True)
        p = jnp.exp(s - m)
        return m, p.astype(BF16), jnp.sum(p, axis=-1, keepdims=True)

    n_chunks = n_pages // chunk_pages
    ahead = [front(ch) for ch in range(min(SAMPLE_AHEAD, n_chunks))]
    pending = [(weights(scored[0][0]), scored[0][1])]
    parts = []

    def value_product(item):
        (m, p, l), vals = item
        parts.append((m, l, jnp.dot(p, vals, preferred_element_type=F32)))

    for ch in range(n_chunks):
        cur = ahead.pop(0)
        if ch + SAMPLE_AHEAD < n_chunks:
            ahead.append(front(ch + SAMPLE_AHEAD))
        s, vals = back(cur)
        pending.append((weights(s), vals))
        value_product(pending.pop(0))
    while pending:
        value_product(pending.pop(0))

    m = parts[0][0]
    for pm, _, _ in parts[1:]:
        m = jnp.maximum(m, pm)
    l = jnp.zeros_like(m)
    acc = jnp.zeros((nq, KV_LORA), F32)
    for pm, pl_, pacc in parts:
        wgt = jnp.exp(pm - m)
        l = l + pl_ * wgt
        acc = acc + pacc * wgt
    o_lat = acc / l
    full = _bdot(o_lat, wuv_ref[...]) * hmask_ref[...]
    o_ref[...] = jnp.sum(full.reshape(n_new, N_HEADS, N_HEADS * V_HEAD), axis=1)


def _sample_attention(q, k, ckv, cache_ckv, cache_kr, page_table, layer, w, rope_k):
    bd, t, hp = q.shape
    n_logical = page_table.shape[1]
    chunk_pages = min(SAMPLE_CHUNK_PAGES, n_logical)
    assert n_logical % chunk_pages == 0 and t <= NEW_PAD
    nq = t * N_HEADS
    n_keys = n_logical * PAGE_SIZE
    head_mask = ((np.arange(nq) % N_HEADS)[:, None] == (np.arange(N_HEADS * V_HEAD) // V_HEAD)[None, :]
                 ).astype(np.float32)
    qh = q.reshape(bd, t, N_HEADS, HEAD_PAD)
    eye = np.eye(N_HEADS, dtype=np.float32)

    qn = jnp.einsum('bthd,hk->bthdk', qh[..., :QK_NOPE], eye).reshape(bd, nq, QK_NOPE * N_HEADS)
    qr = qh[..., QK_NOPE:QK_HEAD].reshape(bd, nq, QK_ROPE)
    qrs = jnp.concatenate([qr[..., QK_ROPE // 2:], qr[..., :QK_ROPE // 2]], axis=-1)
    qf = jnp.einsum('bthl,hk->bthkl', qh, eye).reshape(bd, nq, hp)
    pad_t = lambda z: jnp.pad(z, ((0, 0), (0, NEW_PAD - t), (0, 0)))
    knew = pad_t(k)
    cnew = pad_t(ckv)
    ca, cb = rope_k
    cache_kr_t = jnp.swapaxes(cache_kr, 2, 3)

    def bspec(rows, n):
        return pl.BlockSpec((None, rows, n), lambda b, pt: (b, 0, 0))

    def wspec(shape):
        nd = len(shape)
        return pl.BlockSpec(shape, lambda b, pt: (0,) * nd)

    weights = (ca, cb, w['w_uk_perm'], w['g_k_nope_perm'], w['g_k_rope_col'], w['w_uv_flat'], head_mask)
    in_specs = ([bspec(nq, QK_NOPE * N_HEADS), bspec(nq, QK_ROPE), bspec(nq, QK_ROPE), bspec(nq, hp),
                 bspec(NEW_PAD, hp), bspec(NEW_PAD, KV_LORA)]
                + [wspec(z.shape) for z in weights]
                + [pl.BlockSpec(memory_space=pl.ANY), pl.BlockSpec(memory_space=pl.ANY)])
    grid_spec = pltpu.PrefetchScalarGridSpec(
        num_scalar_prefetch=1, grid=(bd,), in_specs=in_specs,
        out_specs=pl.BlockSpec((None, t, N_HEADS * V_HEAD), lambda b, pt: (b, 0, 0)),
        scratch_shapes=[pltpu.VMEM((2, n_logical, PAGE_SIZE, KV_LORA), F32),
                        pltpu.VMEM((2, n_logical, QK_ROPE, PAGE_SIZE), F32),
                        pltpu.VMEM((n_keys, KV_LORA), BF16),
                        pltpu.SemaphoreType.DMA((2,)), pltpu.SemaphoreType.DMA((2,))])
    return pl.pallas_call(
        functools.partial(_sample_attn_kernel, layer=layer, n_pages=n_logical, n_new=t, chunk_pages=chunk_pages),
        grid_spec=grid_spec,
        out_shape=jax.ShapeDtypeStruct((bd, t, N_HEADS * V_HEAD), F32),
        compiler_params=_cparams(("arbitrary",)),
        name="sample_attn",
    )(page_table.reshape(-1), qn, qr, qrs, qf, knew, cnew, *weights, cache_ckv, cache_kr_t)


def _ssm_in(u_b, bre_ref, bim_ref, store_re, store_im):
    nb = bre_ref.shape[0]
    kw = bre_ref.shape[1]
    nw = bre_ref.shape[2]
    for kb in range(nb):
        blk = u_b[:, kb * kw:(kb + 1) * kw]
        store_re(kb * nw, nw, jnp.dot(blk, bre_ref[kb], preferred_element_type=F32))
        store_im(kb * nw, nw, jnp.dot(blk, bim_ref[kb], preferred_element_type=F32))


def _ssm_out(h_re, h_im, cre_ref, cim_ref):
    nb = cre_ref.shape[0]
    kw = cre_ref.shape[1]
    parts = []
    for kb in range(nb):
        parts.append(jnp.dot(h_re[:, kb * kw:(kb + 1) * kw].astype(BF16), cre_ref[kb], preferred_element_type=F32)
                     - jnp.dot(h_im[:, kb * kw:(kb + 1) * kw].astype(BF16), cim_ref[kb], preferred_element_type=F32))
    return jnp.concatenate(parts, axis=-1)


def _glu(y, wglu_ref):
    g = _bdot(jax.nn.gelu(y), wglu_ref[...])
    half = g.shape[-1] // 2
    return g[:, :half] * jax.nn.sigmoid(g[:, half:])


def _ssm_prompt_kernel(u_ref, perm_ref, permt_ref, bre_ref, bim_ref, pre_ref, pim_ref, cre_ref, cim_ref,
                       dskip_ref, wglu_ref, ys_ref, hre_out, him_out,
                       hre, him, car_re, car_im, hin_re, hin_im, *, rows, sub):
    ti = pl.program_id(1)

    @pl.when(ti == 0)
    def _():
        car_re[...] = jnp.zeros(car_re.shape, F32)
        car_im[...] = jnp.zeros(car_im.shape, F32)

    u = u_ref[...]
    u_hi = u.astype(BF16)
    u_lo = (u - u_hi.astype(F32)).astype(BF16)
    perm = perm_ref[...]
    up = jnp.dot(perm, u_hi, preferred_element_type=F32) + jnp.dot(perm, u_lo, preferred_element_type=F32)

    def st_re(c0, n, val):
        hre[:, c0:c0 + n] = val

    def st_im(c0, n, val):
        him[:, c0:c0 + n] = val

    _ssm_in(up.astype(BF16), bre_ref, bim_ref, st_re, st_im)

    n_cb = N_STATE // SSM_COLS
    for cb in range(n_cb):
        cols = slice(cb * SSM_COLS, (cb + 1) * SSM_COLS)
        a_re = jnp.broadcast_to(pre_ref[0:1, cols], (SUBLANES, SSM_COLS))
        a_im = jnp.broadcast_to(pim_ref[0:1, cols], (SUBLANES, SSM_COLS))

        def local(j, st):
            s_re, s_im = st
            r0 = pl.multiple_of(j * SUBLANES, SUBLANES)
            n_re = a_re * s_re - a_im * s_im + hre[pl.ds(r0, SUBLANES), cols]
            n_im = a_re * s_im + a_im * s_re + him[pl.ds(r0, SUBLANES), cols]
            hre[pl.ds(r0, SUBLANES), cols] = n_re
            him[pl.ds(r0, SUBLANES), cols] = n_im
            return n_re, n_im

        zero = jnp.zeros((SUBLANES, SSM_COLS), F32)
        e_re, e_im = lax.fori_loop(0, sub, local, (zero, zero))

        as_re = pre_ref[sub - 1:sub, cols]
        as_im = pim_ref[sub - 1:sub, cols]
        c_re = car_re[:, cols]
        c_im = car_im[:, cols]
        for s in range(SUBLANES):
            hin_re[s:s + 1, cols] = c_re
            hin_im[s:s + 1, cols] = c_im
            c_re, c_im = (as_re * c_re - as_im * c_im + e_re[s:s + 1],
                          as_re * c_im + as_im * c_re + e_im[s:s + 1])
        car_re[:, cols] = c_re
        car_im[:, cols] = c_im
        g_re = hin_re[:, cols]
        g_im = hin_im[:, cols]

        def fix(j, d):
            d_re, d_im = d
            d_re, d_im = a_re * d_re - a_im * d_im, a_re * d_im + a_im * d_re
            r0 = pl.multiple_of(j * SUBLANES, SUBLANES)
            hre[pl.ds(r0, SUBLANES), cols] = hre[pl.ds(r0, SUBLANES), cols] + d_re
            him[pl.ds(r0, SUBLANES), cols] = him[pl.ds(r0, SUBLANES), cols] + d_im
            return d_re, d_im

        lax.fori_loop(0, sub, fix, (g_re, g_im))

    y = _ssm_out(hre[...], him[...], cre_ref, cim_ref) + dskip_ref[...] * up
    ys = _glu(y, wglu_ref).astype(BF16)
    ys_ref[...] = jnp.dot(permt_ref[...], ys, preferred_element_type=F32).astype(ys_ref.dtype)

    @pl.when(ti == pl.num_programs(1) - 1)
    def _():
        hre_out[...] = car_re[...]
        him_out[...] = car_im[...]


def _ssm_prompt(u, w):
    b, s, width = u.shape
    rows = min(SSM_ROWS, s)
    assert s % rows == 0 and rows % SUBLANES == 0
    sub = rows // SUBLANES
    r = np.arange(rows)
    t_of_r = (r % SUBLANES) * sub + r // SUBLANES
    perm = jnp.asarray((t_of_r[:, None] == np.arange(rows)[None, :]).astype(np.float32), BF16)
    pre = w['pow_re'][:sub]
    pim = w['pow_im'][:sub]
    weights = (perm, perm.T, w['bb_re'], w['bb_im'], pre, pim, w['cc_re'], w['cc_im'], w['d_skip'], w['w_glu'])
    ys, h_re, h_im = pl.pallas_call(
        functools.partial(_ssm_prompt_kernel, rows=rows, sub=sub),
        grid=(b, s // rows),
        in_specs=[pl.BlockSpec((None, rows, width), lambda bi, t: (bi, t, 0))]
                 + [_const_spec(z.shape) for z in weights],
        out_specs=(pl.BlockSpec((None, rows, width), lambda bi, t: (bi, t, 0)),
                   pl.BlockSpec((None, 1, N_STATE), lambda bi, t: (bi, 0, 0)),
                   pl.BlockSpec((None, 1, N_STATE), lambda bi, t: (bi, 0, 0))),
        out_shape=(jax.ShapeDtypeStruct((b, s, width), BF16),
                   jax.ShapeDtypeStruct((b, 1, N_STATE), F32),
                   jax.ShapeDtypeStruct((b, 1, N_STATE), F32)),
        scratch_shapes=[pltpu.VMEM((rows, N_STATE), F32), pltpu.VMEM((rows, N_STATE), F32),
                        pltpu.VMEM((1, N_STATE), F32), pltpu.VMEM((1, N_STATE), F32),
                        pltpu.VMEM((SUBLANES, N_STATE), F32), pltpu.VMEM((SUBLANES, N_STATE), F32)],
        compiler_params=_cparams(("parallel", "arbitrary")),
        name="ssm_prompt",
    )(u, *weights)
    return ys, h_re.reshape(b, N_GROUPS, STATE), h_im.reshape(b, N_GROUPS, STATE)


def _ssm_sample_kernel(u_ref, h0re_ref, h0im_ref, bre_ref, bim_ref, pre_ref, pim_ref, cre_ref, cim_ref,
                       dskip_ref, wglu_ref, ys_ref, hre_out, him_out, bu_re, bu_im, *, steps):
    a_re = pre_ref[0:1, :]
    a_im = pim_ref[0:1, :]
    h_re = h0re_ref[...]
    h_im = h0im_ref[...]

    def st_re(c0, n, val):
        bu_re[:, c0:c0 + n] = val

    def st_im(c0, n, val):
        bu_im[:, c0:c0 + n] = val

    for t in range(steps):
        u = u_ref[t]
        _ssm_in(u.astype(BF16), bre_ref, bim_ref, st_re, st_im)
        h_re, h_im = (a_re * h_re - a_im * h_im + bu_re[...], a_re * h_im + a_im * h_re + bu_im[...])
        y = _ssm_out(h_re, h_im, cre_ref, cim_ref) + dskip_ref[...] * u
        ys_ref[t] = _glu(y, wglu_ref).astype(ys_ref.dtype)
    hre_out[...] = h_re
    him_out[...] = h_im


def _ssm_sample(ut, h0_re, h0_im, w):
    t, bd, width = ut.shape
    ys, h_re, h_im = pl.pallas_call(
        functools.partial(_ssm_sample_kernel, steps=t),
        out_shape=(jax.ShapeDtypeStruct((t, bd, width), BF16),
                   jax.ShapeDtypeStruct((bd, N_STATE), F32),
                   jax.ShapeDtypeStruct((bd, N_STATE), F32)),
        scratch_shapes=[pltpu.VMEM((bd, N_STATE), F32), pltpu.VMEM((bd, N_STATE), F32)],
        compiler_params=pltpu.CompilerParams(vmem_limit_bytes=VMEM_LIMIT_BYTES),
        name="ssm_sample",
    )(ut, h0_re.reshape(bd, N_STATE).astype(F32), h0_im.reshape(bd, N_STATE).astype(F32),
      w['bb_re'], w['bb_im'], w['pow_re'][:1], w['pow_im'][:1], w['cc_re'], w['cc_im'], w['d_skip'], w['w_glu'])
    return ys, h_re.reshape(bd, N_GROUPS, STATE), h_im.reshape(bd, N_GROUPS, STATE)


def _post_kernel(*refs, rows, d_ff, row_major):
    if row_major:
        (x_ref, att_ref, ys_ref, sga_ref, sgs_ref, p_ref, woa_ref, wos_ref, wout_ref, gffn_ref, wup_ref,
         cw_ref, cbias_ref, wdown_ref, gple_ref, wpg_ref, wpp_ref, y_ref, tail_ref, hist) = refs
        hist0_ref = None
    else:
        (x_ref, att_ref, ys_ref, sga_ref, sgs_ref, p_ref, hist0_ref, woa_ref, wos_ref, wout_ref, gffn_ref, wup_ref,
         cw_ref, cbias_ref, wdown_ref, gple_ref, wpg_ref, wpp_ref, y_ref, tail_ref, hist) = refs
    ti = pl.program_id(1)
    pad = SUBLANES

    mixed = (sga_ref[...].astype(F32) * jnp.dot(att_ref[...], woa_ref[...], preferred_element_type=F32)
             + sgs_ref[...].astype(F32) * jnp.dot(ys_ref[...], wos_ref[...], preferred_element_type=F32))
    x1 = x_ref[...] + _bdot(mixed, wout_ref[...])
    xn = _rms(x1, gffn_ref[...]).astype(BF16)

    @pl.when(ti == 0)
    def _():
        if row_major:
            hist[0:pad, :] = jnp.zeros((pad, 2 * d_ff), F32)
        else:
            hist[...] = hist0_ref[...]

    def up_cols(c0):
        cols = slice(c0, c0 + FFN_CHUNK)
        up = jnp.dot(xn, wup_ref[:, cols], preferred_element_type=F32)
        if row_major:
            hist[pad:pad + rows, cols] = up
            s1 = hist[pad - 1:pad - 1 + rows, cols]
            s2 = hist[pad - 2:pad - 2 + rows, cols]
        else:
            s2 = hist[0:rows, cols]
            s1 = hist[rows:2 * rows, cols]
            hist[0:rows, cols] = s1
            hist[rows:2 * rows, cols] = up
        return cols, up, s1, s2

    def conv(args):
        cols, up, s1, s2 = args
        return cbias_ref[:, cols] + s2 * cw_ref[0:1, cols] + s1 * cw_ref[1:2, cols] + up * cw_ref[2:3, cols]

    n_chunks = d_ff // FFN_CHUNK
    ahead = [(up_cols(c * FFN_CHUNK), up_cols(d_ff + c * FFN_CHUNK)) for c in range(min(FFN_AHEAD, n_chunks))]
    f = jnp.zeros((rows, x1.shape[-1]), F32)
    for c in range(n_chunks):
        cur = ahead.pop(0)
        if c + FFN_AHEAD < n_chunks:
            ahead.append((up_cols((c + FFN_AHEAD) * FFN_CHUNK), up_cols(d_ff + (c + FFN_AHEAD) * FFN_CHUNK)))
        hdn = (jax.nn.gelu(conv(cur[0])) * conv(cur[1])).astype(BF16)
        f = f + jnp.dot(hdn, wdown_ref[c * FFN_CHUNK:(c + 1) * FFN_CHUNK, :], preferred_element_type=F32)
    x2 = x1 + f

    if row_major:
        tail = hist[rows:rows + pad, :]
        hist[0:pad, :] = tail
        tail_ref[...] = tail
    else:
        @pl.when(ti == pl.num_programs(1) - 1)
        def _():
            tail_ref[...] = hist[...]

    gate = jax.nn.sigmoid(_bdot(_rms(x2, gple_ref[...]), wpg_ref[...]))
    y_ref[...] = x2 + gate * _bdot(p_ref[...], wpp_ref[...])


def _post(x, att, ys, sga, sgs, p, w, hist0=None, step_rows=None):
    nb, total, d = x.shape
    d_ff = w['w_down'].shape[0]
    assert d_ff % FFN_CHUNK == 0
    row_major = hist0 is None
    rows = min(POST_ROWS, total) if row_major else step_rows
    assert total % rows == 0
    row_spec = lambda n: pl.BlockSpec((None, rows, n), lambda b, t: (b, t, 0))
    acts = [x, att, ys, sga, sgs, p]
    act_specs = [row_spec(z.shape[-1]) for z in acts]
    weights = (w['w_oa'], w['w_os'], w['w_out'], w['g_ffn'], w['w_up'], w['conv_w'], w['conv_b'], w['w_down'],
               w['g_ple'], w['w_ple_gate'], w['w_ple_proj'])
    if row_major:
        hist_rows = SUBLANES
        tail_spec = pl.BlockSpec((None, hist_rows, 2 * d_ff), lambda b, t: (b, 0, 0))
        tail_shape = jax.ShapeDtypeStruct((nb, hist_rows, 2 * d_ff), F32)
        scratch = pltpu.VMEM((rows + hist_rows, 2 * d_ff), F32)
    else:
        assert nb == 1
        hist_rows = 2 * rows
        acts.append(hist0)
        act_specs.append(_const_spec(hist0.shape))
        tail_spec = pl.BlockSpec((hist_rows, 2 * d_ff), lambda b, t: (0, 0), pipeline_mode=pl.Buffered(1))
        tail_shape = jax.ShapeDtypeStruct((hist_rows, 2 * d_ff), F32)
        scratch = pltpu.VMEM((hist_rows, 2 * d_ff), F32)
    return pl.pallas_call(
        functools.partial(_post_kernel, rows=rows, d_ff=d_ff, row_major=row_major),
        grid=(nb, total // rows),
        in_specs=act_specs + [_const_spec(z.shape) for z in weights],
        out_specs=(row_spec(d), tail_spec),
        out_shape=(jax.ShapeDtypeStruct((nb, total, d), F32), tail_shape),
        scratch_shapes=[scratch],
        compiler_params=_cparams(("parallel", "arbitrary")),
        name="post_prompt" if row_major else "post_sample",
    )(*acts, *weights)


def _rope_cos_sin(pos):
    inv_freq = np.power(np.float32(ROPE_THETA), -np.arange(0, QK_ROPE, 2, dtype=np.float32) / np.float32(QK_ROPE))
    ang = (np.asarray(pos, np.float32)[:, None] * inv_freq[None, :]).astype(np.float64)
    return np.cos(ang).astype(np.float32), np.sin(ang).astype(np.float32)


def _rope_tables_q(pos):
    cos, sin = _rope_cos_sin(pos)
    n = cos.shape[0]
    z = lambda k: np.zeros((n, k), np.float32)
    rc = np.concatenate([np.ones((n, QK_NOPE), np.float32), cos, cos, z(HEAD_PAD - QK_HEAD)], axis=1)
    rs = np.concatenate([z(QK_NOPE), -sin, sin, z(HEAD_PAD - QK_HEAD)], axis=1)
    return rc, rs, np.concatenate([cos, cos], axis=1), np.concatenate([-sin, sin], axis=1)


def _rope_tables_k(pos):
    cos, sin = _rope_cos_sin(pos)
    return (np.ascontiguousarray(np.concatenate([cos, cos], axis=1).T),
            np.ascontiguousarray(np.concatenate([sin, -sin], axis=1).T))


def _pad_heads(wm, width):
    rows = wm.shape[0]
    return jnp.pad(wm, ((0, 0), (0, 0), (0, HEAD_PAD - width))).reshape(rows, N_HEADS * HEAD_PAD)


def _layer_weights(lw, d_model, n_pow):
    w = {}
    row = lambda z: z.reshape(1, -1).astype(F32)
    off_ckv = Q_LORA
    off_kr = off_ckv + KV_LORA
    off_u = off_kr + QK_ROPE
    off_ga = off_u + SSM_WIDTH
    off_gs = off_ga + d_model
    win = lw['w_in']
    w['w_in'] = jnp.concatenate(
        [win[:, :off_ckv], win[:, off_ckv:off_kr], win[:, off_u:off_ga], win[:, off_ga:off_gs], win[:, off_gs:],
         win[:, off_kr:off_u], jnp.zeros((d_model, LANES - QK_ROPE), win.dtype)], axis=1).astype(BF16)
    w['g_mix'] = row(lw['g_mix'])
    w['g_cq'] = row(lw['g_cq'])
    w['g_ckv'] = row(lw['g_ckv'])
    w['w_uq'] = _pad_heads(lw['w_uq'], QK_HEAD).astype(BF16)
    w['w_uk'] = _pad_heads(lw['w_uk'], QK_NOPE).astype(BF16)
    w['w_uv_t'] = _pad_heads(lw['w_uv'], V_HEAD).T.astype(BF16)
    const = lambda z, dt: jnp.asarray(np.asarray(z, np.float32), dt)
    lane = np.arange(N_HEADS * HEAD_PAD)
    w['e_kr'] = const((lane[None, :] % HEAD_PAD) == (QK_NOPE + np.arange(QK_ROPE))[:, None], BF16)
    w['g_q'] = row(jnp.pad(lw['g_q'], (0, HEAD_PAD - QK_HEAD)))
    w['g_k_nope'] = row(jnp.pad(lw['g_k'][:QK_NOPE], (0, HEAD_PAD - QK_NOPE)))
    w['g_k_rope'] = row(lw['g_k'][QK_NOPE:])
    half = QK_ROPE // 2
    wq = lw['w_uq'] * lw['g_q'][None, None, :]
    partner = jnp.concatenate([jnp.zeros_like(wq[..., :QK_NOPE]), wq[..., QK_NOPE + half:QK_HEAD],
                               wq[..., QK_NOPE:QK_NOPE + half]], axis=-1)
    w['w_uq_rot'] = _pad_heads(partner, QK_HEAD).astype(BF16)
    idx = np.arange(QK_ROPE)
    w['swap_rope'] = const(idx[:, None] == ((idx + half) % QK_ROPE)[None, :], BF16)
    blk = np.arange(2 * HEAD_PAD) // HEAD_PAD
    w['ones2'] = const(blk[:, None] == blk[None, :], BF16)
    w['v_one'] = const(((lane % HEAD_PAD) == V_HEAD).reshape(-1, 1), F32)
    w['w_uk_perm'] = lw['w_uk'].transpose(0, 2, 1).reshape(KV_LORA, QK_NOPE * N_HEADS).astype(BF16)
    w['g_k_nope_perm'] = row(jnp.repeat(lw['g_k'][:QK_NOPE], N_HEADS))
    w['g_k_rope_col'] = lw['g_k'][QK_NOPE:].reshape(QK_ROPE, 1).astype(F32)
    w['w_uv_flat'] = lw['w_uv'].reshape(KV_LORA, N_HEADS * V_HEAD).astype(BF16)
    bb_re, bb_im, pow_re, pow_im = _ssm_discretize(lw['a_re'].astype(F32), lw['a_im'].astype(F32),
                                                   lw['log_dt'].astype(F32), lw['b_re'].astype(F32),
                                                   lw['b_im'].astype(F32), n_pow)
    w['bb_re'] = _block_diag_in(bb_re)
    w['bb_im'] = _block_diag_in(bb_im)
    w['pow_re'] = pow_re
    w['pow_im'] = pow_im
    w['cc_re'] = _block_diag_out(lw['c_re'].astype(F32))
    w['cc_im'] = _block_diag_out(lw['c_im'].astype(F32))
    w['d_skip'] = row(lw['d_skip'])
    w['w_glu'] = lw['w_glu'].astype(BF16)
    woa = lw['w_oa'].reshape(N_HEADS, V_HEAD, d_model)
    w['w_oa'] = jnp.pad(woa, ((0, 0), (0, HEAD_PAD - V_HEAD), (0, 0))).reshape(N_HEADS * HEAD_PAD, d_model).astype(BF16)
    w['w_oa_flat'] = lw['w_oa'].astype(BF16)
    w['w_os'] = lw['w_os'].astype(BF16)
    w['w_out'] = lw['w_out'].astype(BF16)
    w['g_ffn'] = row(lw['g_ffn'])
    w['w_up'] = lw['w_up'].astype(BF16)
    w['conv_w'] = lw['conv_w'].astype(F32)
    w['conv_b'] = row(lw['conv_b'])
    w['w_down'] = lw['w_down'].astype(BF16)
    w['g_ple'] = row(lw['g_ple'])
    w['w_ple_gate'] = lw['w_ple_gate'].astype(BF16)
    w['w_ple_proj'] = lw['w_ple_proj'].astype(BF16)
    return w


def kernel(x_prompt, x_sample, p_prompt, p_sample, cache_ckv, cache_kr, page_table, state_ssm_re, state_ssm_im,
           state_conv, g_mix, w_in, g_cq, g_ckv, w_uq, w_uk, w_uv, g_q, g_k, a_re, a_im, log_dt, b_re, b_im,
           c_re, c_im, d_skip, w_glu, w_oa, w_os, w_out, g_ffn, w_up, conv_w, conv_b, w_down, g_ple,
           w_ple_gate, w_ple_proj):
    params = dict(g_mix=g_mix, w_in=w_in, g_cq=g_cq, g_ckv=g_ckv, w_uq=w_uq, w_uk=w_uk, w_uv=w_uv, g_q=g_q,
                  g_k=g_k, a_re=a_re, a_im=a_im, log_dt=log_dt, b_re=b_re, b_im=b_im, c_re=c_re, c_im=c_im,
                  d_skip=d_skip, w_glu=w_glu, w_oa=w_oa, w_os=w_os, w_out=w_out, g_ffn=g_ffn, w_up=w_up,
                  conv_w=conv_w, conv_b=conv_b, w_down=w_down, g_ple=g_ple, w_ple_gate=w_ple_gate,
                  w_ple_proj=w_ple_proj)
    depth = w_in.shape[0]
    bp, seq, d_model = x_prompt.shape
    bd, t_new, _ = x_sample.shape
    past_len = page_table.shape[1] * PAGE_SIZE
    d_ff = w_down.shape[1]
    n_pow = max(min(SSM_ROWS, seq) // SUBLANES, 1)

    rope_prompt = _rope_tables_q(np.arange(seq))
    rope_sample = _rope_tables_q(np.repeat(past_len + np.arange(t_new), bd))
    rope_cache = _rope_tables_k(np.arange(past_len))

    yp, ys = x_prompt, x_sample
    outs = [[] for _ in range(10)]
    for i in range(depth):
        w = _layer_weights({k: v[i] for k, v in params.items()}, d_model, n_pow)

        q, k, v, ckv, kr, u, sga, sgs = _inproj(yp, w, rope_prompt, BF16, SCALE * LOG2_E)
        att = _flash_prompt(q, k, v)
        yssm, hr, hi = _ssm_prompt(u, w)
        yp, tail = _post(yp, att, yssm, sga, sgs, p_prompt[i], w)
        outs[0].append(ckv); outs[1].append(kr); outs[4].append(hr); outs[5].append(hi)
        outs[8].append(tail[:, SUBLANES - (CONV_W - 1):, :])

        n_tok = bd * t_new
        steps = lambda z: z.transpose(1, 0, 2).reshape(1, n_tok, z.shape[-1])
        entries = lambda z: z.reshape(t_new, bd, z.shape[-1]).transpose(1, 0, 2)
        q, k, v, ckv, kr, u, sga, sgs = _inproj(steps(ys), w, rope_sample, F32, SCALE)
        ckv = entries(ckv)
        att = _sample_attention(entries(q), entries(k), ckv, cache_ckv, cache_kr, page_table, i, w, rope_cache)
        yssm, hr, hi = _ssm_sample(u.reshape(t_new, bd, -1), state_ssm_re[i], state_ssm_im[i], w)
        hist0 = state_conv[i].astype(F32).transpose(1, 0, 2).reshape((CONV_W - 1) * bd, 2 * d_ff)
        w_s = dict(w, w_oa=w['w_oa_flat'])
        y2, tail = _post(steps(ys), steps(att).astype(BF16), yssm.reshape(1, n_tok, -1), sga, sgs,
                         steps(p_sample[i]), w_s, hist0=hist0, step_rows=bd)
        ys = entries(y2)
        outs[2].append(ckv); outs[3].append(entries(kr))
        outs[6].append(hr); outs[7].append(hi)
        outs[9].append(tail.reshape(CONV_W - 1, bd, 2 * d_ff).transpose(1, 0, 2))

    st = [jnp.stack(o) for o in outs]
    return (yp, ys, st[0], st[1], st[2], st[3], st[4], st[5], st[6], st[7], st[8], st[9])
```

```python
import functools
import math

import jax
import jax.numpy as jnp
import numpy as np
from jax import lax
from jax.experimental import pallas as pl
from jax.experimental.pallas import tpu as pltpu

F32 = jnp.float32
BF16 = jnp.bfloat16

N_HEADS = 8
QK_NOPE = 64
QK_ROPE = 32
QK_HEAD = QK_NOPE + QK_ROPE
V_HEAD = 64
Q_LORA = 384
KV_LORA = 256
ROPE_THETA = 10000.0
SCALE = QK_HEAD ** -0.5
LOG2_E = math.log2(math.e)
NEG_INF = -1e30
SSM_WIDTH = 512
GROUP = 16
N_GROUPS = SSM_WIDTH // GROUP
STATE = 64
N_STATE = N_GROUPS * STATE
CONV_W = 3
EPS = 1e-6
PAGE_SIZE = 128

LANES = 128
SUBLANES = 8
HEAD_PAD = LANES
NEW_PAD = 2 * SUBLANES
V_ROWS = V_HEAD + 2 * SUBLANES
VMEM_LIMIT_BYTES = 56 * 1024 * 1024

INPROJ_ROWS = 512
FLASH_T = 512
FLASH_HEADS = 4
SSM_ROWS = 256
SSM_COLS = 1024
POST_ROWS = 256
FFN_CHUNK = 256
FFN_AHEAD = 4
SAMPLE_CHUNK_PLAN = (16, 16, 16, 8, 4, 4)
SAMPLE_AHEAD = 1
SSM_GROUP_BLOCK = 8


def _cparams(sem):
    return pltpu.CompilerParams(dimension_semantics=sem, vmem_limit_bytes=VMEM_LIMIT_BYTES)


def _const_spec(shape):
    nd = len(shape)
    return pl.BlockSpec(shape, lambda *_: (0,) * nd, pipeline_mode=pl.Buffered(1))


def _rms(x, g):
    return x * lax.rsqrt(jnp.mean(x * x, axis=-1, keepdims=True) + EPS) * g


def _bdot(a, b):
    return jnp.dot(a.astype(BF16), b, preferred_element_type=F32)


def _dot_nt(a, b):
    return lax.dot_general(a, b, (((1,), (1,)), ((), ())), preferred_element_type=F32)


def _ssm_disc_kernel(are_ref, aim_ref, ldt_ref, arer_ref, aimr_ref, ldtr_ref, bre_ref, bim_ref,
                     bbre_ref, bbim_ref, pre_ref, pim_ref, *, n_pow):
    def zoh(a_re, a_im, ldt):
        dt = jnp.exp(ldt)
        mag = jnp.exp(dt * a_re)
        ab_re = mag * jnp.cos(dt * a_im)
        ab_im = mag * jnp.sin(dt * a_im)
        return ab_re, ab_im

    a_re = arer_ref[...]
    a_im = aimr_ref[...]
    ab_re, ab_im = zoh(a_re, a_im, ldtr_ref[...])
    den = a_re * a_re + a_im * a_im
    nr = ab_re - 1.0
    f_re = (nr * a_re + ab_im * a_im) / den
    f_im = (ab_im * a_re - nr * a_im) / den
    b_re = bre_ref[...]
    b_im = bim_ref[...]
    bbre_ref[...] = f_re * b_re - f_im * b_im
    bbim_ref[...] = f_re * b_im + f_im * b_re

    p_re, p_im = zoh(are_ref[...], aim_ref[...], ldt_ref[...])
    c_re, c_im = p_re, p_im
    for j in range(n_pow):
        pre_ref[j] = c_re
        pim_ref[j] = c_im
        c_re, c_im = c_re * p_re - c_im * p_im, c_re * p_im + c_im * p_re


def _ssm_discretize(a_re, a_im, log_dt, b_re, b_im, n_pow):
    g, p = a_re.shape
    rows = g * GROUP
    rep = lambda z: jnp.repeat(z, GROUP, axis=0)
    bt = lambda z: z.transpose(0, 2, 1).reshape(rows, p)
    ldt = log_dt.reshape(g, 1)
    out_shape = (jax.ShapeDtypeStruct((rows, p), F32), jax.ShapeDtypeStruct((rows, p), F32),
                 jax.ShapeDtypeStruct((n_pow, g, p), F32), jax.ShapeDtypeStruct((n_pow, g, p), F32))
    bb_re, bb_im, pow_re, pow_im = pl.pallas_call(
        functools.partial(_ssm_disc_kernel, n_pow=n_pow),
        out_shape=out_shape, name="ssm_disc",
    )(a_re, a_im, ldt, rep(a_re), rep(a_im), rep(ldt), bt(b_re), bt(b_im))
    return bb_re, bb_im, pow_re.reshape(n_pow, g * p), pow_im.reshape(n_pow, g * p)


def _block_diag_in(bb):
    nb = N_GROUPS // SSM_GROUP_BLOCK
    z = jnp.tile(bb.reshape(nb, SSM_GROUP_BLOCK * GROUP, STATE), (1, 1, SSM_GROUP_BLOCK))
    mask = (np.arange(SSM_GROUP_BLOCK * GROUP) // GROUP)[:, None] == (np.arange(SSM_GROUP_BLOCK * STATE) // STATE)[None, :]
    return (z * mask.astype(np.float32)).astype(BF16)


def _block_diag_out(c):
    nb = N_GROUPS // SSM_GROUP_BLOCK
    z = jnp.tile(c.transpose(0, 2, 1).reshape(nb, SSM_GROUP_BLOCK * STATE, GROUP), (1, 1, SSM_GROUP_BLOCK))
    mask = (np.arange(SSM_GROUP_BLOCK * STATE) // STATE)[:, None] == (np.arange(SSM_GROUP_BLOCK * GROUP) // GROUP)[None, :]
    return (z * mask.astype(np.float32)).astype(BF16)


OFF_CQ = 0
OFF_CKV = OFF_CQ + Q_LORA
OFF_U = OFF_CKV + KV_LORA


def _head_sumsq(z, ones2_ref):
    sq = (z * z).astype(BF16)
    span = ones2_ref.shape[0]
    parts = [jnp.dot(sq[:, j:j + span], ones2_ref[...], preferred_element_type=F32)
             for j in range(0, z.shape[-1], span)]
    return jnp.concatenate(parts, axis=-1)


def _inproj_kernel(x_ref, gmix_ref, win_ref, gcq_ref, gckv_ref, wuq_ref, wuqr_ref, wuk_ref, ekr_ref, swap_ref,
                   ones2_ref, wuvt_ref, vone_ref, gq_ref, gkn_ref, gkr_ref, rc_ref, rs_ref, cc_ref, cs_ref,
                   q_ref, k_ref, v_ref, ckv_ref, kr_ref, u_ref, sga_ref, sgs_ref, *, d_model, q_scale):
    off_ga = OFF_U + SSM_WIDTH
    off_gs = off_ga + d_model
    off_kr = off_gs + d_model
    inv = 1.0 / QK_HEAD
    blocks = [slice(h * HEAD_PAD, (h + 1) * HEAD_PAD) for h in range(N_HEADS)]
    xn = _rms(x_ref[...], gmix_ref[...])
    proj = _bdot(xn, win_ref[...])
    cq = _rms(proj[:, OFF_CQ:OFF_CKV], gcq_ref[...])
    ckv = _rms(proj[:, OFF_CKV:OFF_U], gckv_ref[...])
    kr = proj[:, off_kr:off_kr + QK_ROPE]
    u_ref[...] = proj[:, OFF_U:off_ga]
    sga_ref[...] = jax.nn.sigmoid(proj[:, off_ga:off_gs]).astype(sga_ref.dtype)
    sgs_ref[...] = jax.nn.sigmoid(proj[:, off_gs:off_kr]).astype(sgs_ref.dtype)
    ckv_ref[...] = ckv
    kr_ref[...] = kr
    ckv_b = ckv.astype(BF16)
    v_ref[...] = (_dot_nt(wuvt_ref[...], ckv_b) + vone_ref[...]).astype(v_ref.dtype)

    cq_b = cq.astype(BF16)
    qf = jnp.dot(cq_b, wuq_ref[...], preferred_element_type=F32)
    qrot = jnp.dot(cq_b, wuqr_ref[...], preferred_element_type=F32)
    rn_q = lax.rsqrt(_head_sumsq(qf, ones2_ref) * inv + EPS)
    tc = rc_ref[...] * (gq_ref[...] * q_scale)
    ts = rs_ref[...] * q_scale
    for blk in blocks:
        q_ref[:, blk] = (rn_q[:, blk] * (qf[:, blk] * tc + qrot[:, blk] * ts)).astype(q_ref.dtype)

    kn = jnp.dot(ckv_b, wuk_ref[...], preferred_element_type=F32)
    krg = kr * gkr_ref[...]
    krr = krg * cc_ref[...] + _bdot(krg, swap_ref[...]) * cs_ref[...]
    placed = _bdot(krr, ekr_ref[...])
    r2 = jnp.sum(kr * kr, axis=-1, keepdims=True)
    rn_k = lax.rsqrt((_head_sumsq(kn, ones2_ref) + r2) * inv + EPS)
    for blk in blocks:
        k_ref[:, blk] = (rn_k[:, blk] * (kn[:, blk] * gkn_ref[...] + placed[:, blk])).astype(k_ref.dtype)


def _inproj(x, w, rope, act_dtype, q_scale):
    nb, rows, d = x.shape
    tr = min(INPROJ_ROWS, rows)
    assert rows % tr == 0
    hp = N_HEADS * HEAD_PAD
    row_spec = lambda n: pl.BlockSpec((None, tr, n), lambda b, t: (b, t, 0))
    tab_specs = [pl.BlockSpec((tr, z.shape[-1]), lambda b, t: (t, 0)) for z in rope]
    weights = (w['g_mix'], w['w_in'], w['g_cq'], w['g_ckv'], w['w_uq'], w['w_uq_rot'], w['w_uk'], w['e_kr'],
               w['swap_rope'], w['ones2'], w['w_uv_t'], w['v_one'], w['g_q'], w['g_k_nope'], w['g_k_rope'])
    out_shape = (
        jax.ShapeDtypeStruct((nb, rows, hp), act_dtype),
        jax.ShapeDtypeStruct((nb, rows, hp), act_dtype),
        jax.ShapeDtypeStruct((nb, hp, rows), act_dtype),
        jax.ShapeDtypeStruct((nb, rows, KV_LORA), F32),
        jax.ShapeDtypeStruct((nb, rows, QK_ROPE), F32),
        jax.ShapeDtypeStruct((nb, rows, SSM_WIDTH), F32),
        jax.ShapeDtypeStruct((nb, rows, d), BF16),
        jax.ShapeDtypeStruct((nb, rows, d), BF16),
    )
    out_specs = [row_spec(s.shape[-1]) for s in out_shape]
    out_specs[2] = pl.BlockSpec((None, hp, tr), lambda b, t: (b, 0, t))
    return pl.pallas_call(
        functools.partial(_inproj_kernel, d_model=d, q_scale=q_scale),
        grid=(nb, rows // tr),
        in_specs=[row_spec(d)] + [_const_spec(z.shape) for z in weights] + tab_specs,
        out_specs=tuple(out_specs),
        out_shape=out_shape,
        compiler_params=_cparams(("parallel", "parallel")),
        name="inproj",
    )(x, *weights, *rope)


def _flash_kernel(q_ref, k_ref, vt_ref, o_ref, s0, s1, m_sc, acc_sc, *, t, nh):
    qi = pl.program_id(2)
    heads = [slice(h * HEAD_PAD, (h + 1) * HEAD_PAD) for h in range(nh)]
    m_sc[...] = jnp.full(m_sc.shape, NEG_INF, F32)
    acc_sc[...] = jnp.zeros(acc_sc.shape, F32)

    def scores(kj, dst):
        start = pl.multiple_of(kj * t, t)
        for h, hs in enumerate(heads):
            dst[h] = _dot_nt(k_ref[pl.ds(start, t), hs], q_ref[:, hs])

    def fold(kj, src, diagonal):
        start = pl.multiple_of(kj * t, t)
        if diagonal:
            keep = lax.broadcasted_iota(jnp.int32, (t, t), 0) <= lax.broadcasted_iota(jnp.int32, (t, t), 1)
        for h, hs in enumerate(heads):
            st = src[h]
            if diagonal:
                st = jnp.where(keep, st, NEG_INF)
            m = m_sc[h]
            m_new = jnp.maximum(m, jnp.max(st, axis=0, keepdims=True))
            p = jnp.exp2(st - m_new)
            corr = jnp.exp2(m - m_new)
            vrows = slice(h * HEAD_PAD, h * HEAD_PAD + V_ROWS)
            acc_sc[h] = acc_sc[h] * corr + jnp.dot(vt_ref[vrows, pl.ds(start, t)], p.astype(BF16),
                                                   preferred_element_type=F32)
            m_sc[h] = m_new

    scores(0, s0)

    def pair(kk, carry):
        j = 2 * kk
        scores(j + 1, s1)
        fold(j, s0, False)
        scores(j + 2, s0)
        fold(j + 1, s1, False)
        return carry

    lax.fori_loop(0, qi // 2, pair, 0)

    @pl.when(qi % 2 == 0)
    def _():
        fold(qi, s0, True)

    @pl.when(qi % 2 == 1)
    def _():
        scores(qi, s1)
        fold(qi - 1, s0, False)
        fold(qi, s1, True)

    for h0 in range(0, nh, 2):
        pair = []
        for h in (h0, h0 + 1):
            acc = acc_sc[h]
            pair.append(acc[0:V_HEAD, :] / acc[V_HEAD:V_HEAD + 1, :])
        o_ref[:, h0 * V_HEAD:(h0 + 2) * V_HEAD] = jnp.transpose(jnp.concatenate(pair, axis=0)).astype(o_ref.dtype)


def _flash_prompt(q, k, vt):
    b, s, hp = q.shape
    tq = min(FLASH_T, s)
    nh = FLASH_HEADS
    assert s % tq == 0 and N_HEADS % nh == 0 and nh % 2 == 0
    return pl.pallas_call(
        functools.partial(_flash_kernel, t=tq, nh=nh),
        grid=(b, N_HEADS // nh, s // tq),
        in_specs=[pl.BlockSpec((None, tq, nh * HEAD_PAD), lambda bi, h, qi: (bi, qi, h)),
                  pl.BlockSpec((None, s, nh * HEAD_PAD), lambda bi, h, qi: (bi, 0, h)),
                  pl.BlockSpec((None, nh * HEAD_PAD, s), lambda bi, h, qi: (bi, h, 0))],
        out_specs=pl.BlockSpec((None, tq, nh * V_HEAD), lambda bi, h, qi: (bi, qi, h)),
        out_shape=jax.ShapeDtypeStruct((b, s, N_HEADS * V_HEAD), BF16),
        scratch_shapes=[pltpu.VMEM((nh, tq, tq), F32), pltpu.VMEM((nh, tq, tq), F32),
                        pltpu.VMEM((nh, 1, tq), F32), pltpu.VMEM((nh, V_ROWS, tq), F32)],
        compiler_params=_cparams(("parallel", "parallel", "arbitrary")),
        name="flash_prompt",
    )(q, k, vt)


def _sample_attn_kernel(pt_ref, qn_ref, qr_ref, qrs_ref, qf_ref, knew_ref, cnew_ref, ca_ref, cb_ref,
                        wukp_ref, gkn_ref, gkr_ref, wuv_ref, hmask_ref, ckv_hbm, kr_hbm, o_ref,
                        cf, rf, cbuf, sem_c, sem_r, *, layer, n_pages, n_new, chunks):
    b = pl.program_id(0)
    slot = b % 2
    nq = n_new * N_HEADS
    first_page = [sum(chunks[:i]) for i in range(len(chunks))]

    def pages_of(ch):
        return range(first_page[ch], first_page[ch] + chunks[ch])

    def keys_of(ch):
        return slice(first_page[ch] * PAGE_SIZE, (first_page[ch] + chunks[ch]) * PAGE_SIZE)

    def page_copies(entry, sl, j):
        pg = pt_ref[entry * n_pages + j]
        return (pltpu.make_async_copy(ckv_hbm.at[layer, pg], cf.at[sl, j], sem_c.at[sl]),
                pltpu.make_async_copy(kr_hbm.at[layer, pg], rf.at[sl, j], sem_r.at[sl]))

    def start_entry(entry, sl):
        def body(j, carry):
            for cp in page_copies(entry, sl, j):
                cp.start()
            return carry
        lax.fori_loop(0, n_pages, body, 0, unroll=min(8, n_pages))

    @pl.when(b == 0)
    def _():
        start_entry(0, 0)

    @pl.when(b + 1 < pl.num_programs(0))
    def _():
        start_entry(b + 1, 1 - slot)

    pltpu.make_async_copy(ckv_hbm.at[layer, pl.ds(0, n_pages)], cf.at[slot], sem_c.at[slot]).wait()
    pltpu.make_async_copy(kr_hbm.at[layer, pl.ds(0, n_pages)], rf.at[slot], sem_r.at[slot]).wait()

    qn = (qn_ref[...] * gkn_ref[...]).astype(BF16)
    qabs = _dot_nt(qn, wukp_ref[...]).astype(BF16)

    zrows = LANES - NEW_PAD
    knew = jnp.concatenate([knew_ref[...].astype(BF16), jnp.zeros((zrows, knew_ref.shape[-1]), BF16)], axis=0)
    cnew = jnp.concatenate([cnew_ref[...].astype(BF16), jnp.zeros((zrows, KV_LORA), BF16)], axis=0)
    s_new = _dot_nt(qf_ref[...].astype(BF16), knew)
    tok = lax.broadcasted_iota(jnp.int32, (nq, LANES), 0) // N_HEADS
    key = lax.broadcasted_iota(jnp.int32, (nq, LANES), 1)
    scored = [(jnp.where(key <= tok, s_new, NEG_INF), cnew)]

    qr = qr_ref[...].astype(BF16)
    qrs = qrs_ref[...].astype(BF16)
    def front(ch):
        for i in pages_of(ch):
            cbuf[i * PAGE_SIZE:(i + 1) * PAGE_SIZE, :] = cf[slot, i].astype(BF16)
        c = cbuf[keys_of(ch), :]
        kn = jnp.dot(c, wukp_ref[...], preferred_element_type=F32)
        sq = kn * kn
        ss = sq[:, 0:LANES]
        for j in range(1, N_HEADS * QK_NOPE // LANES):
            ss = ss + sq[:, j * LANES:(j + 1) * LANES]
        return ch, c, ss, _dot_nt(qabs, c)

    def back(args):
        ch, c, ss, s_nope = args
        span = keys_of(ch)
        ss_t = jnp.transpose(ss)
        ss_h = ss_t[0:N_HEADS]
        for j in range(1, LANES // N_HEADS):
            ss_h = ss_h + ss_t[j * N_HEADS:(j + 1) * N_HEADS]
        r_t = jnp.concatenate([rf[slot, i] for i in pages_of(ch)], axis=1)
        r2 = jnp.sum(r_t * r_t, axis=0, keepdims=True)
        rn_h = lax.rsqrt((ss_h + r2) * (1.0 / QK_HEAD) + EPS)
        rn = jnp.concatenate([rn_h] * n_new, axis=0)
        rg = r_t * gkr_ref[...]
        s_rope = (jnp.dot(qr, (rg * ca_ref[:, span]).astype(BF16), preferred_element_type=F32)
                  + jnp.dot(qrs, (rg * cb_ref[:, span]).astype(BF16), preferred_element_type=F32))
        return rn * (s_nope + s_rope), c

    def weights(s):
        m = jnp.max(s, axis=-1, keepdims=True)
        p = jnp.exp(s - m)
        return m, p.astype(BF16), jnp.sum(p, axis=-1, keepdims=True)

    n_chunks = len(chunks)
    ahead = [front(ch) for ch in range(min(SAMPLE_AHEAD, n_chunks))]
    pending = [(weights(scored[0][0]), scored[0][1])]
    parts = []

    def value_product(item):
        (m, p, l), vals = item
        parts.append((m, l, jnp.dot(p, vals, preferred_element_type=F32)))

    for ch in range(n_chunks):
        cur = ahead.pop(0)
        if ch + SAMPLE_AHEAD < n_chunks:
            ahead.append(front(ch + SAMPLE_AHEAD))
        s, vals = back(cur)
        pending.append((weights(s), vals))
        value_product(pending.pop(0))
    while pending:
        value_product(pending.pop(0))

    m = parts[0][0]
    for pm, _, _ in parts[1:]:
        m = jnp.maximum(m, pm)
    l = jnp.zeros_like(m)
    acc = jnp.zeros((nq, KV_LORA), F32)
    for pm, pl_, pacc in parts:
        wgt = jnp.exp(pm - m)
        l = l + pl_ * wgt
        acc = acc + pacc * wgt
    o_lat = acc / l
    full = _bdot(o_lat, wuv_ref[...]) * hmask_ref[...]
    o_ref[...] = jnp.sum(full.reshape(n_new, N_HEADS, N_HEADS * V_HEAD), axis=1)


def _sample_attention(q, k, ckv, cache_ckv, cache_kr, page_table, layer, w, rope_k):
    bd, t, hp = q.shape
    n_logical = page_table.shape[1]
    if sum(SAMPLE_CHUNK_PLAN) == n_logical:
        chunks = SAMPLE_CHUNK_PLAN
    else:
        size = min(SAMPLE_CHUNK_PLAN[-1], n_logical)
        assert n_logical % size == 0
        chunks = (size,) * (n_logical // size)
    assert t <= NEW_PAD
    nq = t * N_HEADS
    n_keys = n_logical * PAGE_SIZE
    head_mask = ((np.arange(nq) % N_HEADS)[:, None] == (np.arange(N_HEADS * V_HEAD) // V_HEAD)[None, :]
                 ).astype(np.float32)
    qh = q.reshape(bd, t, N_HEADS, HEAD_PAD)
    eye = np.eye(N_HEADS, dtype=np.float32)

    qn = jnp.einsum('bthd,hk->bthdk', qh[..., :QK_NOPE], eye).reshape(bd, nq, QK_NOPE * N_HEADS)
    qr = qh[..., QK_NOPE:QK_HEAD].reshape(bd, nq, QK_ROPE)
    qrs = jnp.concatenate([qr[..., QK_ROPE // 2:], qr[..., :QK_ROPE // 2]], axis=-1)
    qf = jnp.einsum('bthl,hk->bthkl', qh, eye).reshape(bd, nq, hp)
    pad_t = lambda z: jnp.pad(z, ((0, 0), (0, NEW_PAD - t), (0, 0)))
    knew = pad_t(k)
    cnew = pad_t(ckv)
    ca, cb = rope_k
    cache_kr_t = jnp.swapaxes(cache_kr, 2, 3)

    def bspec(rows, n):
        return pl.BlockSpec((None, rows, n), lambda b, pt: (b, 0, 0))

    def wspec(shape):
        nd = len(shape)
        return pl.BlockSpec(shape, lambda b, pt: (0,) * nd)

    weights = (ca, cb, w['w_uk_perm'], w['g_k_nope_perm'], w['g_k_rope_col'], w['w_uv_flat'], head_mask)
    in_specs = ([bspec(nq, QK_NOPE * N_HEADS), bspec(nq, QK_ROPE), bspec(nq, QK_ROPE), bspec(nq, hp),
                 bspec(NEW_PAD, hp), bspec(NEW_PAD, KV_LORA)]
                + [wspec(z.shape) for z in weights]
                + [pl.BlockSpec(memory_space=pl.ANY), pl.BlockSpec(memory_space=pl.ANY)])
    grid_spec = pltpu.PrefetchScalarGridSpec(
        num_scalar_prefetch=1, grid=(bd,), in_specs=in_specs,
        out_specs=pl.BlockSpec((None, t, N_HEADS * V_HEAD), lambda b, pt: (b, 0, 0)),
        scratch_shapes=[pltpu.VMEM((2, n_logical, PAGE_SIZE, KV_LORA), F32),
                        pltpu.VMEM((2, n_logical, QK_ROPE, PAGE_SIZE), F32),
                        pltpu.VMEM((n_keys, KV_LORA), BF16),
                        pltpu.SemaphoreType.DMA((2,)), pltpu.SemaphoreType.DMA((2,))])
    return pl.pallas_call(
        functools.partial(_sample_attn_kernel, layer=layer, n_pages=n_logical, n_new=t, chunks=chunks),
        grid_spec=grid_spec,
        out_shape=jax.ShapeDtypeStruct((bd, t, N_HEADS * V_HEAD), F32),
        compiler_params=_cparams(("arbitrary",)),
        name="sample_attn",
    )(page_table.reshape(-1), qn, qr, qrs, qf, knew, cnew, *weights, cache_ckv, cache_kr_t)


def _ssm_in(u_b, bre_ref, bim_ref, store_re, store_im):
    nb = bre_ref.shape[0]
    kw = bre_ref.shape[1]
    nw = bre_ref.shape[2]
    for kb in range(nb):
        blk = u_b[:, kb * kw:(kb + 1) * kw]
        store_re(kb * nw, nw, jnp.dot(blk, bre_ref[kb], preferred_element_type=F32))
        store_im(kb * nw, nw, jnp.dot(blk, bim_ref[kb], preferred_element_type=F32))


def _ssm_out(h_re, h_im, cre_ref, cim_ref):
    nb = cre_ref.shape[0]
    kw = cre_ref.shape[1]
    parts = []
    for kb in range(nb):
        parts.append(jnp.dot(h_re[:, kb * kw:(kb + 1) * kw].astype(BF16), cre_ref[kb], preferred_element_type=F32)
                     - jnp.dot(h_im[:, kb * kw:(kb + 1) * kw].astype(BF16), cim_ref[kb], preferred_element_type=F32))
    return jnp.concatenate(parts, axis=-1)


def _glu(y, wglu_ref):
    g = _bdot(jax.nn.gelu(y), wglu_ref[...])
    half = g.shape[-1] // 2
    return g[:, :half] * jax.nn.sigmoid(g[:, half:])


def _ssm_prompt_kernel(u_ref, perm_ref, permt_ref, bre_ref, bim_ref, pre_ref, pim_ref, cre_ref, cim_ref,
                       dskip_ref, wglu_ref, ys_ref, hre_out, him_out,
                       hre, him, car_re, car_im, hin_re, hin_im, *, rows, sub):
    ti = pl.program_id(1)

    @pl.when(ti == 0)
    def _():
        car_re[...] = jnp.zeros(car_re.shape, F32)
        car_im[...] = jnp.zeros(car_im.shape, F32)

    u = u_ref[...]
    u_hi = u.astype(BF16)
    u_lo = (u - u_hi.astype(F32)).astype(BF16)
    perm = perm_ref[...]
    up = jnp.dot(perm, u_hi, preferred_element_type=F32) + jnp.dot(perm, u_lo, preferred_element_type=F32)

    def st_re(c0, n, val):
        hre[:, c0:c0 + n] = val

    def st_im(c0, n, val):
        him[:, c0:c0 + n] = val

    _ssm_in(up.astype(BF16), bre_ref, bim_ref, st_re, st_im)

    n_cb = N_STATE // SSM_COLS
    for cb in range(n_cb):
        cols = slice(cb * SSM_COLS, (cb + 1) * SSM_COLS)
        a_re = jnp.broadcast_to(pre_ref[0:1, cols], (SUBLANES, SSM_COLS))
        a_im = jnp.broadcast_to(pim_ref[0:1, cols], (SUBLANES, SSM_COLS))

        def local(j, st):
            s_re, s_im = st
            r0 = pl.multiple_of(j * SUBLANES, SUBLANES)
            n_re = a_re * s_re - a_im * s_im + hre[pl.ds(r0, SUBLANES), cols]
            n_im = a_re * s_im + a_im * s_re + him[pl.ds(r0, SUBLANES), cols]
            hre[pl.ds(r0, SUBLANES), cols] = n_re
            him[pl.ds(r0, SUBLANES), cols] = n_im
            return n_re, n_im

        zero = jnp.zeros((SUBLANES, SSM_COLS), F32)
        e_re, e_im = lax.fori_loop(0, sub, local, (zero, zero))

        as_re = pre_ref[sub - 1:sub, cols]
        as_im = pim_ref[sub - 1:sub, cols]
        c_re = car_re[:, cols]
        c_im = car_im[:, cols]
        for s in range(SUBLANES):
            hin_re[s:s + 1, cols] = c_re
            hin_im[s:s + 1, cols] = c_im
            c_re, c_im = (as_re * c_re - as_im * c_im + e_re[s:s + 1],
                          as_re * c_im + as_im * c_re + e_im[s:s + 1])
        car_re[:, cols] = c_re
        car_im[:, cols] = c_im
        g_re = hin_re[:, cols]
        g_im = hin_im[:, cols]

        def fix(j, d):
            d_re, d_im = d
            d_re, d_im = a_re * d_re - a_im * d_im, a_re * d_im + a_im * d_re
            r0 = pl.multiple_of(j * SUBLANES, SUBLANES)
            hre[pl.ds(r0, SUBLANES), cols] = hre[pl.ds(r0, SUBLANES), cols] + d_re
            him[pl.ds(r0, SUBLANES), cols] = him[pl.ds(r0, SUBLANES), cols] + d_im
            return d_re, d_im

        lax.fori_loop(0, sub, fix, (g_re, g_im))

    y = _ssm_out(hre[...], him[...], cre_ref, cim_ref) + dskip_ref[...] * up
    ys = _glu(y, wglu_ref).astype(BF16)
    ys_ref[...] = jnp.dot(permt_ref[...], ys, preferred_element_type=F32).astype(ys_ref.dtype)

    @pl.when(ti == pl.num_programs(1) - 1)
    def _():
        hre_out[...] = car_re[...]
        him_out[...] = car_im[...]


def _ssm_prompt(u, w):
    b, s, width = u.shape
    rows = min(SSM_ROWS, s)
    assert s % rows == 0 and rows % SUBLANES == 0
    sub = rows // SUBLANES
    r = np.arange(rows)
    t_of_r = (r % SUBLANES) * sub + r // SUBLANES
    perm = jnp.asarray((t_of_r[:, None] == np.arange(rows)[None, :]).astype(np.float32), BF16)
    pre = w['pow_re'][:sub]
    pim = w['pow_im'][:sub]
    weights = (perm, perm.T, w['bb_re'], w['bb_im'], pre, pim, w['cc_re'], w['cc_im'], w['d_skip'], w['w_glu'])
    ys, h_re, h_im = pl.pallas_call(
        functools.partial(_ssm_prompt_kernel, rows=rows, sub=sub),
        grid=(b, s // rows),
        in_specs=[pl.BlockSpec((None, rows, width), lambda bi, t: (bi, t, 0))]
                 + [_const_spec(z.shape) for z in weights],
        out_specs=(pl.BlockSpec((None, rows, width), lambda bi, t: (bi, t, 0)),
                   pl.BlockSpec((None, 1, N_STATE), lambda bi, t: (bi, 0, 0)),
                   pl.BlockSpec((None, 1, N_STATE), lambda bi, t: (bi, 0, 0))),
        out_shape=(jax.ShapeDtypeStruct((b, s, width), BF16),
                   jax.ShapeDtypeStruct((b, 1, N_STATE), F32),
                   jax.ShapeDtypeStruct((b, 1, N_STATE), F32)),
        scratch_shapes=[pltpu.VMEM((rows, N_STATE), F32), pltpu.VMEM((rows, N_STATE), F32),
                        pltpu.VMEM((1, N_STATE), F32), pltpu.VMEM((1, N_STATE), F32),
                        pltpu.VMEM((SUBLANES, N_STATE), F32), pltpu.VMEM((SUBLANES, N_STATE), F32)],
        compiler_params=_cparams(("parallel", "arbitrary")),
        name="ssm_prompt",
    )(u, *weights)
    return ys, h_re.reshape(b, N_GROUPS, STATE), h_im.reshape(b, N_GROUPS, STATE)


def _ssm_sample_kernel(u_ref, h0re_ref, h0im_ref, bre_ref, bim_ref, pre_ref, pim_ref, cre_ref, cim_ref,
                       dskip_ref, wglu_ref, ys_ref, hre_out, him_out, bu_re, bu_im, *, steps):
    a_re = pre_ref[0:1, :]
    a_im = pim_ref[0:1, :]
    h_re = h0re_ref[...]
    h_im = h0im_ref[...]

    def st_re(c0, n, val):
        bu_re[:, c0:c0 + n] = val

    def st_im(c0, n, val):
        bu_im[:, c0:c0 + n] = val

    for t in range(steps):
        u = u_ref[t]
        _ssm_in(u.astype(BF16), bre_ref, bim_ref, st_re, st_im)
        h_re, h_im = (a_re * h_re - a_im * h_im + bu_re[...], a_re * h_im + a_im * h_re + bu_im[...])
        y = _ssm_out(h_re, h_im, cre_ref, cim_ref) + dskip_ref[...] * u
        ys_ref[t] = _glu(y, wglu_ref).astype(ys_ref.dtype)
    hre_out[...] = h_re
    him_out[...] = h_im


def _ssm_sample(ut, h0_re, h0_im, w):
    t, bd, width = ut.shape
    ys, h_re, h_im = pl.pallas_call(
        functools.partial(_ssm_sample_kernel, steps=t),
        out_shape=(jax.ShapeDtypeStruct((t, bd, width), BF16),
                   jax.ShapeDtypeStruct((bd, N_STATE), F32),
                   jax.ShapeDtypeStruct((bd, N_STATE), F32)),
        scratch_shapes=[pltpu.VMEM((bd, N_STATE), F32), pltpu.VMEM((bd, N_STATE), F32)],
        compiler_params=pltpu.CompilerParams(vmem_limit_bytes=VMEM_LIMIT_BYTES),
        name="ssm_sample",
    )(ut, h0_re.reshape(bd, N_STATE).astype(F32), h0_im.reshape(bd, N_STATE).astype(F32),
      w['bb_re'], w['bb_im'], w['pow_re'][:1], w['pow_im'][:1], w['cc_re'], w['cc_im'], w['d_skip'], w['w_glu'])
    return ys, h_re.reshape(bd, N_GROUPS, STATE), h_im.reshape(bd, N_GROUPS, STATE)


def _post_kernel(*refs, rows, d_ff, row_major):
    if row_major:
        (x_ref, att_ref, ys_ref, sga_ref, sgs_ref, p_ref, woa_ref, wos_ref, wout_ref, gffn_ref, wup_ref,
         cw_ref, cbias_ref, wdown_ref, gple_ref, wpg_ref, wpp_ref, y_ref, tail_ref, hist) = refs
        hist0_ref = None
    else:
        (x_ref, att_ref, ys_ref, sga_ref, sgs_ref, p_ref, hist0_ref, woa_ref, wos_ref, wout_ref, gffn_ref, wup_ref,
         cw_ref, cbias_ref, wdown_ref, gple_ref, wpg_ref, wpp_ref, y_ref, tail_ref, hist) = refs
    ti = pl.program_id(1)
    pad = SUBLANES

    mixed = (sga_ref[...].astype(F32) * jnp.dot(att_ref[...], woa_ref[...], preferred_element_type=F32)
             + sgs_ref[...].astype(F32) * jnp.dot(ys_ref[...], wos_ref[...], preferred_element_type=F32))
    x1 = x_ref[...] + _bdot(mixed, wout_ref[...])
    xn = _rms(x1, gffn_ref[...]).astype(BF16)

    @pl.when(ti == 0)
    def _():
        if row_major:
            hist[0:pad, :] = jnp.zeros((pad, 2 * d_ff), F32)
        else:
            hist[...] = hist0_ref[...]

    def up_cols(c0):
        cols = slice(c0, c0 + FFN_CHUNK)
        up = jnp.dot(xn, wup_ref[:, cols], preferred_element_type=F32)
        if row_major:
            hist[pad:pad + rows, cols] = up
            s1 = hist[pad - 1:pad - 1 + rows, cols]
            s2 = hist[pad - 2:pad - 2 + rows, cols]
        else:
            s2 = hist[0:rows, cols]
            s1 = hist[rows:2 * rows, cols]
            hist[0:rows, cols] = s1
            hist[rows:2 * rows, cols] = up
        return cols, up, s1, s2

    def conv(args):
        cols, up, s1, s2 = args
        return cbias_ref[:, cols] + s2 * cw_ref[0:1, cols] + s1 * cw_ref[1:2, cols] + up * cw_ref[2:3, cols]

    n_chunks = d_ff // FFN_CHUNK
    ahead = [(up_cols(c * FFN_CHUNK), up_cols(d_ff + c * FFN_CHUNK)) for c in range(min(FFN_AHEAD, n_chunks))]
    f = jnp.zeros((rows, x1.shape[-1]), F32)
    for c in range(n_chunks):
        cur = ahead.pop(0)
        if c + FFN_AHEAD < n_chunks:
            ahead.append((up_cols((c + FFN_AHEAD) * FFN_CHUNK), up_cols(d_ff + (c + FFN_AHEAD) * FFN_CHUNK)))
        hdn = (jax.nn.gelu(conv(cur[0])) * conv(cur[1])).astype(BF16)
        f = f + jnp.dot(hdn, wdown_ref[c * FFN_CHUNK:(c + 1) * FFN_CHUNK, :], preferred_element_type=F32)
    x2 = x1 + f

    if row_major:
        tail = hist[rows:rows + pad, :]
        hist[0:pad, :] = tail
        tail_ref[...] = tail
    else:
        @pl.when(ti == pl.num_programs(1) - 1)
        def _():
            tail_ref[...] = hist[...]

    gate = jax.nn.sigmoid(_bdot(_rms(x2, gple_ref[...]), wpg_ref[...]))
    y_ref[...] = x2 + gate * _bdot(p_ref[...], wpp_ref[...])


def _post(x, att, ys, sga, sgs, p, w, hist0=None, step_rows=None):
    nb, total, d = x.shape
    d_ff = w['w_down'].shape[0]
    assert d_ff % FFN_CHUNK == 0
    row_major = hist0 is None
    rows = min(POST_ROWS, total) if row_major else step_rows
    assert total % rows == 0
    row_spec = lambda n: pl.BlockSpec((None, rows, n), lambda b, t: (b, t, 0))
    acts = [x, att, ys, sga, sgs, p]
    act_specs = [row_spec(z.shape[-1]) for z in acts]
    weights = (w['w_oa'], w['w_os'], w['w_out'], w['g_ffn'], w['w_up'], w['conv_w'], w['conv_b'], w['w_down'],
               w['g_ple'], w['w_ple_gate'], w['w_ple_proj'])
    if row_major:
        hist_rows = SUBLANES
        tail_spec = pl.BlockSpec((None, hist_rows, 2 * d_ff), lambda b, t: (b, 0, 0))
        tail_shape = jax.ShapeDtypeStruct((nb, hist_rows, 2 * d_ff), F32)
        scratch = pltpu.VMEM((rows + hist_rows, 2 * d_ff), F32)
    else:
        assert nb == 1
        hist_rows = 2 * rows
        acts.append(hist0)
        act_specs.append(_const_spec(hist0.shape))
        tail_spec = pl.BlockSpec((hist_rows, 2 * d_ff), lambda b, t: (0, 0), pipeline_mode=pl.Buffered(1))
        tail_shape = jax.ShapeDtypeStruct((hist_rows, 2 * d_ff), F32)
        scratch = pltpu.VMEM((hist_rows, 2 * d_ff), F32)
    return pl.pallas_call(
        functools.partial(_post_kernel, rows=rows, d_ff=d_ff, row_major=row_major),
        grid=(nb, total // rows),
        in_specs=act_specs + [_const_spec(z.shape) for z in weights],
        out_specs=(row_spec(d), tail_spec),
        out_shape=(jax.ShapeDtypeStruct((nb, total, d), F32), tail_shape),
        scratch_shapes=[scratch],
        compiler_params=_cparams(("parallel", "arbitrary")),
        name="post_prompt" if row_major else "post_sample",
    )(*acts, *weights)


def _rope_cos_sin(pos):
    inv_freq = np.power(np.float32(ROPE_THETA), -np.arange(0, QK_ROPE, 2, dtype=np.float32) / np.float32(QK_ROPE))
    ang = (np.asarray(pos, np.float32)[:, None] * inv_freq[None, :]).astype(np.float64)
    return np.cos(ang).astype(np.float32), np.sin(ang).astype(np.float32)


def _rope_tables_q(pos):
    cos, sin = _rope_cos_sin(pos)
    n = cos.shape[0]
    z = lambda k: np.zeros((n, k), np.float32)
    rc = np.concatenate([np.ones((n, QK_NOPE), np.float32), cos, cos, z(HEAD_PAD - QK_HEAD)], axis=1)
    rs = np.concatenate([z(QK_NOPE), -sin, sin, z(HEAD_PAD - QK_HEAD)], axis=1)
    return rc, rs, np.concatenate([cos, cos], axis=1), np.concatenate([-sin, sin], axis=1)


def _rope_tables_k(pos):
    cos, sin = _rope_cos_sin(pos)
    return (np.ascontiguousarray(np.concatenate([cos, cos], axis=1).T),
            np.ascontiguousarray(np.concatenate([sin, -sin], axis=1).T))


def _pad_heads(wm, width):
    rows = wm.shape[0]
    return jnp.pad(wm, ((0, 0), (0, 0), (0, HEAD_PAD - width))).reshape(rows, N_HEADS * HEAD_PAD)


def _layer_weights(lw, d_model, n_pow):
    w = {}
    row = lambda z: z.reshape(1, -1).astype(F32)
    off_ckv = Q_LORA
    off_kr = off_ckv + KV_LORA
    off_u = off_kr + QK_ROPE
    off_ga = off_u + SSM_WIDTH
    off_gs = off_ga + d_model
    win = lw['w_in']
    w['w_in'] = jnp.concatenate(
        [win[:, :off_ckv], win[:, off_ckv:off_kr], win[:, off_u:off_ga], win[:, off_ga:off_gs], win[:, off_gs:],
         win[:, off_kr:off_u], jnp.zeros((d_model, LANES - QK_ROPE), win.dtype)], axis=1).astype(BF16)
    w['g_mix'] = row(lw['g_mix'])
    w['g_cq'] = row(lw['g_cq'])
    w['g_ckv'] = row(lw['g_ckv'])
    w['w_uq'] = _pad_heads(lw['w_uq'], QK_HEAD).astype(BF16)
    w['w_uk'] = _pad_heads(lw['w_uk'], QK_NOPE).astype(BF16)
    w['w_uv_t'] = _pad_heads(lw['w_uv'], V_HEAD).T.astype(BF16)
    const = lambda z, dt: jnp.asarray(np.asarray(z, np.float32), dt)
    lane = np.arange(N_HEADS * HEAD_PAD)
    w['e_kr'] = const((lane[None, :] % HEAD_PAD) == (QK_NOPE + np.arange(QK_ROPE))[:, None], BF16)
    w['g_q'] = row(jnp.pad(lw['g_q'], (0, HEAD_PAD - QK_HEAD)))
    w['g_k_nope'] = row(jnp.pad(lw['g_k'][:QK_NOPE], (0, HEAD_PAD - QK_NOPE)))
    w['g_k_rope'] = row(lw['g_k'][QK_NOPE:])
    half = QK_ROPE // 2
    wq = lw['w_uq'] * lw['g_q'][None, None, :]
    partner = jnp.concatenate([jnp.zeros_like(wq[..., :QK_NOPE]), wq[..., QK_NOPE + half:QK_HEAD],
                               wq[..., QK_NOPE:QK_NOPE + half]], axis=-1)
    w['w_uq_rot'] = _pad_heads(partner, QK_HEAD).astype(BF16)
    idx = np.arange(QK_ROPE)
    w['swap_rope'] = const(idx[:, None] == ((idx + half) % QK_ROPE)[None, :], BF16)
    blk = np.arange(2 * HEAD_PAD) // HEAD_PAD
    w['ones2'] = const(blk[:, None] == blk[None, :], BF16)
    w['v_one'] = const(((lane % HEAD_PAD) == V_HEAD).reshape(-1, 1), F32)
    w['w_uk_perm'] = lw['w_uk'].transpose(0, 2, 1).reshape(KV_LORA, QK_NOPE * N_HEADS).astype(BF16)
    w['g_k_nope_perm'] = row(jnp.repeat(lw['g_k'][:QK_NOPE], N_HEADS))
    w['g_k_rope_col'] = lw['g_k'][QK_NOPE:].reshape(QK_ROPE, 1).astype(F32)
    w['w_uv_flat'] = lw['w_uv'].reshape(KV_LORA, N_HEADS * V_HEAD).astype(BF16)
    bb_re, bb_im, pow_re, pow_im = _ssm_discretize(lw['a_re'].astype(F32), lw['a_im'].astype(F32),
                                                   lw['log_dt'].astype(F32), lw['b_re'].astype(F32),
                                                   lw['b_im'].astype(F32), n_pow)
    w['bb_re'] = _block_diag_in(bb_re)
    w['bb_im'] = _block_diag_in(bb_im)
    w['pow_re'] = pow_re
    w['pow_im'] = pow_im
    w['cc_re'] = _block_diag_out(lw['c_re'].astype(F32))
    w['cc_im'] = _block_diag_out(lw['c_im'].astype(F32))
    w['d_skip'] = row(lw['d_skip'])
    w['w_glu'] = lw['w_glu'].astype(BF16)
    w['w_oa'] = lw['w_oa'].astype(BF16)
    w['w_os'] = lw['w_os'].astype(BF16)
    w['w_out'] = lw['w_out'].astype(BF16)
    w['g_ffn'] = row(lw['g_ffn'])
    w['w_up'] = lw['w_up'].astype(BF16)
    w['conv_w'] = lw['conv_w'].astype(F32)
    w['conv_b'] = row(lw['conv_b'])
    w['w_down'] = lw['w_down'].astype(BF16)
    w['g_ple'] = row(lw['g_ple'])
    w['w_ple_gate'] = lw['w_ple_gate'].astype(BF16)
    w['w_ple_proj'] = lw['w_ple_proj'].astype(BF16)
    return w


def kernel(x_prompt, x_sample, p_prompt, p_sample, cache_ckv, cache_kr, page_table, state_ssm_re, state_ssm_im,
           state_conv, g_mix, w_in, g_cq, g_ckv, w_uq, w_uk, w_uv, g_q, g_k, a_re, a_im, log_dt, b_re, b_im,
           c_re, c_im, d_skip, w_glu, w_oa, w_os, w_out, g_ffn, w_up, conv_w, conv_b, w_down, g_ple,
           w_ple_gate, w_ple_proj):
    params = dict(g_mix=g_mix, w_in=w_in, g_cq=g_cq, g_ckv=g_ckv, w_uq=w_uq, w_uk=w_uk, w_uv=w_uv, g_q=g_q,
                  g_k=g_k, a_re=a_re, a_im=a_im, log_dt=log_dt, b_re=b_re, b_im=b_im, c_re=c_re, c_im=c_im,
                  d_skip=d_skip, w_glu=w_glu, w_oa=w_oa, w_os=w_os, w_out=w_out, g_ffn=g_ffn, w_up=w_up,
                  conv_w=conv_w, conv_b=conv_b, w_down=w_down, g_ple=g_ple, w_ple_gate=w_ple_gate,
                  w_ple_proj=w_ple_proj)
    depth = w_in.shape[0]
    bp, seq, d_model = x_prompt.shape
    bd, t_new, _ = x_sample.shape
    past_len = page_table.shape[1] * PAGE_SIZE
    d_ff = w_down.shape[1]
    n_pow = max(min(SSM_ROWS, seq) // SUBLANES, 1)

    rope_prompt = _rope_tables_q(np.arange(seq))
    rope_sample = _rope_tables_q(np.repeat(past_len + np.arange(t_new), bd))
    rope_cache = _rope_tables_k(np.arange(past_len))

    yp, ys = x_prompt, x_sample
    outs = [[] for _ in range(10)]
    for i in range(depth):
        w = _layer_weights({k: v[i] for k, v in params.items()}, d_model, n_pow)

        q, k, v, ckv, kr, u, sga, sgs = _inproj(yp, w, rope_prompt, BF16, SCALE * LOG2_E)
        att = _flash_prompt(q, k, v)
        yssm, hr, hi = _ssm_prompt(u, w)
        yp, tail = _post(yp, att, yssm, sga, sgs, p_prompt[i], w)
        outs[0].append(ckv); outs[1].append(kr); outs[4].append(hr); outs[5].append(hi)
        outs[8].append(tail[:, SUBLANES - (CONV_W - 1):, :])

        n_tok = bd * t_new
        steps = lambda z: z.transpose(1, 0, 2).reshape(1, n_tok, z.shape[-1])
        entries = lambda z: z.reshape(t_new, bd, z.shape[-1]).transpose(1, 0, 2)
        q, k, v, ckv, kr, u, sga, sgs = _inproj(steps(ys), w, rope_sample, F32, SCALE)
        ckv = entries(ckv)
        att = _sample_attention(entries(q), entries(k), ckv, cache_ckv, cache_kr, page_table, i, w, rope_cache)
        yssm, hr, hi = _ssm_sample(u.reshape(t_new, bd, -1), state_ssm_re[i], state_ssm_im[i], w)
        hist0 = state_conv[i].astype(F32).transpose(1, 0, 2).reshape((CONV_W - 1) * bd, 2 * d_ff)
        y2, tail = _post(steps(ys), steps(att).astype(BF16), yssm.reshape(1, n_tok, -1), sga, sgs,
                         steps(p_sample[i]), w, hist0=hist0, step_rows=bd)
        ys = entries(y2)
        outs[2].append(ckv); outs[3].append(entries(kr))
        outs[6].append(hr); outs[7].append(hi)
        outs[9].append(tail.reshape(CONV_W - 1, bd, 2 * d_ff).transpose(1, 0, 2))

    st = [jnp.stack(o) for o in outs]
    return (yp, ys, st[0], st[1], st[2], st[3], st[4], st[5], st[6], st[7], st[8], st[9])
```

```python
import functools
import math

import jax
import jax.numpy as jnp
import numpy as np
from jax import lax
from jax.experimental import pallas as pl
from jax.experimental.pallas import tpu as pltpu

F32 = jnp.float32
BF16 = jnp.bfloat16

N_HEADS = 8
QK_NOPE = 64
QK_ROPE = 32
QK_HEAD = QK_NOPE + QK_ROPE
V_HEAD = 64
Q_LORA = 384
KV_LORA = 256
ROPE_THETA = 10000.0
SCALE = QK_HEAD ** -0.5
LOG2_E = math.log2(math.e)
NEG_INF = -1e30
SSM_WIDTH = 512
GROUP = 16
N_GROUPS = SSM_WIDTH // GROUP
STATE = 64
N_STATE = N_GROUPS * STATE
CONV_W = 3
EPS = 1e-6
PAGE_SIZE = 128

LANES = 128
SUBLANES = 8
HEAD_PAD = LANES
NEW_PAD = 2 * SUBLANES
V_ROWS = V_HEAD + 2 * SUBLANES
VMEM_LIMIT_BYTES = 56 * 1024 * 1024

INPROJ_ROWS = 512
FLASH_T = 512
FLASH_HEADS = 4
SSM_ROWS = 256
SSM_COLS = 1024
POST_ROWS = 256
FFN_CHUNK = 256
FFN_AHEAD = 4
SAMPLE_CHUNK_PLAN = (16, 16, 16, 8, 4, 4)
SAMPLE_ENTRIES = 2
SAMPLE_AHEAD = 1
SSM_GROUP_BLOCK = 8


def _cparams(sem):
    return pltpu.CompilerParams(dimension_semantics=sem, vmem_limit_bytes=VMEM_LIMIT_BYTES)


def _const_spec(shape):
    nd = len(shape)
    return pl.BlockSpec(shape, lambda *_: (0,) * nd, pipeline_mode=pl.Buffered(1))


def _rms(x, g):
    return x * lax.rsqrt(jnp.mean(x * x, axis=-1, keepdims=True) + EPS) * g


def _bdot(a, b):
    return jnp.dot(a.astype(BF16), b, preferred_element_type=F32)


def _dot_nt(a, b):
    return lax.dot_general(a, b, (((1,), (1,)), ((), ())), preferred_element_type=F32)


def _ssm_disc_kernel(are_ref, aim_ref, ldt_ref, arer_ref, aimr_ref, ldtr_ref, bre_ref, bim_ref,
                     bbre_ref, bbim_ref, pre_ref, pim_ref, *, n_pow):
    def zoh(a_re, a_im, ldt):
        dt = jnp.exp(ldt)
        mag = jnp.exp(dt * a_re)
        ab_re = mag * jnp.cos(dt * a_im)
        ab_im = mag * jnp.sin(dt * a_im)
        return ab_re, ab_im

    a_re = arer_ref[...]
    a_im = aimr_ref[...]
    ab_re, ab_im = zoh(a_re, a_im, ldtr_ref[...])
    den = a_re * a_re + a_im * a_im
    nr = ab_re - 1.0
    f_re = (nr * a_re + ab_im * a_im) / den
    f_im = (ab_im * a_re - nr * a_im) / den
    b_re = bre_ref[...]
    b_im = bim_ref[...]
    bbre_ref[...] = f_re * b_re - f_im * b_im
    bbim_ref[...] = f_re * b_im + f_im * b_re

    p_re, p_im = zoh(are_ref[...], aim_ref[...], ldt_ref[...])
    c_re, c_im = p_re, p_im
    for j in range(n_pow):
        pre_ref[j] = c_re
        pim_ref[j] = c_im
        c_re, c_im = c_re * p_re - c_im * p_im, c_re * p_im + c_im * p_re


def _ssm_discretize(a_re, a_im, log_dt, b_re, b_im, n_pow):
    g, p = a_re.shape
    rows = g * GROUP
    rep = lambda z: jnp.repeat(z, GROUP, axis=0)
    bt = lambda z: z.transpose(0, 2, 1).reshape(rows, p)
    ldt = log_dt.reshape(g, 1)
    out_shape = (jax.ShapeDtypeStruct((rows, p), F32), jax.ShapeDtypeStruct((rows, p), F32),
                 jax.ShapeDtypeStruct((n_pow, g, p), F32), jax.ShapeDtypeStruct((n_pow, g, p), F32))
    bb_re, bb_im, pow_re, pow_im = pl.pallas_call(
        functools.partial(_ssm_disc_kernel, n_pow=n_pow),
        out_shape=out_shape, name="ssm_disc",
    )(a_re, a_im, ldt, rep(a_re), rep(a_im), rep(ldt), bt(b_re), bt(b_im))
    return bb_re, bb_im, pow_re.reshape(n_pow, g * p), pow_im.reshape(n_pow, g * p)


def _block_diag_in(bb):
    nb = N_GROUPS // SSM_GROUP_BLOCK
    z = jnp.tile(bb.reshape(nb, SSM_GROUP_BLOCK * GROUP, STATE), (1, 1, SSM_GROUP_BLOCK))
    mask = (np.arange(SSM_GROUP_BLOCK * GROUP) // GROUP)[:, None] == (np.arange(SSM_GROUP_BLOCK * STATE) // STATE)[None, :]
    return (z * mask.astype(np.float32)).astype(BF16)


def _block_diag_out(c):
    nb = N_GROUPS // SSM_GROUP_BLOCK
    z = jnp.tile(c.transpose(0, 2, 1).reshape(nb, SSM_GROUP_BLOCK * STATE, GROUP), (1, 1, SSM_GROUP_BLOCK))
    mask = (np.arange(SSM_GROUP_BLOCK * STATE) // STATE)[:, None] == (np.arange(SSM_GROUP_BLOCK * GROUP) // GROUP)[None, :]
    return (z * mask.astype(np.float32)).astype(BF16)


OFF_CQ = 0
OFF_CKV = OFF_CQ + Q_LORA
OFF_U = OFF_CKV + KV_LORA


def _head_sumsq(z, ones2_ref):
    sq = (z * z).astype(BF16)
    span = ones2_ref.shape[0]
    parts = [jnp.dot(sq[:, j:j + span], ones2_ref[...], preferred_element_type=F32)
             for j in range(0, z.shape[-1], span)]
    return jnp.concatenate(parts, axis=-1)


def _inproj_kernel(x_ref, gmix_ref, win_ref, gcq_ref, gckv_ref, wuq_ref, wuqr_ref, wuk_ref, ekr_ref, swap_ref,
                   ones2_ref, wuvt_ref, vone_ref, gq_ref, gkn_ref, gkr_ref, rc_ref, rs_ref, cc_ref, cs_ref,
                   q_ref, k_ref, v_ref, ckv_ref, kr_ref, u_ref, sga_ref, sgs_ref, *, d_model, q_scale):
    off_ga = OFF_U + SSM_WIDTH
    off_gs = off_ga + d_model
    off_kr = off_gs + d_model
    inv = 1.0 / QK_HEAD
    blocks = [slice(h * HEAD_PAD, (h + 1) * HEAD_PAD) for h in range(N_HEADS)]
    xn = _rms(x_ref[...], gmix_ref[...])
    proj = _bdot(xn, win_ref[...])
    cq = _rms(proj[:, OFF_CQ:OFF_CKV], gcq_ref[...])
    ckv = _rms(proj[:, OFF_CKV:OFF_U], gckv_ref[...])
    kr = proj[:, off_kr:off_kr + QK_ROPE]
    u_ref[...] = proj[:, OFF_U:off_ga]
    sga_ref[...] = jax.nn.sigmoid(proj[:, off_ga:off_gs]).astype(sga_ref.dtype)
    sgs_ref[...] = jax.nn.sigmoid(proj[:, off_gs:off_kr]).astype(sgs_ref.dtype)
    ckv_ref[...] = ckv
    kr_ref[...] = kr
    ckv_b = ckv.astype(BF16)
    v_ref[...] = (_dot_nt(wuvt_ref[...], ckv_b) + vone_ref[...]).astype(v_ref.dtype)

    cq_b = cq.astype(BF16)
    qf = jnp.dot(cq_b, wuq_ref[...], preferred_element_type=F32)
    qrot = jnp.dot(cq_b, wuqr_ref[...], preferred_element_type=F32)
    rn_q = lax.rsqrt(_head_sumsq(qf, ones2_ref) * inv + EPS)
    tc = rc_ref[...] * (gq_ref[...] * q_scale)
    ts = rs_ref[...] * q_scale
    for blk in blocks:
        q_ref[:, blk] = (rn_q[:, blk] * (qf[:, blk] * tc + qrot[:, blk] * ts)).astype(q_ref.dtype)

    kn = jnp.dot(ckv_b, wuk_ref[...], preferred_element_type=F32)
    krg = kr * gkr_ref[...]
    krr = krg * cc_ref[...] + _bdot(krg, swap_ref[...]) * cs_ref[...]
    placed = _bdot(krr, ekr_ref[...])
    r2 = jnp.sum(kr * kr, axis=-1, keepdims=True)
    rn_k = lax.rsqrt((_head_sumsq(kn, ones2_ref) + r2) * inv + EPS)
    for blk in blocks:
        k_ref[:, blk] = (rn_k[:, blk] * (kn[:, blk] * gkn_ref[...] + placed[:, blk])).astype(k_ref.dtype)


def _inproj(x, w, rope, act_dtype, q_scale):
    nb, rows, d = x.shape
    tr = min(INPROJ_ROWS, rows)
    assert rows % tr == 0
    hp = N_HEADS * HEAD_PAD
    row_spec = lambda n: pl.BlockSpec((None, tr, n), lambda b, t: (b, t, 0))
    tab_specs = [pl.BlockSpec((tr, z.shape[-1]), lambda b, t: (t, 0)) for z in rope]
    weights = (w['g_mix'], w['w_in'], w['g_cq'], w['g_ckv'], w['w_uq'], w['w_uq_rot'], w['w_uk'], w['e_kr'],
               w['swap_rope'], w['ones2'], w['w_uv_t'], w['v_one'], w['g_q'], w['g_k_nope'], w['g_k_rope'])
    out_shape = (
        jax.ShapeDtypeStruct((nb, rows, hp), act_dtype),
        jax.ShapeDtypeStruct((nb, rows, hp), act_dtype),
        jax.ShapeDtypeStruct((nb, hp, rows), act_dtype),
        jax.ShapeDtypeStruct((nb, rows, KV_LORA), F32),
        jax.ShapeDtypeStruct((nb, rows, QK_ROPE), F32),
        jax.ShapeDtypeStruct((nb, rows, SSM_WIDTH), F32),
        jax.ShapeDtypeStruct((nb, rows, d), BF16),
        jax.ShapeDtypeStruct((nb, rows, d), BF16),
    )
    out_specs = [row_spec(s.shape[-1]) for s in out_shape]
    out_specs[2] = pl.BlockSpec((None, hp, tr), lambda b, t: (b, 0, t))
    return pl.pallas_call(
        functools.partial(_inproj_kernel, d_model=d, q_scale=q_scale),
        grid=(nb, rows // tr),
        in_specs=[row_spec(d)] + [_const_spec(z.shape) for z in weights] + tab_specs,
        out_specs=tuple(out_specs),
        out_shape=out_shape,
        compiler_params=_cparams(("parallel", "parallel")),
        name="inproj",
    )(x, *weights, *rope)


def _flash_kernel(q_ref, k_ref, vt_ref, o_ref, s0, s1, m_sc, acc_sc, *, t, nh):
    qi = pl.program_id(2)
    heads = [slice(h * HEAD_PAD, (h + 1) * HEAD_PAD) for h in range(nh)]
    m_sc[...] = jnp.full(m_sc.shape, NEG_INF, F32)
    acc_sc[...] = jnp.zeros(acc_sc.shape, F32)

    half = nh // 2
    groups = (range(0, half), range(half, nh))

    def scores(kj, g, dst):
        start = pl.multiple_of(kj * t, t)
        for i, h in enumerate(groups[g]):
            dst[i] = _dot_nt(k_ref[pl.ds(start, t), heads[h]], q_ref[:, heads[h]])

    def fold(kj, g, src, diagonal):
        start = pl.multiple_of(kj * t, t)
        if diagonal:
            keep = lax.broadcasted_iota(jnp.int32, (t, t), 0) <= lax.broadcasted_iota(jnp.int32, (t, t), 1)
        for i, h in enumerate(groups[g]):
            st = src[i]
            if diagonal:
                st = jnp.where(keep, st, NEG_INF)
            m = m_sc[h]
            m_new = jnp.maximum(m, jnp.max(st, axis=0, keepdims=True))
            p = jnp.exp2(st - m_new)
            corr = jnp.exp2(m - m_new)
            vrows = slice(h * HEAD_PAD, h * HEAD_PAD + V_ROWS)
            acc_sc[h] = acc_sc[h] * corr + jnp.dot(vt_ref[vrows, pl.ds(start, t)], p.astype(BF16),
                                                   preferred_element_type=F32)
            m_sc[h] = m_new

    scores(0, 0, s0)

    def chunk(j, carry):
        scores(j, 1, s1)
        fold(j, 0, s0, False)
        scores(j + 1, 0, s0)
        fold(j, 1, s1, False)
        return carry

    lax.fori_loop(0, qi // 2, lambda kk, c: chunk(2 * kk + 1, chunk(2 * kk, c)), 0)

    @pl.when(qi % 2 == 1)
    def _():
        chunk(qi - 1, 0)

    scores(qi, 1, s1)
    fold(qi, 0, s0, True)
    fold(qi, 1, s1, True)

    for h0 in range(0, nh, 2):
        pair = []
        for h in (h0, h0 + 1):
            acc = acc_sc[h]
            pair.append(acc[0:V_HEAD, :] / acc[V_HEAD:V_HEAD + 1, :])
        o_ref[:, h0 * V_HEAD:(h0 + 2) * V_HEAD] = jnp.transpose(jnp.concatenate(pair, axis=0)).astype(o_ref.dtype)


def _flash_prompt(q, k, vt):
    b, s, hp = q.shape
    tq = min(FLASH_T, s)
    nh = FLASH_HEADS
    assert s % tq == 0 and N_HEADS % nh == 0 and nh % 2 == 0
    return pl.pallas_call(
        functools.partial(_flash_kernel, t=tq, nh=nh),
        grid=(b, N_HEADS // nh, s // tq),
        in_specs=[pl.BlockSpec((None, tq, nh * HEAD_PAD), lambda bi, h, qi: (bi, qi, h)),
                  pl.BlockSpec((None, s, nh * HEAD_PAD), lambda bi, h, qi: (bi, 0, h)),
                  pl.BlockSpec((None, nh * HEAD_PAD, s), lambda bi, h, qi: (bi, h, 0))],
        out_specs=pl.BlockSpec((None, tq, nh * V_HEAD), lambda bi, h, qi: (bi, qi, h)),
        out_shape=jax.ShapeDtypeStruct((b, s, N_HEADS * V_HEAD), BF16),
        scratch_shapes=[pltpu.VMEM((nh // 2, tq, tq), F32), pltpu.VMEM((nh // 2, tq, tq), F32),
                        pltpu.VMEM((nh, 1, tq), F32), pltpu.VMEM((nh, V_ROWS, tq), F32)],
        compiler_params=_cparams(("parallel", "parallel", "arbitrary")),
        name="flash_prompt",
    )(q, k, vt)


def _sample_attn_kernel(pt_ref, qn_ref, qr_ref, qrs_ref, qf_ref, knew_ref, cnew_ref, ca_ref, cb_ref,
                        wukp_ref, gkn_ref, gkr_ref, wuv_ref, hmask_ref, ckv_hbm, kr_hbm, o_ref,
                        cf, rf, cbuf, sem_c, sem_r, *, layer, n_pages, n_new, chunks, n_ent):
    step = pl.program_id(0)
    nq = n_new * N_HEADS
    first_page = [sum(chunks[:i]) for i in range(len(chunks))]

    def pages_of(ch):
        return range(first_page[ch], first_page[ch] + chunks[ch])

    def keys_of(ch):
        return slice(first_page[ch] * PAGE_SIZE, (first_page[ch] + chunks[ch]) * PAGE_SIZE)

    par = step % 2

    def page_copies(entry, side, e, j):
        pg = pt_ref[entry * n_pages + j]
        return (pltpu.make_async_copy(ckv_hbm.at[layer, pg], cf.at[side, e, j], sem_c.at[side, e]),
                pltpu.make_async_copy(kr_hbm.at[layer, pg], rf.at[side, e, j], sem_r.at[side, e]))

    def start_step(first_entry, side):
        for e in range(n_ent):
            def body(j, carry):
                for cp in page_copies(first_entry + e, side, e, j):
                    cp.start()
                return carry
            lax.fori_loop(0, n_pages, body, 0, unroll=min(8, n_pages))

    @pl.when(step == 0)
    def _():
        start_step(0, 0)

    @pl.when(step + 1 < pl.num_programs(0))
    def _():
        start_step((step + 1) * n_ent, 1 - par)

    for e in range(n_ent):
        pltpu.make_async_copy(ckv_hbm.at[layer, pl.ds(0, n_pages)], cf.at[par, e], sem_c.at[par, e]).wait()
        pltpu.make_async_copy(kr_hbm.at[layer, pl.ds(0, n_pages)], rf.at[par, e], sem_r.at[par, e]).wait()

    zrows = LANES - NEW_PAD
    tok = lax.broadcasted_iota(jnp.int32, (nq, LANES), 0) // N_HEADS
    key = lax.broadcasted_iota(jnp.int32, (nq, LANES), 1)
    qabs, qr, qrs, own = [], [], [], []
    for e in range(n_ent):
        qn = (qn_ref[e] * gkn_ref[...]).astype(BF16)
        qabs.append(_dot_nt(qn, wukp_ref[...]).astype(BF16))
        qr.append(qr_ref[e].astype(BF16))
        qrs.append(qrs_ref[e].astype(BF16))
        knew = jnp.concatenate([knew_ref[e].astype(BF16), jnp.zeros((zrows, knew_ref.shape[-1]), BF16)], axis=0)
        cnew = jnp.concatenate([cnew_ref[e].astype(BF16), jnp.zeros((zrows, KV_LORA), BF16)], axis=0)
        s_new = _dot_nt(qf_ref[e].astype(BF16), knew)
        own.append((jnp.where(key <= tok, s_new, NEG_INF), cnew))

    def front(item):
        e, ch = item
        for i in pages_of(ch):
            cbuf[e, i * PAGE_SIZE:(i + 1) * PAGE_SIZE, :] = cf[par, e, i].astype(BF16)
        r_t = jnp.concatenate([rf[par, e, i] for i in pages_of(ch)], axis=1)
        c = cbuf[e, keys_of(ch), :]
        kn = jnp.dot(c, wukp_ref[...], preferred_element_type=F32)
        sq = kn * kn
        ss = sq[:, 0:LANES]
        for j in range(1, N_HEADS * QK_NOPE // LANES):
            ss = ss + sq[:, j * LANES:(j + 1) * LANES]
        return e, ch, c, ss, r_t, _dot_nt(qabs[e], c)

    def back(args):
        e, ch, c, ss, r_t, s_nope = args
        span = keys_of(ch)
        ss_t = jnp.transpose(ss)
        ss_h = ss_t[0:N_HEADS]
        for j in range(1, LANES // N_HEADS):
            ss_h = ss_h + ss_t[j * N_HEADS:(j + 1) * N_HEADS]
        r2 = jnp.sum(r_t * r_t, axis=0, keepdims=True)
        rn_h = lax.rsqrt((ss_h + r2) * (1.0 / QK_HEAD) + EPS)
        rn = jnp.concatenate([rn_h] * n_new, axis=0)
        rg = r_t * gkr_ref[...]
        s_rope = (jnp.dot(qr[e], (rg * ca_ref[:, span]).astype(BF16), preferred_element_type=F32)
                  + jnp.dot(qrs[e], (rg * cb_ref[:, span]).astype(BF16), preferred_element_type=F32))
        return e, rn * (s_nope + s_rope), c

    def weights(s):
        m = jnp.max(s, axis=-1, keepdims=True)
        p = jnp.exp(s - m)
        return m, p.astype(BF16), jnp.sum(p, axis=-1, keepdims=True)

    items = [(e, ch) for e in range(n_ent) for ch in range(len(chunks))]
    ahead = [front(it) for it in items[:SAMPLE_AHEAD]]
    pending = [(e, weights(own[e][0]), own[e][1]) for e in range(n_ent)]
    parts = [[] for _ in range(n_ent)]

    def value_product(item):
        e, (m, p, l), vals = item
        parts[e].append((m, l, jnp.dot(p, vals, preferred_element_type=F32)))

    for i in range(len(items)):
        cur = ahead.pop(0)
        if i + SAMPLE_AHEAD < len(items):
            ahead.append(front(items[i + SAMPLE_AHEAD]))
        e, s, vals = back(cur)
        pending.append((e, weights(s), vals))
        value_product(pending.pop(0))
    while pending:
        value_product(pending.pop(0))

    for e in range(n_ent):
        m = parts[e][0][0]
        for pm, _, _ in parts[e][1:]:
            m = jnp.maximum(m, pm)
        l = jnp.zeros_like(m)
        acc = jnp.zeros((nq, KV_LORA), F32)
        for pm, pl_, pacc in parts[e]:
            wgt = jnp.exp(pm - m)
            l = l + pl_ * wgt
            acc = acc + pacc * wgt
        o_lat = acc / l
        full = _bdot(o_lat, wuv_ref[...]) * hmask_ref[...]
        o_ref[e] = jnp.sum(full.reshape(n_new, N_HEADS, N_HEADS * V_HEAD), axis=1)


def _sample_attention(q, k, ckv, cache_ckv, cache_kr, page_table, layer, w, rope_k):
    bd, t, hp = q.shape
    n_logical = page_table.shape[1]
    if sum(SAMPLE_CHUNK_PLAN) == n_logical:
        chunks = SAMPLE_CHUNK_PLAN
    else:
        size = min(SAMPLE_CHUNK_PLAN[-1], n_logical)
        assert n_logical % size == 0
        chunks = (size,) * (n_logical // size)
    assert t <= NEW_PAD
    nq = t * N_HEADS
    n_keys = n_logical * PAGE_SIZE
    head_mask = ((np.arange(nq) % N_HEADS)[:, None] == (np.arange(N_HEADS * V_HEAD) // V_HEAD)[None, :]
                 ).astype(np.float32)
    qh = q.reshape(bd, t, N_HEADS, HEAD_PAD)
    eye = np.eye(N_HEADS, dtype=np.float32)

    qn = jnp.einsum('bthd,hk->bthdk', qh[..., :QK_NOPE], eye).reshape(bd, nq, QK_NOPE * N_HEADS)
    qr = qh[..., QK_NOPE:QK_HEAD].reshape(bd, nq, QK_ROPE)
    qrs = jnp.concatenate([qr[..., QK_ROPE // 2:], qr[..., :QK_ROPE // 2]], axis=-1)
    qf = jnp.einsum('bthl,hk->bthkl', qh, eye).reshape(bd, nq, hp)
    pad_t = lambda z: jnp.pad(z, ((0, 0), (0, NEW_PAD - t), (0, 0)))
    knew = pad_t(k)
    cnew = pad_t(ckv)
    ca, cb = rope_k
    cache_kr_t = jnp.swapaxes(cache_kr, 2, 3)

    n_ent = SAMPLE_ENTRIES if bd % SAMPLE_ENTRIES == 0 else 1

    def bspec(rows, n):
        return pl.BlockSpec((n_ent, rows, n), lambda b, pt: (b, 0, 0))

    def wspec(shape):
        nd = len(shape)
        return pl.BlockSpec(shape, lambda b, pt: (0,) * nd)

    weights = (ca, cb, w['w_uk_perm'], w['g_k_nope_perm'], w['g_k_rope_col'], w['w_uv_flat'], head_mask)
    in_specs = ([bspec(nq, QK_NOPE * N_HEADS), bspec(nq, QK_ROPE), bspec(nq, QK_ROPE), bspec(nq, hp),
                 bspec(NEW_PAD, hp), bspec(NEW_PAD, KV_LORA)]
                + [wspec(z.shape) for z in weights]
                + [pl.BlockSpec(memory_space=pl.ANY), pl.BlockSpec(memory_space=pl.ANY)])
    grid_spec = pltpu.PrefetchScalarGridSpec(
        num_scalar_prefetch=1, grid=(bd // n_ent,), in_specs=in_specs,
        out_specs=pl.BlockSpec((n_ent, t, N_HEADS * V_HEAD), lambda b, pt: (b, 0, 0)),
        scratch_shapes=[pltpu.VMEM((2, n_ent, n_logical, PAGE_SIZE, KV_LORA), F32),
                        pltpu.VMEM((2, n_ent, n_logical, QK_ROPE, PAGE_SIZE), F32),
                        pltpu.VMEM((n_ent, n_keys, KV_LORA), BF16),
                        pltpu.SemaphoreType.DMA((2, n_ent)), pltpu.SemaphoreType.DMA((2, n_ent))])
    return pl.pallas_call(
        functools.partial(_sample_attn_kernel, layer=layer, n_pages=n_logical, n_new=t, chunks=chunks,
                          n_ent=n_ent),
        grid_spec=grid_spec,
        out_shape=jax.ShapeDtypeStruct((bd, t, N_HEADS * V_HEAD), F32),
        compiler_params=_cparams(("arbitrary",)),
        name="sample_attn",
    )(page_table.reshape(-1), qn, qr, qrs, qf, knew, cnew, *weights, cache_ckv, cache_kr_t)


def _ssm_in(u_b, bre_ref, bim_ref, store_re, store_im):
    nb = bre_ref.shape[0]
    kw = bre_ref.shape[1]
    nw = bre_ref.shape[2]
    for kb in range(nb):
        blk = u_b[:, kb * kw:(kb + 1) * kw]
        store_re(kb * nw, nw, jnp.dot(blk, bre_ref[kb], preferred_element_type=F32))
        store_im(kb * nw, nw, jnp.dot(blk, bim_ref[kb], preferred_element_type=F32))


def _ssm_out(h_re, h_im, cre_ref, cim_ref):
    nb = cre_ref.shape[0]
    kw = cre_ref.shape[1]
    parts = []
    for kb in range(nb):
        parts.append(jnp.dot(h_re[:, kb * kw:(kb + 1) * kw].astype(BF16), cre_ref[kb], preferred_element_type=F32)
                     - jnp.dot(h_im[:, kb * kw:(kb + 1) * kw].astype(BF16), cim_ref[kb], preferred_element_type=F32))
    return jnp.concatenate(parts, axis=-1)


def _glu(y, wglu_ref):
    g = _bdot(jax.nn.gelu(y), wglu_ref[...])
    half = g.shape[-1] // 2
    return g[:, :half] * jax.nn.sigmoid(g[:, half:])


def _ssm_prompt_kernel(u_ref, perm_ref, permt_ref, bre_ref, bim_ref, pre_ref, pim_ref, cre_ref, cim_ref,
                       dskip_ref, wglu_ref, ys_ref, hre_out, him_out,
                       hre, him, car_re, car_im, hin_re, hin_im, *, rows, sub):
    ti = pl.program_id(1)

    @pl.when(ti == 0)
    def _():
        car_re[...] = jnp.zeros(car_re.shape, F32)
        car_im[...] = jnp.zeros(car_im.shape, F32)

    u = u_ref[...]
    u_hi = u.astype(BF16)
    u_lo = (u - u_hi.astype(F32)).astype(BF16)
    perm = perm_ref[...]
    up = jnp.dot(perm, u_hi, preferred_element_type=F32) + jnp.dot(perm, u_lo, preferred_element_type=F32)

    def st_re(c0, n, val):
        hre[:, c0:c0 + n] = val

    def st_im(c0, n, val):
        him[:, c0:c0 + n] = val

    _ssm_in(up.astype(BF16), bre_ref, bim_ref, st_re, st_im)

    n_cb = N_STATE // SSM_COLS
    for cb in range(n_cb):
        cols = slice(cb * SSM_COLS, (cb + 1) * SSM_COLS)
        a_re = jnp.broadcast_to(pre_ref[0:1, cols], (SUBLANES, SSM_COLS))
        a_im = jnp.broadcast_to(pim_ref[0:1, cols], (SUBLANES, SSM_COLS))

        def local(j, st):
            s_re, s_im = st
            r0 = pl.multiple_of(j * SUBLANES, SUBLANES)
            n_re = a_re * s_re - a_im * s_im + hre[pl.ds(r0, SUBLANES), cols]
            n_im = a_re * s_im + a_im * s_re + him[pl.ds(r0, SUBLANES), cols]
            hre[pl.ds(r0, SUBLANES), cols] = n_re
            him[pl.ds(r0, SUBLANES), cols] = n_im
            return n_re, n_im

        zero = jnp.zeros((SUBLANES, SSM_COLS), F32)
        e_re, e_im = lax.fori_loop(0, sub, local, (zero, zero))

        as_re = pre_ref[sub - 1:sub, cols]
        as_im = pim_ref[sub - 1:sub, cols]
        c_re = car_re[:, cols]
        c_im = car_im[:, cols]
        for s in range(SUBLANES):
            hin_re[s:s + 1, cols] = c_re
            hin_im[s:s + 1, cols] = c_im
            c_re, c_im = (as_re * c_re - as_im * c_im + e_re[s:s + 1],
                          as_re * c_im + as_im * c_re + e_im[s:s + 1])
        car_re[:, cols] = c_re
        car_im[:, cols] = c_im
        g_re = hin_re[:, cols]
        g_im = hin_im[:, cols]

        def fix(j, d):
            d_re, d_im = d
            d_re, d_im = a_re * d_re - a_im * d_im, a_re * d_im + a_im * d_re
            r0 = pl.multiple_of(j * SUBLANES, SUBLANES)
            hre[pl.ds(r0, SUBLANES), cols] = hre[pl.ds(r0, SUBLANES), cols] + d_re
            him[pl.ds(r0, SUBLANES), cols] = him[pl.ds(r0, SUBLANES), cols] + d_im
            return d_re, d_im

        lax.fori_loop(0, sub, fix, (g_re, g_im))

    y = _ssm_out(hre[...], him[...], cre_ref, cim_ref) + dskip_ref[...] * up
    ys = _glu(y, wglu_ref).astype(BF16)
    ys_ref[...] = jnp.dot(permt_ref[...], ys, preferred_element_type=F32).astype(ys_ref.dtype)

    @pl.when(ti == pl.num_programs(1) - 1)
    def _():
        hre_out[...] = car_re[...]
        him_out[...] = car_im[...]


def _ssm_prompt(u, w):
    b, s, width = u.shape
    rows = min(SSM_ROWS, s)
    assert s % rows == 0 and rows % SUBLANES == 0
    sub = rows // SUBLANES
    r = np.arange(rows)
    t_of_r = (r % SUBLANES) * sub + r // SUBLANES
    perm = jnp.asarray((t_of_r[:, None] == np.arange(rows)[None, :]).astype(np.float32), BF16)
    pre = w['pow_re'][:sub]
    pim = w['pow_im'][:sub]
    weights = (perm, perm.T, w['bb_re'], w['bb_im'], pre, pim, w['cc_re'], w['cc_im'], w['d_skip'], w['w_glu'])
    ys, h_re, h_im = pl.pallas_call(
        functools.partial(_ssm_prompt_kernel, rows=rows, sub=sub),
        grid=(b, s // rows),
        in_specs=[pl.BlockSpec((None, rows, width), lambda bi, t: (bi, t, 0))]
                 + [_const_spec(z.shape) for z in weights],
        out_specs=(pl.BlockSpec((None, rows, width), lambda bi, t: (bi, t, 0)),
                   pl.BlockSpec((None, 1, N_STATE), lambda bi, t: (bi, 0, 0)),
                   pl.BlockSpec((None, 1, N_STATE), lambda bi, t: (bi, 0, 0))),
        out_shape=(jax.ShapeDtypeStruct((b, s, width), BF16),
                   jax.ShapeDtypeStruct((b, 1, N_STATE), F32),
                   jax.ShapeDtypeStruct((b, 1, N_STATE), F32)),
        scratch_shapes=[pltpu.VMEM((rows, N_STATE), F32), pltpu.VMEM((rows, N_STATE), F32),
                        pltpu.VMEM((1, N_STATE), F32), pltpu.VMEM((1, N_STATE), F32),
                        pltpu.VMEM((SUBLANES, N_STATE), F32), pltpu.VMEM((SUBLANES, N_STATE), F32)],
        compiler_params=_cparams(("parallel", "arbitrary")),
        name="ssm_prompt",
    )(u, *weights)
    return ys, h_re.reshape(b, N_GROUPS, STATE), h_im.reshape(b, N_GROUPS, STATE)


def _ssm_sample_kernel(u_ref, h0re_ref, h0im_ref, bre_ref, bim_ref, pre_ref, pim_ref, cre_ref, cim_ref,
                       dskip_ref, wglu_ref, ys_ref, hre_out, him_out, bu_re, bu_im, *, steps):
    a_re = pre_ref[0:1, :]
    a_im = pim_ref[0:1, :]
    h_re = h0re_ref[...]
    h_im = h0im_ref[...]

    def st_re(c0, n, val):
        bu_re[:, c0:c0 + n] = val

    def st_im(c0, n, val):
        bu_im[:, c0:c0 + n] = val

    for t in range(steps):
        u = u_ref[t]
        _ssm_in(u.astype(BF16), bre_ref, bim_ref, st_re, st_im)
        h_re, h_im = (a_re * h_re - a_im * h_im + bu_re[...], a_re * h_im + a_im * h_re + bu_im[...])
        y = _ssm_out(h_re, h_im, cre_ref, cim_ref) + dskip_ref[...] * u
        ys_ref[t] = _glu(y, wglu_ref).astype(ys_ref.dtype)
    hre_out[...] = h_re
    him_out[...] = h_im


def _ssm_sample(ut, h0_re, h0_im, w):
    t, bd, width = ut.shape
    ys, h_re, h_im = pl.pallas_call(
        functools.partial(_ssm_sample_kernel, steps=t),
        out_shape=(jax.ShapeDtypeStruct((t, bd, width), BF16),
                   jax.ShapeDtypeStruct((bd, N_STATE), F32),
                   jax.ShapeDtypeStruct((bd, N_STATE), F32)),
        scratch_shapes=[pltpu.VMEM((bd, N_STATE), F32), pltpu.VMEM((bd, N_STATE), F32)],
        compiler_params=pltpu.CompilerParams(vmem_limit_bytes=VMEM_LIMIT_BYTES),
        name="ssm_sample",
    )(ut, h0_re.reshape(bd, N_STATE).astype(F32), h0_im.reshape(bd, N_STATE).astype(F32),
      w['bb_re'], w['bb_im'], w['pow_re'][:1], w['pow_im'][:1], w['cc_re'], w['cc_im'], w['d_skip'], w['w_glu'])
    return ys, h_re.reshape(bd, N_GROUPS, STATE), h_im.reshape(bd, N_GROUPS, STATE)


def _post_kernel(*refs, rows, d_ff, row_major):
    if row_major:
        (x_ref, att_ref, ys_ref, sga_ref, sgs_ref, p_ref, woa_ref, wos_ref, wout_ref, gffn_ref, wup_ref,
         cw_ref, cbias_ref, wdown_ref, gple_ref, wpg_ref, wpp_ref, y_ref, tail_ref, hist) = refs
        hist0_ref = None
    else:
        (x_ref, att_ref, ys_ref, sga_ref, sgs_ref, p_ref, hist0_ref, woa_ref, wos_ref, wout_ref, gffn_ref, wup_ref,
         cw_ref, cbias_ref, wdown_ref, gple_ref, wpg_ref, wpp_ref, y_ref, tail_ref, hist) = refs
    ti = pl.program_id(1)
    pad = SUBLANES

    mixed = (sga_ref[...].astype(F32) * jnp.dot(att_ref[...], woa_ref[...], preferred_element_type=F32)
             + sgs_ref[...].astype(F32) * jnp.dot(ys_ref[...], wos_ref[...], preferred_element_type=F32))
    x1 = x_ref[...] + _bdot(mixed, wout_ref[...])
    xn = _rms(x1, gffn_ref[...]).astype(BF16)

    @pl.when(ti == 0)
    def _():
        if row_major:
            hist[0:pad, :] = jnp.zeros((pad, 2 * d_ff), F32)
        else:
            hist[...] = hist0_ref[...]

    def up_cols(c0):
        cols = slice(c0, c0 + FFN_CHUNK)
        up = jnp.dot(xn, wup_ref[:, cols], preferred_element_type=F32)
        if row_major:
            hist[pad:pad + rows, cols] = up
            s1 = hist[pad - 1:pad - 1 + rows, cols]
            s2 = hist[pad - 2:pad - 2 + rows, cols]
        else:
            s2 = hist[0:rows, cols]
            s1 = hist[rows:2 * rows, cols]
            hist[0:rows, cols] = s1
            hist[rows:2 * rows, cols] = up
        return cols, up, s1, s2

    def conv(args):
        cols, up, s1, s2 = args
        return cbias_ref[:, cols] + s2 * cw_ref[0:1, cols] + s1 * cw_ref[1:2, cols] + up * cw_ref[2:3, cols]

    n_chunks = d_ff // FFN_CHUNK
    ahead = [(up_cols(c * FFN_CHUNK), up_cols(d_ff + c * FFN_CHUNK)) for c in range(min(FFN_AHEAD, n_chunks))]
    f = jnp.zeros((rows, x1.shape[-1]), F32)
    for c in range(n_chunks):
        cur = ahead.pop(0)
        if c + FFN_AHEAD < n_chunks:
            ahead.append((up_cols((c + FFN_AHEAD) * FFN_CHUNK), up_cols(d_ff + (c + FFN_AHEAD) * FFN_CHUNK)))
        hdn = (jax.nn.gelu(conv(cur[0])) * conv(cur[1])).astype(BF16)
        f = f + jnp.dot(hdn, wdown_ref[c * FFN_CHUNK:(c + 1) * FFN_CHUNK, :], preferred_element_type=F32)
    x2 = x1 + f

    if row_major:
        tail = hist[rows:rows + pad, :]
        hist[0:pad, :] = tail
        tail_ref[...] = tail
    else:
        @pl.when(ti == pl.num_programs(1) - 1)
        def _():
            tail_ref[...] = hist[...]

    gate = jax.nn.sigmoid(_bdot(_rms(x2, gple_ref[...]), wpg_ref[...]))
    y_ref[...] = x2 + gate * _bdot(p_ref[...], wpp_ref[...])


def _post(x, att, ys, sga, sgs, p, w, hist0=None, step_rows=None):
    nb, total, d = x.shape
    d_ff = w['w_down'].shape[0]
    assert d_ff % FFN_CHUNK == 0
    row_major = hist0 is None
    rows = min(POST_ROWS, total) if row_major else step_rows
    assert total % rows == 0
    row_spec = lambda n: pl.BlockSpec((None, rows, n), lambda b, t: (b, t, 0))
    acts = [x, att, ys, sga, sgs, p]
    act_specs = [row_spec(z.shape[-1]) for z in acts]
    weights = (w['w_oa'], w['w_os'], w['w_out'], w['g_ffn'], w['w_up'], w['conv_w'], w['conv_b'], w['w_down'],
               w['g_ple'], w['w_ple_gate'], w['w_ple_proj'])
    if row_major:
        hist_rows = SUBLANES
        tail_spec = pl.BlockSpec((None, hist_rows, 2 * d_ff), lambda b, t: (b, 0, 0))
        tail_shape = jax.ShapeDtypeStruct((nb, hist_rows, 2 * d_ff), F32)
        scratch = pltpu.VMEM((rows + hist_rows, 2 * d_ff), F32)
    else:
        assert nb == 1
        hist_rows = 2 * rows
        acts.append(hist0)
        act_specs.append(_const_spec(hist0.shape))
        tail_spec = pl.BlockSpec((hist_rows, 2 * d_ff), lambda b, t: (0, 0), pipeline_mode=pl.Buffered(1))
        tail_shape = jax.ShapeDtypeStruct((hist_rows, 2 * d_ff), F32)
        scratch = pltpu.VMEM((hist_rows, 2 * d_ff), F32)
    return pl.pallas_call(
        functools.partial(_post_kernel, rows=rows, d_ff=d_ff, row_major=row_major),
        grid=(nb, total // rows),
        in_specs=act_specs + [_const_spec(z.shape) for z in weights],
        out_specs=(row_spec(d), tail_spec),
        out_shape=(jax.ShapeDtypeStruct((nb, total, d), F32), tail_shape),
        scratch_shapes=[scratch],
        compiler_params=_cparams(("parallel", "arbitrary")),
        name="post_prompt" if row_major else "post_sample",
    )(*acts, *weights)


def _rope_cos_sin(pos):
    inv_freq = np.power(np.float32(ROPE_THETA), -np.arange(0, QK_ROPE, 2, dtype=np.float32) / np.float32(QK_ROPE))
    ang = (np.asarray(pos, np.float32)[:, None] * inv_freq[None, :]).astype(np.float64)
    return np.cos(ang).astype(np.float32), np.sin(ang).astype(np.float32)


def _rope_tables_q(pos):
    cos, sin = _rope_cos_sin(pos)
    n = cos.shape[0]
    z = lambda k: np.zeros((n, k), np.float32)
    rc = np.concatenate([np.ones((n, QK_NOPE), np.float32), cos, cos, z(HEAD_PAD - QK_HEAD)], axis=1)
    rs = np.concatenate([z(QK_NOPE), -sin, sin, z(HEAD_PAD - QK_HEAD)], axis=1)
    return rc, rs, np.concatenate([cos, cos], axis=1), np.concatenate([-sin, sin], axis=1)


def _rope_tables_k(pos):
    cos, sin = _rope_cos_sin(pos)
    return (np.ascontiguousarray(np.concatenate([cos, cos], axis=1).T),
            np.ascontiguousarray(np.concatenate([sin, -sin], axis=1).T))


def _pad_heads(wm, width):
    rows = wm.shape[0]
    return jnp.pad(wm, ((0, 0), (0, 0), (0, HEAD_PAD - width))).reshape(rows, N_HEADS * HEAD_PAD)


def _layer_weights(lw, d_model, n_pow):
    w = {}
    row = lambda z: z.reshape(1, -1).astype(F32)
    off_ckv = Q_LORA
    off_kr = off_ckv + KV_LORA
    off_u = off_kr + QK_ROPE
    off_ga = off_u + SSM_WIDTH
    off_gs = off_ga + d_model
    win = lw['w_in']
    w['w_in'] = jnp.concatenate(
        [win[:, :off_ckv], win[:, off_ckv:off_kr], win[:, off_u:off_ga], win[:, off_ga:off_gs], win[:, off_gs:],
         win[:, off_kr:off_u], jnp.zeros((d_model, LANES - QK_ROPE), win.dtype)], axis=1).astype(BF16)
    w['g_mix'] = row(lw['g_mix'])
    w['g_cq'] = row(lw['g_cq'])
    w['g_ckv'] = row(lw['g_ckv'])
    w['w_uq'] = _pad_heads(lw['w_uq'], QK_HEAD).astype(BF16)
    w['w_uk'] = _pad_heads(lw['w_uk'], QK_NOPE).astype(BF16)
    w['w_uv_t'] = _pad_heads(lw['w_uv'], V_HEAD).T.astype(BF16)
    const = lambda z, dt: jnp.asarray(np.asarray(z, np.float32), dt)
    lane = np.arange(N_HEADS * HEAD_PAD)
    w['e_kr'] = const((lane[None, :] % HEAD_PAD) == (QK_NOPE + np.arange(QK_ROPE))[:, None], BF16)
    w['g_q'] = row(jnp.pad(lw['g_q'], (0, HEAD_PAD - QK_HEAD)))
    w['g_k_nope'] = row(jnp.pad(lw['g_k'][:QK_NOPE], (0, HEAD_PAD - QK_NOPE)))
    w['g_k_rope'] = row(lw['g_k'][QK_NOPE:])
    half = QK_ROPE // 2
    wq = lw['w_uq'] * lw['g_q'][None, None, :]
    partner = jnp.concatenate([jnp.zeros_like(wq[..., :QK_NOPE]), wq[..., QK_NOPE + half:QK_HEAD],
                               wq[..., QK_NOPE:QK_NOPE + half]], axis=-1)
    w['w_uq_rot'] = _pad_heads(partner, QK_HEAD).astype(BF16)
    idx = np.arange(QK_ROPE)
    w['swap_rope'] = const(idx[:, None] == ((idx + half) % QK_ROPE)[None, :], BF16)
    blk = np.arange(2 * HEAD_PAD) // HEAD_PAD
    w['ones2'] = const(blk[:, None] == blk[None, :], BF16)
    w['v_one'] = const(((lane % HEAD_PAD) == V_HEAD).reshape(-1, 1), F32)
    w['w_uk_perm'] = lw['w_uk'].transpose(0, 2, 1).reshape(KV_LORA, QK_NOPE * N_HEADS).astype(BF16)
    w['g_k_nope_perm'] = row(jnp.repeat(lw['g_k'][:QK_NOPE], N_HEADS))
    w['g_k_rope_col'] = lw['g_k'][QK_NOPE:].reshape(QK_ROPE, 1).astype(F32)
    w['w_uv_flat'] = lw['w_uv'].reshape(KV_LORA, N_HEADS * V_HEAD).astype(BF16)
    bb_re, bb_im, pow_re, pow_im = _ssm_discretize(lw['a_re'].astype(F32), lw['a_im'].astype(F32),
                                                   lw['log_dt'].astype(F32), lw['b_re'].astype(F32),
                                                   lw['b_im'].astype(F32), n_pow)
    w['bb_re'] = _block_diag_in(bb_re)
    w['bb_im'] = _block_diag_in(bb_im)
    w['pow_re'] = pow_re
    w['pow_im'] = pow_im
    w['cc_re'] = _block_diag_out(lw['c_re'].astype(F32))
    w['cc_im'] = _block_diag_out(lw['c_im'].astype(F32))
    w['d_skip'] = row(lw['d_skip'])
    w['w_glu'] = lw['w_glu'].astype(BF16)
    w['w_oa'] = lw['w_oa'].astype(BF16)
    w['w_os'] = lw['w_os'].astype(BF16)
    w['w_out'] = lw['w_out'].astype(BF16)
    w['g_ffn'] = row(lw['g_ffn'])
    w['w_up'] = lw['w_up'].astype(BF16)
    w['conv_w'] = lw['conv_w'].astype(F32)
    w['conv_b'] = row(lw['conv_b'])
    w['w_down'] = lw['w_down'].astype(BF16)
    w['g_ple'] = row(lw['g_ple'])
    w['w_ple_gate'] = lw['w_ple_gate'].astype(BF16)
    w['w_ple_proj'] = lw['w_ple_proj'].astype(BF16)
    return w


def kernel(x_prompt, x_sample, p_prompt, p_sample, cache_ckv, cache_kr, page_table, state_ssm_re, state_ssm_im,
           state_conv, g_mix, w_in, g_cq, g_ckv, w_uq, w_uk, w_uv, g_q, g_k, a_re, a_im, log_dt, b_re, b_im,
           c_re, c_im, d_skip, w_glu, w_oa, w_os, w_out, g_ffn, w_up, conv_w, conv_b, w_down, g_ple,
           w_ple_gate, w_ple_proj):
    params = dict(g_mix=g_mix, w_in=w_in, g_cq=g_cq, g_ckv=g_ckv, w_uq=w_uq, w_uk=w_uk, w_uv=w_uv, g_q=g_q,
                  g_k=g_k, a_re=a_re, a_im=a_im, log_dt=log_dt, b_re=b_re, b_im=b_im, c_re=c_re, c_im=c_im,
                  d_skip=d_skip, w_glu=w_glu, w_oa=w_oa, w_os=w_os, w_out=w_out, g_ffn=g_ffn, w_up=w_up,
                  conv_w=conv_w, conv_b=conv_b, w_down=w_down, g_ple=g_ple, w_ple_gate=w_ple_gate,
                  w_ple_proj=w_ple_proj)
    depth = w_in.shape[0]
    bp, seq, d_model = x_prompt.shape
    bd, t_new, _ = x_sample.shape
    past_len = page_table.shape[1] * PAGE_SIZE
    d_ff = w_down.shape[1]
    n_pow = max(min(SSM_ROWS, seq) // SUBLANES, 1)

    rope_prompt = _rope_tables_q(np.arange(seq))
    rope_sample = _rope_tables_q(np.repeat(past_len + np.arange(t_new), bd))
    rope_cache = _rope_tables_k(np.arange(past_len))

    yp, ys = x_prompt, x_sample
    outs = [[] for _ in range(10)]
    for i in range(depth):
        w = _layer_weights({k: v[i] for k, v in params.items()}, d_model, n_pow)

        q, k, v, ckv, kr, u, sga, sgs = _inproj(yp, w, rope_prompt, BF16, SCALE * LOG2_E)
        att = _flash_prompt(q, k, v)
        yssm, hr, hi = _ssm_prompt(u, w)
        yp, tail = _post(yp, att, yssm, sga, sgs, p_prompt[i], w)
        outs[0].append(ckv); outs[1].append(kr); outs[4].append(hr); outs[5].append(hi)
        outs[8].append(tail[:, SUBLANES - (CONV_W - 1):, :])

        n_tok = bd * t_new
        steps = lambda z: z.transpose(1, 0, 2).reshape(1, n_tok, z.shape[-1])
        entries = lambda z: z.reshape(t_new, bd, z.shape[-1]).transpose(1, 0, 2)
        q, k, v, ckv, kr, u, sga, sgs = _inproj(steps(ys), w, rope_sample, F32, SCALE)
        ckv = entries(ckv)
        att = _sample_attention(entries(q), entries(k), ckv, cache_ckv, cache_kr, page_table, i, w, rope_cache)
        yssm, hr, hi = _ssm_sample(u.reshape(t_new, bd, -1), state_ssm_re[i], state_ssm_im[i], w)
        hist0 = state_conv[i].astype(F32).transpose(1, 0, 2).reshape((CONV_W - 1) * bd, 2 * d_ff)
        y2, tail = _post(steps(ys), steps(att).astype(BF16), yssm.reshape(1, n_tok, -1), sga, sgs,
                         steps(p_sample[i]), w, hist0=hist0, step_rows=bd)
        ys = entries(y2)
        outs[2].append(ckv); outs[3].append(entries(kr))
        outs[6].append(hr); outs[7].append(hi)
        outs[9].append(tail.reshape(CONV_W - 1, bd, 2 * d_ff).transpose(1, 0, 2))

    st = [jnp.stack(o) for o in outs]
    return (yp, ys, st[0], st[1], st[2], st[3], st[4], st[5], st[6], st[7], st[8], st[9])
```

```python
import functools
import math

import jax
import jax.numpy as jnp
import numpy as np
from jax import lax
from jax.experimental import pallas as pl
from jax.experimental.pallas import tpu as pltpu

F32 = jnp.float32
BF16 = jnp.bfloat16

N_HEADS = 8
QK_NOPE = 64
QK_ROPE = 32
QK_HEAD = QK_NOPE + QK_ROPE
V_HEAD = 64
Q_LORA = 384
KV_LORA = 256
ROPE_THETA = 10000.0
SCALE = QK_HEAD ** -0.5
LOG2_E = math.log2(math.e)
NEG_INF = -1e30
SSM_WIDTH = 512
GROUP = 16
N_GROUPS = SSM_WIDTH // GROUP
STATE = 64
N_STATE = N_GROUPS * STATE
CONV_W = 3
EPS = 1e-6
PAGE_SIZE = 128

LANES = 128
SUBLANES = 8
HEAD_PAD = LANES
NEW_PAD = 2 * SUBLANES
V_ROWS = V_HEAD + 2 * SUBLANES
VMEM_LIMIT_BYTES = 56 * 1024 * 1024

INPROJ_ROWS = 512
FLASH_T = 512
FLASH_HEADS = 4
SSM_ROWS = 256
SSM_COLS = 1024
POST_ROWS = 256
FFN_CHUNK = 256
FFN_AHEAD = 4
SAMPLE_CHUNK_PLAN = (16, 16, 16, 8, 4, 4)
SAMPLE_ENTRIES = 2
SAMPLE_AHEAD = 1
SSM_GROUP_BLOCK = 8


def _cparams(sem):
    return pltpu.CompilerParams(dimension_semantics=sem, vmem_limit_bytes=VMEM_LIMIT_BYTES)


def _const_spec(shape):
    nd = len(shape)
    return pl.BlockSpec(shape, lambda *_: (0,) * nd, pipeline_mode=pl.Buffered(1))


def _rms(x, g):
    return x * lax.rsqrt(jnp.mean(x * x, axis=-1, keepdims=True) + EPS) * g


def _bdot(a, b):
    return jnp.dot(a.astype(BF16), b, preferred_element_type=F32)


def _dot_nt(a, b):
    return lax.dot_general(a, b, (((1,), (1,)), ((), ())), preferred_element_type=F32)


def _ssm_disc_kernel(are_ref, aim_ref, ldt_ref, arer_ref, aimr_ref, ldtr_ref, bre_ref, bim_ref,
                     bbre_ref, bbim_ref, pre_ref, pim_ref, *, n_pow):
    def zoh(a_re, a_im, ldt):
        dt = jnp.exp(ldt)
        mag = jnp.exp(dt * a_re)
        ab_re = mag * jnp.cos(dt * a_im)
        ab_im = mag * jnp.sin(dt * a_im)
        return ab_re, ab_im

    a_re = arer_ref[...]
    a_im = aimr_ref[...]
    ab_re, ab_im = zoh(a_re, a_im, ldtr_ref[...])
    den = a_re * a_re + a_im * a_im
    nr = ab_re - 1.0
    f_re = (nr * a_re + ab_im * a_im) / den
    f_im = (ab_im * a_re - nr * a_im) / den
    b_re = bre_ref[...]
    b_im = bim_ref[...]
    bbre_ref[...] = f_re * b_re - f_im * b_im
    bbim_ref[...] = f_re * b_im + f_im * b_re

    p_re, p_im = zoh(are_ref[...], aim_ref[...], ldt_ref[...])
    c_re, c_im = p_re, p_im
    for j in range(n_pow):
        pre_ref[j] = c_re
        pim_ref[j] = c_im
        c_re, c_im = c_re * p_re - c_im * p_im, c_re * p_im + c_im * p_re


def _ssm_discretize(a_re, a_im, log_dt, b_re, b_im, n_pow):
    g, p = a_re.shape
    rows = g * GROUP
    rep = lambda z: jnp.repeat(z, GROUP, axis=0)
    bt = lambda z: z.transpose(0, 2, 1).reshape(rows, p)
    ldt = log_dt.reshape(g, 1)
    out_shape = (jax.ShapeDtypeStruct((rows, p), F32), jax.ShapeDtypeStruct((rows, p), F32),
                 jax.ShapeDtypeStruct((n_pow, g, p), F32), jax.ShapeDtypeStruct((n_pow, g, p), F32))
    bb_re, bb_im, pow_re, pow_im = pl.pallas_call(
        functools.partial(_ssm_disc_kernel, n_pow=n_pow),
        out_shape=out_shape, name="ssm_disc",
    )(a_re, a_im, ldt, rep(a_re), rep(a_im), rep(ldt), bt(b_re), bt(b_im))
    return bb_re, bb_im, pow_re.reshape(n_pow, g * p), pow_im.reshape(n_pow, g * p)


def _block_diag_in(bb):
    nb = N_GROUPS // SSM_GROUP_BLOCK
    z = jnp.tile(bb.reshape(nb, SSM_GROUP_BLOCK * GROUP, STATE), (1, 1, SSM_GROUP_BLOCK))
    mask = (np.arange(SSM_GROUP_BLOCK * GROUP) // GROUP)[:, None] == (np.arange(SSM_GROUP_BLOCK * STATE) // STATE)[None, :]
    return (z * mask.astype(np.float32)).astype(BF16)


def _block_diag_out(c):
    nb = N_GROUPS // SSM_GROUP_BLOCK
    z = jnp.tile(c.transpose(0, 2, 1).reshape(nb, SSM_GROUP_BLOCK * STATE, GROUP), (1, 1, SSM_GROUP_BLOCK))
    mask = (np.arange(SSM_GROUP_BLOCK * STATE) // STATE)[:, None] == (np.arange(SSM_GROUP_BLOCK * GROUP) // GROUP)[None, :]
    return (z * mask.astype(np.float32)).astype(BF16)


OFF_CQ = 0
OFF_CKV = OFF_CQ + Q_LORA
OFF_U = OFF_CKV + KV_LORA


def _head_sumsq(z, ones2_ref):
    sq = (z * z).astype(BF16)
    span = ones2_ref.shape[0]
    parts = [jnp.dot(sq[:, j:j + span], ones2_ref[...], preferred_element_type=F32)
             for j in range(0, z.shape[-1], span)]
    return jnp.concatenate(parts, axis=-1)


def _inproj_kernel(x_ref, gmix_ref, win_ref, gcq_ref, gckv_ref, wuq_ref, wuqr_ref, wuk_ref, ekr_ref, swap_ref,
                   ones2_ref, wuvt_ref, vone_ref, gq_ref, gkn_ref, gkr_ref, rc_ref, rs_ref, cc_ref, cs_ref,
                   q_ref, k_ref, v_ref, ckv_ref, kr_ref, u_ref, sga_ref, sgs_ref, *, d_model, q_scale):
    off_ga = OFF_U + SSM_WIDTH
    off_gs = off_ga + d_model
    off_kr = off_gs + d_model
    inv = 1.0 / QK_HEAD
    blocks = [slice(h * HEAD_PAD, (h + 1) * HEAD_PAD) for h in range(N_HEADS)]
    xn = _rms(x_ref[...], gmix_ref[...])
    proj = _dot_nt(xn.astype(BF16), win_ref[...])
    cq = _rms(proj[:, OFF_CQ:OFF_CKV], gcq_ref[...])
    ckv = _rms(proj[:, OFF_CKV:OFF_U], gckv_ref[...])
    kr = proj[:, off_kr:off_kr + QK_ROPE]
    u_ref[...] = proj[:, OFF_U:off_ga]
    sga_ref[...] = jax.nn.sigmoid(proj[:, off_ga:off_gs]).astype(sga_ref.dtype)
    sgs_ref[...] = jax.nn.sigmoid(proj[:, off_gs:off_kr]).astype(sgs_ref.dtype)
    ckv_ref[...] = ckv
    kr_ref[...] = kr
    ckv_b = ckv.astype(BF16)
    v_ref[...] = (_dot_nt(wuvt_ref[...], ckv_b) + vone_ref[...]).astype(v_ref.dtype)

    cq_b = cq.astype(BF16)
    qf = jnp.dot(cq_b, wuq_ref[...], preferred_element_type=F32)
    qrot = jnp.dot(cq_b, wuqr_ref[...], preferred_element_type=F32)
    rn_q = lax.rsqrt(_head_sumsq(qf, ones2_ref) * inv + EPS)
    tc = rc_ref[...] * (gq_ref[...] * q_scale)
    ts = rs_ref[...] * q_scale
    for blk in blocks:
        q_ref[:, blk] = (rn_q[:, blk] * (qf[:, blk] * tc + qrot[:, blk] * ts)).astype(q_ref.dtype)

    kn = jnp.dot(ckv_b, wuk_ref[...], preferred_element_type=F32)
    krg = kr * gkr_ref[...]
    krr = krg * cc_ref[...] + _bdot(krg, swap_ref[...]) * cs_ref[...]
    placed = _bdot(krr, ekr_ref[...])
    r2 = jnp.sum(kr * kr, axis=-1, keepdims=True)
    rn_k = lax.rsqrt((_head_sumsq(kn, ones2_ref) + r2) * inv + EPS)
    for blk in blocks:
        k_ref[:, blk] = (rn_k[:, blk] * (kn[:, blk] * gkn_ref[...] + placed[:, blk])).astype(k_ref.dtype)


def _inproj(x, w, rope, act_dtype, q_scale):
    nb, rows, d = x.shape
    tr = min(INPROJ_ROWS, rows)
    assert rows % tr == 0
    hp = N_HEADS * HEAD_PAD
    row_spec = lambda n: pl.BlockSpec((None, tr, n), lambda b, t: (b, t, 0))
    tab_specs = [pl.BlockSpec((tr, z.shape[-1]), lambda b, t: (t, 0)) for z in rope]
    weights = (w['g_mix'], w['w_in'], w['g_cq'], w['g_ckv'], w['w_uq'], w['w_uq_rot'], w['w_uk'], w['e_kr'],
               w['swap_rope'], w['ones2'], w['w_uv_t'], w['v_one'], w['g_q'], w['g_k_nope'], w['g_k_rope'])
    out_shape = (
        jax.ShapeDtypeStruct((nb, rows, hp), act_dtype),
        jax.ShapeDtypeStruct((nb, rows, hp), act_dtype),
        jax.ShapeDtypeStruct((nb, hp, rows), act_dtype),
        jax.ShapeDtypeStruct((nb, rows, KV_LORA), F32),
        jax.ShapeDtypeStruct((nb, rows, QK_ROPE), F32),
        jax.ShapeDtypeStruct((nb, rows, SSM_WIDTH), F32),
        jax.ShapeDtypeStruct((nb, rows, d), BF16),
        jax.ShapeDtypeStruct((nb, rows, d), BF16),
    )
    out_specs = [row_spec(s.shape[-1]) for s in out_shape]
    out_specs[2] = pl.BlockSpec((None, hp, tr), lambda b, t: (b, 0, t))
    return pl.pallas_call(
        functools.partial(_inproj_kernel, d_model=d, q_scale=q_scale),
        grid=(nb, rows // tr),
        in_specs=[row_spec(d)] + [_const_spec(z.shape) for z in weights] + tab_specs,
        out_specs=tuple(out_specs),
        out_shape=out_shape,
        compiler_params=_cparams(("parallel", "parallel")),
        name="inproj",
    )(x, *weights, *rope)


def _flash_kernel(q_ref, k_ref, vt_ref, o_ref, s0, s1, m_sc, acc_sc, qt_sc, *, t, nh):
    qi = pl.program_id(2)
    heads = [slice(h * HEAD_PAD, (h + 1) * HEAD_PAD) for h in range(nh)]
    m_sc[...] = jnp.full(m_sc.shape, NEG_INF, F32)
    acc_sc[...] = jnp.zeros(acc_sc.shape, F32)

    half = nh // 2
    groups = (range(0, half), range(half, nh))

    for h in range(nh):
        qt_sc[h] = jnp.transpose(q_ref[:, heads[h]].astype(F32)).astype(BF16)

    def scores(kj, g, dst):
        start = pl.multiple_of(kj * t, t)
        for i, h in enumerate(groups[g]):
            dst[i] = jnp.dot(k_ref[pl.ds(start, t), heads[h]], qt_sc[h],
                             preferred_element_type=F32)

    def fold(kj, g, src, diagonal):
        start = pl.multiple_of(kj * t, t)
        if diagonal:
            keep = lax.broadcasted_iota(jnp.int32, (t, t), 0) <= lax.broadcasted_iota(jnp.int32, (t, t), 1)
        for i, h in enumerate(groups[g]):
            st = src[i]
            if diagonal:
                st = jnp.where(keep, st, NEG_INF)
            m = m_sc[h]
            m_new = jnp.maximum(m, jnp.max(st, axis=0, keepdims=True))
            p = jnp.exp2(st - m_new)
            corr = jnp.exp2(m - m_new)
            vrows = slice(h * HEAD_PAD, h * HEAD_PAD + V_ROWS)
            acc_sc[h] = acc_sc[h] * corr + jnp.dot(vt_ref[vrows, pl.ds(start, t)], p.astype(BF16),
                                                   preferred_element_type=F32)
            m_sc[h] = m_new

    scores(0, 0, s0)

    def chunk(j, carry):
        scores(j, 1, s1)
        fold(j, 0, s0, False)
        scores(j + 1, 0, s0)
        fold(j, 1, s1, False)
        return carry

    lax.fori_loop(0, qi // 2, lambda kk, c: chunk(2 * kk + 1, chunk(2 * kk, c)), 0)

    @pl.when(qi % 2 == 1)
    def _():
        chunk(qi - 1, 0)

    scores(qi, 1, s1)
    fold(qi, 0, s0, True)
    fold(qi, 1, s1, True)

    for h0 in range(0, nh, 2):
        pair = []
        for h in (h0, h0 + 1):
            acc = acc_sc[h]
            pair.append(acc[0:V_HEAD, :] / acc[V_HEAD:V_HEAD + 1, :])
        o_ref[:, h0 * V_HEAD:(h0 + 2) * V_HEAD] = jnp.transpose(jnp.concatenate(pair, axis=0)).astype(o_ref.dtype)


def _flash_prompt(q, k, vt):
    b, s, hp = q.shape
    tq = min(FLASH_T, s)
    nh = FLASH_HEADS
    assert s % tq == 0 and N_HEADS % nh == 0 and nh % 2 == 0
    return pl.pallas_call(
        functools.partial(_flash_kernel, t=tq, nh=nh),
        grid=(b, N_HEADS // nh, s // tq),
        in_specs=[pl.BlockSpec((None, tq, nh * HEAD_PAD), lambda bi, h, qi: (bi, qi, h)),
                  pl.BlockSpec((None, s, nh * HEAD_PAD), lambda bi, h, qi: (bi, 0, h)),
                  pl.BlockSpec((None, nh * HEAD_PAD, s), lambda bi, h, qi: (bi, h, 0))],
        out_specs=pl.BlockSpec((None, tq, nh * V_HEAD), lambda bi, h, qi: (bi, qi, h)),
        out_shape=jax.ShapeDtypeStruct((b, s, N_HEADS * V_HEAD), BF16),
        scratch_shapes=[pltpu.VMEM((nh // 2, tq, tq), F32), pltpu.VMEM((nh // 2, tq, tq), F32),
                        pltpu.VMEM((nh, 1, tq), F32), pltpu.VMEM((nh, V_ROWS, tq), F32),
                        pltpu.VMEM((nh, HEAD_PAD, tq), BF16)],
        compiler_params=_cparams(("parallel", "parallel", "arbitrary")),
        name="flash_prompt",
    )(q, k, vt)


def _sample_attn_kernel(pt_ref, qn_ref, qr_ref, qrs_ref, qf_ref, knew_ref, cnew_ref, ca_ref, cb_ref,
                        wukp_ref, gkn_ref, gkr_ref, wuv_ref, hmask_ref, ckv_hbm, kr_hbm, o_ref,
                        cf, rf, cbuf, sem_c, sem_r, *, layer, n_pages, n_new, chunks, n_ent):
    step = pl.program_id(0)
    nq = n_new * N_HEADS
    first_page = [sum(chunks[:i]) for i in range(len(chunks))]

    def pages_of(ch):
        return range(first_page[ch], first_page[ch] + chunks[ch])

    def keys_of(ch):
        return slice(first_page[ch] * PAGE_SIZE, (first_page[ch] + chunks[ch]) * PAGE_SIZE)

    par = step % 2

    def page_copies(entry, side, e, j):
        pg = pt_ref[entry * n_pages + j]
        return (pltpu.make_async_copy(ckv_hbm.at[layer, pg], cf.at[side, e, j], sem_c.at[side, e]),
                pltpu.make_async_copy(kr_hbm.at[layer, pg], rf.at[side, e, j], sem_r.at[side, e]))

    def start_step(first_entry, side):
        for e in range(n_ent):
            def body(j, carry):
                for cp in page_copies(first_entry + e, side, e, j):
                    cp.start()
                return carry
            lax.fori_loop(0, n_pages, body, 0, unroll=min(8, n_pages))

    @pl.when(step == 0)
    def _():
        start_step(0, 0)

    @pl.when(step + 1 < pl.num_programs(0))
    def _():
        start_step((step + 1) * n_ent, 1 - par)

    for e in range(n_ent):
        pltpu.make_async_copy(ckv_hbm.at[layer, pl.ds(0, n_pages)], cf.at[par, e], sem_c.at[par, e]).wait()
        pltpu.make_async_copy(kr_hbm.at[layer, pl.ds(0, n_pages)], rf.at[par, e], sem_r.at[par, e]).wait()

    zrows = LANES - NEW_PAD
    tok = lax.broadcasted_iota(jnp.int32, (nq, LANES), 0) // N_HEADS
    key = lax.broadcasted_iota(jnp.int32, (nq, LANES), 1)
    qabs, qr, qrs, own = [], [], [], []
    for e in range(n_ent):
        qn = (qn_ref[e] * gkn_ref[...]).astype(BF16)
        qabs.append(_dot_nt(qn, wukp_ref[...]).astype(BF16))
        qr.append(qr_ref[e].astype(BF16))
        qrs.append(qrs_ref[e].astype(BF16))
        knew = jnp.concatenate([knew_ref[e].astype(BF16), jnp.zeros((zrows, knew_ref.shape[-1]), BF16)], axis=0)
        cnew = jnp.concatenate([cnew_ref[e].astype(BF16), jnp.zeros((zrows, KV_LORA), BF16)], axis=0)
        s_new = _dot_nt(qf_ref[e].astype(BF16), knew)
        own.append((jnp.where(key <= tok, s_new, NEG_INF), cnew))

    def front(item):
        e, ch = item
        for i in pages_of(ch):
            cbuf[e, i * PAGE_SIZE:(i + 1) * PAGE_SIZE, :] = cf[par, e, i].astype(BF16)
        r_t = jnp.concatenate([rf[par, e, i] for i in pages_of(ch)], axis=1)
        c = cbuf[e, keys_of(ch), :]
        kn = jnp.dot(c, wukp_ref[...], preferred_element_type=F32)
        sq = kn * kn
        ss = sq[:, 0:LANES]
        for j in range(1, N_HEADS * QK_NOPE // LANES):
            ss = ss + sq[:, j * LANES:(j + 1) * LANES]
        return e, ch, c, ss, r_t, _dot_nt(qabs[e], c)

    def back(args):
        e, ch, c, ss, r_t, s_nope = args
        span = keys_of(ch)
        ss_t = jnp.transpose(ss)
        ss_h = ss_t[0:N_HEADS]
        for j in range(1, LANES // N_HEADS):
            ss_h = ss_h + ss_t[j * N_HEADS:(j + 1) * N_HEADS]
        r2 = jnp.sum(r_t * r_t, axis=0, keepdims=True)
        rn_h = lax.rsqrt((ss_h + r2) * (1.0 / QK_HEAD) + EPS)
        rn = jnp.concatenate([rn_h] * n_new, axis=0)
        rg = r_t * gkr_ref[...]
        s_rope = (jnp.dot(qr[e], (rg * ca_ref[:, span]).astype(BF16), preferred_element_type=F32)
                  + jnp.dot(qrs[e], (rg * cb_ref[:, span]).astype(BF16), preferred_element_type=F32))
        return e, rn * (s_nope + s_rope), c

    def weights(s):
        m = jnp.max(s, axis=-1, keepdims=True)
        p = jnp.exp(s - m)
        return m, p.astype(BF16), jnp.sum(p, axis=-1, keepdims=True)

    items = [(e, ch) for e in range(n_ent) for ch in range(len(chunks))]
    ahead = [front(it) for it in items[:SAMPLE_AHEAD]]
    pending = [(e, weights(own[e][0]), own[e][1]) for e in range(n_ent)]
    parts = [[] for _ in range(n_ent)]

    def value_product(item):
        e, (m, p, l), vals = item
        parts[e].append((m, l, jnp.dot(p, vals, preferred_element_type=F32)))

    for i in range(len(items)):
        cur = ahead.pop(0)
        if i + SAMPLE_AHEAD < len(items):
            ahead.append(front(items[i + SAMPLE_AHEAD]))
        e, s, vals = back(cur)
        pending.append((e, weights(s), vals))
        value_product(pending.pop(0))
    while pending:
        value_product(pending.pop(0))

    for e in range(n_ent):
        m = parts[e][0][0]
        for pm, _, _ in parts[e][1:]:
            m = jnp.maximum(m, pm)
        l = jnp.zeros_like(m)
        acc = jnp.zeros((nq, KV_LORA), F32)
        for pm, pl_, pacc in parts[e]:
            wgt = jnp.exp(pm - m)
            l = l + pl_ * wgt
            acc = acc + pacc * wgt
        o_lat = acc / l
        full = _bdot(o_lat, wuv_ref[...]) * hmask_ref[...]
        o_ref[e] = jnp.sum(full.reshape(n_new, N_HEADS, N_HEADS * V_HEAD), axis=1)


def _sample_attention(q, k, ckv, cache_ckv, cache_kr, page_table, layer, w, rope_k):
    bd, t, hp = q.shape
    n_logical = page_table.shape[1]
    if sum(SAMPLE_CHUNK_PLAN) == n_logical:
        chunks = SAMPLE_CHUNK_PLAN
    else:
        size = min(SAMPLE_CHUNK_PLAN[-1], n_logical)
        assert n_logical % size == 0
        chunks = (size,) * (n_logical // size)
    assert t <= NEW_PAD
    nq = t * N_HEADS
    n_keys = n_logical * PAGE_SIZE
    head_mask = ((np.arange(nq) % N_HEADS)[:, None] == (np.arange(N_HEADS * V_HEAD) // V_HEAD)[None, :]
                 ).astype(np.float32)
    qh = q.reshape(bd, t, N_HEADS, HEAD_PAD)
    eye = np.eye(N_HEADS, dtype=np.float32)

    qn = jnp.einsum('bthd,hk->bthdk', qh[..., :QK_NOPE], eye).reshape(bd, nq, QK_NOPE * N_HEADS)
    qr = qh[..., QK_NOPE:QK_HEAD].reshape(bd, nq, QK_ROPE)
    qrs = jnp.concatenate([qr[..., QK_ROPE // 2:], qr[..., :QK_ROPE // 2]], axis=-1)
    qf = jnp.einsum('bthl,hk->bthkl', qh, eye).reshape(bd, nq, hp)
    pad_t = lambda z: jnp.pad(z, ((0, 0), (0, NEW_PAD - t), (0, 0)))
    knew = pad_t(k)
    cnew = pad_t(ckv)
    ca, cb = rope_k
    cache_kr_t = jnp.swapaxes(cache_kr, 2, 3)

    n_ent = SAMPLE_ENTRIES if bd % SAMPLE_ENTRIES == 0 else 1

    def bspec(rows, n):
        return pl.BlockSpec((n_ent, rows, n), lambda b, pt: (b, 0, 0))

    def wspec(shape):
        nd = len(shape)
        return pl.BlockSpec(shape, lambda b, pt: (0,) * nd)

    weights = (ca, cb, w['w_uk_perm'], w['g_k_nope_perm'], w['g_k_rope_col'], w['w_uv_flat'], head_mask)
    in_specs = ([bspec(nq, QK_NOPE * N_HEADS), bspec(nq, QK_ROPE), bspec(nq, QK_ROPE), bspec(nq, hp),
                 bspec(NEW_PAD, hp), bspec(NEW_PAD, KV_LORA)]
                + [wspec(z.shape) for z in weights]
                + [pl.BlockSpec(memory_space=pl.ANY), pl.BlockSpec(memory_space=pl.ANY)])
    grid_spec = pltpu.PrefetchScalarGridSpec(
        num_scalar_prefetch=1, grid=(bd // n_ent,), in_specs=in_specs,
        out_specs=pl.BlockSpec((n_ent, t, N_HEADS * V_HEAD), lambda b, pt: (b, 0, 0)),
        scratch_shapes=[pltpu.VMEM((2, n_ent, n_logical, PAGE_SIZE, KV_LORA), F32),
                        pltpu.VMEM((2, n_ent, n_logical, QK_ROPE, PAGE_SIZE), F32),
                        pltpu.VMEM((n_ent, n_keys, KV_LORA), BF16),
                        pltpu.SemaphoreType.DMA((2, n_ent)), pltpu.SemaphoreType.DMA((2, n_ent))])
    return pl.pallas_call(
        functools.partial(_sample_attn_kernel, layer=layer, n_pages=n_logical, n_new=t, chunks=chunks,
                          n_ent=n_ent),
        grid_spec=grid_spec,
        out_shape=jax.ShapeDtypeStruct((bd, t, N_HEADS * V_HEAD), F32),
        compiler_params=_cparams(("arbitrary",)),
        name="sample_attn",
    )(page_table.reshape(-1), qn, qr, qrs, qf, knew, cnew, *weights, cache_ckv, cache_kr_t)


def _ssm_in(u_b, bre_ref, bim_ref, store_re, store_im):
    nb = bre_ref.shape[0]
    kw = bre_ref.shape[1]
    nw = bre_ref.shape[2]
    for kb in range(nb):
        blk = u_b[:, kb * kw:(kb + 1) * kw]
        store_re(kb * nw, nw, jnp.dot(blk, bre_ref[kb], preferred_element_type=F32))
        store_im(kb * nw, nw, jnp.dot(blk, bim_ref[kb], preferred_element_type=F32))


def _ssm_out(h_re, h_im, cre_ref, cim_ref):
    nb = cre_ref.shape[0]
    kw = cre_ref.shape[1]
    parts = []
    for kb in range(nb):
        parts.append(jnp.dot(h_re[:, kb * kw:(kb + 1) * kw].astype(BF16), cre_ref[kb], preferred_element_type=F32)
                     - jnp.dot(h_im[:, kb * kw:(kb + 1) * kw].astype(BF16), cim_ref[kb], preferred_element_type=F32))
    return jnp.concatenate(parts, axis=-1)


def _glu(y, wglu_ref):
    g = _bdot(jax.nn.gelu(y), wglu_ref[...])
    half = g.shape[-1] // 2
    return g[:, :half] * jax.nn.sigmoid(g[:, half:])


def _ssm_prompt_kernel(u_ref, perm_ref, permt_ref, bre_ref, bim_ref, pre_ref, pim_ref, cre_ref, cim_ref,
                       dskip_ref, wglu_ref, ys_ref, hre_out, him_out,
                       hre, him, car_re, car_im, hin_re, hin_im, *, rows, sub):
    ti = pl.program_id(1)

    @pl.when(ti == 0)
    def _():
        car_re[...] = jnp.zeros(car_re.shape, F32)
        car_im[...] = jnp.zeros(car_im.shape, F32)

    u = u_ref[...]
    u_hi = u.astype(BF16)
    u_lo = (u - u_hi.astype(F32)).astype(BF16)
    perm = perm_ref[...]
    up = jnp.dot(perm, u_hi, preferred_element_type=F32) + jnp.dot(perm, u_lo, preferred_element_type=F32)

    def st_re(c0, n, val):
        hre[:, c0:c0 + n] = val

    def st_im(c0, n, val):
        him[:, c0:c0 + n] = val

    _ssm_in(up.astype(BF16), bre_ref, bim_ref, st_re, st_im)

    n_cb = N_STATE // SSM_COLS
    for cb in range(n_cb):
        cols = slice(cb * SSM_COLS, (cb + 1) * SSM_COLS)
        a_re = jnp.broadcast_to(pre_ref[0:1, cols], (SUBLANES, SSM_COLS))
        a_im = jnp.broadcast_to(pim_ref[0:1, cols], (SUBLANES, SSM_COLS))

        def local(j, st):
            s_re, s_im = st
            r0 = pl.multiple_of(j * SUBLANES, SUBLANES)
            n_re = a_re * s_re - a_im * s_im + hre[pl.ds(r0, SUBLANES), cols]
            n_im = a_re * s_im + a_im * s_re + him[pl.ds(r0, SUBLANES), cols]
            hre[pl.ds(r0, SUBLANES), cols] = n_re
            him[pl.ds(r0, SUBLANES), cols] = n_im
            return n_re, n_im

        zero = jnp.zeros((SUBLANES, SSM_COLS), F32)
        e_re, e_im = lax.fori_loop(0, sub, local, (zero, zero))

        as_re = pre_ref[sub - 1:sub, cols]
        as_im = pim_ref[sub - 1:sub, cols]
        c_re = car_re[:, cols]
        c_im = car_im[:, cols]
        for s in range(SUBLANES):
            hin_re[s:s + 1, cols] = c_re
            hin_im[s:s + 1, cols] = c_im
            c_re, c_im = (as_re * c_re - as_im * c_im + e_re[s:s + 1],
                          as_re * c_im + as_im * c_re + e_im[s:s + 1])
        car_re[:, cols] = c_re
        car_im[:, cols] = c_im
        g_re = hin_re[:, cols]
        g_im = hin_im[:, cols]

        def fix(j, d):
            d_re, d_im = d
            d_re, d_im = a_re * d_re - a_im * d_im, a_re * d_im + a_im * d_re
            r0 = pl.multiple_of(j * SUBLANES, SUBLANES)
            hre[pl.ds(r0, SUBLANES), cols] = hre[pl.ds(r0, SUBLANES), cols] + d_re
            him[pl.ds(r0, SUBLANES), cols] = him[pl.ds(r0, SUBLANES), cols] + d_im
            return d_re, d_im

        lax.fori_loop(0, sub, fix, (g_re, g_im))

    y = _ssm_out(hre[...], him[...], cre_ref, cim_ref) + dskip_ref[...] * up
    ys = _glu(y, wglu_ref).astype(BF16)
    ys_ref[...] = jnp.dot(permt_ref[...], ys, preferred_element_type=F32).astype(ys_ref.dtype)

    @pl.when(ti == pl.num_programs(1) - 1)
    def _():
        hre_out[...] = car_re[...]
        him_out[...] = car_im[...]


def _ssm_prompt(u, w):
    b, s, width = u.shape
    rows = min(SSM_ROWS, s)
    assert s % rows == 0 and rows % SUBLANES == 0
    sub = rows // SUBLANES
    r = np.arange(rows)
    t_of_r = (r % SUBLANES) * sub + r // SUBLANES
    perm = jnp.asarray((t_of_r[:, None] == np.arange(rows)[None, :]).astype(np.float32), BF16)
    pre = w['pow_re'][:sub]
    pim = w['pow_im'][:sub]
    weights = (perm, perm.T, w['bb_re'], w['bb_im'], pre, pim, w['cc_re'], w['cc_im'], w['d_skip'], w['w_glu'])
    ys, h_re, h_im = pl.pallas_call(
        functools.partial(_ssm_prompt_kernel, rows=rows, sub=sub),
        grid=(b, s // rows),
        in_specs=[pl.BlockSpec((None, rows, width), lambda bi, t: (bi, t, 0))]
                 + [_const_spec(z.shape) for z in weights],
        out_specs=(pl.BlockSpec((None, rows, width), lambda bi, t: (bi, t, 0)),
                   pl.BlockSpec((None, 1, N_STATE), lambda bi, t: (bi, 0, 0)),
                   pl.BlockSpec((None, 1, N_STATE), lambda bi, t: (bi, 0, 0))),
        out_shape=(jax.ShapeDtypeStruct((b, s, width), BF16),
                   jax.ShapeDtypeStruct((b, 1, N_STATE), F32),
                   jax.ShapeDtypeStruct((b, 1, N_STATE), F32)),
        scratch_shapes=[pltpu.VMEM((rows, N_STATE), F32), pltpu.VMEM((rows, N_STATE), F32),
                        pltpu.VMEM((1, N_STATE), F32), pltpu.VMEM((1, N_STATE), F32),
                        pltpu.VMEM((SUBLANES, N_STATE), F32), pltpu.VMEM((SUBLANES, N_STATE), F32)],
        compiler_params=_cparams(("parallel", "arbitrary")),
        name="ssm_prompt",
    )(u, *weights)
    return ys, h_re.reshape(b, N_GROUPS, STATE), h_im.reshape(b, N_GROUPS, STATE)


def _ssm_sample_kernel(u_ref, h0re_ref, h0im_ref, bre_ref, bim_ref, pre_ref, pim_ref, cre_ref, cim_ref,
                       dskip_ref, wglu_ref, ys_ref, hre_out, him_out, bu_re, bu_im, *, steps):
    a_re = pre_ref[0:1, :]
    a_im = pim_ref[0:1, :]
    h_re = h0re_ref[...]
    h_im = h0im_ref[...]

    def st_re(c0, n, val):
        bu_re[:, c0:c0 + n] = val

    def st_im(c0, n, val):
        bu_im[:, c0:c0 + n] = val

    for t in range(steps):
        u = u_ref[t]
        _ssm_in(u.astype(BF16), bre_ref, bim_ref, st_re, st_im)
        h_re, h_im = (a_re * h_re - a_im * h_im + bu_re[...], a_re * h_im + a_im * h_re + bu_im[...])
        y = _ssm_out(h_re, h_im, cre_ref, cim_ref) + dskip_ref[...] * u
        ys_ref[t] = _glu(y, wglu_ref).astype(ys_ref.dtype)
    hre_out[...] = h_re
    him_out[...] = h_im


def _ssm_sample(ut, h0_re, h0_im, w):
    t, bd, width = ut.shape
    ys, h_re, h_im = pl.pallas_call(
        functools.partial(_ssm_sample_kernel, steps=t),
        out_shape=(jax.ShapeDtypeStruct((t, bd, width), BF16),
                   jax.ShapeDtypeStruct((bd, N_STATE), F32),
                   jax.ShapeDtypeStruct((bd, N_STATE), F32)),
        scratch_shapes=[pltpu.VMEM((bd, N_STATE), F32), pltpu.VMEM((bd, N_STATE), F32)],
        compiler_params=pltpu.CompilerParams(vmem_limit_bytes=VMEM_LIMIT_BYTES),
        name="ssm_sample",
    )(ut, h0_re.reshape(bd, N_STATE).astype(F32), h0_im.reshape(bd, N_STATE).astype(F32),
      w['bb_re'], w['bb_im'], w['pow_re'][:1], w['pow_im'][:1], w['cc_re'], w['cc_im'], w['d_skip'], w['w_glu'])
    return ys, h_re.reshape(bd, N_GROUPS, STATE), h_im.reshape(bd, N_GROUPS, STATE)


def _post_kernel(*refs, rows, d_ff, row_major):
    if row_major:
        (x_ref, att_ref, ys_ref, sga_ref, sgs_ref, p_ref, woa_ref, wos_ref, wout_ref, gffn_ref, wup_ref,
         cw_ref, cbias_ref, wdown_ref, gple_ref, wpg_ref, wpp_ref, y_ref, tail_ref, hist) = refs
        hist0_ref = None
    else:
        (x_ref, att_ref, ys_ref, sga_ref, sgs_ref, p_ref, hist0_ref, woa_ref, wos_ref, wout_ref, gffn_ref, wup_ref,
         cw_ref, cbias_ref, wdown_ref, gple_ref, wpg_ref, wpp_ref, y_ref, tail_ref, hist) = refs
    ti = pl.program_id(1)
    pad = SUBLANES

    mixed = (sga_ref[...].astype(F32) * jnp.dot(att_ref[...], woa_ref[...], preferred_element_type=F32)
             + sgs_ref[...].astype(F32) * jnp.dot(ys_ref[...], wos_ref[...], preferred_element_type=F32))
    x1 = x_ref[...] + _bdot(mixed, wout_ref[...])
    xn = _rms(x1, gffn_ref[...]).astype(BF16)

    @pl.when(ti == 0)
    def _():
        if row_major:
            hist[0:pad, :] = jnp.zeros((pad, 2 * d_ff), F32)
        else:
            hist[...] = hist0_ref[...]

    def up_cols(c0):
        cols = slice(c0, c0 + FFN_CHUNK)
        up = jnp.dot(xn, wup_ref[:, cols], preferred_element_type=F32)
        if row_major:
            hist[pad:pad + rows, cols] = up
            s1 = hist[pad - 1:pad - 1 + rows, cols]
            s2 = hist[pad - 2:pad - 2 + rows, cols]
        else:
            s2 = hist[0:rows, cols]
            s1 = hist[rows:2 * rows, cols]
            hist[0:rows, cols] = s1
            hist[rows:2 * rows, cols] = up
        return cols, up, s1, s2

    def conv(args):
        cols, up, s1, s2 = args
        return cbias_ref[:, cols] + s2 * cw_ref[0:1, cols] + s1 * cw_ref[1:2, cols] + up * cw_ref[2:3, cols]

    n_chunks = d_ff // FFN_CHUNK
    ahead = [(up_cols(c * FFN_CHUNK), up_cols(d_ff + c * FFN_CHUNK)) for c in range(min(FFN_AHEAD, n_chunks))]
    f = jnp.zeros((rows, x1.shape[-1]), F32)
    for c in range(n_chunks):
        cur = ahead.pop(0)
        if c + FFN_AHEAD < n_chunks:
            ahead.append((up_cols((c + FFN_AHEAD) * FFN_CHUNK), up_cols(d_ff + (c + FFN_AHEAD) * FFN_CHUNK)))
        hdn = (jax.nn.gelu(conv(cur[0])) * conv(cur[1])).astype(BF16)
        f = f + jnp.dot(hdn, wdown_ref[c * FFN_CHUNK:(c + 1) * FFN_CHUNK, :], preferred_element_type=F32)
    x2 = x1 + f

    if row_major:
        tail = hist[rows:rows + pad, :]
        hist[0:pad, :] = tail
        tail_ref[...] = tail
    else:
        @pl.when(ti == pl.num_programs(1) - 1)
        def _():
            tail_ref[...] = hist[...]

    gate = jax.nn.sigmoid(_bdot(_rms(x2, gple_ref[...]), wpg_ref[...]))
    y_ref[...] = x2 + gate * _bdot(p_ref[...], wpp_ref[...])


def _post(x, att, ys, sga, sgs, p, w, hist0=None, step_rows=None):
    nb, total, d = x.shape
    d_ff = w['w_down'].shape[0]
    assert d_ff % FFN_CHUNK == 0
    row_major = hist0 is None
    rows = min(POST_ROWS, total) if row_major else step_rows
    assert total % rows == 0
    row_spec = lambda n: pl.BlockSpec((None, rows, n), lambda b, t: (b, t, 0))
    acts = [x, att, ys, sga, sgs, p]
    act_specs = [row_spec(z.shape[-1]) for z in acts]
    weights = (w['w_oa'], w['w_os'], w['w_out'], w['g_ffn'], w['w_up'], w['conv_w'], w['conv_b'], w['w_down'],
               w['g_ple'], w['w_ple_gate'], w['w_ple_proj'])
    if row_major:
        hist_rows = SUBLANES
        tail_spec = pl.BlockSpec((None, hist_rows, 2 * d_ff), lambda b, t: (b, 0, 0))
        tail_shape = jax.ShapeDtypeStruct((nb, hist_rows, 2 * d_ff), F32)
        scratch = pltpu.VMEM((rows + hist_rows, 2 * d_ff), F32)
    else:
        assert nb == 1
        hist_rows = 2 * rows
        acts.append(hist0)
        act_specs.append(_const_spec(hist0.shape))
        tail_spec = pl.BlockSpec((hist_rows, 2 * d_ff), lambda b, t: (0, 0), pipeline_mode=pl.Buffered(1))
        tail_shape = jax.ShapeDtypeStruct((hist_rows, 2 * d_ff), F32)
        scratch = pltpu.VMEM((hist_rows, 2 * d_ff), F32)
    return pl.pallas_call(
        functools.partial(_post_kernel, rows=rows, d_ff=d_ff, row_major=row_major),
        grid=(nb, total // rows),
        in_specs=act_specs + [_const_spec(z.shape) for z in weights],
        out_specs=(row_spec(d), tail_spec),
        out_shape=(jax.ShapeDtypeStruct((nb, total, d), F32), tail_shape),
        scratch_shapes=[scratch],
        compiler_params=_cparams(("parallel", "arbitrary")),
        name="post_prompt" if row_major else "post_sample",
    )(*acts, *weights)


def _rope_cos_sin(pos):
    inv_freq = np.power(np.float32(ROPE_THETA), -np.arange(0, QK_ROPE, 2, dtype=np.float32) / np.float32(QK_ROPE))
    ang = (np.asarray(pos, np.float32)[:, None] * inv_freq[None, :]).astype(np.float64)
    return np.cos(ang).astype(np.float32), np.sin(ang).astype(np.float32)


def _rope_tables_q(pos):
    cos, sin = _rope_cos_sin(pos)
    n = cos.shape[0]
    z = lambda k: np.zeros((n, k), np.float32)
    rc = np.concatenate([np.ones((n, QK_NOPE), np.float32), cos, cos, z(HEAD_PAD - QK_HEAD)], axis=1)
    rs = np.concatenate([z(QK_NOPE), -sin, sin, z(HEAD_PAD - QK_HEAD)], axis=1)
    return rc, rs, np.concatenate([cos, cos], axis=1), np.concatenate([-sin, sin], axis=1)


def _rope_tables_k(pos):
    cos, sin = _rope_cos_sin(pos)
    return (np.ascontiguousarray(np.concatenate([cos, cos], axis=1).T),
            np.ascontiguousarray(np.concatenate([sin, -sin], axis=1).T))


def _pad_heads(wm, width):
    rows = wm.shape[0]
    return jnp.pad(wm, ((0, 0), (0, 0), (0, HEAD_PAD - width))).reshape(rows, N_HEADS * HEAD_PAD)


def _layer_weights(lw, d_model, n_pow):
    w = {}
    row = lambda z: z.reshape(1, -1).astype(F32)
    off_ckv = Q_LORA
    off_kr = off_ckv + KV_LORA
    off_u = off_kr + QK_ROPE
    off_ga = off_u + SSM_WIDTH
    off_gs = off_ga + d_model
    win = lw['w_in']
    wt = win.T.astype(BF16)
    w['w_in'] = jnp.concatenate(
        [wt[:off_kr], wt[off_u:], wt[off_kr:off_u], jnp.zeros((LANES - QK_ROPE, d_model), BF16)], axis=0)
    w['g_mix'] = row(lw['g_mix'])
    w['g_cq'] = row(lw['g_cq'])
    w['g_ckv'] = row(lw['g_ckv'])
    w['w_uq'] = _pad_heads(lw['w_uq'], QK_HEAD).astype(BF16)
    w['w_uk'] = _pad_heads(lw['w_uk'], QK_NOPE).astype(BF16)
    w['w_uv_t'] = _pad_heads(lw['w_uv'], V_HEAD).T.astype(BF16)
    const = lambda z, dt: jnp.asarray(np.asarray(z, np.float32), dt)
    lane = np.arange(N_HEADS * HEAD_PAD)
    w['e_kr'] = const((lane[None, :] % HEAD_PAD) == (QK_NOPE + np.arange(QK_ROPE))[:, None], BF16)
    w['g_q'] = row(jnp.pad(lw['g_q'], (0, HEAD_PAD - QK_HEAD)))
    w['g_k_nope'] = row(jnp.pad(lw['g_k'][:QK_NOPE], (0, HEAD_PAD - QK_NOPE)))
    w['g_k_rope'] = row(lw['g_k'][QK_NOPE:])
    half = QK_ROPE // 2
    wq = lw['w_uq'] * lw['g_q'][None, None, :]
    partner = jnp.concatenate([jnp.zeros_like(wq[..., :QK_NOPE]), wq[..., QK_NOPE + half:QK_HEAD],
                               wq[..., QK_NOPE:QK_NOPE + half]], axis=-1)
    w['w_uq_rot'] = _pad_heads(partner, QK_HEAD).astype(BF16)
    idx = np.arange(QK_ROPE)
    w['swap_rope'] = const(idx[:, None] == ((idx + half) % QK_ROPE)[None, :], BF16)
    blk = np.arange(2 * HEAD_PAD) // HEAD_PAD
    w['ones2'] = const(blk[:, None] == blk[None, :], BF16)
    w['v_one'] = const(((lane % HEAD_PAD) == V_HEAD).reshape(-1, 1), F32)
    w['w_uk_perm'] = lw['w_uk'].transpose(0, 2, 1).reshape(KV_LORA, QK_NOPE * N_HEADS).astype(BF16)
    w['g_k_nope_perm'] = row(jnp.repeat(lw['g_k'][:QK_NOPE], N_HEADS))
    w['g_k_rope_col'] = lw['g_k'][QK_NOPE:].reshape(QK_ROPE, 1).astype(F32)
    w['w_uv_flat'] = lw['w_uv'].reshape(KV_LORA, N_HEADS * V_HEAD).astype(BF16)
    bb_re, bb_im, pow_re, pow_im = _ssm_discretize(lw['a_re'].astype(F32), lw['a_im'].astype(F32),
                                                   lw['log_dt'].astype(F32), lw['b_re'].astype(F32),
                                                   lw['b_im'].astype(F32), n_pow)
    w['bb_re'] = _block_diag_in(bb_re)
    w['bb_im'] = _block_diag_in(bb_im)
    w['pow_re'] = pow_re
    w['pow_im'] = pow_im
    w['cc_re'] = _block_diag_out(lw['c_re'].astype(F32))
    w['cc_im'] = _block_diag_out(lw['c_im'].astype(F32))
    w['d_skip'] = row(lw['d_skip'])
    w['w_glu'] = lw['w_glu'].astype(BF16)
    w['w_oa'] = lw['w_oa'].astype(BF16)
    w['w_os'] = lw['w_os'].astype(BF16)
    w['w_out'] = lw['w_out'].astype(BF16)
    w['g_ffn'] = row(lw['g_ffn'])
    w['w_up'] = lw['w_up'].astype(BF16)
    w['conv_w'] = lw['conv_w'].astype(F32)
    w['conv_b'] = row(lw['conv_b'])
    w['w_down'] = lw['w_down'].astype(BF16)
    w['g_ple'] = row(lw['g_ple'])
    w['w_ple_gate'] = lw['w_ple_gate'].astype(BF16)
    w['w_ple_proj'] = lw['w_ple_proj'].astype(BF16)
    return w


def kernel(x_prompt, x_sample, p_prompt, p_sample, cache_ckv, cache_kr, page_table, state_ssm_re, state_ssm_im,
           state_conv, g_mix, w_in, g_cq, g_ckv, w_uq, w_uk, w_uv, g_q, g_k, a_re, a_im, log_dt, b_re, b_im,
           c_re, c_im, d_skip, w_glu, w_oa, w_os, w_out, g_ffn, w_up, conv_w, conv_b, w_down, g_ple,
           w_ple_gate, w_ple_proj):
    params = dict(g_mix=g_mix, w_in=w_in, g_cq=g_cq, g_ckv=g_ckv, w_uq=w_uq, w_uk=w_uk, w_uv=w_uv, g_q=g_q,
                  g_k=g_k, a_re=a_re, a_im=a_im, log_dt=log_dt, b_re=b_re, b_im=b_im, c_re=c_re, c_im=c_im,
                  d_skip=d_skip, w_glu=w_glu, w_oa=w_oa, w_os=w_os, w_out=w_out, g_ffn=g_ffn, w_up=w_up,
                  conv_w=conv_w, conv_b=conv_b, w_down=w_down, g_ple=g_ple, w_ple_gate=w_ple_gate,
                  w_ple_proj=w_ple_proj)
    depth = w_in.shape[0]
    bp, seq, d_model = x_prompt.shape
    bd, t_new, _ = x_sample.shape
    past_len = page_table.shape[1] * PAGE_SIZE
    d_ff = w_down.shape[1]
    n_pow = max(min(SSM_ROWS, seq) // SUBLANES, 1)

    rope_prompt = _rope_tables_q(np.arange(seq))
    rope_sample = _rope_tables_q(np.repeat(past_len + np.arange(t_new), bd))
    rope_cache = _rope_tables_k(np.arange(past_len))

    yp, ys = x_prompt, x_sample
    outs = [[] for _ in range(10)]
    for i in range(depth):
        w = _layer_weights({k: v[i] for k, v in params.items()}, d_model, n_pow)

        q, k, v, ckv, kr, u, sga, sgs = _inproj(yp, w, rope_prompt, BF16, SCALE * LOG2_E)
        att = _flash_prompt(q, k, v)
        yssm, hr, hi = _ssm_prompt(u, w)
        yp, tail = _post(yp, att, yssm, sga, sgs, p_prompt[i], w)
        outs[0].append(ckv); outs[1].append(kr); outs[4].append(hr); outs[5].append(hi)
        outs[8].append(tail[:, SUBLANES - (CONV_W - 1):, :])

        n_tok = bd * t_new
        steps = lambda z: z.transpose(1, 0, 2).reshape(1, n_tok, z.shape[-1])
        entries = lambda z: z.reshape(t_new, bd, z.shape[-1]).transpose(1, 0, 2)
        q, k, v, ckv, kr, u, sga, sgs = _inproj(steps(ys), w, rope_sample, F32, SCALE)
        ckv = entries(ckv)
        att = _sample_attention(entries(q), entries(k), ckv, cache_ckv, cache_kr, page_table, i, w, rope_cache)
        yssm, hr, hi = _ssm_sample(u.reshape(t_new, bd, -1), state_ssm_re[i], state_ssm_im[i], w)
        hist0 = state_conv[i].astype(F32).transpose(1, 0, 2).reshape((CONV_W - 1) * bd, 2 * d_ff)
        y2, tail = _post(steps(ys), steps(att).astype(BF16), yssm.reshape(1, n_tok, -1), sga, sgs,
                         steps(p_sample[i]), w, hist0=hist0, step_rows=bd)
        ys = entries(y2)
        outs[2].append(ckv); outs[3].append(entries(kr))
        outs[6].append(hr); outs[7].append(hi)
        outs[9].append(tail.reshape(CONV_W - 1, bd, 2 * d_ff).transpose(1, 0, 2))

    st = [jnp.stack(o) for o in outs]
    return (yp, ys, st[0], st[1], st[2], st[3], st[4], st[5], st[6], st[7], st[8], st[9])
```

```python
import functools
import math

import jax
import jax.numpy as jnp
import numpy as np
from jax import lax
from jax.experimental import pallas as pl
from jax.experimental.pallas import tpu as pltpu

F32 = jnp.float32
BF16 = jnp.bfloat16

N_HEADS = 8
QK_NOPE = 64
QK_ROPE = 32
QK_HEAD = QK_NOPE + QK_ROPE
V_HEAD = 64
Q_LORA = 384
KV_LORA = 256
ROPE_THETA = 10000.0
SCALE = QK_HEAD ** -0.5
LOG2_E = math.log2(math.e)
NEG_INF = -1e30
SSM_WIDTH = 512
GROUP = 16
N_GROUPS = SSM_WIDTH // GROUP
STATE = 64
N_STATE = N_GROUPS * STATE
CONV_W = 3
EPS = 1e-6
PAGE_SIZE = 128

LANES = 128
SUBLANES = 8
HEAD_PAD = LANES
NEW_PAD = 2 * SUBLANES
V_ROWS = V_HEAD + 2 * SUBLANES
VMEM_LIMIT_BYTES = 56 * 1024 * 1024

INPROJ_ROWS = 512
FLASH_T = 512
FLASH_HEADS = 4
SSM_ROWS = 512
SSM_COLS = 1024
POST_ROWS = 256
FFN_CHUNK = 256
FFN_AHEAD = 4
SAMPLE_CHUNK_PLAN = (16, 16, 16, 8, 4, 4)
SAMPLE_ENTRIES = 2
SAMPLE_AHEAD = 1
SSM_GROUP_BLOCK = 8


def _cparams(sem):
    return pltpu.CompilerParams(dimension_semantics=sem, vmem_limit_bytes=VMEM_LIMIT_BYTES)


def _const_spec(shape):
    nd = len(shape)
    return pl.BlockSpec(shape, lambda *_: (0,) * nd, pipeline_mode=pl.Buffered(1))


def _rms(x, g):
    return x * lax.rsqrt(jnp.mean(x * x, axis=-1, keepdims=True) + EPS) * g


def _bdot(a, b):
    return jnp.dot(a.astype(BF16), b, preferred_element_type=F32)


def _dot_nt(a, b):
    return lax.dot_general(a, b, (((1,), (1,)), ((), ())), preferred_element_type=F32)


def _ssm_disc_kernel(are_ref, aim_ref, ldt_ref, arer_ref, aimr_ref, ldtr_ref, bre_ref, bim_ref,
                     bbre_ref, bbim_ref, pre_ref, pim_ref, *, n_pow):
    def zoh(a_re, a_im, ldt):
        dt = jnp.exp(ldt)
        mag = jnp.exp(dt * a_re)
        ab_re = mag * jnp.cos(dt * a_im)
        ab_im = mag * jnp.sin(dt * a_im)
        return ab_re, ab_im

    a_re = arer_ref[...]
    a_im = aimr_ref[...]
    ab_re, ab_im = zoh(a_re, a_im, ldtr_ref[...])
    den = a_re * a_re + a_im * a_im
    nr = ab_re - 1.0
    f_re = (nr * a_re + ab_im * a_im) / den
    f_im = (ab_im * a_re - nr * a_im) / den
    b_re = bre_ref[...]
    b_im = bim_ref[...]
    bbre_ref[...] = f_re * b_re - f_im * b_im
    bbim_ref[...] = f_re * b_im + f_im * b_re

    p_re, p_im = zoh(are_ref[...], aim_ref[...], ldt_ref[...])
    c_re, c_im = p_re, p_im
    for j in range(n_pow):
        pre_ref[j] = c_re
        pim_ref[j] = c_im
        c_re, c_im = c_re * p_re - c_im * p_im, c_re * p_im + c_im * p_re


def _ssm_discretize(a_re, a_im, log_dt, b_re, b_im, n_pow):
    g, p = a_re.shape
    rows = g * GROUP
    rep = lambda z: jnp.repeat(z, GROUP, axis=0)
    bt = lambda z: z.transpose(0, 2, 1).reshape(rows, p)
    ldt = log_dt.reshape(g, 1)
    out_shape = (jax.ShapeDtypeStruct((rows, p), F32), jax.ShapeDtypeStruct((rows, p), F32),
                 jax.ShapeDtypeStruct((n_pow, g, p), F32), jax.ShapeDtypeStruct((n_pow, g, p), F32))
    bb_re, bb_im, pow_re, pow_im = pl.pallas_call(
        functools.partial(_ssm_disc_kernel, n_pow=n_pow),
        out_shape=out_shape, name="ssm_disc",
    )(a_re, a_im, ldt, rep(a_re), rep(a_im), rep(ldt), bt(b_re), bt(b_im))
    return bb_re, bb_im, pow_re.reshape(n_pow, g * p), pow_im.reshape(n_pow, g * p)


def _block_diag_in(bb):
    nb = N_GROUPS // SSM_GROUP_BLOCK
    z = jnp.tile(bb.reshape(nb, SSM_GROUP_BLOCK * GROUP, STATE), (1, 1, SSM_GROUP_BLOCK))
    mask = (np.arange(SSM_GROUP_BLOCK * GROUP) // GROUP)[:, None] == (np.arange(SSM_GROUP_BLOCK * STATE) // STATE)[None, :]
    return (z * mask.astype(np.float32)).astype(BF16)


def _block_diag_out(c):
    nb = N_GROUPS // SSM_GROUP_BLOCK
    z = jnp.tile(c.transpose(0, 2, 1).reshape(nb, SSM_GROUP_BLOCK * STATE, GROUP), (1, 1, SSM_GROUP_BLOCK))
    mask = (np.arange(SSM_GROUP_BLOCK * STATE) // STATE)[:, None] == (np.arange(SSM_GROUP_BLOCK * GROUP) // GROUP)[None, :]
    return (z * mask.astype(np.float32)).astype(BF16)


OFF_CQ = 0
OFF_CKV = OFF_CQ + Q_LORA
OFF_U = OFF_CKV + KV_LORA


def _head_sumsq(z, ones2_ref):
    sq = (z * z).astype(BF16)
    span = ones2_ref.shape[0]
    parts = [jnp.dot(sq[:, j:j + span], ones2_ref[...], preferred_element_type=F32)
             for j in range(0, z.shape[-1], span)]
    return jnp.concatenate(parts, axis=-1)


def _inproj_kernel(x_ref, gmix_ref, win_ref, gcq_ref, gckv_ref, wuq_ref, wuqr_ref, wuk_ref, ekr_ref, swap_ref,
                   ones2_ref, wuvt_ref, vone_ref, gq_ref, gkn_ref, gkr_ref, rc_ref, rs_ref, cc_ref, cs_ref,
                   q_ref, k_ref, v_ref, ckv_ref, kr_ref, u_ref, sga_ref, sgs_ref, *, d_model, q_scale):
    off_ga = OFF_U + SSM_WIDTH
    off_gs = off_ga + d_model
    off_kr = off_gs + d_model
    inv = 1.0 / QK_HEAD
    blocks = [slice(h * HEAD_PAD, (h + 1) * HEAD_PAD) for h in range(N_HEADS)]
    xn = _rms(x_ref[...], gmix_ref[...])
    proj = _dot_nt(xn.astype(BF16), win_ref[...])
    cq = _rms(proj[:, OFF_CQ:OFF_CKV], gcq_ref[...])
    ckv = _rms(proj[:, OFF_CKV:OFF_U], gckv_ref[...])
    kr = proj[:, off_kr:off_kr + QK_ROPE]
    u_ref[...] = proj[:, OFF_U:off_ga]
    sga_ref[...] = jax.nn.sigmoid(proj[:, off_ga:off_gs]).astype(sga_ref.dtype)
    sgs_ref[...] = jax.nn.sigmoid(proj[:, off_gs:off_kr]).astype(sgs_ref.dtype)
    ckv_ref[...] = ckv
    kr_ref[...] = kr
    ckv_b = ckv.astype(BF16)
    v_ref[...] = (_dot_nt(wuvt_ref[...], ckv_b) + vone_ref[...]).astype(v_ref.dtype)

    cq_b = cq.astype(BF16)
    qf = jnp.dot(cq_b, wuq_ref[...], preferred_element_type=F32)
    qrot = jnp.dot(cq_b, wuqr_ref[...], preferred_element_type=F32)
    rn_q = lax.rsqrt(_head_sumsq(qf, ones2_ref) * inv + EPS)
    tc = rc_ref[...] * (gq_ref[...] * q_scale)
    ts = rs_ref[...] * q_scale
    for blk in blocks:
        q_ref[:, blk] = (rn_q[:, blk] * (qf[:, blk] * tc + qrot[:, blk] * ts)).astype(q_ref.dtype)

    kn = jnp.dot(ckv_b, wuk_ref[...], preferred_element_type=F32)
    krg = kr * gkr_ref[...]
    krr = krg * cc_ref[...] + _bdot(krg, swap_ref[...]) * cs_ref[...]
    placed = _bdot(krr, ekr_ref[...])
    r2 = jnp.sum(kr * kr, axis=-1, keepdims=True)
    rn_k = lax.rsqrt((_head_sumsq(kn, ones2_ref) + r2) * inv + EPS)
    for blk in blocks:
        k_ref[:, blk] = (rn_k[:, blk] * (kn[:, blk] * gkn_ref[...] + placed[:, blk])).astype(k_ref.dtype)


def _inproj(x, w, rope, act_dtype, q_scale):
    nb, rows, d = x.shape
    tr = min(INPROJ_ROWS, rows)
    assert rows % tr == 0
    hp = N_HEADS * HEAD_PAD
    row_spec = lambda n: pl.BlockSpec((None, tr, n), lambda b, t: (b, t, 0))
    tab_specs = [pl.BlockSpec((tr, z.shape[-1]), lambda b, t: (t, 0)) for z in rope]
    weights = (w['g_mix'], w['w_in'], w['g_cq'], w['g_ckv'], w['w_uq'], w['w_uq_rot'], w['w_uk'], w['e_kr'],
               w['swap_rope'], w['ones2'], w['w_uv_t'], w['v_one'], w['g_q'], w['g_k_nope'], w['g_k_rope'])
    out_shape = (
        jax.ShapeDtypeStruct((nb, rows, hp), act_dtype),
        jax.ShapeDtypeStruct((nb, rows, hp), act_dtype),
        jax.ShapeDtypeStruct((nb, hp, rows), act_dtype),
        jax.ShapeDtypeStruct((nb, rows, KV_LORA), F32),
        jax.ShapeDtypeStruct((nb, rows, QK_ROPE), F32),
        jax.ShapeDtypeStruct((nb, rows, SSM_WIDTH), F32),
        jax.ShapeDtypeStruct((nb, rows, d), BF16),
        jax.ShapeDtypeStruct((nb, rows, d), BF16),
    )
    out_specs = [row_spec(s.shape[-1]) for s in out_shape]
    out_specs[2] = pl.BlockSpec((None, hp, tr), lambda b, t: (b, 0, t))
    return pl.pallas_call(
        functools.partial(_inproj_kernel, d_model=d, q_scale=q_scale),
        grid=(nb, rows // tr),
        in_specs=[row_spec(d)] + [_const_spec(z.shape) for z in weights] + tab_specs,
        out_specs=tuple(out_specs),
        out_shape=out_shape,
        compiler_params=_cparams(("parallel", "parallel")),
        name="inproj",
    )(x, *weights, *rope)


def _flash_kernel(q_ref, k_ref, vt_ref, o_ref, s0, s1, m_sc, acc_sc, qt_sc, *, t, nh):
    qi = pl.program_id(2)
    heads = [slice(h * HEAD_PAD, (h + 1) * HEAD_PAD) for h in range(nh)]
    m_sc[...] = jnp.full(m_sc.shape, NEG_INF, F32)
    acc_sc[...] = jnp.zeros(acc_sc.shape, F32)

    half = nh // 2
    groups = (range(0, half), range(half, nh))

    for h in range(nh):
        qt_sc[h] = jnp.transpose(q_ref[:, heads[h]].astype(F32)).astype(BF16)

    def scores(kj, g, dst):
        start = pl.multiple_of(kj * t, t)
        for i, h in enumerate(groups[g]):
            dst[i] = jnp.dot(k_ref[pl.ds(start, t), heads[h]], qt_sc[h],
                             preferred_element_type=F32)

    def fold(kj, g, src, diagonal):
        start = pl.multiple_of(kj * t, t)
        if diagonal:
            keep = lax.broadcasted_iota(jnp.int32, (t, t), 0) <= lax.broadcasted_iota(jnp.int32, (t, t), 1)
        for i, h in enumerate(groups[g]):
            st = src[i]
            if diagonal:
                st = jnp.where(keep, st, NEG_INF)
            m = m_sc[h]
            m_new = jnp.maximum(m, jnp.max(st, axis=0, keepdims=True))
            p = jnp.exp2(st - m_new)
            corr = jnp.exp2(m - m_new)
            vrows = slice(h * HEAD_PAD, h * HEAD_PAD + V_ROWS)
            acc_sc[h] = acc_sc[h] * corr + jnp.dot(vt_ref[vrows, pl.ds(start, t)], p.astype(BF16),
                                                   preferred_element_type=F32)
            m_sc[h] = m_new

    scores(0, 0, s0)

    def chunk(j, carry):
        scores(j, 1, s1)
        fold(j, 0, s0, False)
        scores(j + 1, 0, s0)
        fold(j, 1, s1, False)
        return carry

    lax.fori_loop(0, qi // 2, lambda kk, c: chunk(2 * kk + 1, chunk(2 * kk, c)), 0)

    @pl.when(qi % 2 == 1)
    def _():
        chunk(qi - 1, 0)

    scores(qi, 1, s1)
    fold(qi, 0, s0, True)
    fold(qi, 1, s1, True)

    for h0 in range(0, nh, 2):
        pair = []
        for h in (h0, h0 + 1):
            acc = acc_sc[h]
            pair.append(acc[0:V_HEAD, :] / acc[V_HEAD:V_HEAD + 1, :])
        o_ref[:, h0 * V_HEAD:(h0 + 2) * V_HEAD] = jnp.transpose(jnp.concatenate(pair, axis=0)).astype(o_ref.dtype)


def _flash_prompt(q, k, vt):
    b, s, hp = q.shape
    tq = min(FLASH_T, s)
    nh = FLASH_HEADS
    assert s % tq == 0 and N_HEADS % nh == 0 and nh % 2 == 0
    return pl.pallas_call(
        functools.partial(_flash_kernel, t=tq, nh=nh),
        grid=(b, N_HEADS // nh, s // tq),
        in_specs=[pl.BlockSpec((None, tq, nh * HEAD_PAD), lambda bi, h, qi: (bi, qi, h)),
                  pl.BlockSpec((None, s, nh * HEAD_PAD), lambda bi, h, qi: (bi, 0, h)),
                  pl.BlockSpec((None, nh * HEAD_PAD, s), lambda bi, h, qi: (bi, h, 0))],
        out_specs=pl.BlockSpec((None, tq, nh * V_HEAD), lambda bi, h, qi: (bi, qi, h)),
        out_shape=jax.ShapeDtypeStruct((b, s, N_HEADS * V_HEAD), BF16),
        scratch_shapes=[pltpu.VMEM((nh // 2, tq, tq), F32), pltpu.VMEM((nh // 2, tq, tq), F32),
                        pltpu.VMEM((nh, 1, tq), F32), pltpu.VMEM((nh, V_ROWS, tq), F32),
                        pltpu.VMEM((nh, HEAD_PAD, tq), BF16)],
        compiler_params=_cparams(("parallel", "parallel", "arbitrary")),
        name="flash_prompt",
    )(q, k, vt)


def _sample_attn_kernel(pt_ref, qn_ref, qr_ref, qrs_ref, qf_ref, knew_ref, cnew_ref, ca_ref, cb_ref,
                        wukp_ref, gkn_ref, gkr_ref, wuv_ref, hmask_ref, ckv_hbm, kr_hbm, o_ref,
                        cf, rf, cbuf, sem_c, sem_r, *, layer, n_pages, n_new, chunks, n_ent):
    step = pl.program_id(0)
    nq = n_new * N_HEADS
    first_page = [sum(chunks[:i]) for i in range(len(chunks))]

    def pages_of(ch):
        return range(first_page[ch], first_page[ch] + chunks[ch])

    def keys_of(ch):
        return slice(first_page[ch] * PAGE_SIZE, (first_page[ch] + chunks[ch]) * PAGE_SIZE)

    par = step % 2

    def page_copies(entry, side, e, j):
        pg = pt_ref[entry * n_pages + j]
        return (pltpu.make_async_copy(ckv_hbm.at[layer, pg], cf.at[side, e, j], sem_c.at[side, e]),
                pltpu.make_async_copy(kr_hbm.at[layer, pg], rf.at[side, e, j], sem_r.at[side, e]))

    def start_step(first_entry, side):
        for e in range(n_ent):
            def body(j, carry):
                for cp in page_copies(first_entry + e, side, e, j):
                    cp.start()
                return carry
            lax.fori_loop(0, n_pages, body, 0, unroll=min(8, n_pages))

    @pl.when(step == 0)
    def _():
        start_step(0, 0)

    @pl.when(step + 1 < pl.num_programs(0))
    def _():
        start_step((step + 1) * n_ent, 1 - par)

    for e in range(n_ent):
        pltpu.make_async_copy(ckv_hbm.at[layer, pl.ds(0, n_pages)], cf.at[par, e], sem_c.at[par, e]).wait()
        pltpu.make_async_copy(kr_hbm.at[layer, pl.ds(0, n_pages)], rf.at[par, e], sem_r.at[par, e]).wait()

    zrows = LANES - NEW_PAD
    tok = lax.broadcasted_iota(jnp.int32, (nq, LANES), 0) // N_HEADS
    key = lax.broadcasted_iota(jnp.int32, (nq, LANES), 1)
    qabs, qr, qrs, own = [], [], [], []
    for e in range(n_ent):
        qn = (qn_ref[e] * gkn_ref[...]).astype(BF16)
        qabs.append(_dot_nt(qn, wukp_ref[...]).astype(BF16))
        qr.append(qr_ref[e].astype(BF16))
        qrs.append(qrs_ref[e].astype(BF16))
        knew = jnp.concatenate([knew_ref[e].astype(BF16), jnp.zeros((zrows, knew_ref.shape[-1]), BF16)], axis=0)
        cnew = jnp.concatenate([cnew_ref[e].astype(BF16), jnp.zeros((zrows, KV_LORA), BF16)], axis=0)
        s_new = _dot_nt(qf_ref[e].astype(BF16), knew)
        own.append((jnp.where(key <= tok, s_new, NEG_INF), cnew))

    def front(item):
        e, ch = item
        for i in pages_of(ch):
            cbuf[e, i * PAGE_SIZE:(i + 1) * PAGE_SIZE, :] = cf[par, e, i].astype(BF16)
        r_t = jnp.concatenate([rf[par, e, i] for i in pages_of(ch)], axis=1)
        c = cbuf[e, keys_of(ch), :]
        kn = jnp.dot(c, wukp_ref[...], preferred_element_type=F32)
        sq = kn * kn
        ss = sq[:, 0:LANES]
        for j in range(1, N_HEADS * QK_NOPE // LANES):
            ss = ss + sq[:, j * LANES:(j + 1) * LANES]
        return e, ch, c, ss, r_t, _dot_nt(qabs[e], c)

    def back(args):
        e, ch, c, ss, r_t, s_nope = args
        span = keys_of(ch)
        ss_t = jnp.transpose(ss)
        ss_h = ss_t[0:N_HEADS]
        for j in range(1, LANES // N_HEADS):
            ss_h = ss_h + ss_t[j * N_HEADS:(j + 1) * N_HEADS]
        r2 = jnp.sum(r_t * r_t, axis=0, keepdims=True)
        rn_h = lax.rsqrt((ss_h + r2) * (1.0 / QK_HEAD) + EPS)
        rn = jnp.concatenate([rn_h] * n_new, axis=0)
        rg = r_t * gkr_ref[...]
        s_rope = (jnp.dot(qr[e], (rg * ca_ref[:, span]).astype(BF16), preferred_element_type=F32)
                  + jnp.dot(qrs[e], (rg * cb_ref[:, span]).astype(BF16), preferred_element_type=F32))
        return e, rn * (s_nope + s_rope), c

    def weights(s):
        m = jnp.max(s, axis=-1, keepdims=True)
        p = jnp.exp(s - m)
        return m, p.astype(BF16), jnp.sum(p, axis=-1, keepdims=True)

    items = [(e, ch) for e in range(n_ent) for ch in range(len(chunks))]
    ahead = [front(it) for it in items[:SAMPLE_AHEAD]]
    pending = [(e, weights(own[e][0]), own[e][1]) for e in range(n_ent)]
    parts = [[] for _ in range(n_ent)]

    def value_product(item):
        e, (m, p, l), vals = item
        parts[e].append((m, l, jnp.dot(p, vals, preferred_element_type=F32)))

    for i in range(len(items)):
        cur = ahead.pop(0)
        if i + SAMPLE_AHEAD < len(items):
            ahead.append(front(items[i + SAMPLE_AHEAD]))
        e, s, vals = back(cur)
        pending.append((e, weights(s), vals))
        value_product(pending.pop(0))
    while pending:
        value_product(pending.pop(0))

    for e in range(n_ent):
        m = parts[e][0][0]
        for pm, _, _ in parts[e][1:]:
            m = jnp.maximum(m, pm)
        l = jnp.zeros_like(m)
        acc = jnp.zeros((nq, KV_LORA), F32)
        for pm, pl_, pacc in parts[e]:
            wgt = jnp.exp(pm - m)
            l = l + pl_ * wgt
            acc = acc + pacc * wgt
        o_lat = acc / l
        full = _bdot(o_lat, wuv_ref[...]) * hmask_ref[...]
        o_ref[e] = jnp.sum(full.reshape(n_new, N_HEADS, N_HEADS * V_HEAD), axis=1)


def _sample_attention(q, k, ckv, cache_ckv, cache_kr, page_table, layer, w, rope_k):
    bd, t, hp = q.shape
    n_logical = page_table.shape[1]
    if sum(SAMPLE_CHUNK_PLAN) == n_logical:
        chunks = SAMPLE_CHUNK_PLAN
    else:
        size = min(SAMPLE_CHUNK_PLAN[-1], n_logical)
        assert n_logical % size == 0
        chunks = (size,) * (n_logical // size)
    assert t <= NEW_PAD
    nq = t * N_HEADS
    n_keys = n_logical * PAGE_SIZE
    head_mask = ((np.arange(nq) % N_HEADS)[:, None] == (np.arange(N_HEADS * V_HEAD) // V_HEAD)[None, :]
                 ).astype(np.float32)
    qh = q.reshape(bd, t, N_HEADS, HEAD_PAD)
    eye = np.eye(N_HEADS, dtype=np.float32)

    qn = jnp.einsum('bthd,hk->bthdk', qh[..., :QK_NOPE], eye).reshape(bd, nq, QK_NOPE * N_HEADS)
    qr = qh[..., QK_NOPE:QK_HEAD].reshape(bd, nq, QK_ROPE)
    qrs = jnp.concatenate([qr[..., QK_ROPE // 2:], qr[..., :QK_ROPE // 2]], axis=-1)
    qf = jnp.einsum('bthl,hk->bthkl', qh, eye).reshape(bd, nq, hp)
    pad_t = lambda z: jnp.pad(z, ((0, 0), (0, NEW_PAD - t), (0, 0)))
    knew = pad_t(k)
    cnew = pad_t(ckv)
    ca, cb = rope_k
    cache_kr_t = jnp.swapaxes(cache_kr, 2, 3)

    n_ent = SAMPLE_ENTRIES if bd % SAMPLE_ENTRIES == 0 else 1

    def bspec(rows, n):
        return pl.BlockSpec((n_ent, rows, n), lambda b, pt: (b, 0, 0))

    def wspec(shape):
        nd = len(shape)
        return pl.BlockSpec(shape, lambda b, pt: (0,) * nd)

    weights = (ca, cb, w['w_uk_perm'], w['g_k_nope_perm'], w['g_k_rope_col'], w['w_uv_flat'], head_mask)
    in_specs = ([bspec(nq, QK_NOPE * N_HEADS), bspec(nq, QK_ROPE), bspec(nq, QK_ROPE), bspec(nq, hp),
                 bspec(NEW_PAD, hp), bspec(NEW_PAD, KV_LORA)]
                + [wspec(z.shape) for z in weights]
                + [pl.BlockSpec(memory_space=pl.ANY), pl.BlockSpec(memory_space=pl.ANY)])
    grid_spec = pltpu.PrefetchScalarGridSpec(
        num_scalar_prefetch=1, grid=(bd // n_ent,), in_specs=in_specs,
        out_specs=pl.BlockSpec((n_ent, t, N_HEADS * V_HEAD), lambda b, pt: (b, 0, 0)),
        scratch_shapes=[pltpu.VMEM((2, n_ent, n_logical, PAGE_SIZE, KV_LORA), F32),
                        pltpu.VMEM((2, n_ent, n_logical, QK_ROPE, PAGE_SIZE), F32),
                        pltpu.VMEM((n_ent, n_keys, KV_LORA), BF16),
                        pltpu.SemaphoreType.DMA((2, n_ent)), pltpu.SemaphoreType.DMA((2, n_ent))])
    return pl.pallas_call(
        functools.partial(_sample_attn_kernel, layer=layer, n_pages=n_logical, n_new=t, chunks=chunks,
                          n_ent=n_ent),
        grid_spec=grid_spec,
        out_shape=jax.ShapeDtypeStruct((bd, t, N_HEADS * V_HEAD), F32),
        compiler_params=_cparams(("arbitrary",)),
        name="sample_attn",
    )(page_table.reshape(-1), qn, qr, qrs, qf, knew, cnew, *weights, cache_ckv, cache_kr_t)


def _ssm_in(u_b, bre_ref, bim_ref, store_re, store_im):
    nb = bre_ref.shape[0]
    kw = bre_ref.shape[1]
    nw = bre_ref.shape[2]
    for kb in range(nb):
        blk = u_b[:, kb * kw:(kb + 1) * kw]
        store_re(kb * nw, nw, jnp.dot(blk, bre_ref[kb], preferred_element_type=F32))
        store_im(kb * nw, nw, jnp.dot(blk, bim_ref[kb], preferred_element_type=F32))


def _ssm_out(h_re, h_im, cre_ref, cim_ref):
    nb = cre_ref.shape[0]
    kw = cre_ref.shape[1]
    parts = []
    for kb in range(nb):
        parts.append(jnp.dot(h_re[:, kb * kw:(kb + 1) * kw].astype(BF16), cre_ref[kb], preferred_element_type=F32)
                     - jnp.dot(h_im[:, kb * kw:(kb + 1) * kw].astype(BF16), cim_ref[kb], preferred_element_type=F32))
    return jnp.concatenate(parts, axis=-1)


def _glu(y, wglu_ref):
    g = _bdot(jax.nn.gelu(y), wglu_ref[...])
    half = g.shape[-1] // 2
    return g[:, :half] * jax.nn.sigmoid(g[:, half:])


def _ssm_prompt_kernel(u_ref, perm_ref, permt_ref, bre_ref, bim_ref, pre_ref, pim_ref, cre_ref, cim_ref,
                       dskip_ref, wglu_ref, ys_ref, hre_out, him_out,
                       hre, him, car_re, car_im, hin_re, hin_im, *, rows, sub):
    ti = pl.program_id(1)

    @pl.when(ti == 0)
    def _():
        car_re[...] = jnp.zeros(car_re.shape, F32)
        car_im[...] = jnp.zeros(car_im.shape, F32)

    u = u_ref[...]
    u_hi = u.astype(BF16)
    u_lo = (u - u_hi.astype(F32)).astype(BF16)
    perm = perm_ref[...]
    up = jnp.dot(perm, u_hi, preferred_element_type=F32) + jnp.dot(perm, u_lo, preferred_element_type=F32)

    def st_re(c0, n, val):
        hre[:, c0:c0 + n] = val

    def st_im(c0, n, val):
        him[:, c0:c0 + n] = val

    _ssm_in(up.astype(BF16), bre_ref, bim_ref, st_re, st_im)

    n_cb = N_STATE // SSM_COLS
    for cb in range(n_cb):
        cols = slice(cb * SSM_COLS, (cb + 1) * SSM_COLS)
        a_re = jnp.broadcast_to(pre_ref[0:1, cols], (SUBLANES, SSM_COLS))
        a_im = jnp.broadcast_to(pim_ref[0:1, cols], (SUBLANES, SSM_COLS))

        def local(j, st):
            s_re, s_im = st
            r0 = pl.multiple_of(j * SUBLANES, SUBLANES)
            n_re = a_re * s_re - a_im * s_im + hre[pl.ds(r0, SUBLANES), cols]
            n_im = a_re * s_im + a_im * s_re + him[pl.ds(r0, SUBLANES), cols]
            hre[pl.ds(r0, SUBLANES), cols] = n_re
            him[pl.ds(r0, SUBLANES), cols] = n_im
            return n_re, n_im

        zero = jnp.zeros((SUBLANES, SSM_COLS), F32)
        e_re, e_im = lax.fori_loop(0, sub, local, (zero, zero))

        as_re = pre_ref[sub - 1:sub, cols]
        as_im = pim_ref[sub - 1:sub, cols]
        c_re = car_re[:, cols]
        c_im = car_im[:, cols]
        for s in range(SUBLANES):
            hin_re[s:s + 1, cols] = c_re
            hin_im[s:s + 1, cols] = c_im
            c_re, c_im = (as_re * c_re - as_im * c_im + e_re[s:s + 1],
                          as_re * c_im + as_im * c_re + e_im[s:s + 1])
        car_re[:, cols] = c_re
        car_im[:, cols] = c_im
        g_re = hin_re[:, cols]
        g_im = hin_im[:, cols]

        def fix(j, d):
            d_re, d_im = d
            d_re, d_im = a_re * d_re - a_im * d_im, a_re * d_im + a_im * d_re
            r0 = pl.multiple_of(j * SUBLANES, SUBLANES)
            hre[pl.ds(r0, SUBLANES), cols] = hre[pl.ds(r0, SUBLANES), cols] + d_re
            him[pl.ds(r0, SUBLANES), cols] = him[pl.ds(r0, SUBLANES), cols] + d_im
            return d_re, d_im

        lax.fori_loop(0, sub, fix, (g_re, g_im))

    y = _ssm_out(hre[...], him[...], cre_ref, cim_ref) + dskip_ref[...] * up
    ys = _glu(y, wglu_ref).astype(BF16)
    ys_ref[...] = jnp.dot(permt_ref[...], ys, preferred_element_type=F32).astype(ys_ref.dtype)

    @pl.when(ti == pl.num_programs(1) - 1)
    def _():
        hre_out[...] = car_re[...]
        him_out[...] = car_im[...]


def _ssm_prompt(u, w):
    b, s, width = u.shape
    rows = min(SSM_ROWS, s)
    assert s % rows == 0 and rows % SUBLANES == 0
    sub = rows // SUBLANES
    r = np.arange(rows)
    t_of_r = (r % SUBLANES) * sub + r // SUBLANES
    perm = jnp.asarray((t_of_r[:, None] == np.arange(rows)[None, :]).astype(np.float32), BF16)
    pre = w['pow_re'][:sub]
    pim = w['pow_im'][:sub]
    weights = (perm, perm.T, w['bb_re'], w['bb_im'], pre, pim, w['cc_re'], w['cc_im'], w['d_skip'], w['w_glu'])
    ys, h_re, h_im = pl.pallas_call(
        functools.partial(_ssm_prompt_kernel, rows=rows, sub=sub),
        grid=(b, s // rows),
        in_specs=[pl.BlockSpec((None, rows, width), lambda bi, t: (bi, t, 0))]
                 + [_const_spec(z.shape) for z in weights],
        out_specs=(pl.BlockSpec((None, rows, width), lambda bi, t: (bi, t, 0)),
                   pl.BlockSpec((None, 1, N_STATE), lambda bi, t: (bi, 0, 0)),
                   pl.BlockSpec((None, 1, N_STATE), lambda bi, t: (bi, 0, 0))),
        out_shape=(jax.ShapeDtypeStruct((b, s, width), BF16),
                   jax.ShapeDtypeStruct((b, 1, N_STATE), F32),
                   jax.ShapeDtypeStruct((b, 1, N_STATE), F32)),
        scratch_shapes=[pltpu.VMEM((rows, N_STATE), F32), pltpu.VMEM((rows, N_STATE), F32),
                        pltpu.VMEM((1, N_STATE), F32), pltpu.VMEM((1, N_STATE), F32),
                        pltpu.VMEM((SUBLANES, N_STATE), F32), pltpu.VMEM((SUBLANES, N_STATE), F32)],
        compiler_params=_cparams(("parallel", "arbitrary")),
        name="ssm_prompt",
    )(u, *weights)
    return ys, h_re.reshape(b, N_GROUPS, STATE), h_im.reshape(b, N_GROUPS, STATE)


def _ssm_sample_kernel(u_ref, h0re_ref, h0im_ref, bre_ref, bim_ref, pre_ref, pim_ref, cre_ref, cim_ref,
                       dskip_ref, wglu_ref, ys_ref, hre_out, him_out, bu_re, bu_im, *, steps):
    a_re = pre_ref[0:1, :]
    a_im = pim_ref[0:1, :]
    h_re = h0re_ref[...]
    h_im = h0im_ref[...]

    def st_re(c0, n, val):
        bu_re[:, c0:c0 + n] = val

    def st_im(c0, n, val):
        bu_im[:, c0:c0 + n] = val

    for t in range(steps):
        u = u_ref[t]
        _ssm_in(u.astype(BF16), bre_ref, bim_ref, st_re, st_im)
        h_re, h_im = (a_re * h_re - a_im * h_im + bu_re[...], a_re * h_im + a_im * h_re + bu_im[...])
        y = _ssm_out(h_re, h_im, cre_ref, cim_ref) + dskip_ref[...] * u
        ys_ref[t] = _glu(y, wglu_ref).astype(ys_ref.dtype)
    hre_out[...] = h_re
    him_out[...] = h_im


def _ssm_sample(ut, h0_re, h0_im, w):
    t, bd, width = ut.shape
    ys, h_re, h_im = pl.pallas_call(
        functools.partial(_ssm_sample_kernel, steps=t),
        out_shape=(jax.ShapeDtypeStruct((t, bd, width), BF16),
                   jax.ShapeDtypeStruct((bd, N_STATE), F32),
                   jax.ShapeDtypeStruct((bd, N_STATE), F32)),
        scratch_shapes=[pltpu.VMEM((bd, N_STATE), F32), pltpu.VMEM((bd, N_STATE), F32)],
        compiler_params=pltpu.CompilerParams(vmem_limit_bytes=VMEM_LIMIT_BYTES),
        name="ssm_sample",
    )(ut, h0_re.reshape(bd, N_STATE).astype(F32), h0_im.reshape(bd, N_STATE).astype(F32),
      w['bb_re'], w['bb_im'], w['pow_re'][:1], w['pow_im'][:1], w['cc_re'], w['cc_im'], w['d_skip'], w['w_glu'])
    return ys, h_re.reshape(bd, N_GROUPS, STATE), h_im.reshape(bd, N_GROUPS, STATE)


def _post_kernel(*refs, rows, d_ff, row_major):
    if row_major:
        (x_ref, att_ref, ys_ref, sga_ref, sgs_ref, p_ref, woa_ref, wos_ref, wout_ref, gffn_ref, wup_ref,
         cw_ref, cbias_ref, wdown_ref, gple_ref, wpg_ref, wpp_ref, y_ref, tail_ref, hist) = refs
        hist0_ref = None
    else:
        (x_ref, att_ref, ys_ref, sga_ref, sgs_ref, p_ref, hist0_ref, woa_ref, wos_ref, wout_ref, gffn_ref, wup_ref,
         cw_ref, cbias_ref, wdown_ref, gple_ref, wpg_ref, wpp_ref, y_ref, tail_ref, hist) = refs
    ti = pl.program_id(1)
    pad = SUBLANES

    mixed = (sga_ref[...].astype(F32) * jnp.dot(att_ref[...], woa_ref[...], preferred_element_type=F32)
             + sgs_ref[...].astype(F32) * jnp.dot(ys_ref[...], wos_ref[...], preferred_element_type=F32))
    x1 = x_ref[...] + _bdot(mixed, wout_ref[...])
    xn = _rms(x1, gffn_ref[...]).astype(BF16)

    @pl.when(ti == 0)
    def _():
        if row_major:
            hist[0:pad, :] = jnp.zeros((pad, 2 * d_ff), F32)
        else:
            hist[...] = hist0_ref[...]

    def up_cols(c0):
        cols = slice(c0, c0 + FFN_CHUNK)
        up = jnp.dot(xn, wup_ref[:, cols], preferred_element_type=F32)
        if row_major:
            hist[pad:pad + rows, cols] = up
            s1 = hist[pad - 1:pad - 1 + rows, cols]
            s2 = hist[pad - 2:pad - 2 + rows, cols]
        else:
            s2 = hist[0:rows, cols]
            s1 = hist[rows:2 * rows, cols]
            hist[0:rows, cols] = s1
            hist[rows:2 * rows, cols] = up
        return cols, up, s1, s2

    def conv(args):
        cols, up, s1, s2 = args
        return cbias_ref[:, cols] + s2 * cw_ref[0:1, cols] + s1 * cw_ref[1:2, cols] + up * cw_ref[2:3, cols]

    n_chunks = d_ff // FFN_CHUNK
    ahead = [(up_cols(c * FFN_CHUNK), up_cols(d_ff + c * FFN_CHUNK)) for c in range(min(FFN_AHEAD, n_chunks))]
    f = jnp.zeros((rows, x1.shape[-1]), F32)
    for c in range(n_chunks):
        cur = ahead.pop(0)
        if c + FFN_AHEAD < n_chunks:
            ahead.append((up_cols((c + FFN_AHEAD) * FFN_CHUNK), up_cols(d_ff + (c + FFN_AHEAD) * FFN_CHUNK)))
        hdn = (jax.nn.gelu(conv(cur[0])) * conv(cur[1])).astype(BF16)
        f = f + jnp.dot(hdn, wdown_ref[c * FFN_CHUNK:(c + 1) * FFN_CHUNK, :], preferred_element_type=F32)
    x2 = x1 + f

    if row_major:
        tail = hist[rows:rows + pad, :]
        hist[0:pad, :] = tail
        tail_ref[...] = tail
    else:
        @pl.when(ti == pl.num_programs(1) - 1)
        def _():
            tail_ref[...] = hist[...]

    gate = jax.nn.sigmoid(_bdot(_rms(x2, gple_ref[...]), wpg_ref[...]))
    y_ref[...] = x2 + gate * _bdot(p_ref[...], wpp_ref[...])


def _post(x, att, ys, sga, sgs, p, w, hist0=None, step_rows=None):
    nb, total, d = x.shape
    d_ff = w['w_down'].shape[0]
    assert d_ff % FFN_CHUNK == 0
    row_major = hist0 is None
    rows = min(POST_ROWS, total) if row_major else step_rows
    assert total % rows == 0
    row_spec = lambda n: pl.BlockSpec((None, rows, n), lambda b, t: (b, t, 0))
    acts = [x, att, ys, sga, sgs, p]
    act_specs = [row_spec(z.shape[-1]) for z in acts]
    weights = (w['w_oa'], w['w_os'], w['w_out'], w['g_ffn'], w['w_up'], w['conv_w'], w['conv_b'], w['w_down'],
               w['g_ple'], w['w_ple_gate'], w['w_ple_proj'])
    if row_major:
        hist_rows = SUBLANES
        tail_spec = pl.BlockSpec((None, hist_rows, 2 * d_ff), lambda b, t: (b, 0, 0))
        tail_shape = jax.ShapeDtypeStruct((nb, hist_rows, 2 * d_ff), F32)
        scratch = pltpu.VMEM((rows + hist_rows, 2 * d_ff), F32)
    else:
        assert nb == 1
        hist_rows = 2 * rows
        acts.append(hist0)
        act_specs.append(_const_spec(hist0.shape))
        tail_spec = pl.BlockSpec((hist_rows, 2 * d_ff), lambda b, t: (0, 0), pipeline_mode=pl.Buffered(1))
        tail_shape = jax.ShapeDtypeStruct((hist_rows, 2 * d_ff), F32)
        scratch = pltpu.VMEM((hist_rows, 2 * d_ff), F32)
    return pl.pallas_call(
        functools.partial(_post_kernel, rows=rows, d_ff=d_ff, row_major=row_major),
        grid=(nb, total // rows),
        in_specs=act_specs + [_const_spec(z.shape) for z in weights],
        out_specs=(row_spec(d), tail_spec),
        out_shape=(jax.ShapeDtypeStruct((nb, total, d), F32), tail_shape),
        scratch_shapes=[scratch],
        compiler_params=_cparams(("parallel", "arbitrary")),
        name="post_prompt" if row_major else "post_sample",
    )(*acts, *weights)


def _rope_cos_sin(pos):
    inv_freq = np.power(np.float32(ROPE_THETA), -np.arange(0, QK_ROPE, 2, dtype=np.float32) / np.float32(QK_ROPE))
    ang = (np.asarray(pos, np.float32)[:, None] * inv_freq[None, :]).astype(np.float64)
    return np.cos(ang).astype(np.float32), np.sin(ang).astype(np.float32)


def _rope_tables_q(pos):
    cos, sin = _rope_cos_sin(pos)
    n = cos.shape[0]
    z = lambda k: np.zeros((n, k), np.float32)
    rc = np.concatenate([np.ones((n, QK_NOPE), np.float32), cos, cos, z(HEAD_PAD - QK_HEAD)], axis=1)
    rs = np.concatenate([z(QK_NOPE), -sin, sin, z(HEAD_PAD - QK_HEAD)], axis=1)
    return rc, rs, np.concatenate([cos, cos], axis=1), np.concatenate([-sin, sin], axis=1)


def _rope_tables_k(pos):
    cos, sin = _rope_cos_sin(pos)
    return (np.ascontiguousarray(np.concatenate([cos, cos], axis=1).T),
            np.ascontiguousarray(np.concatenate([sin, -sin], axis=1).T))


def _pad_heads(wm, width):
    rows = wm.shape[0]
    return jnp.pad(wm, ((0, 0), (0, 0), (0, HEAD_PAD - width))).reshape(rows, N_HEADS * HEAD_PAD)


def _layer_weights(lw, d_model, n_pow):
    w = {}
    row = lambda z: z.reshape(1, -1).astype(F32)
    off_ckv = Q_LORA
    off_kr = off_ckv + KV_LORA
    off_u = off_kr + QK_ROPE
    off_ga = off_u + SSM_WIDTH
    off_gs = off_ga + d_model
    win = lw['w_in']
    wt = win.T.astype(BF16)
    w['w_in'] = jnp.concatenate(
        [wt[:off_kr], wt[off_u:], wt[off_kr:off_u], jnp.zeros((LANES - QK_ROPE, d_model), BF16)], axis=0)
    w['g_mix'] = row(lw['g_mix'])
    w['g_cq'] = row(lw['g_cq'])
    w['g_ckv'] = row(lw['g_ckv'])
    w['w_uq'] = _pad_heads(lw['w_uq'], QK_HEAD).astype(BF16)
    w['w_uk'] = _pad_heads(lw['w_uk'], QK_NOPE).astype(BF16)
    w['w_uv_t'] = _pad_heads(lw['w_uv'], V_HEAD).T.astype(BF16)
    const = lambda z, dt: jnp.asarray(np.asarray(z, np.float32), dt)
    lane = np.arange(N_HEADS * HEAD_PAD)
    w['e_kr'] = const((lane[None, :] % HEAD_PAD) == (QK_NOPE + np.arange(QK_ROPE))[:, None], BF16)
    w['g_q'] = row(jnp.pad(lw['g_q'], (0, HEAD_PAD - QK_HEAD)))
    w['g_k_nope'] = row(jnp.pad(lw['g_k'][:QK_NOPE], (0, HEAD_PAD - QK_NOPE)))
    w['g_k_rope'] = row(lw['g_k'][QK_NOPE:])
    half = QK_ROPE // 2
    wq = lw['w_uq'] * lw['g_q'][None, None, :]
    partner = jnp.concatenate([jnp.zeros_like(wq[..., :QK_NOPE]), wq[..., QK_NOPE + half:QK_HEAD],
                               wq[..., QK_NOPE:QK_NOPE + half]], axis=-1)
    w['w_uq_rot'] = _pad_heads(partner, QK_HEAD).astype(BF16)
    idx = np.arange(QK_ROPE)
    w['swap_rope'] = const(idx[:, None] == ((idx + half) % QK_ROPE)[None, :], BF16)
    blk = np.arange(2 * HEAD_PAD) // HEAD_PAD
    w['ones2'] = const(blk[:, None] == blk[None, :], BF16)
    w['v_one'] = const(((lane % HEAD_PAD) == V_HEAD).reshape(-1, 1), F32)
    w['w_uk_perm'] = lw['w_uk'].transpose(0, 2, 1).reshape(KV_LORA, QK_NOPE * N_HEADS).astype(BF16)
    w['g_k_nope_perm'] = row(jnp.repeat(lw['g_k'][:QK_NOPE], N_HEADS))
    w['g_k_rope_col'] = lw['g_k'][QK_NOPE:].reshape(QK_ROPE, 1).astype(F32)
    w['w_uv_flat'] = lw['w_uv'].reshape(KV_LORA, N_HEADS * V_HEAD).astype(BF16)
    bb_re, bb_im, pow_re, pow_im = _ssm_discretize(lw['a_re'].astype(F32), lw['a_im'].astype(F32),
                                                   lw['log_dt'].astype(F32), lw['b_re'].astype(F32),
                                                   lw['b_im'].astype(F32), n_pow)
    w['bb_re'] = _block_diag_in(bb_re)
    w['bb_im'] = _block_diag_in(bb_im)
    w['pow_re'] = pow_re
    w['pow_im'] = pow_im
    w['cc_re'] = _block_diag_out(lw['c_re'].astype(F32))
    w['cc_im'] = _block_diag_out(lw['c_im'].astype(F32))
    w['d_skip'] = row(lw['d_skip'])
    w['w_glu'] = lw['w_glu'].astype(BF16)
    w['w_oa'] = lw['w_oa'].astype(BF16)
    w['w_os'] = lw['w_os'].astype(BF16)
    w['w_out'] = lw['w_out'].astype(BF16)
    w['g_ffn'] = row(lw['g_ffn'])
    w['w_up'] = lw['w_up'].astype(BF16)
    w['conv_w'] = lw['conv_w'].astype(F32)
    w['conv_b'] = row(lw['conv_b'])
    w['w_down'] = lw['w_down'].astype(BF16)
    w['g_ple'] = row(lw['g_ple'])
    w['w_ple_gate'] = lw['w_ple_gate'].astype(BF16)
    w['w_ple_proj'] = lw['w_ple_proj'].astype(BF16)
    return w


def kernel(x_prompt, x_sample, p_prompt, p_sample, cache_ckv, cache_kr, page_table, state_ssm_re, state_ssm_im,
           state_conv, g_mix, w_in, g_cq, g_ckv, w_uq, w_uk, w_uv, g_q, g_k, a_re, a_im, log_dt, b_re, b_im,
           c_re, c_im, d_skip, w_glu, w_oa, w_os, w_out, g_ffn, w_up, conv_w, conv_b, w_down, g_ple,
           w_ple_gate, w_ple_proj):
    params = dict(g_mix=g_mix, w_in=w_in, g_cq=g_cq, g_ckv=g_ckv, w_uq=w_uq, w_uk=w_uk, w_uv=w_uv, g_q=g_q,
                  g_k=g_k, a_re=a_re, a_im=a_im, log_dt=log_dt, b_re=b_re, b_im=b_im, c_re=c_re, c_im=c_im,
                  d_skip=d_skip, w_glu=w_glu, w_oa=w_oa, w_os=w_os, w_out=w_out, g_ffn=g_ffn, w_up=w_up,
                  conv_w=conv_w, conv_b=conv_b, w_down=w_down, g_ple=g_ple, w_ple_gate=w_ple_gate,
                  w_ple_proj=w_ple_proj)
    depth = w_in.shape[0]
    bp, seq, d_model = x_prompt.shape
    bd, t_new, _ = x_sample.shape
    past_len = page_table.shape[1] * PAGE_SIZE
    d_ff = w_down.shape[1]
    n_pow = max(min(SSM_ROWS, seq) // SUBLANES, 1)

    rope_prompt = _rope_tables_q(np.arange(seq))
    rope_sample = _rope_tables_q(np.repeat(past_len + np.arange(t_new), bd))
    rope_cache = _rope_tables_k(np.arange(past_len))

    yp, ys = x_prompt, x_sample
    outs = [[] for _ in range(10)]
    for i in range(depth):
        w = _layer_weights({k: v[i] for k, v in params.items()}, d_model, n_pow)

        q, k, v, ckv, kr, u, sga, sgs = _inproj(yp, w, rope_prompt, BF16, SCALE * LOG2_E)
        att = _flash_prompt(q, k, v)
        yssm, hr, hi = _ssm_prompt(u, w)
        yp, tail = _post(yp, att, yssm, sga, sgs, p_prompt[i], w)
        outs[0].append(ckv); outs[1].append(kr); outs[4].append(hr); outs[5].append(hi)
        outs[8].append(tail[:, SUBLANES - (CONV_W - 1):, :])

        n_tok = bd * t_new
        steps = lambda z: z.transpose(1, 0, 2).reshape(1, n_tok, z.shape[-1])
        entries = lambda z: z.reshape(t_new, bd, z.shape[-1]).transpose(1, 0, 2)
        q, k, v, ckv, kr, u, sga, sgs = _inproj(steps(ys), w, rope_sample, F32, SCALE)
        ckv = entries(ckv)
        att = _sample_attention(entries(q), entries(k), ckv, cache_ckv, cache_kr, page_table, i, w, rope_cache)
        yssm, hr, hi = _ssm_sample(u.reshape(t_new, bd, -1), state_ssm_re[i], state_ssm_im[i], w)
        hist0 = state_conv[i].astype(F32).transpose(1, 0, 2).reshape((CONV_W - 1) * bd, 2 * d_ff)
        y2, tail = _post(steps(ys), steps(att).astype(BF16), yssm.reshape(1, n_tok, -1), sga, sgs,
                         steps(p_sample[i]), w, hist0=hist0, step_rows=bd)
        ys = entries(y2)
        outs[2].append(ckv); outs[3].append(entries(kr))
        outs[6].append(hr); outs[7].append(hi)
        outs[9].append(tail.reshape(CONV_W - 1, bd, 2 * d_ff).transpose(1, 0, 2))

    st = [jnp.stack(o) for o in outs]
    return (yp, ys, st[0], st[1], st[2], st[3], st[4], st[5], st[6], st[7], st[8], st[9])
```

```python
import functools
import math

import jax
import jax.numpy as jnp
import numpy as np
from jax import lax
from jax.experimental import pallas as pl
from jax.experimental.pallas import tpu as pltpu

F32 = jnp.float32
BF16 = jnp.bfloat16

N_HEADS = 8
QK_NOPE = 64
QK_ROPE = 32
QK_HEAD = QK_NOPE + QK_ROPE
V_HEAD = 64
Q_LORA = 384
KV_LORA = 256
ROPE_THETA = 10000.0
SCALE = QK_HEAD ** -0.5
LOG2_E = math.log2(math.e)
NEG_INF = -1e30
SSM_WIDTH = 512
GROUP = 16
N_GROUPS = SSM_WIDTH // GROUP
STATE = 64
N_STATE = N_GROUPS * STATE
CONV_W = 3
EPS = 1e-6
PAGE_SIZE = 128

LANES = 128
SUBLANES = 8
HEAD_PAD = LANES
NEW_PAD = 2 * SUBLANES
V_ROWS = V_HEAD + 2 * SUBLANES
VMEM_LIMIT_BYTES = 56 * 1024 * 1024

INPROJ_ROWS = 512
FLASH_T = 512
FLASH_HEADS = 4
SSM_ROWS = 512
SSM_COLS = 1024
POST_ROWS = 256
FFN_CHUNK = 256
FFN_AHEAD = 6
SAMPLE_CHUNK_PLAN = (16, 16, 16, 8, 4, 4)
SAMPLE_ENTRIES = 2
SAMPLE_AHEAD = 1
SSM_GROUP_BLOCK = 8


def _cparams(sem):
    return pltpu.CompilerParams(dimension_semantics=sem, vmem_limit_bytes=VMEM_LIMIT_BYTES)


def _const_spec(shape):
    nd = len(shape)
    return pl.BlockSpec(shape, lambda *_: (0,) * nd, pipeline_mode=pl.Buffered(1))


def _rms(x, g):
    return x * lax.rsqrt(jnp.mean(x * x, axis=-1, keepdims=True) + EPS) * g


def _bdot(a, b):
    return jnp.dot(a.astype(BF16), b, preferred_element_type=F32)


def _dot_nt(a, b):
    return lax.dot_general(a, b, (((1,), (1,)), ((), ())), preferred_element_type=F32)


def _ssm_disc_kernel(are_ref, aim_ref, ldt_ref, arer_ref, aimr_ref, ldtr_ref, bre_ref, bim_ref,
                     bbre_ref, bbim_ref, pre_ref, pim_ref, *, n_pow):
    def zoh(a_re, a_im, ldt):
        dt = jnp.exp(ldt)
        mag = jnp.exp(dt * a_re)
        ab_re = mag * jnp.cos(dt * a_im)
        ab_im = mag * jnp.sin(dt * a_im)
        return ab_re, ab_im

    a_re = arer_ref[...]
    a_im = aimr_ref[...]
    ab_re, ab_im = zoh(a_re, a_im, ldtr_ref[...])
    den = a_re * a_re + a_im * a_im
    nr = ab_re - 1.0
    f_re = (nr * a_re + ab_im * a_im) / den
    f_im = (ab_im * a_re - nr * a_im) / den
    b_re = bre_ref[...]
    b_im = bim_ref[...]
    bbre_ref[...] = f_re * b_re - f_im * b_im
    bbim_ref[...] = f_re * b_im + f_im * b_re

    p_re, p_im = zoh(are_ref[...], aim_ref[...], ldt_ref[...])
    c_re, c_im = p_re, p_im
    for j in range(n_pow):
        pre_ref[j] = c_re
        pim_ref[j] = c_im
        c_re, c_im = c_re * p_re - c_im * p_im, c_re * p_im + c_im * p_re


def _ssm_discretize(a_re, a_im, log_dt, b_re, b_im, n_pow):
    g, p = a_re.shape
    rows = g * GROUP
    rep = lambda z: jnp.repeat(z, GROUP, axis=0)
    bt = lambda z: z.transpose(0, 2, 1).reshape(rows, p)
    ldt = log_dt.reshape(g, 1)
    out_shape = (jax.ShapeDtypeStruct((rows, p), F32), jax.ShapeDtypeStruct((rows, p), F32),
                 jax.ShapeDtypeStruct((n_pow, g, p), F32), jax.ShapeDtypeStruct((n_pow, g, p), F32))
    bb_re, bb_im, pow_re, pow_im = pl.pallas_call(
        functools.partial(_ssm_disc_kernel, n_pow=n_pow),
        out_shape=out_shape, name="ssm_disc",
    )(a_re, a_im, ldt, rep(a_re), rep(a_im), rep(ldt), bt(b_re), bt(b_im))
    return bb_re, bb_im, pow_re.reshape(n_pow, g * p), pow_im.reshape(n_pow, g * p)


def _block_diag_in(bb):
    nb = N_GROUPS // SSM_GROUP_BLOCK
    z = jnp.tile(bb.reshape(nb, SSM_GROUP_BLOCK * GROUP, STATE), (1, 1, SSM_GROUP_BLOCK))
    mask = (np.arange(SSM_GROUP_BLOCK * GROUP) // GROUP)[:, None] == (np.arange(SSM_GROUP_BLOCK * STATE) // STATE)[None, :]
    return (z * mask.astype(np.float32)).astype(BF16)


def _block_diag_out(c):
    nb = N_GROUPS // SSM_GROUP_BLOCK
    z = jnp.tile(c.transpose(0, 2, 1).reshape(nb, SSM_GROUP_BLOCK * STATE, GROUP), (1, 1, SSM_GROUP_BLOCK))
    mask = (np.arange(SSM_GROUP_BLOCK * STATE) // STATE)[:, None] == (np.arange(SSM_GROUP_BLOCK * GROUP) // GROUP)[None, :]
    return (z * mask.astype(np.float32)).astype(BF16)


OFF_CQ = 0
OFF_CKV = OFF_CQ + Q_LORA
OFF_U = OFF_CKV + KV_LORA


def _head_sumsq(z, ones2_ref):
    sq = (z * z).astype(BF16)
    span = ones2_ref.shape[0]
    parts = [jnp.dot(sq[:, j:j + span], ones2_ref[...], preferred_element_type=F32)
             for j in range(0, z.shape[-1], span)]
    return jnp.concatenate(parts, axis=-1)


def _inproj_kernel(x_ref, gmix_ref, win_ref, gcq_ref, gckv_ref, wuq_ref, wuqr_ref, wuk_ref, ekr_ref, swap_ref,
                   ones2_ref, wuvt_ref, vone_ref, gq_ref, gkn_ref, gkr_ref, rc_ref, rs_ref, cc_ref, cs_ref,
                   q_ref, k_ref, v_ref, ckv_ref, kr_ref, u_ref, sga_ref, sgs_ref, *, d_model, q_scale):
    off_ga = OFF_U + SSM_WIDTH
    off_gs = off_ga + d_model
    off_kr = off_gs + d_model
    inv = 1.0 / QK_HEAD
    blocks = [slice(h * HEAD_PAD, (h + 1) * HEAD_PAD) for h in range(N_HEADS)]
    xn = _rms(x_ref[...], gmix_ref[...])
    proj = _dot_nt(xn.astype(BF16), win_ref[...])
    cq = _rms(proj[:, OFF_CQ:OFF_CKV], gcq_ref[...])
    ckv = _rms(proj[:, OFF_CKV:OFF_U], gckv_ref[...])
    kr = proj[:, off_kr:off_kr + QK_ROPE]
    u_ref[...] = proj[:, OFF_U:off_ga]
    sga_ref[...] = jax.nn.sigmoid(proj[:, off_ga:off_gs]).astype(sga_ref.dtype)
    sgs_ref[...] = jax.nn.sigmoid(proj[:, off_gs:off_kr]).astype(sgs_ref.dtype)
    ckv_ref[...] = ckv
    kr_ref[...] = kr
    ckv_b = ckv.astype(BF16)
    v_ref[...] = (_dot_nt(wuvt_ref[...], ckv_b) + vone_ref[...]).astype(v_ref.dtype)

    cq_b = cq.astype(BF16)
    qf = jnp.dot(cq_b, wuq_ref[...], preferred_element_type=F32)
    qrot = jnp.dot(cq_b, wuqr_ref[...], preferred_element_type=F32)
    rn_q = lax.rsqrt(_head_sumsq(qf, ones2_ref) * inv + EPS)
    tc = rc_ref[...] * (gq_ref[...] * q_scale)
    ts = rs_ref[...] * q_scale
    for blk in blocks:
        q_ref[:, blk] = (rn_q[:, blk] * (qf[:, blk] * tc + qrot[:, blk] * ts)).astype(q_ref.dtype)

    kn = jnp.dot(ckv_b, wuk_ref[...], preferred_element_type=F32)
    krg = kr * gkr_ref[...]
    krr = krg * cc_ref[...] + _bdot(krg, swap_ref[...]) * cs_ref[...]
    placed = _bdot(krr, ekr_ref[...])
    r2 = jnp.sum(kr * kr, axis=-1, keepdims=True)
    rn_k = lax.rsqrt((_head_sumsq(kn, ones2_ref) + r2) * inv + EPS)
    for blk in blocks:
        k_ref[:, blk] = (rn_k[:, blk] * (kn[:, blk] * gkn_ref[...] + placed[:, blk])).astype(k_ref.dtype)


def _inproj(x, w, rope, act_dtype, q_scale):
    nb, rows, d = x.shape
    tr = min(INPROJ_ROWS, rows)
    assert rows % tr == 0
    hp = N_HEADS * HEAD_PAD
    row_spec = lambda n: pl.BlockSpec((None, tr, n), lambda b, t: (b, t, 0))
    tab_specs = [pl.BlockSpec((tr, z.shape[-1]), lambda b, t: (t, 0)) for z in rope]
    weights = (w['g_mix'], w['w_in'], w['g_cq'], w['g_ckv'], w['w_uq'], w['w_uq_rot'], w['w_uk'], w['e_kr'],
               w['swap_rope'], w['ones2'], w['w_uv_t'], w['v_one'], w['g_q'], w['g_k_nope'], w['g_k_rope'])
    out_shape = (
        jax.ShapeDtypeStruct((nb, rows, hp), act_dtype),
        jax.ShapeDtypeStruct((nb, rows, hp), act_dtype),
        jax.ShapeDtypeStruct((nb, hp, rows), act_dtype),
        jax.ShapeDtypeStruct((nb, rows, KV_LORA), F32),
        jax.ShapeDtypeStruct((nb, rows, QK_ROPE), F32),
        jax.ShapeDtypeStruct((nb, rows, SSM_WIDTH), F32),
        jax.ShapeDtypeStruct((nb, rows, d), BF16),
        jax.ShapeDtypeStruct((nb, rows, d), BF16),
    )
    out_specs = [row_spec(s.shape[-1]) for s in out_shape]
    out_specs[2] = pl.BlockSpec((None, hp, tr), lambda b, t: (b, 0, t))
    return pl.pallas_call(
        functools.partial(_inproj_kernel, d_model=d, q_scale=q_scale),
        grid=(nb, rows // tr),
        in_specs=[row_spec(d)] + [_const_spec(z.shape) for z in weights] + tab_specs,
        out_specs=tuple(out_specs),
        out_shape=out_shape,
        compiler_params=_cparams(("parallel", "parallel")),
        name="inproj",
    )(x, *weights, *rope)


def _flash_kernel(q_ref, k_ref, vt_ref, o_ref, s0, s1, m_sc, acc_sc, qt_sc, *, t, nh):
    qi = pl.program_id(2)
    heads = [slice(h * HEAD_PAD, (h + 1) * HEAD_PAD) for h in range(nh)]
    m_sc[...] = jnp.full(m_sc.shape, NEG_INF, F32)
    acc_sc[...] = jnp.zeros(acc_sc.shape, F32)

    half = nh // 2
    groups = (range(0, half), range(half, nh))

    for h in range(nh):
        qt_sc[h] = jnp.transpose(q_ref[:, heads[h]].astype(F32)).astype(BF16)

    def scores(kj, g, dst):
        start = pl.multiple_of(kj * t, t)
        for i, h in enumerate(groups[g]):
            dst[i] = jnp.dot(k_ref[pl.ds(start, t), heads[h]], qt_sc[h],
                             preferred_element_type=F32)

    def fold(kj, g, src, diagonal):
        start = pl.multiple_of(kj * t, t)
        if diagonal:
            keep = lax.broadcasted_iota(jnp.int32, (t, t), 0) <= lax.broadcasted_iota(jnp.int32, (t, t), 1)
        for i, h in enumerate(groups[g]):
            st = src[i]
            if diagonal:
                st = jnp.where(keep, st, NEG_INF)
            m = m_sc[h]
            m_new = jnp.maximum(m, jnp.max(st, axis=0, keepdims=True))
            p = jnp.exp2(st - m_new)
            corr = jnp.exp2(m - m_new)
            vrows = slice(h * HEAD_PAD, h * HEAD_PAD + V_ROWS)
            acc_sc[h] = acc_sc[h] * corr + jnp.dot(vt_ref[vrows, pl.ds(start, t)], p.astype(BF16),
                                                   preferred_element_type=F32)
            m_sc[h] = m_new

    scores(0, 0, s0)

    def chunk(j, carry):
        scores(j, 1, s1)
        fold(j, 0, s0, False)
        scores(j + 1, 0, s0)
        fold(j, 1, s1, False)
        return carry

    lax.fori_loop(0, qi // 2, lambda kk, c: chunk(2 * kk + 1, chunk(2 * kk, c)), 0)

    @pl.when(qi % 2 == 1)
    def _():
        chunk(qi - 1, 0)

    scores(qi, 1, s1)
    fold(qi, 0, s0, True)
    fold(qi, 1, s1, True)

    for h0 in range(0, nh, 2):
        pair = []
        for h in (h0, h0 + 1):
            acc = acc_sc[h]
            pair.append(acc[0:V_HEAD, :] / acc[V_HEAD:V_HEAD + 1, :])
        o_ref[:, h0 * V_HEAD:(h0 + 2) * V_HEAD] = jnp.transpose(jnp.concatenate(pair, axis=0)).astype(o_ref.dtype)


def _flash_prompt(q, k, vt):
    b, s, hp = q.shape
    tq = min(FLASH_T, s)
    nh = FLASH_HEADS
    assert s % tq == 0 and N_HEADS % nh == 0 and nh % 2 == 0
    return pl.pallas_call(
        functools.partial(_flash_kernel, t=tq, nh=nh),
        grid=(b, N_HEADS // nh, s // tq),
        in_specs=[pl.BlockSpec((None, tq, nh * HEAD_PAD), lambda bi, h, qi: (bi, qi, h)),
                  pl.BlockSpec((None, s, nh * HEAD_PAD), lambda bi, h, qi: (bi, 0, h)),
                  pl.BlockSpec((None, nh * HEAD_PAD, s), lambda bi, h, qi: (bi, h, 0))],
        out_specs=pl.BlockSpec((None, tq, nh * V_HEAD), lambda bi, h, qi: (bi, qi, h)),
        out_shape=jax.ShapeDtypeStruct((b, s, N_HEADS * V_HEAD), BF16),
        scratch_shapes=[pltpu.VMEM((nh // 2, tq, tq), F32), pltpu.VMEM((nh // 2, tq, tq), F32),
                        pltpu.VMEM((nh, 1, tq), F32), pltpu.VMEM((nh, V_ROWS, tq), F32),
                        pltpu.VMEM((nh, HEAD_PAD, tq), BF16)],
        compiler_params=_cparams(("parallel", "parallel", "arbitrary")),
        name="flash_prompt",
    )(q, k, vt)


def _sample_attn_kernel(pt_ref, qn_ref, qr_ref, qrs_ref, qf_ref, knew_ref, cnew_ref, ca_ref, cb_ref,
                        wukp_ref, gkn_ref, gkr_ref, wuv_ref, hmask_ref, ckv_hbm, kr_hbm, o_ref,
                        cf, rf, cbuf, sem_c, sem_r, *, layer, n_pages, n_new, chunks, n_ent):
    step = pl.program_id(0)
    nq = n_new * N_HEADS
    first_page = [sum(chunks[:i]) for i in range(len(chunks))]

    def pages_of(ch):
        return range(first_page[ch], first_page[ch] + chunks[ch])

    def keys_of(ch):
        return slice(first_page[ch] * PAGE_SIZE, (first_page[ch] + chunks[ch]) * PAGE_SIZE)

    par = step % 2

    def page_copies(entry, side, e, j):
        pg = pt_ref[entry * n_pages + j]
        return (pltpu.make_async_copy(ckv_hbm.at[layer, pg], cf.at[side, e, j], sem_c.at[side, e]),
                pltpu.make_async_copy(kr_hbm.at[layer, pg], rf.at[side, e, j], sem_r.at[side, e]))

    def start_step(first_entry, side):
        for e in range(n_ent):
            def body(j, carry):
                for cp in page_copies(first_entry + e, side, e, j):
                    cp.start()
                return carry
            lax.fori_loop(0, n_pages, body, 0, unroll=min(8, n_pages))

    @pl.when(step == 0)
    def _():
        start_step(0, 0)

    @pl.when(step + 1 < pl.num_programs(0))
    def _():
        start_step((step + 1) * n_ent, 1 - par)

    for e in range(n_ent):
        pltpu.make_async_copy(ckv_hbm.at[layer, pl.ds(0, n_pages)], cf.at[par, e], sem_c.at[par, e]).wait()
        pltpu.make_async_copy(kr_hbm.at[layer, pl.ds(0, n_pages)], rf.at[par, e], sem_r.at[par, e]).wait()

    zrows = LANES - NEW_PAD
    tok = lax.broadcasted_iota(jnp.int32, (nq, LANES), 0) // N_HEADS
    key = lax.broadcasted_iota(jnp.int32, (nq, LANES), 1)
    qabs, qr, qrs, own = [], [], [], []
    for e in range(n_ent):
        qn = (qn_ref[e] * gkn_ref[...]).astype(BF16)
        qabs.append(_dot_nt(qn, wukp_ref[...]).astype(BF16))
        qr.append(qr_ref[e].astype(BF16))
        qrs.append(qrs_ref[e].astype(BF16))
        knew = jnp.concatenate([knew_ref[e].astype(BF16), jnp.zeros((zrows, knew_ref.shape[-1]), BF16)], axis=0)
        cnew = jnp.concatenate([cnew_ref[e].astype(BF16), jnp.zeros((zrows, KV_LORA), BF16)], axis=0)
        s_new = _dot_nt(qf_ref[e].astype(BF16), knew)
        own.append((jnp.where(key <= tok, s_new, NEG_INF), cnew))

    def front(item):
        e, ch = item
        for i in pages_of(ch):
            cbuf[e, i * PAGE_SIZE:(i + 1) * PAGE_SIZE, :] = cf[par, e, i].astype(BF16)
        r_t = jnp.concatenate([rf[par, e, i] for i in pages_of(ch)], axis=1)
        c = cbuf[e, keys_of(ch), :]
        kn = jnp.dot(c, wukp_ref[...], preferred_element_type=F32)
        sq = kn * kn
        ss = sq[:, 0:LANES]
        for j in range(1, N_HEADS * QK_NOPE // LANES):
            ss = ss + sq[:, j * LANES:(j + 1) * LANES]
        return e, ch, c, ss, r_t, _dot_nt(qabs[e], c)

    def back(args):
        e, ch, c, ss, r_t, s_nope = args
        span = keys_of(ch)
        ss_t = jnp.transpose(ss)
        ss_h = ss_t[0:N_HEADS]
        for j in range(1, LANES // N_HEADS):
            ss_h = ss_h + ss_t[j * N_HEADS:(j + 1) * N_HEADS]
        r2 = jnp.sum(r_t * r_t, axis=0, keepdims=True)
        rn_h = lax.rsqrt((ss_h + r2) * (1.0 / QK_HEAD) + EPS)
        rn = jnp.concatenate([rn_h] * n_new, axis=0)
        rg = r_t * gkr_ref[...]
        s_rope = (jnp.dot(qr[e], (rg * ca_ref[:, span]).astype(BF16), preferred_element_type=F32)
                  + jnp.dot(qrs[e], (rg * cb_ref[:, span]).astype(BF16), preferred_element_type=F32))
        return e, rn * (s_nope + s_rope), c

    def weights(s):
        m = jnp.max(s, axis=-1, keepdims=True)
        p = jnp.exp(s - m)
        return m, p.astype(BF16), jnp.sum(p, axis=-1, keepdims=True)

    items = [(e, ch) for e in range(n_ent) for ch in range(len(chunks))]
    ahead = [front(it) for it in items[:SAMPLE_AHEAD]]
    pending = [(e, weights(own[e][0]), own[e][1]) for e in range(n_ent)]
    parts = [[] for _ in range(n_ent)]

    def value_product(item):
        e, (m, p, l), vals = item
        parts[e].append((m, l, jnp.dot(p, vals, preferred_element_type=F32)))

    for i in range(len(items)):
        cur = ahead.pop(0)
        if i + SAMPLE_AHEAD < len(items):
            ahead.append(front(items[i + SAMPLE_AHEAD]))
        e, s, vals = back(cur)
        pending.append((e, weights(s), vals))
        value_product(pending.pop(0))
    while pending:
        value_product(pending.pop(0))

    for e in range(n_ent):
        m = parts[e][0][0]
        for pm, _, _ in parts[e][1:]:
            m = jnp.maximum(m, pm)
        l = jnp.zeros_like(m)
        acc = jnp.zeros((nq, KV_LORA), F32)
        for pm, pl_, pacc in parts[e]:
            wgt = jnp.exp(pm - m)
            l = l + pl_ * wgt
            acc = acc + pacc * wgt
        o_lat = acc / l
        full = _bdot(o_lat, wuv_ref[...]) * hmask_ref[...]
        o_ref[e] = jnp.sum(full.reshape(n_new, N_HEADS, N_HEADS * V_HEAD), axis=1)


def _sample_attention(q, k, ckv, cache_ckv, cache_kr, page_table, layer, w, rope_k):
    bd, t, hp = q.shape
    n_logical = page_table.shape[1]
    if sum(SAMPLE_CHUNK_PLAN) == n_logical:
        chunks = SAMPLE_CHUNK_PLAN
    else:
        size = min(SAMPLE_CHUNK_PLAN[-1], n_logical)
        assert n_logical % size == 0
        chunks = (size,) * (n_logical // size)
    assert t <= NEW_PAD
    nq = t * N_HEADS
    n_keys = n_logical * PAGE_SIZE
    head_mask = ((np.arange(nq) % N_HEADS)[:, None] == (np.arange(N_HEADS * V_HEAD) // V_HEAD)[None, :]
                 ).astype(np.float32)
    qh = q.reshape(bd, t, N_HEADS, HEAD_PAD)
    eye = np.eye(N_HEADS, dtype=np.float32)

    qn = jnp.einsum('bthd,hk->bthdk', qh[..., :QK_NOPE], eye).reshape(bd, nq, QK_NOPE * N_HEADS)
    qr = qh[..., QK_NOPE:QK_HEAD].reshape(bd, nq, QK_ROPE)
    qrs = jnp.concatenate([qr[..., QK_ROPE // 2:], qr[..., :QK_ROPE // 2]], axis=-1)
    qf = jnp.einsum('bthl,hk->bthkl', qh, eye).reshape(bd, nq, hp)
    pad_t = lambda z: jnp.pad(z, ((0, 0), (0, NEW_PAD - t), (0, 0)))
    knew = pad_t(k)
    cnew = pad_t(ckv)
    ca, cb = rope_k
    cache_kr_t = jnp.swapaxes(cache_kr, 2, 3)

    n_ent = SAMPLE_ENTRIES if bd % SAMPLE_ENTRIES == 0 else 1

    def bspec(rows, n):
        return pl.BlockSpec((n_ent, rows, n), lambda b, pt: (b, 0, 0))

    def wspec(shape):
        nd = len(shape)
        return pl.BlockSpec(shape, lambda b, pt: (0,) * nd)

    weights = (ca, cb, w['w_uk_perm'], w['g_k_nope_perm'], w['g_k_rope_col'], w['w_uv_flat'], head_mask)
    in_specs = ([bspec(nq, QK_NOPE * N_HEADS), bspec(nq, QK_ROPE), bspec(nq, QK_ROPE), bspec(nq, hp),
                 bspec(NEW_PAD, hp), bspec(NEW_PAD, KV_LORA)]
                + [wspec(z.shape) for z in weights]
                + [pl.BlockSpec(memory_space=pl.ANY), pl.BlockSpec(memory_space=pl.ANY)])
    grid_spec = pltpu.PrefetchScalarGridSpec(
        num_scalar_prefetch=1, grid=(bd // n_ent,), in_specs=in_specs,
        out_specs=pl.BlockSpec((n_ent, t, N_HEADS * V_HEAD), lambda b, pt: (b, 0, 0)),
        scratch_shapes=[pltpu.VMEM((2, n_ent, n_logical, PAGE_SIZE, KV_LORA), F32),
                        pltpu.VMEM((2, n_ent, n_logical, QK_ROPE, PAGE_SIZE), F32),
                        pltpu.VMEM((n_ent, n_keys, KV_LORA), BF16),
                        pltpu.SemaphoreType.DMA((2, n_ent)), pltpu.SemaphoreType.DMA((2, n_ent))])
    return pl.pallas_call(
        functools.partial(_sample_attn_kernel, layer=layer, n_pages=n_logical, n_new=t, chunks=chunks,
                          n_ent=n_ent),
        grid_spec=grid_spec,
        out_shape=jax.ShapeDtypeStruct((bd, t, N_HEADS * V_HEAD), F32),
        compiler_params=_cparams(("arbitrary",)),
        name="sample_attn",
    )(page_table.reshape(-1), qn, qr, qrs, qf, knew, cnew, *weights, cache_ckv, cache_kr_t)


def _ssm_in(u_b, bre_ref, bim_ref, store_re, store_im):
    nb = bre_ref.shape[0]
    kw = bre_ref.shape[1]
    nw = bre_ref.shape[2]
    for kb in range(nb):
        blk = u_b[:, kb * kw:(kb + 1) * kw]
        store_re(kb * nw, nw, jnp.dot(blk, bre_ref[kb], preferred_element_type=F32))
        store_im(kb * nw, nw, jnp.dot(blk, bim_ref[kb], preferred_element_type=F32))


def _ssm_out(h_re, h_im, cre_ref, cim_ref):
    nb = cre_ref.shape[0]
    kw = cre_ref.shape[1]
    parts = []
    for kb in range(nb):
        parts.append(jnp.dot(h_re[:, kb * kw:(kb + 1) * kw].astype(BF16), cre_ref[kb], preferred_element_type=F32)
                     - jnp.dot(h_im[:, kb * kw:(kb + 1) * kw].astype(BF16), cim_ref[kb], preferred_element_type=F32))
    return jnp.concatenate(parts, axis=-1)


def _glu(y, wglu_ref):
    g = _bdot(jax.nn.gelu(y), wglu_ref[...])
    half = g.shape[-1] // 2
    return g[:, :half] * jax.nn.sigmoid(g[:, half:])


def _ssm_prompt_kernel(u_ref, perm_ref, permt_ref, bre_ref, bim_ref, pre_ref, pim_ref, cre_ref, cim_ref,
                       dskip_ref, wglu_ref, ys_ref, hre_out, him_out,
                       hre, him, car_re, car_im, hin_re, hin_im, *, rows, sub):
    ti = pl.program_id(1)

    @pl.when(ti == 0)
    def _():
        car_re[...] = jnp.zeros(car_re.shape, F32)
        car_im[...] = jnp.zeros(car_im.shape, F32)

    u = u_ref[...]
    u_hi = u.astype(BF16)
    u_lo = (u - u_hi.astype(F32)).astype(BF16)
    perm = perm_ref[...]
    up = jnp.dot(perm, u_hi, preferred_element_type=F32) + jnp.dot(perm, u_lo, preferred_element_type=F32)

    def st_re(c0, n, val):
        hre[:, c0:c0 + n] = val

    def st_im(c0, n, val):
        him[:, c0:c0 + n] = val

    _ssm_in(up.astype(BF16), bre_ref, bim_ref, st_re, st_im)

    n_cb = N_STATE // SSM_COLS
    for cb in range(n_cb):
        cols = slice(cb * SSM_COLS, (cb + 1) * SSM_COLS)
        a_re = jnp.broadcast_to(pre_ref[0:1, cols], (SUBLANES, SSM_COLS))
        a_im = jnp.broadcast_to(pim_ref[0:1, cols], (SUBLANES, SSM_COLS))

        def local(j, st):
            s_re, s_im = st
            r0 = pl.multiple_of(j * SUBLANES, SUBLANES)
            n_re = a_re * s_re - a_im * s_im + hre[pl.ds(r0, SUBLANES), cols]
            n_im = a_re * s_im + a_im * s_re + him[pl.ds(r0, SUBLANES), cols]
            hre[pl.ds(r0, SUBLANES), cols] = n_re
            him[pl.ds(r0, SUBLANES), cols] = n_im
            return n_re, n_im

        zero = jnp.zeros((SUBLANES, SSM_COLS), F32)
        e_re, e_im = lax.fori_loop(0, sub, local, (zero, zero))

        as_re = pre_ref[sub - 1:sub, cols]
        as_im = pim_ref[sub - 1:sub, cols]
        c_re = car_re[:, cols]
        c_im = car_im[:, cols]
        for s in range(SUBLANES):
            hin_re[s:s + 1, cols] = c_re
            hin_im[s:s + 1, cols] = c_im
            c_re, c_im = (as_re * c_re - as_im * c_im + e_re[s:s + 1],
                          as_re * c_im + as_im * c_re + e_im[s:s + 1])
        car_re[:, cols] = c_re
        car_im[:, cols] = c_im
        g_re = hin_re[:, cols]
        g_im = hin_im[:, cols]

        def fix(j, d):
            d_re, d_im = d
            d_re, d_im = a_re * d_re - a_im * d_im, a_re * d_im + a_im * d_re
            r0 = pl.multiple_of(j * SUBLANES, SUBLANES)
            hre[pl.ds(r0, SUBLANES), cols] = hre[pl.ds(r0, SUBLANES), cols] + d_re
            him[pl.ds(r0, SUBLANES), cols] = him[pl.ds(r0, SUBLANES), cols] + d_im
            return d_re, d_im

        lax.fori_loop(0, sub, fix, (g_re, g_im))

    y = _ssm_out(hre[...], him[...], cre_ref, cim_ref) + dskip_ref[...] * up
    ys = _glu(y, wglu_ref).astype(BF16)
    ys_ref[...] = jnp.dot(permt_ref[...], ys, preferred_element_type=F32).astype(ys_ref.dtype)

    @pl.when(ti == pl.num_programs(1) - 1)
    def _():
        hre_out[...] = car_re[...]
        him_out[...] = car_im[...]


def _ssm_prompt(u, w):
    b, s, width = u.shape
    rows = min(SSM_ROWS, s)
    assert s % rows == 0 and rows % SUBLANES == 0
    sub = rows // SUBLANES
    r = np.arange(rows)
    t_of_r = (r % SUBLANES) * sub + r // SUBLANES
    perm = jnp.asarray((t_of_r[:, None] == np.arange(rows)[None, :]).astype(np.float32), BF16)
    pre = w['pow_re'][:sub]
    pim = w['pow_im'][:sub]
    weights = (perm, perm.T, w['bb_re'], w['bb_im'], pre, pim, w['cc_re'], w['cc_im'], w['d_skip'], w['w_glu'])
    ys, h_re, h_im = pl.pallas_call(
        functools.partial(_ssm_prompt_kernel, rows=rows, sub=sub),
        grid=(b, s // rows),
        in_specs=[pl.BlockSpec((None, rows, width), lambda bi, t: (bi, t, 0))]
                 + [_const_spec(z.shape) for z in weights],
        out_specs=(pl.BlockSpec((None, rows, width), lambda bi, t: (bi, t, 0)),
                   pl.BlockSpec((None, 1, N_STATE), lambda bi, t: (bi, 0, 0)),
                   pl.BlockSpec((None, 1, N_STATE), lambda bi, t: (bi, 0, 0))),
        out_shape=(jax.ShapeDtypeStruct((b, s, width), BF16),
                   jax.ShapeDtypeStruct((b, 1, N_STATE), F32),
                   jax.ShapeDtypeStruct((b, 1, N_STATE), F32)),
        scratch_shapes=[pltpu.VMEM((rows, N_STATE), F32), pltpu.VMEM((rows, N_STATE), F32),
                        pltpu.VMEM((1, N_STATE), F32), pltpu.VMEM((1, N_STATE), F32),
                        pltpu.VMEM((SUBLANES, N_STATE), F32), pltpu.VMEM((SUBLANES, N_STATE), F32)],
        compiler_params=_cparams(("parallel", "arbitrary")),
        name="ssm_prompt",
    )(u, *weights)
    return ys, h_re.reshape(b, N_GROUPS, STATE), h_im.reshape(b, N_GROUPS, STATE)


def _ssm_sample_kernel(u_ref, h0re_ref, h0im_ref, bre_ref, bim_ref, pre_ref, pim_ref, cre_ref, cim_ref,
                       dskip_ref, wglu_ref, ys_ref, hre_out, him_out, bu_re, bu_im, *, steps):
    a_re = pre_ref[0:1, :]
    a_im = pim_ref[0:1, :]
    h_re = h0re_ref[...]
    h_im = h0im_ref[...]

    def st_re(c0, n, val):
        bu_re[:, c0:c0 + n] = val

    def st_im(c0, n, val):
        bu_im[:, c0:c0 + n] = val

    for t in range(steps):
        u = u_ref[t]
        _ssm_in(u.astype(BF16), bre_ref, bim_ref, st_re, st_im)
        h_re, h_im = (a_re * h_re - a_im * h_im + bu_re[...], a_re * h_im + a_im * h_re + bu_im[...])
        y = _ssm_out(h_re, h_im, cre_ref, cim_ref) + dskip_ref[...] * u
        ys_ref[t] = _glu(y, wglu_ref).astype(ys_ref.dtype)
    hre_out[...] = h_re
    him_out[...] = h_im


def _ssm_sample(ut, h0_re, h0_im, w):
    t, bd, width = ut.shape
    ys, h_re, h_im = pl.pallas_call(
        functools.partial(_ssm_sample_kernel, steps=t),
        out_shape=(jax.ShapeDtypeStruct((t, bd, width), BF16),
                   jax.ShapeDtypeStruct((bd, N_STATE), F32),
                   jax.ShapeDtypeStruct((bd, N_STATE), F32)),
        scratch_shapes=[pltpu.VMEM((bd, N_STATE), F32), pltpu.VMEM((bd, N_STATE), F32)],
        compiler_params=pltpu.CompilerParams(vmem_limit_bytes=VMEM_LIMIT_BYTES),
        name="ssm_sample",
    )(ut, h0_re.reshape(bd, N_STATE).astype(F32), h0_im.reshape(bd, N_STATE).astype(F32),
      w['bb_re'], w['bb_im'], w['pow_re'][:1], w['pow_im'][:1], w['cc_re'], w['cc_im'], w['d_skip'], w['w_glu'])
    return ys, h_re.reshape(bd, N_GROUPS, STATE), h_im.reshape(bd, N_GROUPS, STATE)


def _post_kernel(*refs, rows, d_ff, row_major):
    if row_major:
        (x_ref, att_ref, ys_ref, sga_ref, sgs_ref, p_ref, woa_ref, wos_ref, wout_ref, gffn_ref, wup_ref,
         cw_ref, cbias_ref, wdown_ref, gple_ref, wpg_ref, wpp_ref, y_ref, tail_ref, hist) = refs
        hist0_ref = None
    else:
        (x_ref, att_ref, ys_ref, sga_ref, sgs_ref, p_ref, hist0_ref, woa_ref, wos_ref, wout_ref, gffn_ref, wup_ref,
         cw_ref, cbias_ref, wdown_ref, gple_ref, wpg_ref, wpp_ref, y_ref, tail_ref, hist) = refs
    ti = pl.program_id(1)
    pad = SUBLANES

    mixed = (sga_ref[...].astype(F32) * jnp.dot(att_ref[...], woa_ref[...], preferred_element_type=F32)
             + sgs_ref[...].astype(F32) * jnp.dot(ys_ref[...], wos_ref[...], preferred_element_type=F32))
    x1 = x_ref[...] + _bdot(mixed, wout_ref[...])
    xn = _rms(x1, gffn_ref[...]).astype(BF16)

    @pl.when(ti == 0)
    def _():
        if row_major:
            hist[0:pad, :] = jnp.zeros((pad, 2 * d_ff), F32)
        else:
            hist[...] = hist0_ref[...]

    def up_cols(c0):
        cols = slice(c0, c0 + FFN_CHUNK)
        up = jnp.dot(xn, wup_ref[:, cols], preferred_element_type=F32)
        if row_major:
            hist[pad:pad + rows, cols] = up
            s1 = hist[pad - 1:pad - 1 + rows, cols]
            s2 = hist[pad - 2:pad - 2 + rows, cols]
        else:
            s2 = hist[0:rows, cols]
            s1 = hist[rows:2 * rows, cols]
            hist[0:rows, cols] = s1
            hist[rows:2 * rows, cols] = up
        return cols, up, s1, s2

    def conv(args):
        cols, up, s1, s2 = args
        return cbias_ref[:, cols] + s2 * cw_ref[0:1, cols] + s1 * cw_ref[1:2, cols] + up * cw_ref[2:3, cols]

    n_chunks = d_ff // FFN_CHUNK
    ahead = [(up_cols(c * FFN_CHUNK), up_cols(d_ff + c * FFN_CHUNK)) for c in range(min(FFN_AHEAD, n_chunks))]
    f = jnp.zeros((rows, x1.shape[-1]), F32)
    for c in range(n_chunks):
        cur = ahead.pop(0)
        if c + FFN_AHEAD < n_chunks:
            ahead.append((up_cols((c + FFN_AHEAD) * FFN_CHUNK), up_cols(d_ff + (c + FFN_AHEAD) * FFN_CHUNK)))
        hdn = (jax.nn.gelu(conv(cur[0])) * conv(cur[1])).astype(BF16)
        f = f + jnp.dot(hdn, wdown_ref[c * FFN_CHUNK:(c + 1) * FFN_CHUNK, :], preferred_element_type=F32)
    x2 = x1 + f

    if row_major:
        tail = hist[rows:rows + pad, :]
        hist[0:pad, :] = tail
        tail_ref[...] = tail
    else:
        @pl.when(ti == pl.num_programs(1) - 1)
        def _():
            tail_ref[...] = hist[...]

    gate = jax.nn.sigmoid(_bdot(_rms(x2, gple_ref[...]), wpg_ref[...]))
    y_ref[...] = x2 + gate * _bdot(p_ref[...], wpp_ref[...])


def _post(x, att, ys, sga, sgs, p, w, hist0=None, step_rows=None):
    nb, total, d = x.shape
    d_ff = w['w_down'].shape[0]
    assert d_ff % FFN_CHUNK == 0
    row_major = hist0 is None
    rows = min(POST_ROWS, total) if row_major else step_rows
    assert total % rows == 0
    row_spec = lambda n: pl.BlockSpec((None, rows, n), lambda b, t: (b, t, 0))
    acts = [x, att, ys, sga, sgs, p]
    act_specs = [row_spec(z.shape[-1]) for z in acts]
    weights = (w['w_oa'], w['w_os'], w['w_out'], w['g_ffn'], w['w_up'], w['conv_w'], w['conv_b'], w['w_down'],
               w['g_ple'], w['w_ple_gate'], w['w_ple_proj'])
    if row_major:
        hist_rows = SUBLANES
        tail_spec = pl.BlockSpec((None, hist_rows, 2 * d_ff), lambda b, t: (b, 0, 0))
        tail_shape = jax.ShapeDtypeStruct((nb, hist_rows, 2 * d_ff), F32)
        scratch = pltpu.VMEM((rows + hist_rows, 2 * d_ff), F32)
    else:
        assert nb == 1
        hist_rows = 2 * rows
        acts.append(hist0)
        act_specs.append(_const_spec(hist0.shape))
        tail_spec = pl.BlockSpec((hist_rows, 2 * d_ff), lambda b, t: (0, 0), pipeline_mode=pl.Buffered(1))
        tail_shape = jax.ShapeDtypeStruct((hist_rows, 2 * d_ff), F32)
        scratch = pltpu.VMEM((hist_rows, 2 * d_ff), F32)
    return pl.pallas_call(
        functools.partial(_post_kernel, rows=rows, d_ff=d_ff, row_major=row_major),
        grid=(nb, total // rows),
        in_specs=act_specs + [_const_spec(z.shape) for z in weights],
        out_specs=(row_spec(d), tail_spec),
        out_shape=(jax.ShapeDtypeStruct((nb, total, d), F32), tail_shape),
        scratch_shapes=[scratch],
        compiler_params=_cparams(("parallel", "arbitrary")),
        name="post_prompt" if row_major else "post_sample",
    )(*acts, *weights)


def _rope_cos_sin(pos):
    inv_freq = np.power(np.float32(ROPE_THETA), -np.arange(0, QK_ROPE, 2, dtype=np.float32) / np.float32(QK_ROPE))
    ang = (np.asarray(pos, np.float32)[:, None] * inv_freq[None, :]).astype(np.float64)
    return np.cos(ang).astype(np.float32), np.sin(ang).astype(np.float32)


def _rope_tables_q(pos):
    cos, sin = _rope_cos_sin(pos)
    n = cos.shape[0]
    z = lambda k: np.zeros((n, k), np.float32)
    rc = np.concatenate([np.ones((n, QK_NOPE), np.float32), cos, cos, z(HEAD_PAD - QK_HEAD)], axis=1)
    rs = np.concatenate([z(QK_NOPE), -sin, sin, z(HEAD_PAD - QK_HEAD)], axis=1)
    return rc, rs, np.concatenate([cos, cos], axis=1), np.concatenate([-sin, sin], axis=1)


def _rope_tables_k(pos):
    cos, sin = _rope_cos_sin(pos)
    return (np.ascontiguousarray(np.concatenate([cos, cos], axis=1).T),
            np.ascontiguousarray(np.concatenate([sin, -sin], axis=1).T))


def _pad_heads(wm, width):
    rows = wm.shape[0]
    return jnp.pad(wm, ((0, 0), (0, 0), (0, HEAD_PAD - width))).reshape(rows, N_HEADS * HEAD_PAD)


def _layer_weights(lw, d_model, n_pow):
    w = {}
    row = lambda z: z.reshape(1, -1).astype(F32)
    off_ckv = Q_LORA
    off_kr = off_ckv + KV_LORA
    off_u = off_kr + QK_ROPE
    off_ga = off_u + SSM_WIDTH
    off_gs = off_ga + d_model
    win = lw['w_in']
    wt = win.T.astype(BF16)
    w['w_in'] = jnp.concatenate(
        [wt[:off_kr], wt[off_u:], wt[off_kr:off_u], jnp.zeros((LANES - QK_ROPE, d_model), BF16)], axis=0)
    w['g_mix'] = row(lw['g_mix'])
    w['g_cq'] = row(lw['g_cq'])
    w['g_ckv'] = row(lw['g_ckv'])
    w['w_uq'] = _pad_heads(lw['w_uq'], QK_HEAD).astype(BF16)
    w['w_uk'] = _pad_heads(lw['w_uk'], QK_NOPE).astype(BF16)
    w['w_uv_t'] = _pad_heads(lw['w_uv'], V_HEAD).T.astype(BF16)
    const = lambda z, dt: jnp.asarray(np.asarray(z, np.float32), dt)
    lane = np.arange(N_HEADS * HEAD_PAD)
    w['e_kr'] = const((lane[None, :] % HEAD_PAD) == (QK_NOPE + np.arange(QK_ROPE))[:, None], BF16)
    w['g_q'] = row(jnp.pad(lw['g_q'], (0, HEAD_PAD - QK_HEAD)))
    w['g_k_nope'] = row(jnp.pad(lw['g_k'][:QK_NOPE], (0, HEAD_PAD - QK_NOPE)))
    w['g_k_rope'] = row(lw['g_k'][QK_NOPE:])
    half = QK_ROPE // 2
    wq = lw['w_uq'] * lw['g_q'][None, None, :]
    partner = jnp.concatenate([jnp.zeros_like(wq[..., :QK_NOPE]), wq[..., QK_NOPE + half:QK_HEAD],
                               wq[..., QK_NOPE:QK_NOPE + half]], axis=-1)
    w['w_uq_rot'] = _pad_heads(partner, QK_HEAD).astype(BF16)
    idx = np.arange(QK_ROPE)
    w['swap_rope'] = const(idx[:, None] == ((idx + half) % QK_ROPE)[None, :], BF16)
    blk = np.arange(2 * HEAD_PAD) // HEAD_PAD
    w['ones2'] = const(blk[:, None] == blk[None, :], BF16)
    w['v_one'] = const(((lane % HEAD_PAD) == V_HEAD).reshape(-1, 1), F32)
    w['w_uk_perm'] = lw['w_uk'].transpose(0, 2, 1).reshape(KV_LORA, QK_NOPE * N_HEADS).astype(BF16)
    w['g_k_nope_perm'] = row(jnp.repeat(lw['g_k'][:QK_NOPE], N_HEADS))
    w['g_k_rope_col'] = lw['g_k'][QK_NOPE:].reshape(QK_ROPE, 1).astype(F32)
    w['w_uv_flat'] = lw['w_uv'].reshape(KV_LORA, N_HEADS * V_HEAD).astype(BF16)
    bb_re, bb_im, pow_re, pow_im = _ssm_discretize(lw['a_re'].astype(F32), lw['a_im'].astype(F32),
                                                   lw['log_dt'].astype(F32), lw['b_re'].astype(F32),
                                                   lw['b_im'].astype(F32), n_pow)
    w['bb_re'] = _block_diag_in(bb_re)
    w['bb_im'] = _block_diag_in(bb_im)
    w['pow_re'] = pow_re
    w['pow_im'] = pow_im
    w['cc_re'] = _block_diag_out(lw['c_re'].astype(F32))
    w['cc_im'] = _block_diag_out(lw['c_im'].astype(F32))
    w['d_skip'] = row(lw['d_skip'])
    w['w_glu'] = lw['w_glu'].astype(BF16)
    w['w_oa'] = lw['w_oa'].astype(BF16)
    w['w_os'] = lw['w_os'].astype(BF16)
    w['w_out'] = lw['w_out'].astype(BF16)
    w['g_ffn'] = row(lw['g_ffn'])
    w['w_up'] = lw['w_up'].astype(BF16)
    w['conv_w'] = lw['conv_w'].astype(F32)
    w['conv_b'] = row(lw['conv_b'])
    w['w_down'] = lw['w_down'].astype(BF16)
    w['g_ple'] = row(lw['g_ple'])
    w['w_ple_gate'] = lw['w_ple_gate'].astype(BF16)
    w['w_ple_proj'] = lw['w_ple_proj'].astype(BF16)
    return w


def kernel(x_prompt, x_sample, p_prompt, p_sample, cache_ckv, cache_kr, page_table, state_ssm_re, state_ssm_im,
           state_conv, g_mix, w_in, g_cq, g_ckv, w_uq, w_uk, w_uv, g_q, g_k, a_re, a_im, log_dt, b_re, b_im,
           c_re, c_im, d_skip, w_glu, w_oa, w_os, w_out, g_ffn, w_up, conv_w, conv_b, w_down, g_ple,
           w_ple_gate, w_ple_proj):
    params = dict(g_mix=g_mix, w_in=w_in, g_cq=g_cq, g_ckv=g_ckv, w_uq=w_uq, w_uk=w_uk, w_uv=w_uv, g_q=g_q,
                  g_k=g_k, a_re=a_re, a_im=a_im, log_dt=log_dt, b_re=b_re, b_im=b_im, c_re=c_re, c_im=c_im,
                  d_skip=d_skip, w_glu=w_glu, w_oa=w_oa, w_os=w_os, w_out=w_out, g_ffn=g_ffn, w_up=w_up,
                  conv_w=conv_w, conv_b=conv_b, w_down=w_down, g_ple=g_ple, w_ple_gate=w_ple_gate,
                  w_ple_proj=w_ple_proj)
    depth = w_in.shape[0]
    bp, seq, d_model = x_prompt.shape
    bd, t_new, _ = x_sample.shape
    past_len = page_table.shape[1] * PAGE_SIZE
    d_ff = w_down.shape[1]
    n_pow = max(min(SSM_ROWS, seq) // SUBLANES, 1)

    rope_prompt = _rope_tables_q(np.arange(seq))
    rope_sample = _rope_tables_q(np.repeat(past_len + np.arange(t_new), bd))
    rope_cache = _rope_tables_k(np.arange(past_len))

    yp, ys = x_prompt, x_sample
    outs = [[] for _ in range(10)]
    for i in range(depth):
        w = _layer_weights({k: v[i] for k, v in params.items()}, d_model, n_pow)

        q, k, v, ckv, kr, u, sga, sgs = _inproj(yp, w, rope_prompt, BF16, SCALE * LOG2_E)
        att = _flash_prompt(q, k, v)
        yssm, hr, hi = _ssm_prompt(u, w)
        yp, tail = _post(yp, att, yssm, sga, sgs, p_prompt[i], w)
        outs[0].append(ckv); outs[1].append(kr); outs[4].append(hr); outs[5].append(hi)
        outs[8].append(tail[:, SUBLANES - (CONV_W - 1):, :])

        n_tok = bd * t_new
        steps = lambda z: z.transpose(1, 0, 2).reshape(1, n_tok, z.shape[-1])
        entries = lambda z: z.reshape(t_new, bd, z.shape[-1]).transpose(1, 0, 2)
        q, k, v, ckv, kr, u, sga, sgs = _inproj(steps(ys), w, rope_sample, F32, SCALE)
        ckv = entries(ckv)
        att = _sample_attention(entries(q), entries(k), ckv, cache_ckv, cache_kr, page_table, i, w, rope_cache)
        yssm, hr, hi = _ssm_sample(u.reshape(t_new, bd, -1), state_ssm_re[i], state_ssm_im[i], w)
        hist0 = state_conv[i].astype(F32).transpose(1, 0, 2).reshape((CONV_W - 1) * bd, 2 * d_ff)
        y2, tail = _post(steps(ys), steps(att).astype(BF16), yssm.reshape(1, n_tok, -1), sga, sgs,
                         steps(p_sample[i]), w, hist0=hist0, step_rows=bd)
        ys = entries(y2)
        outs[2].append(ckv); outs[3].append(entries(kr))
        outs[6].append(hr); outs[7].append(hi)
        outs[9].append(tail.reshape(CONV_W - 1, bd, 2 * d_ff).transpose(1, 0, 2))

    st = [jnp.stack(o) for o in outs]
    return (yp, ys, st[0], st[1], st[2], st[3], st[4], st[5], st[6], st[7], st[8], st[9])
```

```python
import functools
import math

import jax
import jax.numpy as jnp
import numpy as np
from jax import lax
from jax.experimental import pallas as pl
from jax.experimental.pallas import tpu as pltpu

F32 = jnp.float32
BF16 = jnp.bfloat16

N_HEADS = 8
QK_NOPE = 64
QK_ROPE = 32
QK_HEAD = QK_NOPE + QK_ROPE
V_HEAD = 64
Q_LORA = 384
KV_LORA = 256
ROPE_THETA = 10000.0
SCALE = QK_HEAD ** -0.5
LOG2_E = math.log2(math.e)
NEG_INF = -1e30
SSM_WIDTH = 512
GROUP = 16
N_GROUPS = SSM_WIDTH // GROUP
STATE = 64
N_STATE = N_GROUPS * STATE
CONV_W = 3
EPS = 1e-6
PAGE_SIZE = 128

LANES = 128
SUBLANES = 8
HEAD_PAD = LANES
NEW_PAD = 2 * SUBLANES
V_ROWS = V_HEAD + 2 * SUBLANES
VMEM_LIMIT_BYTES = 56 * 1024 * 1024

INPROJ_ROWS = 512
FLASH_T = 512
FLASH_HEADS = 4
SSM_ROWS = 512
SSM_COLS = 1024
POST_ROWS = 256
FFN_CHUNK = 256
FFN_AHEAD = 4
SAMPLE_CHUNK_PLAN = (16, 16, 16, 8, 4, 4)
SAMPLE_ENTRIES = 2
SAMPLE_AHEAD = 1
SSM_GROUP_BLOCK = 8


def _cparams(sem):
    return pltpu.CompilerParams(dimension_semantics=sem, vmem_limit_bytes=VMEM_LIMIT_BYTES)


def _const_spec(shape):
    nd = len(shape)
    return pl.BlockSpec(shape, lambda *_: (0,) * nd, pipeline_mode=pl.Buffered(1))


def _rms(x, g):
    return x * lax.rsqrt(jnp.mean(x * x, axis=-1, keepdims=True) + EPS) * g


def _bdot(a, b):
    return jnp.dot(a.astype(BF16), b, preferred_element_type=F32)


def _dot_nt(a, b):
    return lax.dot_general(a, b, (((1,), (1,)), ((), ())), preferred_element_type=F32)


def _ssm_disc_kernel(are_ref, aim_ref, ldt_ref, arer_ref, aimr_ref, ldtr_ref, bre_ref, bim_ref,
                     bbre_ref, bbim_ref, pre_ref, pim_ref, *, n_pow):
    def zoh(a_re, a_im, ldt):
        dt = jnp.exp(ldt)
        mag = jnp.exp(dt * a_re)
        ab_re = mag * jnp.cos(dt * a_im)
        ab_im = mag * jnp.sin(dt * a_im)
        return ab_re, ab_im

    a_re = arer_ref[...]
    a_im = aimr_ref[...]
    ab_re, ab_im = zoh(a_re, a_im, ldtr_ref[...])
    den = a_re * a_re + a_im * a_im
    nr = ab_re - 1.0
    f_re = (nr * a_re + ab_im * a_im) / den
    f_im = (ab_im * a_re - nr * a_im) / den
    b_re = bre_ref[...]
    b_im = bim_ref[...]
    bbre_ref[...] = f_re * b_re - f_im * b_im
    bbim_ref[...] = f_re * b_im + f_im * b_re

    p_re, p_im = zoh(are_ref[...], aim_ref[...], ldt_ref[...])
    c_re, c_im = p_re, p_im
    for j in range(n_pow):
        pre_ref[j] = c_re
        pim_ref[j] = c_im
        c_re, c_im = c_re * p_re - c_im * p_im, c_re * p_im + c_im * p_re


def _ssm_discretize(a_re, a_im, log_dt, b_re, b_im, n_pow):
    g, p = a_re.shape
    rows = g * GROUP
    rep = lambda z: jnp.repeat(z, GROUP, axis=0)
    bt = lambda z: z.transpose(0, 2, 1).reshape(rows, p)
    ldt = log_dt.reshape(g, 1)
    out_shape = (jax.ShapeDtypeStruct((rows, p), F32), jax.ShapeDtypeStruct((rows, p), F32),
                 jax.ShapeDtypeStruct((n_pow, g, p), F32), jax.ShapeDtypeStruct((n_pow, g, p), F32))
    bb_re, bb_im, pow_re, pow_im = pl.pallas_call(
        functools.partial(_ssm_disc_kernel, n_pow=n_pow),
        out_shape=out_shape, name="ssm_disc",
    )(a_re, a_im, ldt, rep(a_re), rep(a_im), rep(ldt), bt(b_re), bt(b_im))
    return bb_re, bb_im, pow_re.reshape(n_pow, g * p), pow_im.reshape(n_pow, g * p)


def _block_diag_in(bb):
    nb = N_GROUPS // SSM_GROUP_BLOCK
    z = jnp.tile(bb.reshape(nb, SSM_GROUP_BLOCK * GROUP, STATE), (1, 1, SSM_GROUP_BLOCK))
    mask = (np.arange(SSM_GROUP_BLOCK * GROUP) // GROUP)[:, None] == (np.arange(SSM_GROUP_BLOCK * STATE) // STATE)[None, :]
    return (z * mask.astype(np.float32)).astype(BF16)


def _block_diag_out(c):
    nb = N_GROUPS // SSM_GROUP_BLOCK
    z = jnp.tile(c.transpose(0, 2, 1).reshape(nb, SSM_GROUP_BLOCK * STATE, GROUP), (1, 1, SSM_GROUP_BLOCK))
    mask = (np.arange(SSM_GROUP_BLOCK * STATE) // STATE)[:, None] == (np.arange(SSM_GROUP_BLOCK * GROUP) // GROUP)[None, :]
    return (z * mask.astype(np.float32)).astype(BF16)


OFF_CQ = 0
OFF_CKV = OFF_CQ + Q_LORA
OFF_U = OFF_CKV + KV_LORA


def _head_sumsq(z, ones2_ref):
    parts = []
    for j in range(0, z.shape[-1], HEAD_PAD):
        blk = z[:, j:j + HEAD_PAD]
        parts.append(jnp.broadcast_to(jnp.sum(blk * blk, axis=-1, keepdims=True), blk.shape))
    return jnp.concatenate(parts, axis=-1)


def _inproj_kernel(x_ref, gmix_ref, win_ref, gcq_ref, gckv_ref, wuq_ref, wuqr_ref, wuk_ref, ekr_ref, swap_ref,
                   ones2_ref, wuvt_ref, vone_ref, gq_ref, gkn_ref, gkr_ref, rc_ref, rs_ref, cc_ref, cs_ref,
                   q_ref, k_ref, v_ref, ckv_ref, kr_ref, u_ref, sga_ref, sgs_ref, *, d_model, q_scale):
    off_ga = OFF_U + SSM_WIDTH
    off_gs = off_ga + d_model
    off_kr = off_gs + d_model
    inv = 1.0 / QK_HEAD
    blocks = [slice(h * HEAD_PAD, (h + 1) * HEAD_PAD) for h in range(N_HEADS)]
    xn = _rms(x_ref[...], gmix_ref[...])
    proj = _dot_nt(xn.astype(BF16), win_ref[...])
    cq = _rms(proj[:, OFF_CQ:OFF_CKV], gcq_ref[...])
    ckv = _rms(proj[:, OFF_CKV:OFF_U], gckv_ref[...])
    kr = proj[:, off_kr:off_kr + QK_ROPE]
    u_ref[...] = proj[:, OFF_U:off_ga]
    sga_ref[...] = jax.nn.sigmoid(proj[:, off_ga:off_gs]).astype(sga_ref.dtype)
    sgs_ref[...] = jax.nn.sigmoid(proj[:, off_gs:off_kr]).astype(sgs_ref.dtype)
    ckv_ref[...] = ckv
    kr_ref[...] = kr
    ckv_b = ckv.astype(BF16)
    v_ref[...] = (_dot_nt(wuvt_ref[...], ckv_b) + vone_ref[...]).astype(v_ref.dtype)

    cq_b = cq.astype(BF16)
    qf = jnp.dot(cq_b, wuq_ref[...], preferred_element_type=F32)
    qrot = jnp.dot(cq_b, wuqr_ref[...], preferred_element_type=F32)
    rn_q = lax.rsqrt(_head_sumsq(qf, ones2_ref) * inv + EPS)
    tc = rc_ref[...] * (gq_ref[...] * q_scale)
    ts = rs_ref[...] * q_scale
    for blk in blocks:
        q_ref[:, blk] = (rn_q[:, blk] * (qf[:, blk] * tc + qrot[:, blk] * ts)).astype(q_ref.dtype)

    kn = jnp.dot(ckv_b, wuk_ref[...], preferred_element_type=F32)
    krg = kr * gkr_ref[...]
    krr = krg * cc_ref[...] + _bdot(krg, swap_ref[...]) * cs_ref[...]
    placed = _bdot(krr, ekr_ref[...])
    r2 = jnp.sum(kr * kr, axis=-1, keepdims=True)
    rn_k = lax.rsqrt((_head_sumsq(kn, ones2_ref) + r2) * inv + EPS)
    for blk in blocks:
        k_ref[:, blk] = (rn_k[:, blk] * (kn[:, blk] * gkn_ref[...] + placed[:, blk])).astype(k_ref.dtype)


def _inproj(x, w, rope, act_dtype, q_scale):
    nb, rows, d = x.shape
    tr = min(INPROJ_ROWS, rows)
    assert rows % tr == 0
    hp = N_HEADS * HEAD_PAD
    row_spec = lambda n: pl.BlockSpec((None, tr, n), lambda b, t: (b, t, 0))
    tab_specs = [pl.BlockSpec((tr, z.shape[-1]), lambda b, t: (t, 0)) for z in rope]
    weights = (w['g_mix'], w['w_in'], w['g_cq'], w['g_ckv'], w['w_uq'], w['w_uq_rot'], w['w_uk'], w['e_kr'],
               w['swap_rope'], w['ones2'], w['w_uv_t'], w['v_one'], w['g_q'], w['g_k_nope'], w['g_k_rope'])
    out_shape = (
        jax.ShapeDtypeStruct((nb, rows, hp), act_dtype),
        jax.ShapeDtypeStruct((nb, rows, hp), act_dtype),
        jax.ShapeDtypeStruct((nb, hp, rows), act_dtype),
        jax.ShapeDtypeStruct((nb, rows, KV_LORA), F32),
        jax.ShapeDtypeStruct((nb, rows, QK_ROPE), F32),
        jax.ShapeDtypeStruct((nb, rows, SSM_WIDTH), F32),
        jax.ShapeDtypeStruct((nb, rows, d), BF16),
        jax.ShapeDtypeStruct((nb, rows, d), BF16),
    )
    out_specs = [row_spec(s.shape[-1]) for s in out_shape]
    out_specs[2] = pl.BlockSpec((None, hp, tr), lambda b, t: (b, 0, t))
    return pl.pallas_call(
        functools.partial(_inproj_kernel, d_model=d, q_scale=q_scale),
        grid=(nb, rows // tr),
        in_specs=[row_spec(d)] + [_const_spec(z.shape) for z in weights] + tab_specs,
        out_specs=tuple(out_specs),
        out_shape=out_shape,
        compiler_params=_cparams(("parallel", "parallel")),
        name="inproj",
    )(x, *weights, *rope)


def _flash_kernel(q_ref, k_ref, vt_ref, o_ref, s0, s1, m_sc, acc_sc, qt_sc, *, t, nh):
    qi = pl.program_id(2)
    heads = [slice(h * HEAD_PAD, (h + 1) * HEAD_PAD) for h in range(nh)]
    m_sc[...] = jnp.full(m_sc.shape, NEG_INF, F32)
    acc_sc[...] = jnp.zeros(acc_sc.shape, F32)

    half = nh // 2
    groups = (range(0, half), range(half, nh))

    for h in range(nh):
        qt_sc[h] = jnp.transpose(q_ref[:, heads[h]].astype(F32)).astype(BF16)

    def scores(kj, g, dst):
        start = pl.multiple_of(kj * t, t)
        for i, h in enumerate(groups[g]):
            dst[i] = jnp.dot(k_ref[pl.ds(start, t), heads[h]], qt_sc[h],
                             preferred_element_type=F32)

    def fold(kj, g, src, diagonal):
        start = pl.multiple_of(kj * t, t)
        if diagonal:
            keep = lax.broadcasted_iota(jnp.int32, (t, t), 0) <= lax.broadcasted_iota(jnp.int32, (t, t), 1)
        for i, h in enumerate(groups[g]):
            st = src[i]
            if diagonal:
                st = jnp.where(keep, st, NEG_INF)
            m = m_sc[h]
            m_new = jnp.maximum(m, jnp.max(st, axis=0, keepdims=True))
            p = jnp.exp2(st - m_new)
            corr = jnp.exp2(m - m_new)
            vrows = slice(h * HEAD_PAD, h * HEAD_PAD + V_ROWS)
            acc_sc[h] = acc_sc[h] * corr + jnp.dot(vt_ref[vrows, pl.ds(start, t)], p.astype(BF16),
                                                   preferred_element_type=F32)
            m_sc[h] = m_new

    scores(0, 0, s0)

    def chunk(j, carry):
        scores(j, 1, s1)
        fold(j, 0, s0, False)
        scores(j + 1, 0, s0)
        fold(j, 1, s1, False)
        return carry

    lax.fori_loop(0, qi // 2, lambda kk, c: chunk(2 * kk + 1, chunk(2 * kk, c)), 0)

    @pl.when(qi % 2 == 1)
    def _():
        chunk(qi - 1, 0)

    scores(qi, 1, s1)
    fold(qi, 0, s0, True)
    fold(qi, 1, s1, True)

    for h0 in range(0, nh, 2):
        pair = []
        for h in (h0, h0 + 1):
            acc = acc_sc[h]
            pair.append(acc[0:V_HEAD, :] / acc[V_HEAD:V_HEAD + 1, :])
        o_ref[:, h0 * V_HEAD:(h0 + 2) * V_HEAD] = jnp.transpose(jnp.concatenate(pair, axis=0)).astype(o_ref.dtype)


def _flash_prompt(q, k, vt):
    b, s, hp = q.shape
    tq = min(FLASH_T, s)
    nh = FLASH_HEADS
    assert s % tq == 0 and N_HEADS % nh == 0 and nh % 2 == 0
    return pl.pallas_call(
        functools.partial(_flash_kernel, t=tq, nh=nh),
        grid=(b, N_HEADS // nh, s // tq),
        in_specs=[pl.BlockSpec((None, tq, nh * HEAD_PAD), lambda bi, h, qi: (bi, qi, h)),
                  pl.BlockSpec((None, s, nh * HEAD_PAD), lambda bi, h, qi: (bi, 0, h)),
                  pl.BlockSpec((None, nh * HEAD_PAD, s), lambda bi, h, qi: (bi, h, 0))],
        out_specs=pl.BlockSpec((None, tq, nh * V_HEAD), lambda bi, h, qi: (bi, qi, h)),
        out_shape=jax.ShapeDtypeStruct((b, s, N_HEADS * V_HEAD), BF16),
        scratch_shapes=[pltpu.VMEM((nh // 2, tq, tq), F32), pltpu.VMEM((nh // 2, tq, tq), F32),
                        pltpu.VMEM((nh, 1, tq), F32), pltpu.VMEM((nh, V_ROWS, tq), F32),
                        pltpu.VMEM((nh, HEAD_PAD, tq), BF16)],
        compiler_params=_cparams(("parallel", "parallel", "arbitrary")),
        name="flash_prompt",
    )(q, k, vt)


def _sample_attn_kernel(pt_ref, qn_ref, qr_ref, qrs_ref, qf_ref, knew_ref, cnew_ref, ca_ref, cb_ref,
                        wukp_ref, gkn_ref, gkr_ref, wuv_ref, hmask_ref, ckv_hbm, kr_hbm, o_ref,
                        cf, rf, cbuf, sem_c, sem_r, *, layer, n_pages, n_new, chunks, n_ent):
    step = pl.program_id(0)
    nq = n_new * N_HEADS
    first_page = [sum(chunks[:i]) for i in range(len(chunks))]

    def pages_of(ch):
        return range(first_page[ch], first_page[ch] + chunks[ch])

    def keys_of(ch):
        return slice(first_page[ch] * PAGE_SIZE, (first_page[ch] + chunks[ch]) * PAGE_SIZE)

    par = step % 2

    def page_copies(entry, side, e, j):
        pg = pt_ref[entry * n_pages + j]
        return (pltpu.make_async_copy(ckv_hbm.at[layer, pg], cf.at[side, e, j], sem_c.at[side, e]),
                pltpu.make_async_copy(kr_hbm.at[layer, pg], rf.at[side, e, j], sem_r.at[side, e]))

    def start_step(first_entry, side):
        for e in range(n_ent):
            def body(j, carry):
                for cp in page_copies(first_entry + e, side, e, j):
                    cp.start()
                return carry
            lax.fori_loop(0, n_pages, body, 0, unroll=min(8, n_pages))

    @pl.when(step == 0)
    def _():
        start_step(0, 0)

    @pl.when(step + 1 < pl.num_programs(0))
    def _():
        start_step((step + 1) * n_ent, 1 - par)

    for e in range(n_ent):
        pltpu.make_async_copy(ckv_hbm.at[layer, pl.ds(0, n_pages)], cf.at[par, e], sem_c.at[par, e]).wait()
        pltpu.make_async_copy(kr_hbm.at[layer, pl.ds(0, n_pages)], rf.at[par, e], sem_r.at[par, e]).wait()

    zrows = LANES - NEW_PAD
    tok = lax.broadcasted_iota(jnp.int32, (nq, LANES), 0) // N_HEADS
    key = lax.broadcasted_iota(jnp.int32, (nq, LANES), 1)
    qabs, qr, qrs, own = [], [], [], []
    for e in range(n_ent):
        qn = (qn_ref[e] * gkn_ref[...]).astype(BF16)
        qabs.append(_dot_nt(qn, wukp_ref[...]).astype(BF16))
        qr.append(qr_ref[e].astype(BF16))
        qrs.append(qrs_ref[e].astype(BF16))
        knew = jnp.concatenate([knew_ref[e].astype(BF16), jnp.zeros((zrows, knew_ref.shape[-1]), BF16)], axis=0)
        cnew = jnp.concatenate([cnew_ref[e].astype(BF16), jnp.zeros((zrows, KV_LORA), BF16)], axis=0)
        s_new = _dot_nt(qf_ref[e].astype(BF16), knew)
        own.append((jnp.where(key <= tok, s_new, NEG_INF), cnew))

    def front(item):
        e, ch = item
        for i in pages_of(ch):
            cbuf[e, i * PAGE_SIZE:(i + 1) * PAGE_SIZE, :] = cf[par, e, i].astype(BF16)
        r_t = jnp.concatenate([rf[par, e, i] for i in pages_of(ch)], axis=1)
        c = cbuf[e, keys_of(ch), :]
        kn = jnp.dot(c, wukp_ref[...], preferred_element_type=F32)
        sq = kn * kn
        ss = sq[:, 0:LANES]
        for j in range(1, N_HEADS * QK_NOPE // LANES):
            ss = ss + sq[:, j * LANES:(j + 1) * LANES]
        return e, ch, c, ss, r_t, _dot_nt(qabs[e], c)

    def back(args):
        e, ch, c, ss, r_t, s_nope = args
        span = keys_of(ch)
        ss_t = jnp.transpose(ss)
        ss_h = ss_t[0:N_HEADS]
        for j in range(1, LANES // N_HEADS):
            ss_h = ss_h + ss_t[j * N_HEADS:(j + 1) * N_HEADS]
        r2 = jnp.sum(r_t * r_t, axis=0, keepdims=True)
        rn_h = lax.rsqrt((ss_h + r2) * (1.0 / QK_HEAD) + EPS)
        rn = jnp.concatenate([rn_h] * n_new, axis=0)
        rg = r_t * gkr_ref[...]
        s_rope = (jnp.dot(qr[e], (rg * ca_ref[:, span]).astype(BF16), preferred_element_type=F32)
                  + jnp.dot(qrs[e], (rg * cb_ref[:, span]).astype(BF16), preferred_element_type=F32))
        return e, rn * (s_nope + s_rope), c

    def weights(s):
        m = jnp.max(s, axis=-1, keepdims=True)
        p = jnp.exp(s - m)
        return m, p.astype(BF16), jnp.sum(p, axis=-1, keepdims=True)

    items = [(e, ch) for e in range(n_ent) for ch in range(len(chunks))]
    ahead = [front(it) for it in items[:SAMPLE_AHEAD]]
    pending = [(e, weights(own[e][0]), own[e][1]) for e in range(n_ent)]
    parts = [[] for _ in range(n_ent)]

    def value_product(item):
        e, (m, p, l), vals = item
        parts[e].append((m, l, jnp.dot(p, vals, preferred_element_type=F32)))

    for i in range(len(items)):
        cur = ahead.pop(0)
        if i + SAMPLE_AHEAD < len(items):
            ahead.append(front(items[i + SAMPLE_AHEAD]))
        e, s, vals = back(cur)
        pending.append((e, weights(s), vals))
        value_product(pending.pop(0))
    while pending:
        value_product(pending.pop(0))

    for e in range(n_ent):
        m = parts[e][0][0]
        for pm, _, _ in parts[e][1:]:
            m = jnp.maximum(m, pm)
        l = jnp.zeros_like(m)
        acc = jnp.zeros((nq, KV_LORA), F32)
        for pm, pl_, pacc in parts[e]:
            wgt = jnp.exp(pm - m)
            l = l + pl_ * wgt
            acc = acc + pacc * wgt
        o_lat = acc / l
        full = _bdot(o_lat, wuv_ref[...]) * hmask_ref[...]
        o_ref[e] = jnp.sum(full.reshape(n_new, N_HEADS, N_HEADS * V_HEAD), axis=1)


def _sample_attention(q, k, ckv, cache_ckv, cache_kr, page_table, layer, w, rope_k):
    bd, t, hp = q.shape
    n_logical = page_table.shape[1]
    if sum(SAMPLE_CHUNK_PLAN) == n_logical:
        chunks = SAMPLE_CHUNK_PLAN
    else:
        size = min(SAMPLE_CHUNK_PLAN[-1], n_logical)
        assert n_logical % size == 0
        chunks = (size,) * (n_logical // size)
    assert t <= NEW_PAD
    nq = t * N_HEADS
    n_keys = n_logical * PAGE_SIZE
    head_mask = ((np.arange(nq) % N_HEADS)[:, None] == (np.arange(N_HEADS * V_HEAD) // V_HEAD)[None, :]
                 ).astype(np.float32)
    qh = q.reshape(bd, t, N_HEADS, HEAD_PAD)
    eye = np.eye(N_HEADS, dtype=np.float32)

    qn = jnp.einsum('bthd,hk->bthdk', qh[..., :QK_NOPE], eye).reshape(bd, nq, QK_NOPE * N_HEADS)
    qr = qh[..., QK_NOPE:QK_HEAD].reshape(bd, nq, QK_ROPE)
    qrs = jnp.concatenate([qr[..., QK_ROPE // 2:], qr[..., :QK_ROPE // 2]], axis=-1)
    qf = jnp.einsum('bthl,hk->bthkl', qh, eye).reshape(bd, nq, hp)
    pad_t = lambda z: jnp.pad(z, ((0, 0), (0, NEW_PAD - t), (0, 0)))
    knew = pad_t(k)
    cnew = pad_t(ckv)
    ca, cb = rope_k
    cache_kr_t = jnp.swapaxes(cache_kr, 2, 3)

    n_ent = SAMPLE_ENTRIES if bd % SAMPLE_ENTRIES == 0 else 1

    def bspec(rows, n):
        return pl.BlockSpec((n_ent, rows, n), lambda b, pt: (b, 0, 0))

    def wspec(shape):
        nd = len(shape)
        return pl.BlockSpec(shape, lambda b, pt: (0,) * nd)

    weights = (ca, cb, w['w_uk_perm'], w['g_k_nope_perm'], w['g_k_rope_col'], w['w_uv_flat'], head_mask)
    in_specs = ([bspec(nq, QK_NOPE * N_HEADS), bspec(nq, QK_ROPE), bspec(nq, QK_ROPE), bspec(nq, hp),
                 bspec(NEW_PAD, hp), bspec(NEW_PAD, KV_LORA)]
                + [wspec(z.shape) for z in weights]
                + [pl.BlockSpec(memory_space=pl.ANY), pl.BlockSpec(memory_space=pl.ANY)])
    grid_spec = pltpu.PrefetchScalarGridSpec(
        num_scalar_prefetch=1, grid=(bd // n_ent,), in_specs=in_specs,
        out_specs=pl.BlockSpec((n_ent, t, N_HEADS * V_HEAD), lambda b, pt: (b, 0, 0)),
        scratch_shapes=[pltpu.VMEM((2, n_ent, n_logical, PAGE_SIZE, KV_LORA), F32),
                        pltpu.VMEM((2, n_ent, n_logical, QK_ROPE, PAGE_SIZE), F32),
                        pltpu.VMEM((n_ent, n_keys, KV_LORA), BF16),
                        pltpu.SemaphoreType.DMA((2, n_ent)), pltpu.SemaphoreType.DMA((2, n_ent))])
    return pl.pallas_call(
        functools.partial(_sample_attn_kernel, layer=layer, n_pages=n_logical, n_new=t, chunks=chunks,
                          n_ent=n_ent),
        grid_spec=grid_spec,
        out_shape=jax.ShapeDtypeStruct((bd, t, N_HEADS * V_HEAD), F32),
        compiler_params=_cparams(("arbitrary",)),
        name="sample_attn",
    )(page_table.reshape(-1), qn, qr, qrs, qf, knew, cnew, *weights, cache_ckv, cache_kr_t)


def _ssm_in(u_b, bre_ref, bim_ref, store_re, store_im):
    nb = bre_ref.shape[0]
    kw = bre_ref.shape[1]
    nw = bre_ref.shape[2]
    for kb in range(nb):
        blk = u_b[:, kb * kw:(kb + 1) * kw]
        store_re(kb * nw, nw, jnp.dot(blk, bre_ref[kb], preferred_element_type=F32))
        store_im(kb * nw, nw, jnp.dot(blk, bim_ref[kb], preferred_element_type=F32))


def _ssm_out(h_re, h_im, cre_ref, cim_ref):
    nb = cre_ref.shape[0]
    kw = cre_ref.shape[1]
    parts = []
    for kb in range(nb):
        parts.append(jnp.dot(h_re[:, kb * kw:(kb + 1) * kw].astype(BF16), cre_ref[kb], preferred_element_type=F32)
                     - jnp.dot(h_im[:, kb * kw:(kb + 1) * kw].astype(BF16), cim_ref[kb], preferred_element_type=F32))
    return jnp.concatenate(parts, axis=-1)


def _glu(y, wglu_ref):
    g = _bdot(jax.nn.gelu(y), wglu_ref[...])
    half = g.shape[-1] // 2
    return g[:, :half] * jax.nn.sigmoid(g[:, half:])


def _ssm_prompt_kernel(u_ref, perm_ref, permt_ref, bre_ref, bim_ref, pre_ref, pim_ref, cre_ref, cim_ref,
                       dskip_ref, wglu_ref, ys_ref, hre_out, him_out,
                       hre, him, car_re, car_im, hin_re, hin_im, *, rows, sub):
    ti = pl.program_id(1)

    @pl.when(ti == 0)
    def _():
        car_re[...] = jnp.zeros(car_re.shape, F32)
        car_im[...] = jnp.zeros(car_im.shape, F32)

    u = u_ref[...]
    u_hi = u.astype(BF16)
    u_lo = (u - u_hi.astype(F32)).astype(BF16)
    perm = perm_ref[...]
    up = jnp.dot(perm, u_hi, preferred_element_type=F32) + jnp.dot(perm, u_lo, preferred_element_type=F32)

    def st_re(c0, n, val):
        hre[:, c0:c0 + n] = val

    def st_im(c0, n, val):
        him[:, c0:c0 + n] = val

    _ssm_in(up.astype(BF16), bre_ref, bim_ref, st_re, st_im)

    n_cb = N_STATE // SSM_COLS
    for cb in range(n_cb):
        cols = slice(cb * SSM_COLS, (cb + 1) * SSM_COLS)
        a_re = jnp.broadcast_to(pre_ref[0:1, cols], (SUBLANES, SSM_COLS))
        a_im = jnp.broadcast_to(pim_ref[0:1, cols], (SUBLANES, SSM_COLS))

        def local(j, st):
            s_re, s_im = st
            r0 = pl.multiple_of(j * SUBLANES, SUBLANES)
            n_re = a_re * s_re - a_im * s_im + hre[pl.ds(r0, SUBLANES), cols]
            n_im = a_re * s_im + a_im * s_re + him[pl.ds(r0, SUBLANES), cols]
            hre[pl.ds(r0, SUBLANES), cols] = n_re
            him[pl.ds(r0, SUBLANES), cols] = n_im
            return n_re, n_im

        zero = jnp.zeros((SUBLANES, SSM_COLS), F32)
        e_re, e_im = lax.fori_loop(0, sub, local, (zero, zero))

        as_re = pre_ref[sub - 1:sub, cols]
        as_im = pim_ref[sub - 1:sub, cols]
        c_re = car_re[:, cols]
        c_im = car_im[:, cols]
        for s in range(SUBLANES):
            hin_re[s:s + 1, cols] = c_re
            hin_im[s:s + 1, cols] = c_im
            c_re, c_im = (as_re * c_re - as_im * c_im + e_re[s:s + 1],
                          as_re * c_im + as_im * c_re + e_im[s:s + 1])
        car_re[:, cols] = c_re
        car_im[:, cols] = c_im
        g_re = hin_re[:, cols]
        g_im = hin_im[:, cols]

        def fix(j, d):
            d_re, d_im = d
            d_re, d_im = a_re * d_re - a_im * d_im, a_re * d_im + a_im * d_re
            r0 = pl.multiple_of(j * SUBLANES, SUBLANES)
            hre[pl.ds(r0, SUBLANES), cols] = hre[pl.ds(r0, SUBLANES), cols] + d_re
            him[pl.ds(r0, SUBLANES), cols] = him[pl.ds(r0, SUBLANES), cols] + d_im
            return d_re, d_im

        lax.fori_loop(0, sub, fix, (g_re, g_im))

    y = _ssm_out(hre[...], him[...], cre_ref, cim_ref) + dskip_ref[...] * up
    ys = _glu(y, wglu_ref).astype(BF16)
    ys_ref[...] = jnp.dot(permt_ref[...], ys, preferred_element_type=F32).astype(ys_ref.dtype)

    @pl.when(ti == pl.num_programs(1) - 1)
    def _():
        hre_out[...] = car_re[...]
        him_out[...] = car_im[...]


def _ssm_prompt(u, w):
    b, s, width = u.shape
    rows = min(SSM_ROWS, s)
    assert s % rows == 0 and rows % SUBLANES == 0
    sub = rows // SUBLANES
    r = np.arange(rows)
    t_of_r = (r % SUBLANES) * sub + r // SUBLANES
    perm = jnp.asarray((t_of_r[:, None] == np.arange(rows)[None, :]).astype(np.float32), BF16)
    pre = w['pow_re'][:sub]
    pim = w['pow_im'][:sub]
    weights = (perm, perm.T, w['bb_re'], w['bb_im'], pre, pim, w['cc_re'], w['cc_im'], w['d_skip'], w['w_glu'])
    ys, h_re, h_im = pl.pallas_call(
        functools.partial(_ssm_prompt_kernel, rows=rows, sub=sub),
        grid=(b, s // rows),
        in_specs=[pl.BlockSpec((None, rows, width), lambda bi, t: (bi, t, 0))]
                 + [_const_spec(z.shape) for z in weights],
        out_specs=(pl.BlockSpec((None, rows, width), lambda bi, t: (bi, t, 0)),
                   pl.BlockSpec((None, 1, N_STATE), lambda bi, t: (bi, 0, 0)),
                   pl.BlockSpec((None, 1, N_STATE), lambda bi, t: (bi, 0, 0))),
        out_shape=(jax.ShapeDtypeStruct((b, s, width), BF16),
                   jax.ShapeDtypeStruct((b, 1, N_STATE), F32),
                   jax.ShapeDtypeStruct((b, 1, N_STATE), F32)),
        scratch_shapes=[pltpu.VMEM((rows, N_STATE), F32), pltpu.VMEM((rows, N_STATE), F32),
                        pltpu.VMEM((1, N_STATE), F32), pltpu.VMEM((1, N_STATE), F32),
                        pltpu.VMEM((SUBLANES, N_STATE), F32), pltpu.VMEM((SUBLANES, N_STATE), F32)],
        compiler_params=_cparams(("parallel", "arbitrary")),
        name="ssm_prompt",
    )(u, *weights)
    return ys, h_re.reshape(b, N_GROUPS, STATE), h_im.reshape(b, N_GROUPS, STATE)


def _ssm_sample_kernel(u_ref, h0re_ref, h0im_ref, bre_ref, bim_ref, pre_ref, pim_ref, cre_ref, cim_ref,
                       dskip_ref, wglu_ref, ys_ref, hre_out, him_out, bu_re, bu_im, *, steps):
    a_re = pre_ref[0:1, :]
    a_im = pim_ref[0:1, :]
    h_re = h0re_ref[...]
    h_im = h0im_ref[...]

    def st_re(c0, n, val):
        bu_re[:, c0:c0 + n] = val

    def st_im(c0, n, val):
        bu_im[:, c0:c0 + n] = val

    for t in range(steps):
        u = u_ref[t]
        _ssm_in(u.astype(BF16), bre_ref, bim_ref, st_re, st_im)
        h_re, h_im = (a_re * h_re - a_im * h_im + bu_re[...], a_re * h_im + a_im * h_re + bu_im[...])
        y = _ssm_out(h_re, h_im, cre_ref, cim_ref) + dskip_ref[...] * u
        ys_ref[t] = _glu(y, wglu_ref).astype(ys_ref.dtype)
    hre_out[...] = h_re
    him_out[...] = h_im


def _ssm_sample(ut, h0_re, h0_im, w):
    t, bd, width = ut.shape
    ys, h_re, h_im = pl.pallas_call(
        functools.partial(_ssm_sample_kernel, steps=t),
        out_shape=(jax.ShapeDtypeStruct((t, bd, width), BF16),
                   jax.ShapeDtypeStruct((bd, N_STATE), F32),
                   jax.ShapeDtypeStruct((bd, N_STATE), F32)),
        scratch_shapes=[pltpu.VMEM((bd, N_STATE), F32), pltpu.VMEM((bd, N_STATE), F32)],
        compiler_params=pltpu.CompilerParams(vmem_limit_bytes=VMEM_LIMIT_BYTES),
        name="ssm_sample",
    )(ut, h0_re.reshape(bd, N_STATE).astype(F32), h0_im.reshape(bd, N_STATE).astype(F32),
      w['bb_re'], w['bb_im'], w['pow_re'][:1], w['pow_im'][:1], w['cc_re'], w['cc_im'], w['d_skip'], w['w_glu'])
    return ys, h_re.reshape(bd, N_GROUPS, STATE), h_im.reshape(bd, N_GROUPS, STATE)


def _post_kernel(*refs, rows, d_ff, row_major):
    if row_major:
        (x_ref, att_ref, ys_ref, sga_ref, sgs_ref, p_ref, woa_ref, wos_ref, wout_ref, gffn_ref, wup_ref,
         cw_ref, cbias_ref, wdown_ref, gple_ref, wpg_ref, wpp_ref, y_ref, tail_ref, hist) = refs
        hist0_ref = None
    else:
        (x_ref, att_ref, ys_ref, sga_ref, sgs_ref, p_ref, hist0_ref, woa_ref, wos_ref, wout_ref, gffn_ref, wup_ref,
         cw_ref, cbias_ref, wdown_ref, gple_ref, wpg_ref, wpp_ref, y_ref, tail_ref, hist) = refs
    ti = pl.program_id(1)
    pad = SUBLANES

    mixed = (sga_ref[...].astype(F32) * jnp.dot(att_ref[...], woa_ref[...], preferred_element_type=F32)
             + sgs_ref[...].astype(F32) * jnp.dot(ys_ref[...], wos_ref[...], preferred_element_type=F32))
    x1 = x_ref[...] + _bdot(mixed, wout_ref[...])
    xn = _rms(x1, gffn_ref[...]).astype(BF16)

    @pl.when(ti == 0)
    def _():
        if row_major:
            hist[0:pad, :] = jnp.zeros((pad, 2 * d_ff), F32)
        else:
            hist[...] = hist0_ref[...]

    def up_cols(c0):
        cols = slice(c0, c0 + FFN_CHUNK)
        up = jnp.dot(xn, wup_ref[:, cols], preferred_element_type=F32)
        if row_major:
            hist[pad:pad + rows, cols] = up
            s1 = hist[pad - 1:pad - 1 + rows, cols]
            s2 = hist[pad - 2:pad - 2 + rows, cols]
        else:
            s2 = hist[0:rows, cols]
            s1 = hist[rows:2 * rows, cols]
            hist[0:rows, cols] = s1
            hist[rows:2 * rows, cols] = up
        return cols, up, s1, s2

    def conv(args):
        cols, up, s1, s2 = args
        return cbias_ref[:, cols] + s2 * cw_ref[0:1, cols] + s1 * cw_ref[1:2, cols] + up * cw_ref[2:3, cols]

    n_chunks = d_ff // FFN_CHUNK
    ahead = [(up_cols(c * FFN_CHUNK), up_cols(d_ff + c * FFN_CHUNK)) for c in range(min(FFN_AHEAD, n_chunks))]
    f = jnp.zeros((rows, x1.shape[-1]), F32)
    for c in range(n_chunks):
        cur = ahead.pop(0)
        if c + FFN_AHEAD < n_chunks:
            ahead.append((up_cols((c + FFN_AHEAD) * FFN_CHUNK), up_cols(d_ff + (c + FFN_AHEAD) * FFN_CHUNK)))
        hdn = (jax.nn.gelu(conv(cur[0])) * conv(cur[1])).astype(BF16)
        f = f + jnp.dot(hdn, wdown_ref[c * FFN_CHUNK:(c + 1) * FFN_CHUNK, :], preferred_element_type=F32)
    x2 = x1 + f

    if row_major:
        tail = hist[rows:rows + pad, :]
        hist[0:pad, :] = tail
        tail_ref[...] = tail
    else:
        @pl.when(ti == pl.num_programs(1) - 1)
        def _():
            tail_ref[...] = hist[...]

    gate = jax.nn.sigmoid(_bdot(_rms(x2, gple_ref[...]), wpg_ref[...]))
    y_ref[...] = x2 + gate * _bdot(p_ref[...], wpp_ref[...])


def _post(x, att, ys, sga, sgs, p, w, hist0=None, step_rows=None):
    nb, total, d = x.shape
    d_ff = w['w_down'].shape[0]
    assert d_ff % FFN_CHUNK == 0
    row_major = hist0 is None
    rows = min(POST_ROWS, total) if row_major else step_rows
    assert total % rows == 0
    row_spec = lambda n: pl.BlockSpec((None, rows, n), lambda b, t: (b, t, 0))
    acts = [x, att, ys, sga, sgs, p]
    act_specs = [row_spec(z.shape[-1]) for z in acts]
    weights = (w['w_oa'], w['w_os'], w['w_out'], w['g_ffn'], w['w_up'], w['conv_w'], w['conv_b'], w['w_down'],
               w['g_ple'], w['w_ple_gate'], w['w_ple_proj'])
    if row_major:
        hist_rows = SUBLANES
        tail_spec = pl.BlockSpec((None, hist_rows, 2 * d_ff), lambda b, t: (b, 0, 0))
        tail_shape = jax.ShapeDtypeStruct((nb, hist_rows, 2 * d_ff), F32)
        scratch = pltpu.VMEM((rows + hist_rows, 2 * d_ff), F32)
    else:
        assert nb == 1
        hist_rows = 2 * rows
        acts.append(hist0)
        act_specs.append(_const_spec(hist0.shape))
        tail_spec = pl.BlockSpec((hist_rows, 2 * d_ff), lambda b, t: (0, 0), pipeline_mode=pl.Buffered(1))
        tail_shape = jax.ShapeDtypeStruct((hist_rows, 2 * d_ff), F32)
        scratch = pltpu.VMEM((hist_rows, 2 * d_ff), F32)
    return pl.pallas_call(
        functools.partial(_post_kernel, rows=rows, d_ff=d_ff, row_major=row_major),
        grid=(nb, total // rows),
        in_specs=act_specs + [_const_spec(z.shape) for z in weights],
        out_specs=(row_spec(d), tail_spec),
        out_shape=(jax.ShapeDtypeStruct((nb, total, d), F32), tail_shape),
        scratch_shapes=[scratch],
        compiler_params=_cparams(("parallel", "arbitrary")),
        name="post_prompt" if row_major else "post_sample",
    )(*acts, *weights)


def _rope_cos_sin(pos):
    inv_freq = np.power(np.float32(ROPE_THETA), -np.arange(0, QK_ROPE, 2, dtype=np.float32) / np.float32(QK_ROPE))
    ang = (np.asarray(pos, np.float32)[:, None] * inv_freq[None, :]).astype(np.float64)
    return np.cos(ang).astype(np.float32), np.sin(ang).astype(np.float32)


def _rope_tables_q(pos):
    cos, sin = _rope_cos_sin(pos)
    n = cos.shape[0]
    z = lambda k: np.zeros((n, k), np.float32)
    rc = np.concatenate([np.ones((n, QK_NOPE), np.float32), cos, cos, z(HEAD_PAD - QK_HEAD)], axis=1)
    rs = np.concatenate([z(QK_NOPE), -sin, sin, z(HEAD_PAD - QK_HEAD)], axis=1)
    return rc, rs, np.concatenate([cos, cos], axis=1), np.concatenate([-sin, sin], axis=1)


def _rope_tables_k(pos):
    cos, sin = _rope_cos_sin(pos)
    return (np.ascontiguousarray(np.concatenate([cos, cos], axis=1).T),
            np.ascontiguousarray(np.concatenate([sin, -sin], axis=1).T))


def _pad_heads(wm, width):
    rows = wm.shape[0]
    return jnp.pad(wm, ((0, 0), (0, 0), (0, HEAD_PAD - width))).reshape(rows, N_HEADS * HEAD_PAD)


def _layer_weights(lw, d_model, n_pow):
    w = {}
    row = lambda z: z.reshape(1, -1).astype(F32)
    off_ckv = Q_LORA
    off_kr = off_ckv + KV_LORA
    off_u = off_kr + QK_ROPE
    off_ga = off_u + SSM_WIDTH
    off_gs = off_ga + d_model
    win = lw['w_in']
    wt = win.T.astype(BF16)
    w['w_in'] = jnp.concatenate(
        [wt[:off_kr], wt[off_u:], wt[off_kr:off_u], jnp.zeros((LANES - QK_ROPE, d_model), BF16)], axis=0)
    w['g_mix'] = row(lw['g_mix'])
    w['g_cq'] = row(lw['g_cq'])
    w['g_ckv'] = row(lw['g_ckv'])
    w['w_uq'] = _pad_heads(lw['w_uq'], QK_HEAD).astype(BF16)
    w['w_uk'] = _pad_heads(lw['w_uk'], QK_NOPE).astype(BF16)
    w['w_uv_t'] = _pad_heads(lw['w_uv'], V_HEAD).T.astype(BF16)
    const = lambda z, dt: jnp.asarray(np.asarray(z, np.float32), dt)
    lane = np.arange(N_HEADS * HEAD_PAD)
    w['e_kr'] = const((lane[None, :] % HEAD_PAD) == (QK_NOPE + np.arange(QK_ROPE))[:, None], BF16)
    w['g_q'] = row(jnp.pad(lw['g_q'], (0, HEAD_PAD - QK_HEAD)))
    w['g_k_nope'] = row(jnp.pad(lw['g_k'][:QK_NOPE], (0, HEAD_PAD - QK_NOPE)))
    w['g_k_rope'] = row(lw['g_k'][QK_NOPE:])
    half = QK_ROPE // 2
    wq = lw['w_uq'] * lw['g_q'][None, None, :]
    partner = jnp.concatenate([jnp.zeros_like(wq[..., :QK_NOPE]), wq[..., QK_NOPE + half:QK_HEAD],
                               wq[..., QK_NOPE:QK_NOPE + half]], axis=-1)
    w['w_uq_rot'] = _pad_heads(partner, QK_HEAD).astype(BF16)
    idx = np.arange(QK_ROPE)
    w['swap_rope'] = const(idx[:, None] == ((idx + half) % QK_ROPE)[None, :], BF16)
    blk = np.arange(2 * HEAD_PAD) // HEAD_PAD
    w['ones2'] = const(blk[:, None] == blk[None, :], BF16)
    w['v_one'] = const(((lane % HEAD_PAD) == V_HEAD).reshape(-1, 1), F32)
    w['w_uk_perm'] = lw['w_uk'].transpose(0, 2, 1).reshape(KV_LORA, QK_NOPE * N_HEADS).astype(BF16)
    w['g_k_nope_perm'] = row(jnp.repeat(lw['g_k'][:QK_NOPE], N_HEADS))
    w['g_k_rope_col'] = lw['g_k'][QK_NOPE:].reshape(QK_ROPE, 1).astype(F32)
    w['w_uv_flat'] = lw['w_uv'].reshape(KV_LORA, N_HEADS * V_HEAD).astype(BF16)
    bb_re, bb_im, pow_re, pow_im = _ssm_discretize(lw['a_re'].astype(F32), lw['a_im'].astype(F32),
                                                   lw['log_dt'].astype(F32), lw['b_re'].astype(F32),
                                                   lw['b_im'].astype(F32), n_pow)
    w['bb_re'] = _block_diag_in(bb_re)
    w['bb_im'] = _block_diag_in(bb_im)
    w['pow_re'] = pow_re
    w['pow_im'] = pow_im
    w['cc_re'] = _block_diag_out(lw['c_re'].astype(F32))
    w['cc_im'] = _block_diag_out(lw['c_im'].astype(F32))
    w['d_skip'] = row(lw['d_skip'])
    w['w_glu'] = lw['w_glu'].astype(BF16)
    w['w_oa'] = lw['w_oa'].astype(BF16)
    w['w_os'] = lw['w_os'].astype(BF16)
    w['w_out'] = lw['w_out'].astype(BF16)
    w['g_ffn'] = row(lw['g_ffn'])
    w['w_up'] = lw['w_up'].astype(BF16)
    w['conv_w'] = lw['conv_w'].astype(F32)
    w['conv_b'] = row(lw['conv_b'])
    w['w_down'] = lw['w_down'].astype(BF16)
    w['g_ple'] = row(lw['g_ple'])
    w['w_ple_gate'] = lw['w_ple_gate'].astype(BF16)
    w['w_ple_proj'] = lw['w_ple_proj'].astype(BF16)
    return w


def kernel(x_prompt, x_sample, p_prompt, p_sample, cache_ckv, cache_kr, page_table, state_ssm_re, state_ssm_im,
           state_conv, g_mix, w_in, g_cq, g_ckv, w_uq, w_uk, w_uv, g_q, g_k, a_re, a_im, log_dt, b_re, b_im,
           c_re, c_im, d_skip, w_glu, w_oa, w_os, w_out, g_ffn, w_up, conv_w, conv_b, w_down, g_ple,
           w_ple_gate, w_ple_proj):
    params = dict(g_mix=g_mix, w_in=w_in, g_cq=g_cq, g_ckv=g_ckv, w_uq=w_uq, w_uk=w_uk, w_uv=w_uv, g_q=g_q,
                  g_k=g_k, a_re=a_re, a_im=a_im, log_dt=log_dt, b_re=b_re, b_im=b_im, c_re=c_re, c_im=c_im,
                  d_skip=d_skip, w_glu=w_glu, w_oa=w_oa, w_os=w_os, w_out=w_out, g_ffn=g_ffn, w_up=w_up,
                  conv_w=conv_w, conv_b=conv_b, w_down=w_down, g_ple=g_ple, w_ple_gate=w_ple_gate,
                  w_ple_proj=w_ple_proj)
    depth = w_in.shape[0]
    bp, seq, d_model = x_prompt.shape
    bd, t_new, _ = x_sample.shape
    past_len = page_table.shape[1] * PAGE_SIZE
    d_ff = w_down.shape[1]
    n_pow = max(min(SSM_ROWS, seq) // SUBLANES, 1)

    rope_prompt = _rope_tables_q(np.arange(seq))
    rope_sample = _rope_tables_q(np.repeat(past_len + np.arange(t_new), bd))
    rope_cache = _rope_tables_k(np.arange(past_len))

    yp, ys = x_prompt, x_sample
    outs = [[] for _ in range(10)]
    for i in range(depth):
        w = _layer_weights({k: v[i] for k, v in params.items()}, d_model, n_pow)

        q, k, v, ckv, kr, u, sga, sgs = _inproj(yp, w, rope_prompt, BF16, SCALE * LOG2_E)
        att = _flash_prompt(q, k, v)
        yssm, hr, hi = _ssm_prompt(u, w)
        yp, tail = _post(yp, att, yssm, sga, sgs, p_prompt[i], w)
        outs[0].append(ckv); outs[1].append(kr); outs[4].append(hr); outs[5].append(hi)
        outs[8].append(tail[:, SUBLANES - (CONV_W - 1):, :])

        n_tok = bd * t_new
        steps = lambda z: z.transpose(1, 0, 2).reshape(1, n_tok, z.shape[-1])
        entries = lambda z: z.reshape(t_new, bd, z.shape[-1]).transpose(1, 0, 2)
        q, k, v, ckv, kr, u, sga, sgs = _inproj(steps(ys), w, rope_sample, F32, SCALE)
        ckv = entries(ckv)
        att = _sample_attention(entries(q), entries(k), ckv, cache_ckv, cache_kr, page_table, i, w, rope_cache)
        yssm, hr, hi = _ssm_sample(u.reshape(t_new, bd, -1), state_ssm_re[i], state_ssm_im[i], w)
        hist0 = state_conv[i].astype(F32).transpose(1, 0, 2).reshape((CONV_W - 1) * bd, 2 * d_ff)
        y2, tail = _post(steps(ys), steps(att).astype(BF16), yssm.reshape(1, n_tok, -1), sga, sgs,
                         steps(p_sample[i]), w, hist0=hist0, step_rows=bd)
        ys = entries(y2)
        outs[2].append(ckv); outs[3].append(entries(kr))
        outs[6].append(hr); outs[7].append(hi)
        outs[9].append(tail.reshape(CONV_W - 1, bd, 2 * d_ff).transpose(1, 0, 2))

    st = [jnp.stack(o) for o in outs]
    return (yp, ys, st[0], st[1], st[2], st[3], st[4], st[5], st[6], st[7], st[8], st[9])
```
